```python
import math
import jax, jax.numpy as jnp
from jax import lax
import numpy as np

D_MODEL = 1024
BATCH = 8
SEQ = 8192
DEPTH = 4

N_MIXERS = 3
N_A = (DEPTH + 2) // 3
N_B = (DEPTH + 1) // 3
N_C = DEPTH // 3
BLOCK = 128
N_META = 16
PAD = BLOCK - N_META
NEG = -1e30

DEEPNORM_ALPHA = (2.0 * DEPTH) ** 0.25
DEEPNORM_BETA = (8.0 * DEPTH) ** -0.25
LN_EPS = 1e-5
RMS_EPS = 1e-6

FOX_HEADS = 16
FOX_HEAD_DIM = 64
FOX_GATE_BIAS = 4.0

SWA_Q_HEADS = 16
SWA_KV_HEADS = 2
SWA_HEAD_DIM = 64
WINDOW = 128
ROPE_THETA = 500000.0
ROPE_DIM = SWA_HEAD_DIM // 4

MLA_HEADS = 16
MLA_Q_LORA = 384
MLA_KV_LORA = 256
MLA_NOPE = 64
MLA_ROPE = 32
MLA_V = 64
MLA_ROPE_THETA = 10000.0

D_FF = 2816
CONV_W = 3

kernel_name = "hybrid_fox_swa_mla_convffn_trunk"


def layer_norm(x, g, b):
    xf = x.astype(jnp.float32)
    mu = jnp.mean(xf, axis=-1, keepdims=True)
    var = jnp.mean(jnp.square(xf - mu), axis=-1, keepdims=True)
    return ((xf - mu) * lax.rsqrt(var + LN_EPS) * g + b).astype(x.dtype)


def rms_norm(x, g):
    xf = x.astype(jnp.float32)
    return (xf * lax.rsqrt(jnp.mean(jnp.square(xf), axis=-1, keepdims=True) + RMS_EPS) * g).astype(x.dtype)


def rope_angles(pos, dim, theta):
    inv = theta ** (-jnp.arange(0, dim, 2, dtype=jnp.float32) / dim)
    ang = pos.astype(jnp.float32)[:, None] * inv[None, :]
    return jnp.cos(ang), jnp.sin(ang)


def apply_rope(x, cos, sin):
    x1, x2 = jnp.split(x.astype(jnp.float32), 2, axis=-1)
    c = cos[None, :, None, :]
    s = sin[None, :, None, :]
    return jnp.concatenate([x1 * c - x2 * s, x1 * s + x2 * c], axis=-1).astype(x.dtype)


def partial_rope(x, cos, sin):
    return jnp.concatenate([apply_rope(x[..., :ROPE_DIM], cos, sin), x[..., ROPE_DIM:]], axis=-1)


def dense_causal_attention(q, k, v, decay=None):
    B, L, H, dk = q.shape
    nblk = L // BLOCK
    scale = dk ** -0.5
    kpos = jnp.arange(L)
    kvalid = kpos >= PAD
    qb = q.reshape(B, nblk, BLOCK, H, dk).transpose(1, 0, 2, 3, 4)
    xs = (jnp.arange(nblk), qb)
    if decay is not None:
        ck = decay.transpose(0, 2, 1)
        cb = decay.reshape(B, nblk, BLOCK, H).transpose(1, 0, 3, 2)
        xs = xs + (cb,)

    def one_block(args):
        i, qi = args[0], args[1]
        s = jnp.einsum('bqhd,bkhd->bhqk', qi, k, preferred_element_type=jnp.float32) * scale
        if decay is not None:
            s = s + args[2][..., None] - ck[:, :, None, :]
        qpos = i * BLOCK + jnp.arange(BLOCK)
        mask = (kpos[None, :] <= qpos[:, None]) & kvalid[None, :]
        s = jnp.where(mask[None, None], s, NEG)
        p = jax.nn.softmax(s, axis=-1)
        return jnp.einsum('bhqk,bkhd->bqhd', p.astype(v.dtype), v)

    out = lax.map(one_block, xs)
    return out.transpose(1, 0, 2, 3, 4).reshape(B, L, H, v.shape[-1])


def fox_mixer(h, w_in, b_f, w_o):
    B, L, _ = h.shape
    hd = FOX_HEADS * FOX_HEAD_DIM
    proj = h @ w_in
    q, k, v, fg = jnp.split(proj, [hd, 2 * hd, 3 * hd], axis=-1)
    q = q.reshape(B, L, FOX_HEADS, FOX_HEAD_DIM)
    k = k.reshape(B, L, FOX_HEADS, FOX_HEAD_DIM)
    v = v.reshape(B, L, FOX_HEADS, FOX_HEAD_DIM)
    log_f = jax.nn.log_sigmoid((fg + b_f).astype(jnp.float32))
    c = jnp.cumsum(log_f, axis=1)
    o = dense_causal_attention(q, k, v, c)
    return o.reshape(B, L, hd) @ w_o


def swa_mixer(h, w_in, sinks, w_o, cos, sin):
    B, L, _ = h.shape
    G = SWA_Q_HEADS // SWA_KV_HEADS
    d = SWA_HEAD_DIM
    nblk = L // BLOCK
    qd, kd = SWA_Q_HEADS * d, SWA_KV_HEADS * d
    q, k, v = jnp.split(h @ w_in, [qd, qd + kd], axis=-1)
    q = partial_rope(q.reshape(B, L, SWA_Q_HEADS, d), cos, sin)
    k = partial_rope(k.reshape(B, L, SWA_KV_HEADS, d), cos, sin)
    v = v.reshape(B, L, SWA_KV_HEADS, d)

    qb = q.reshape(B, nblk, BLOCK, SWA_KV_HEADS, G, d)
    kb = k.reshape(B, nblk, BLOCK, SWA_KV_HEADS, d)
    vb = v.reshape(B, nblk, BLOCK, SWA_KV_HEADS, d)
    pad_blk = ((0, 0), (1, 0), (0, 0), (0, 0), (0, 0))
    kband = jnp.concatenate([jnp.pad(kb, pad_blk)[:, :-1], kb], axis=2)
    vband = jnp.concatenate([jnp.pad(vb, pad_blk)[:, :-1], vb], axis=2)
    kmeta = k[:, PAD:PAD + N_META]
    vmeta = v[:, PAD:PAD + N_META]

    scale = d ** -0.5
    s_band = jnp.einsum('bnqhgd,bnkhd->bnhgqk', qb, kband, preferred_element_type=jnp.float32) * scale
    s_meta = jnp.einsum('bnqhgd,bmhd->bnhgqm', qb, kmeta, preferred_element_type=jnp.float32) * scale

    qpos = jnp.arange(L).reshape(nblk, BLOCK)
    kpos = (jnp.arange(nblk)[:, None] - 1) * BLOCK + jnp.arange(2 * BLOCK)[None, :]
    diff = qpos[:, :, None] - kpos[:, None, :]
    band_mask = (diff >= 0) & (diff < WINDOW) & (kpos[:, None, :] >= PAD + N_META)
    meta_pos = PAD + jnp.arange(N_META)
    meta_mask = meta_pos[None, None, :] <= qpos[:, :, None]
    s_band = jnp.where(band_mask[None, :, None, None], s_band, NEG)
    s_meta = jnp.where(meta_mask[None, :, None, None], s_meta, NEG)

    sink = sinks.astype(jnp.float32).reshape(SWA_KV_HEADS, G)[None, None, :, :, None, None]
    m = jnp.maximum(jnp.maximum(s_band.max(-1, keepdims=True), s_meta.max(-1, keepdims=True)), sink)
    p_band = jnp.exp(s_band - m)
    p_meta = jnp.exp(s_meta - m)
    denom = p_band.sum(-1, keepdims=True) + p_meta.sum(-1, keepdims=True) + jnp.exp(sink - m)
    p_band = (p_band / denom).astype(v.dtype)
    p_meta = (p_meta / denom).astype(v.dtype)
    o = (jnp.einsum('bnhgqk,bnkhd->bnqhgd', p_band, vband)
         + jnp.einsum('bnhgqm,bmhd->bnqhgd', p_meta, vmeta))
    return o.reshape(B, L, qd) @ w_o


def mla_mixer(h, w_a, g_q, g_kv, w_uq, w_ukv, w_o, cos, sin):
    B, L, _ = h.shape
    cq, ckv, kr = jnp.split(h @ w_a, [MLA_Q_LORA, MLA_Q_LORA + MLA_KV_LORA], axis=-1)
    cq = rms_norm(cq, g_q)
    ckv = rms_norm(ckv, g_kv)
    q = (cq @ w_uq).reshape(B, L, MLA_HEADS, MLA_NOPE + MLA_ROPE)
    q_nope, q_rope = jnp.split(q, [MLA_NOPE], axis=-1)
    q_rope = apply_rope(q_rope, cos, sin)
    k_rope = apply_rope(kr[:, :, None, :], cos, sin)
    kv = (ckv @ w_ukv).reshape(B, L, MLA_HEADS, MLA_NOPE + MLA_V)
    k_nope, v = jnp.split(kv, [MLA_NOPE], axis=-1)
    qf = jnp.concatenate([q_nope, q_rope], axis=-1)
    kf = jnp.concatenate([k_nope, jnp.broadcast_to(k_rope, (B, L, MLA_HEADS, MLA_ROPE))], axis=-1)
    o = dense_causal_attention(qf, kf, v)
    return o.reshape(B, L, MLA_HEADS * MLA_V) @ w_o


def conv_glu_ffn(h, w_in, conv_w, conv_b, w_out, valid):
    L = h.shape[1]
    u = (h @ w_in) * valid.astype(h.dtype)[None, :, None]
    up = jnp.pad(u, ((0, 0), (CONV_W - 1, 0), (0, 0)))
    y = conv_b + up[:, 0:L] * conv_w[0]
    for j in range(1, CONV_W):
        y = y + up[:, j:j + L] * conv_w[j]
    g, val = jnp.split(y, 2, axis=-1)
    return (jax.nn.silu(g) * val) @ w_out


def _dense(key, shape, fan_in, gain=1.0):
    return jax.random.normal(key, shape, jnp.float32) * (gain * fan_in ** -0.5)


def _fwd_setup_inputs(seed: int = 0) -> dict:
    key = jax.random.key(seed)
    ks = jax.random.split(key, 24)
    D = D_MODEL
    beta = DEEPNORM_BETA
    nrm = lambda k, s: jax.random.normal(k, s, jnp.float32)

    x = nrm(ks[0], (BATCH, SEQ, D))
    meta_tokens = nrm(ks[1], (N_META, D))
    ln1_g = 1.0 + 0.01 * nrm(ks[2], (DEPTH, D))
    ln1_b = 0.01 * nrm(ks[3], (DEPTH, D))
    ln2_g = 1.0 + 0.01 * nrm(ks[4], (DEPTH, D))
    ln2_b = 0.01 * nrm(ks[5], (DEPTH, D))

    fhd = FOX_HEADS * FOX_HEAD_DIM
    fox_cols = jnp.concatenate([jnp.ones((2 * fhd,), jnp.float32), jnp.full((fhd,), beta, jnp.float32),
                                jnp.ones((FOX_HEADS,), jnp.float32)])
    fox_w_in = _dense(ks[6], (N_A, D, 3 * fhd + FOX_HEADS), D) * fox_cols
    fox_b_f = FOX_GATE_BIAS + 0.5 * nrm(ks[7], (N_A, FOX_HEADS))
    fox_w_o = _dense(ks[8], (N_A, fhd, D), fhd, beta)

    sqd, skd = SWA_Q_HEADS * SWA_HEAD_DIM, SWA_KV_HEADS * SWA_HEAD_DIM
    swa_cols = jnp.concatenate([jnp.ones((sqd + skd,), jnp.float32), jnp.full((skd,), beta, jnp.float32)])
    swa_w_in = _dense(ks[9], (N_B, D, sqd + 2 * skd), D) * swa_cols
    swa_sinks = 0.5 * nrm(ks[10], (N_B, SWA_Q_HEADS))
    swa_w_o = _dense(ks[11], (N_B, sqd, D), sqd, beta)

    mla_w_a = _dense(ks[12], (N_C, D, MLA_Q_LORA + MLA_KV_LORA + MLA_ROPE), D)
    mla_g_q = 1.0 + 0.01 * nrm(ks[13], (N_C, MLA_Q_LORA))
    mla_g_kv = 1.0 + 0.01 * nrm(ks[14], (N_C, MLA_KV_LORA))
    mla_w_uq = _dense(ks[15], (N_C, MLA_Q_LORA, MLA_HEADS * (MLA_NOPE + MLA_ROPE)), MLA_Q_LORA)
    ukv_cols = jnp.tile(jnp.concatenate([jnp.ones((MLA_NOPE,), jnp.float32),
                                         jnp.full((MLA_V,), beta, jnp.float32)]), MLA_HEADS)
    mla_w_ukv = _dense(ks[16], (N_C, MLA_KV_LORA, MLA_HEADS * (MLA_NOPE + MLA_V)), MLA_KV_LORA) * ukv_cols
    mla_w_o = _dense(ks[17], (N_C, MLA_HEADS * MLA_V, D), MLA_HEADS * MLA_V, beta)

    ffn_w_in = _dense(ks[18], (DEPTH, D, 2 * D_FF), D, beta)
    ffn_conv_w = _dense(ks[19], (DEPTH, CONV_W, 2 * D_FF), CONV_W)
    ffn_conv_b = 0.01 * nrm(ks[20], (DEPTH, 2 * D_FF))
    ffn_w_out = _dense(ks[21], (DEPTH, D_FF, D), D_FF, beta)

    return {"x": x, "meta_tokens": meta_tokens, "ln1_g": ln1_g, "ln1_b": ln1_b, "ln2_g": ln2_g, "ln2_b": ln2_b,
            "fox_w_in": fox_w_in, "fox_b_f": fox_b_f, "fox_w_o": fox_w_o,
            "swa_w_in": swa_w_in, "swa_sinks": swa_sinks, "swa_w_o": swa_w_o,
            "mla_w_a": mla_w_a, "mla_g_q": mla_g_q, "mla_g_kv": mla_g_kv, "mla_w_uq": mla_w_uq,
            "mla_w_ukv": mla_w_ukv, "mla_w_o": mla_w_o,
            "ffn_w_in": ffn_w_in, "ffn_conv_w": ffn_conv_w, "ffn_conv_b": ffn_conv_b, "ffn_w_out": ffn_w_out}


def _fwd_reference(x, meta_tokens, ln1_g, ln1_b, ln2_g, ln2_b,
              fox_w_in, fox_b_f, fox_w_o,
              swa_w_in, swa_sinks, swa_w_o,
              mla_w_a, mla_g_q, mla_g_kv, mla_w_uq, mla_w_ukv, mla_w_o,
              ffn_w_in, ffn_conv_w, ffn_conv_b, ffn_w_out):
    B, S, D = x.shape
    h = jnp.concatenate([jnp.zeros((B, PAD, D), x.dtype),
                         jnp.broadcast_to(meta_tokens.astype(x.dtype)[None], (B, N_META, D)), x], axis=1)
    L = h.shape[1]
    idx = jnp.arange(L)
    pos = idx - PAD
    valid = idx >= PAD
    cos_p, sin_p = rope_angles(pos, ROPE_DIM, ROPE_THETA)
    cos_m, sin_m = rope_angles(pos, MLA_ROPE, MLA_ROPE_THETA)

    for i in range(DEPTH):
        kind, j = i % N_MIXERS, i // N_MIXERS
        if kind == 0:
            mix = fox_mixer(h, fox_w_in[j], fox_b_f[j], fox_w_o[j])
        elif kind == 1:
            mix = swa_mixer(h, swa_w_in[j], swa_sinks[j], swa_w_o[j], cos_p, sin_p)
        else:
            mix = mla_mixer(h, mla_w_a[j], mla_g_q[j], mla_g_kv[j], mla_w_uq[j], mla_w_ukv[j], mla_w_o[j],
                            cos_m, sin_m)
        h = layer_norm(DEEPNORM_ALPHA * h + mix, ln1_g[i], ln1_b[i])
        ffn = conv_glu_ffn(h, ffn_w_in[i], ffn_conv_w[i], ffn_conv_b[i], ffn_w_out[i], valid)
        h = layer_norm(DEEPNORM_ALPHA * h + ffn, ln2_g[i], ln2_b[i])
    return h[:, BLOCK:]


import jax as _jax
import jax.numpy as _jnp

TWIN_FORMAT = 'train_step'
FWD_PARAMS = ['x', 'meta_tokens', 'ln1_g', 'ln1_b', 'ln2_g', 'ln2_b', 'fox_w_in', 'fox_b_f', 'fox_w_o', 'swa_w_in', 'swa_sinks', 'swa_w_o', 'mla_w_a', 'mla_g_q', 'mla_g_kv', 'mla_w_uq', 'mla_w_ukv', 'mla_w_o', 'ffn_w_in', 'ffn_conv_w', 'ffn_conv_b', 'ffn_w_out']
TWIN_WEIGHTS = ['meta_tokens', 'ln1_g', 'ln1_b', 'ln2_g', 'ln2_b', 'fox_w_in', 'fox_b_f', 'fox_w_o', 'swa_w_in', 'swa_sinks', 'swa_w_o', 'mla_w_a', 'mla_g_q', 'mla_g_kv', 'mla_w_uq', 'mla_w_ukv', 'mla_w_o', 'ffn_w_in', 'ffn_conv_w', 'ffn_conv_b', 'ffn_w_out']
TWIN_DIFF_INPUT = 'x'
TWIN_INPUTS = ['x', 'meta_tokens', 'ln1_g', 'ln1_b', 'ln2_g', 'ln2_b', 'fox_w_in', 'fox_b_f', 'fox_w_o', 'swa_w_in', 'swa_sinks', 'swa_w_o', 'mla_w_a', 'mla_g_q', 'mla_g_kv', 'mla_w_uq', 'mla_w_ukv', 'mla_w_o', 'ffn_w_in', 'ffn_conv_w', 'ffn_conv_b', 'ffn_w_out', 'loss_target', 'm_meta_tokens', 'm_ln1_g', 'm_ln1_b', 'm_ln2_g', 'm_ln2_b', 'm_fox_w_in', 'm_fox_b_f', 'm_fox_w_o', 'm_swa_w_in', 'm_swa_sinks', 'm_swa_w_o', 'm_mla_w_a', 'm_mla_g_q', 'm_mla_g_kv', 'm_mla_w_uq', 'm_mla_w_ukv', 'm_mla_w_o', 'm_ffn_w_in', 'm_ffn_conv_w', 'm_ffn_conv_b', 'm_ffn_w_out', 'v_meta_tokens', 'v_ln1_g', 'v_ln1_b', 'v_ln2_g', 'v_ln2_b', 'v_fox_w_in', 'v_fox_b_f', 'v_fox_w_o', 'v_swa_w_in', 'v_swa_sinks', 'v_swa_w_o', 'v_mla_w_a', 'v_mla_g_q', 'v_mla_g_kv', 'v_mla_w_uq', 'v_mla_w_ukv', 'v_mla_w_o', 'v_ffn_w_in', 'v_ffn_conv_w', 'v_ffn_conv_b', 'v_ffn_w_out']
TWIN_OUTPUTS = ['loss', 'grad_x', 'grad_meta_tokens', 'grad_ln1_g', 'grad_ln1_b', 'grad_ln2_g', 'grad_ln2_b', 'grad_fox_w_in', 'grad_fox_b_f', 'grad_fox_w_o', 'grad_swa_w_in', 'grad_swa_sinks', 'grad_swa_w_o', 'grad_mla_w_a', 'grad_mla_g_q', 'grad_mla_g_kv', 'grad_mla_w_uq', 'grad_mla_w_ukv', 'grad_mla_w_o', 'grad_ffn_w_in', 'grad_ffn_conv_w', 'grad_ffn_conv_b', 'grad_ffn_w_out', 'delta_meta_tokens', 'delta_ln1_g', 'delta_ln1_b', 'delta_ln2_g', 'delta_ln2_b', 'delta_fox_w_in', 'delta_fox_b_f', 'delta_fox_w_o', 'delta_swa_w_in', 'delta_swa_sinks', 'delta_swa_w_o', 'delta_mla_w_a', 'delta_mla_g_q', 'delta_mla_g_kv', 'delta_mla_w_uq', 'delta_mla_w_ukv', 'delta_mla_w_o', 'delta_ffn_w_in', 'delta_ffn_conv_w', 'delta_ffn_conv_b', 'delta_ffn_w_out', 'new_m_meta_tokens', 'new_m_ln1_g', 'new_m_ln1_b', 'new_m_ln2_g', 'new_m_ln2_b', 'new_m_fox_w_in', 'new_m_fox_b_f', 'new_m_fox_w_o', 'new_m_swa_w_in', 'new_m_swa_sinks', 'new_m_swa_w_o', 'new_m_mla_w_a', 'new_m_mla_g_q', 'new_m_mla_g_kv', 'new_m_mla_w_uq', 'new_m_mla_w_ukv', 'new_m_mla_w_o', 'new_m_ffn_w_in', 'new_m_ffn_conv_w', 'new_m_ffn_conv_b', 'new_m_ffn_w_out', 'new_v_meta_tokens', 'new_v_ln1_g', 'new_v_ln1_b', 'new_v_ln2_g', 'new_v_ln2_b', 'new_v_fox_w_in', 'new_v_fox_b_f', 'new_v_fox_w_o', 'new_v_swa_w_in', 'new_v_swa_sinks', 'new_v_swa_w_o', 'new_v_mla_w_a', 'new_v_mla_g_q', 'new_v_mla_g_kv', 'new_v_mla_w_uq', 'new_v_mla_w_ukv', 'new_v_mla_w_o', 'new_v_ffn_w_in', 'new_v_ffn_conv_w', 'new_v_ffn_conv_b', 'new_v_ffn_w_out']
TWIN_LEAF_KINDS = {'loss': 'loss', 'grad_x': 'grad_x', 'grad_meta_tokens': 'grad_w', 'grad_ln1_g': 'grad_w', 'grad_ln1_b': 'grad_w', 'grad_ln2_g': 'grad_w', 'grad_ln2_b': 'grad_w', 'grad_fox_w_in': 'grad_w', 'grad_fox_b_f': 'grad_w', 'grad_fox_w_o': 'grad_w', 'grad_swa_w_in': 'grad_w', 'grad_swa_sinks': 'grad_w', 'grad_swa_w_o': 'grad_w', 'grad_mla_w_a': 'grad_w', 'grad_mla_g_q': 'grad_w', 'grad_mla_g_kv': 'grad_w', 'grad_mla_w_uq': 'grad_w', 'grad_mla_w_ukv': 'grad_w', 'grad_mla_w_o': 'grad_w', 'grad_ffn_w_in': 'grad_w', 'grad_ffn_conv_w': 'grad_w', 'grad_ffn_conv_b': 'grad_w', 'grad_ffn_w_out': 'grad_w', 'delta_meta_tokens': 'delta_w', 'delta_ln1_g': 'delta_w', 'delta_ln1_b': 'delta_w', 'delta_ln2_g': 'delta_w', 'delta_ln2_b': 'delta_w', 'delta_fox_w_in': 'delta_w', 'delta_fox_b_f': 'delta_w', 'delta_fox_w_o': 'delta_w', 'delta_swa_w_in': 'delta_w', 'delta_swa_sinks': 'delta_w', 'delta_swa_w_o': 'delta_w', 'delta_mla_w_a': 'delta_w', 'delta_mla_g_q': 'delta_w', 'delta_mla_g_kv': 'delta_w', 'delta_mla_w_uq': 'delta_w', 'delta_mla_w_ukv': 'delta_w', 'delta_mla_w_o': 'delta_w', 'delta_ffn_w_in': 'delta_w', 'delta_ffn_conv_w': 'delta_w', 'delta_ffn_conv_b': 'delta_w', 'delta_ffn_w_out': 'delta_w', 'new_m_meta_tokens': 'new_m', 'new_m_ln1_g': 'new_m', 'new_m_ln1_b': 'new_m', 'new_m_ln2_g': 'new_m', 'new_m_ln2_b': 'new_m', 'new_m_fox_w_in': 'new_m', 'new_m_fox_b_f': 'new_m', 'new_m_fox_w_o': 'new_m', 'new_m_swa_w_in': 'new_m', 'new_m_swa_sinks': 'new_m', 'new_m_swa_w_o': 'new_m', 'new_m_mla_w_a': 'new_m', 'new_m_mla_g_q': 'new_m', 'new_m_mla_g_kv': 'new_m', 'new_m_mla_w_uq': 'new_m', 'new_m_mla_w_ukv': 'new_m', 'new_m_mla_w_o': 'new_m', 'new_m_ffn_w_in': 'new_m', 'new_m_ffn_conv_w': 'new_m', 'new_m_ffn_conv_b': 'new_m', 'new_m_ffn_w_out': 'new_m', 'new_v_meta_tokens': 'new_v', 'new_v_ln1_g': 'new_v', 'new_v_ln1_b': 'new_v', 'new_v_ln2_g': 'new_v', 'new_v_ln2_b': 'new_v', 'new_v_fox_w_in': 'new_v', 'new_v_fox_b_f': 'new_v', 'new_v_fox_w_o': 'new_v', 'new_v_swa_w_in': 'new_v', 'new_v_swa_sinks': 'new_v', 'new_v_swa_w_o': 'new_v', 'new_v_mla_w_a': 'new_v', 'new_v_mla_g_q': 'new_v', 'new_v_mla_g_kv': 'new_v', 'new_v_mla_w_uq': 'new_v', 'new_v_mla_w_ukv': 'new_v', 'new_v_mla_w_o': 'new_v', 'new_v_ffn_w_in': 'new_v', 'new_v_ffn_conv_w': 'new_v', 'new_v_ffn_conv_b': 'new_v', 'new_v_ffn_w_out': 'new_v'}


def _forward(args):
    return _fwd_reference(*[args[k] for k in FWD_PARAMS])


def _output_shape():
    def fwd():
        inp = _fwd_setup_inputs(0)
        return _fwd_reference(*[inp[k] for k in FWD_PARAMS])
    out = _jax.eval_shape(fwd)
    return out.shape, out.dtype

N_MICROBATCH = 1
ADAM_LR = 0.001
ADAM_B1 = 0.9
ADAM_B2 = 0.999
ADAM_EPS = 1e-08
ADAM_WD = 0.01
ADAM_STEP = 10
PER_EXAMPLE_BATCH_AXIS = {'x': 0, 'loss_target': 0}
SHARED_INPUTS = []
_WEIGHT_DTYPES = {'meta_tokens': _jnp.float32, 'ln1_g': _jnp.float32, 'ln1_b': _jnp.float32, 'ln2_g': _jnp.float32, 'ln2_b': _jnp.float32, 'fox_w_in': _jnp.float32, 'fox_b_f': _jnp.float32, 'fox_w_o': _jnp.float32, 'swa_w_in': _jnp.float32, 'swa_sinks': _jnp.float32, 'swa_w_o': _jnp.float32, 'mla_w_a': _jnp.float32, 'mla_g_q': _jnp.float32, 'mla_g_kv': _jnp.float32, 'mla_w_uq': _jnp.float32, 'mla_w_ukv': _jnp.float32, 'mla_w_o': _jnp.float32, 'ffn_w_in': _jnp.float32, 'ffn_conv_w': _jnp.float32, 'ffn_conv_b': _jnp.float32, 'ffn_w_out': _jnp.float32}
MOMENT_SCALE = {'meta_tokens': 1.463149e-03, 'ln1_g': 1.023452e+00, 'ln1_b': 5.238266e-01, 'ln2_g': 3.203927e+01, 'ln2_b': 8.349345e-01, 'fox_w_in': 9.159230e-03, 'fox_b_f': 2.693706e-02, 'fox_w_o': 1.402958e-02, 'swa_w_in': 1.110907e-02, 'swa_sinks': 7.405990e-04, 'swa_w_o': 1.123641e-02, 'mla_w_a': 9.152506e-03, 'mla_g_q': 6.612097e-03, 'mla_g_kv': 1.145216e-02, 'mla_w_uq': 3.489441e-03, 'mla_w_ukv': 8.167755e-03, 'mla_w_o': 1.118029e-02, 'ffn_w_in': 8.683948e-03, 'ffn_conv_w': 3.668200e-03, 'ffn_conv_b': 8.893590e-03, 'ffn_w_out': 1.424675e-02}


def _to_microbatches(a, axis):
    t = _jnp.moveaxis(a, axis, 0)
    t = t.reshape((N_MICROBATCH, t.shape[0] // N_MICROBATCH) + t.shape[1:])
    return _jnp.moveaxis(t, 1, axis + 1)


def setup_inputs(seed: int = 0) -> dict:
    inp = _fwd_setup_inputs(seed)
    key = _jax.random.fold_in(_jax.random.key(seed), 7919)
    shape, _ = _output_shape()
    out = dict(inp)
    out["loss_target"] = _jax.random.normal(_jax.random.fold_in(key, 0), shape, _jnp.float32)
    for i, name in enumerate(TWIN_WEIGHTS):
        w = inp[name].astype(_jnp.float32)
        if MOMENT_SCALE is None:
            s = _jnp.sqrt(_jnp.mean(_jnp.square(w)) + 1e-30)
        else:
            s = MOMENT_SCALE[name]
        km, kv = _jax.random.split(_jax.random.fold_in(key, i + 1))
        out[name] = w
        out["m_" + name] = s * _jax.random.normal(km, w.shape, _jnp.float32)
        out["v_" + name] = (s * s) * _jax.random.uniform(kv, w.shape, _jnp.float32, 0.5, 1.5)
    if N_MICROBATCH > 1:
        for name, axis in PER_EXAMPLE_BATCH_AXIS.items():
            out[name] = _to_microbatches(out[name], axis)
    return {'x': out['x'], 'meta_tokens': out['meta_tokens'], 'ln1_g': out['ln1_g'], 'ln1_b': out['ln1_b'], 'ln2_g': out['ln2_g'], 'ln2_b': out['ln2_b'], 'fox_w_in': out['fox_w_in'], 'fox_b_f': out['fox_b_f'], 'fox_w_o': out['fox_w_o'], 'swa_w_in': out['swa_w_in'], 'swa_sinks': out['swa_sinks'], 'swa_w_o': out['swa_w_o'], 'mla_w_a': out['mla_w_a'], 'mla_g_q': out['mla_g_q'], 'mla_g_kv': out['mla_g_kv'], 'mla_w_uq': out['mla_w_uq'], 'mla_w_ukv': out['mla_w_ukv'], 'mla_w_o': out['mla_w_o'], 'ffn_w_in': out['ffn_w_in'], 'ffn_conv_w': out['ffn_conv_w'], 'ffn_conv_b': out['ffn_conv_b'], 'ffn_w_out': out['ffn_w_out'], 'loss_target': out['loss_target'], 'm_meta_tokens': out['m_meta_tokens'], 'm_ln1_g': out['m_ln1_g'], 'm_ln1_b': out['m_ln1_b'], 'm_ln2_g': out['m_ln2_g'], 'm_ln2_b': out['m_ln2_b'], 'm_fox_w_in': out['m_fox_w_in'], 'm_fox_b_f': out['m_fox_b_f'], 'm_fox_w_o': out['m_fox_w_o'], 'm_swa_w_in': out['m_swa_w_in'], 'm_swa_sinks': out['m_swa_sinks'], 'm_swa_w_o': out['m_swa_w_o'], 'm_mla_w_a': out['m_mla_w_a'], 'm_mla_g_q': out['m_mla_g_q'], 'm_mla_g_kv': out['m_mla_g_kv'], 'm_mla_w_uq': out['m_mla_w_uq'], 'm_mla_w_ukv': out['m_mla_w_ukv'], 'm_mla_w_o': out['m_mla_w_o'], 'm_ffn_w_in': out['m_ffn_w_in'], 'm_ffn_conv_w': out['m_ffn_conv_w'], 'm_ffn_conv_b': out['m_ffn_conv_b'], 'm_ffn_w_out': out['m_ffn_w_out'], 'v_meta_tokens': out['v_meta_tokens'], 'v_ln1_g': out['v_ln1_g'], 'v_ln1_b': out['v_ln1_b'], 'v_ln2_g': out['v_ln2_g'], 'v_ln2_b': out['v_ln2_b'], 'v_fox_w_in': out['v_fox_w_in'], 'v_fox_b_f': out['v_fox_b_f'], 'v_fox_w_o': out['v_fox_w_o'], 'v_swa_w_in': out['v_swa_w_in'], 'v_swa_sinks': out['v_swa_sinks'], 'v_swa_w_o': out['v_swa_w_o'], 'v_mla_w_a': out['v_mla_w_a'], 'v_mla_g_q': out['v_mla_g_q'], 'v_mla_g_kv': out['v_mla_g_kv'], 'v_mla_w_uq': out['v_mla_w_uq'], 'v_mla_w_ukv': out['v_mla_w_ukv'], 'v_mla_w_o': out['v_mla_w_o'], 'v_ffn_w_in': out['v_ffn_w_in'], 'v_ffn_conv_w': out['v_ffn_conv_w'], 'v_ffn_conv_b': out['v_ffn_conv_b'], 'v_ffn_w_out': out['v_ffn_w_out']}


def _loss(weights, diff, rest, loss_target):
    with _jax.named_scope("forward"):
        args = {**rest, TWIN_DIFF_INPUT: diff, **{k: w.astype(_WEIGHT_DTYPES[k]) for k, w in weights.items()}}
        y = _forward(args)
    with _jax.named_scope("loss_head"):
        err = _jnp.square(y.astype(_jnp.float32) - loss_target)
        return 0.5 * _jnp.sum(_jnp.mean(err, axis=-1)) if err.ndim else 0.5 * err


def _adamw(w, g, m, v):
    m = ADAM_B1 * m + (1.0 - ADAM_B1) * g
    v = ADAM_B2 * v + (1.0 - ADAM_B2) * _jnp.square(g)
    m_hat = m / (1.0 - ADAM_B1 ** ADAM_STEP)
    v_hat = v / (1.0 - ADAM_B2 ** ADAM_STEP)
    delta = -ADAM_LR * (m_hat / (_jnp.sqrt(v_hat) + ADAM_EPS) + ADAM_WD * w)
    return delta, m, v


def reference(x, meta_tokens, ln1_g, ln1_b, ln2_g, ln2_b, fox_w_in, fox_b_f, fox_w_o, swa_w_in, swa_sinks, swa_w_o, mla_w_a, mla_g_q, mla_g_kv, mla_w_uq, mla_w_ukv, mla_w_o, ffn_w_in, ffn_conv_w, ffn_conv_b, ffn_w_out, loss_target, m_meta_tokens, m_ln1_g, m_ln1_b, m_ln2_g, m_ln2_b, m_fox_w_in, m_fox_b_f, m_fox_w_o, m_swa_w_in, m_swa_sinks, m_swa_w_o, m_mla_w_a, m_mla_g_q, m_mla_g_kv, m_mla_w_uq, m_mla_w_ukv, m_mla_w_o, m_ffn_w_in, m_ffn_conv_w, m_ffn_conv_b, m_ffn_w_out, v_meta_tokens, v_ln1_g, v_ln1_b, v_ln2_g, v_ln2_b, v_fox_w_in, v_fox_b_f, v_fox_w_o, v_swa_w_in, v_swa_sinks, v_swa_w_o, v_mla_w_a, v_mla_g_q, v_mla_g_kv, v_mla_w_uq, v_mla_w_ukv, v_mla_w_o, v_ffn_w_in, v_ffn_conv_w, v_ffn_conv_b, v_ffn_w_out):
    given = dict(x=x, meta_tokens=meta_tokens, ln1_g=ln1_g, ln1_b=ln1_b, ln2_g=ln2_g, ln2_b=ln2_b, fox_w_in=fox_w_in, fox_b_f=fox_b_f, fox_w_o=fox_w_o, swa_w_in=swa_w_in, swa_sinks=swa_sinks, swa_w_o=swa_w_o, mla_w_a=mla_w_a, mla_g_q=mla_g_q, mla_g_kv=mla_g_kv, mla_w_uq=mla_w_uq, mla_w_ukv=mla_w_ukv, mla_w_o=mla_w_o, ffn_w_in=ffn_w_in, ffn_conv_w=ffn_conv_w, ffn_conv_b=ffn_conv_b, ffn_w_out=ffn_w_out, loss_target=loss_target, m_meta_tokens=m_meta_tokens, m_ln1_g=m_ln1_g, m_ln1_b=m_ln1_b, m_ln2_g=m_ln2_g, m_ln2_b=m_ln2_b, m_fox_w_in=m_fox_w_in, m_fox_b_f=m_fox_b_f, m_fox_w_o=m_fox_w_o, m_swa_w_in=m_swa_w_in, m_swa_sinks=m_swa_sinks, m_swa_w_o=m_swa_w_o, m_mla_w_a=m_mla_w_a, m_mla_g_q=m_mla_g_q, m_mla_g_kv=m_mla_g_kv, m_mla_w_uq=m_mla_w_uq, m_mla_w_ukv=m_mla_w_ukv, m_mla_w_o=m_mla_w_o, m_ffn_w_in=m_ffn_w_in, m_ffn_conv_w=m_ffn_conv_w, m_ffn_conv_b=m_ffn_conv_b, m_ffn_w_out=m_ffn_w_out, v_meta_tokens=v_meta_tokens, v_ln1_g=v_ln1_g, v_ln1_b=v_ln1_b, v_ln2_g=v_ln2_g, v_ln2_b=v_ln2_b, v_fox_w_in=v_fox_w_in, v_fox_b_f=v_fox_b_f, v_fox_w_o=v_fox_w_o, v_swa_w_in=v_swa_w_in, v_swa_sinks=v_swa_sinks, v_swa_w_o=v_swa_w_o, v_mla_w_a=v_mla_w_a, v_mla_g_q=v_mla_g_q, v_mla_g_kv=v_mla_g_kv, v_mla_w_uq=v_mla_w_uq, v_mla_w_ukv=v_mla_w_ukv, v_mla_w_o=v_mla_w_o, v_ffn_w_in=v_ffn_w_in, v_ffn_conv_w=v_ffn_conv_w, v_ffn_conv_b=v_ffn_conv_b, v_ffn_w_out=v_ffn_w_out)
    weights = {n: given[n] for n in TWIN_WEIGHTS}
    shared = {n: given[n] for n in SHARED_INPUTS}
    per_example = {n: given[n] for n in ['x']}
    grad_fn = _jax.value_and_grad(_loss, argnums=(0, 1))

    def one_microbatch(ex, loss_target):
        ex = dict(ex)
        diff = ex.pop(TWIN_DIFF_INPUT)
        return grad_fn(weights, diff, {**shared, **ex}, loss_target)

    if N_MICROBATCH == 1:
        loss, (grad_w, grad_x) = one_microbatch(per_example, given["loss_target"])
    else:
        def body(carry, xs):
            loss_sum, grad_sum = carry
            l_k, (gw_k, gx_k) = one_microbatch(xs[0], xs[1])
            with _jax.named_scope("update"):
                return (loss_sum + l_k, _jax.tree.map(_jnp.add, grad_sum, gw_k)), gx_k

        init = (_jnp.zeros((), _jnp.float32), _jax.tree.map(_jnp.zeros_like, weights))
        (loss, grad_w), grad_x = _jax.lax.scan(body, init, (per_example, given["loss_target"]))
    with _jax.named_scope("update"):
        delta_w, new_m, new_v = {}, {}, {}
        for n in TWIN_WEIGHTS:
            delta_w[n], new_m[n], new_v[n] = _adamw(weights[n], grad_w[n], given["m_" + n], given["v_" + n])
    return (loss, grad_x, *[grad_w[n] for n in TWIN_WEIGHTS], *[delta_w[n] for n in TWIN_WEIGHTS],
            *[new_m[n] for n in TWIN_WEIGHTS], *[new_v[n] for n in TWIN_WEIGHTS])
```

```python
import functools

import jax
import jax.numpy as jnp
import numpy as np
from jax import lax
from jax.experimental import pallas as pl
from jax.experimental.pallas import tpu as pltpu

F32 = jnp.float32
BF16 = jnp.bfloat16

D_MODEL = 1024
DEPTH = 4
N_META = 16
BLOCK = 128
PAD = BLOCK - N_META
NEG = -1e30
ALPHA = (2.0 * DEPTH) ** 0.25
LN_EPS = 1e-5
RMS_EPS = 1e-6
FOX_HEADS, FOX_HD = 16, 64
SWA_QH, SWA_KVH, SWA_HD = 16, 2, 64
SWA_G = SWA_QH // SWA_KVH
ROPE_THETA = 500000.0
ROPE_DIM = SWA_HD // 4
MLA_HEADS, MLA_QL, MLA_KVL, MLA_NOPE, MLA_ROPE, MLA_V = 16, 384, 256, 64, 32, 64
MLA_ROPE_THETA = 10000.0
D_FF = 2816
CONV_W = 3
FFN_TC = 256
ADAM_LR, ADAM_B1, ADAM_B2, ADAM_EPS, ADAM_WD, ADAM_STEP = 0.001, 0.9, 0.999, 1e-08, 0.01, 10

LANES = 128
SUBLANES_F32 = 8
SUBLANES_BF16 = 16
VMEM_LIMIT = 48 * 1024 * 1024
N_DEV = 8
MESH = pl.DeviceIdType.MESH
ROW_TILE = 640


def _tile(n, cap):
    if n <= cap:
        return n
    best = 0
    for t in range(LANES, cap + 1, LANES):
        if n % t == 0:
            best = t
    assert best, (n, cap)
    return best


def _params(*sem):
    return pltpu.CompilerParams(dimension_semantics=sem, vmem_limit_bytes=VMEM_LIMIT)


def _mm(a, b, *, ta=False, tb=False, out_dtype, name):
    (K, M) = a.shape if ta else a.shape[::-1]
    (N, K2) = b.shape if tb else b.shape[::-1]
    assert K == K2, (a.shape, b.shape, ta, tb)
    tm, tn, tk = _tile(M, 1024), _tile(N, 1536), _tile(K, 1536)
    nk = K // tk
    dn = (((0 if ta else 1,), (1 if tb else 0,)), ((), ()))

    def kern(a_ref, b_ref, o_ref, *acc):
        p = lax.dot_general(a_ref[...].astype(BF16), b_ref[...].astype(BF16), dn, preferred_element_type=F32)
        if nk == 1:
            o_ref[...] = p.astype(o_ref.dtype)
            return
        acc_ref, = acc
        k = pl.program_id(2)

        @pl.when(k == 0)
        def _():
            acc_ref[...] = p

        @pl.when(k > 0)
        def _():
            acc_ref[...] += p

        @pl.when(k == nk - 1)
        def _():
            o_ref[...] = acc_ref[...].astype(o_ref.dtype)

    a_spec = pl.BlockSpec((tk, tm), lambda i, j, k: (k, i)) if ta else pl.BlockSpec((tm, tk), lambda i, j, k: (i, k))
    b_spec = pl.BlockSpec((tn, tk), lambda i, j, k: (j, k)) if tb else pl.BlockSpec((tk, tn), lambda i, j, k: (k, j))
    return pl.pallas_call(
        kern, name=name, grid=(M // tm, N // tn, nk),
        in_specs=[a_spec, b_spec], out_specs=pl.BlockSpec((tm, tn), lambda i, j, k: (i, j)),
        out_shape=jax.ShapeDtypeStruct((M, N), out_dtype),
        scratch_shapes=[] if nk == 1 else [pltpu.VMEM((tm, tn), F32)],
        compiler_params=_params("parallel", "parallel", "arbitrary"),
    )(a, b)


def _linear(name, out_dtype):
    @jax.custom_vjp
    def f(x, w):
        return _mm(x, w, out_dtype=out_dtype, name=name + "_fwd")

    def fwd(x, w):
        return f(x, w), (x, w)

    def bwd(res, g):
        x, w = res
        dx = _mm(g, w, tb=True, out_dtype=x.dtype, name=name + "_dx")
        dw = _mm(x, g, ta=True, out_dtype=w.dtype, name=name + "_dw")
        return dx, dw

    f.defvjp(fwd, bwd)
    return f


def _ln_stats(z):
    mu = jnp.mean(z, axis=-1, keepdims=True)
    zc = z - mu
    var = jnp.mean(zc * zc, axis=-1, keepdims=True)
    return zc, lax.rsqrt(var + LN_EPS)


def _ln_fwd_call(h, mix, g, b, name):
    L, D = h.shape
    tm = _tile(L, ROW_TILE)

    def kern(h_ref, m_ref, g_ref, b_ref, o_ref):
        zc, rstd = _ln_stats(ALPHA * h_ref[...] + m_ref[...])
        o_ref[...] = zc * rstd * g_ref[...] + b_ref[...]

    row = pl.BlockSpec((tm, D), lambda i: (i, 0))
    vec = pl.BlockSpec((1, D), lambda i: (0, 0))
    return pl.pallas_call(kern, name=name, grid=(L // tm,), in_specs=[row, row, vec, vec], out_specs=row,
                          out_shape=jax.ShapeDtypeStruct((L, D), F32), compiler_params=_params("parallel"))(h, mix, g, b)


def _ln_bwd_call(h, mix, g, dout, name):
    L, D = h.shape
    tm = _tile(L, ROW_TILE)

    def kern(h_ref, m_ref, g_ref, d_ref, dz_ref, dg_ref, db_ref):
        i = pl.program_id(0)
        zc, rstd = _ln_stats(ALPHA * h_ref[...] + m_ref[...])
        xhat = zc * rstd
        d = d_ref[...]
        dxh = d * g_ref[...]
        m1 = jnp.mean(dxh, axis=-1, keepdims=True)
        m2 = jnp.mean(dxh * xhat, axis=-1, keepdims=True)
        row = i * tm + lax.broadcasted_iota(jnp.int32, (tm, 1), 0)
        dz_ref[...] = jnp.where(row >= PAD, rstd * (dxh - m1 - xhat * m2), 0.0)
        pg = jnp.sum(d * xhat, axis=0, keepdims=True)
        pb = jnp.sum(d, axis=0, keepdims=True)

        @pl.when(i == 0)
        def _():
            dg_ref[...] = pg
            db_ref[...] = pb

        @pl.when(i > 0)
        def _():
            dg_ref[...] += pg
            db_ref[...] += pb

    row = pl.BlockSpec((tm, D), lambda i: (i, 0))
    vec = pl.BlockSpec((1, D), lambda i: (0, 0))
    return pl.pallas_call(
        kern, name=name, grid=(L // tm,), in_specs=[row, row, vec, row], out_specs=[row, vec, vec],
        out_shape=[jax.ShapeDtypeStruct((L, D), F32), jax.ShapeDtypeStruct((1, D), F32), jax.ShapeDtypeStruct((1, D), F32)],
        compiler_params=_params("arbitrary"))(h, mix, g, dout)


def _deepnorm(name):
    @jax.custom_vjp
    def f(h, mix, g, b):
        return _ln_fwd_call(h, mix, g, b, name + "_fwd")

    def fwd(h, mix, g, b):
        return f(h, mix, g, b), (h, mix, g)

    def bwd(res, dout):
        h, mix, g = res
        dz, dg, db = _ln_bwd_call(h, mix, g, dout, name + "_bwd")
        return ALPHA * dz, dz, dg, db

    f.defvjp(fwd, bwd)
    return f


def _rms_fwd_call(x, g, name):
    L, n = x.shape
    tm = _tile(L, ROW_TILE)

    def kern(x_ref, g_ref, o_ref):
        x = x_ref[...]
        o_ref[...] = x * lax.rsqrt(jnp.mean(x * x, axis=-1, keepdims=True) + RMS_EPS) * g_ref[...]

    row = pl.BlockSpec((tm, n), lambda i: (i, 0))
    vec = pl.BlockSpec((1, n), lambda i: (0, 0))
    return pl.pallas_call(kern, name=name, grid=(L // tm,), in_specs=[row, vec], out_specs=row,
                          out_shape=jax.ShapeDtypeStruct((L, n), F32), compiler_params=_params("parallel"))(x, g)


def _rms_bwd_call(x, g, dout, name):
    L, n = x.shape
    tm = _tile(L, ROW_TILE)

    def kern(x_ref, g_ref, d_ref, dx_ref, dg_ref):
        i = pl.program_id(0)
        x = x_ref[...]
        rstd = lax.rsqrt(jnp.mean(x * x, axis=-1, keepdims=True) + RMS_EPS)
        xhat = x * rstd
        d = d_ref[...]
        dxh = d * g_ref[...]
        dx_ref[...] = rstd * (dxh - xhat * jnp.mean(dxh * xhat, axis=-1, keepdims=True))
        pg = jnp.sum(d * xhat, axis=0, keepdims=True)

        @pl.when(i == 0)
        def _():
            dg_ref[...] = pg

        @pl.when(i > 0)
        def _():
            dg_ref[...] += pg

    row = pl.BlockSpec((tm, n), lambda i: (i, 0))
    vec = pl.BlockSpec((1, n), lambda i: (0, 0))
    return pl.pallas_call(
        kern, name=name, grid=(L // tm,), in_specs=[row, vec, row], out_specs=[row, vec],
        out_shape=[jax.ShapeDtypeStruct((L, n), F32), jax.ShapeDtypeStruct((1, n), F32)],
        compiler_params=_params("arbitrary"))(x, g, dout)


def _rmsnorm(name):
    @jax.custom_vjp
    def f(x, g):
        return _rms_fwd_call(x, g, name + "_fwd")

    def fwd(x, g):
        return f(x, g), (x, g)

    def bwd(res, dout):
        x, g = res
        dx, dg = _rms_bwd_call(x, g, dout, name + "_bwd")
        return dx, dg

    f.defvjp(fwd, bwd)
    return f


def _rope_call(x, c, s1, s2, r, out_dtype, name):
    L, W = x.shape
    reps = W // LANES
    tm = _tile(L, ROW_TILE)

    def kern(x_ref, c_ref, s1_ref, s2_ref, o_ref):
        x = x_ref[...].astype(F32)
        wide = lambda t: jnp.tile(t[...], (1, reps)) if reps > 1 else t[...]
        out = x * wide(c_ref) + pltpu.roll(x, W - r, 1) * wide(s1_ref) + pltpu.roll(x, r, 1) * wide(s2_ref)
        o_ref[...] = out.astype(o_ref.dtype)

    row = pl.BlockSpec((tm, W), lambda i: (i, 0))
    tab = pl.BlockSpec((tm, LANES), lambda i: (i, 0))
    return pl.pallas_call(kern, name=name, grid=(L // tm,), in_specs=[row, tab, tab, tab], out_specs=row,
                          out_shape=jax.ShapeDtypeStruct((L, W), out_dtype), compiler_params=_params("parallel"))(x, c, s1, s2)


def _rope(name, r, out_dtype):
    @jax.custom_vjp
    def f(x, c, s1, s2):
        return _rope_call(x, c, s1, s2, r, out_dtype, name + "_fwd")

    def fwd(x, c, s1, s2):
        return f(x, c, s1, s2), (c, s1, s2, jnp.zeros((), x.dtype))

    def bwd(res, g):
        c, s1, s2, proto = res
        dx = _rope_call(g, c, -s1, -s2, r, proto.dtype, name + "_bwd")
        return dx, jnp.zeros_like(c), jnp.zeros_like(s1), jnp.zeros_like(s2)

    f.defvjp(fwd, bwd)
    return f


def _rope_tables(L, dim, theta):
    pos = (jnp.arange(L) - PAD).astype(F32)
    inv = theta ** (-jnp.arange(0, dim, 2, dtype=F32) / dim)
    ang = pos[:, None] * inv[None, :]
    return jnp.cos(ang), jnp.sin(ang)


def _rope_lanes(cos, sin, period):
    L, half = cos.shape
    one = jnp.ones((L, period - 2 * half), F32)
    zero = jnp.zeros((L, period - 2 * half), F32)
    z_h = jnp.zeros((L, half), F32)
    c = jnp.concatenate([cos, cos, one], axis=1)
    s1 = jnp.concatenate([-sin, z_h, zero], axis=1)
    s2 = jnp.concatenate([z_h, sin, zero], axis=1)
    reps = LANES // period
    return tuple(jnp.tile(t, (1, reps)) for t in (c, s1, s2))


HALO = SUBLANES_BF16


def _conv_taps(buf_ref, off, rows, w_ref, b_ref):
    w = w_ref[...]
    return (b_ref[...] + buf_ref[pl.ds(off - 2, rows), :] * w[0:1, :]
            + buf_ref[pl.ds(off - 1, rows), :] * w[1:2, :] + buf_ref[pl.ds(off, rows), :] * w[2:3, :])


def _glu_fwd_call(u, cw, cb, name):
    L, C2 = u.shape
    tc2 = 2 * FFN_TC
    nt = C2 // tc2
    tm = _tile(L, ROW_TILE)
    hb = tm // HALO

    def kern(u_ref, up_ref, w_ref, b_ref, o_ref, buf):
        i = pl.program_id(1)
        row = i * tm + lax.broadcasted_iota(jnp.int32, (tm, 1), 0)
        buf[pl.ds(HALO, tm), :] = jnp.where(row >= PAD, u_ref[...], 0.0)
        prow = i * tm - HALO + lax.broadcasted_iota(jnp.int32, (HALO, 1), 0)
        buf[pl.ds(0, HALO), :] = jnp.where(prow >= PAD, up_ref[...], 0.0)
        y = _conv_taps(buf, HALO, tm, w_ref, b_ref)
        gate, val = y[:, :FFN_TC], y[:, FFN_TC:]
        o_ref[...] = (gate * jax.nn.sigmoid(gate) * val).astype(o_ref.dtype)

    return pl.pallas_call(
        kern, name=name, grid=(nt, L // tm),
        in_specs=[pl.BlockSpec((tm, tc2), lambda j, i: (i, j)),
                  pl.BlockSpec((HALO, tc2), lambda j, i: (jnp.maximum(i * hb - 1, 0), j)),
                  pl.BlockSpec((CONV_W, tc2), lambda j, i: (0, j)),
                  pl.BlockSpec((1, tc2), lambda j, i: (0, j))],
        out_specs=pl.BlockSpec((tm, FFN_TC), lambda j, i: (i, j)),
        out_shape=jax.ShapeDtypeStruct((L, C2 // 2), BF16),
        scratch_shapes=[pltpu.VMEM((tm + HALO, tc2), F32)],
        compiler_params=_params("parallel", "parallel"))(u, u, cw, cb)


def _glu_bwd_call(u, cw, cb, dact, name):
    L, C2 = u.shape
    tc2 = 2 * FFN_TC
    nt = C2 // tc2
    tm = _tile(L, ROW_TILE)
    hb = tm // HALO
    nrow = L // tm
    ext = tm + HALO

    def kern(u_ref, up_ref, un_ref, w_ref, b_ref, d_ref, dn_ref, du_ref, dw_ref, db_ref, ubuf, dybuf):
        i = pl.program_id(1)
        r0 = i * tm
        mask = lambda blk, start: jnp.where(
            (start + lax.broadcasted_iota(jnp.int32, (blk.shape[0], 1), 0) >= PAD), blk, 0.0)
        ubuf[pl.ds(0, HALO), :] = mask(up_ref[...], r0 - HALO)
        ubuf[pl.ds(HALO, tm), :] = mask(u_ref[...], r0)
        ubuf[pl.ds(HALO + tm, HALO), :] = un_ref[...]
        y = _conv_taps(ubuf, HALO, ext, w_ref, b_ref)
        gate, val = y[:, :FFN_TC], y[:, FFN_TC:]
        sg = jax.nn.sigmoid(gate)
        d = jnp.concatenate([d_ref[...], dn_ref[...]], axis=0).astype(F32)
        erow = r0 + lax.broadcasted_iota(jnp.int32, (ext, 1), 0)
        d = jnp.where(erow < L, d, 0.0)
        dgate = d * val * (sg * (1.0 + gate * (1.0 - sg)))
        dval = d * (gate * sg)
        dybuf[...] = jnp.concatenate([dgate, dval], axis=1)
        w = w_ref[...]
        du = (dybuf[pl.ds(0, tm), :] * w[2:3, :] + dybuf[pl.ds(1, tm), :] * w[1:2, :]
              + dybuf[pl.ds(2, tm), :] * w[0:1, :])
        row = r0 + lax.broadcasted_iota(jnp.int32, (tm, 1), 0)
        du_ref[...] = jnp.where(row >= PAD, du, 0.0).astype(du_ref.dtype)
        dy = dybuf[pl.ds(0, tm), :]
        pw = jnp.concatenate([jnp.sum(dy * ubuf[pl.ds(HALO - 2 + t, tm), :], axis=0, keepdims=True)
                              for t in range(CONV_W)], axis=0)
        pb = jnp.sum(dy, axis=0, keepdims=True)

        @pl.when(i == 0)
        def _():
            dw_ref[...] = pw
            db_ref[...] = pb

        @pl.when(i > 0)
        def _():
            dw_ref[...] += pw
            db_ref[...] += pb

    last_halo = L // HALO - 1
    return pl.pallas_call(
        kern, name=name, grid=(nt, nrow),
        in_specs=[pl.BlockSpec((tm, tc2), lambda j, i: (i, j)),
                  pl.BlockSpec((HALO, tc2), lambda j, i: (jnp.maximum(i * hb - 1, 0), j)),
                  pl.BlockSpec((HALO, tc2), lambda j, i: (jnp.minimum((i + 1) * hb, last_halo), j)),
                  pl.BlockSpec((CONV_W, tc2), lambda j, i: (0, j)),
                  pl.BlockSpec((1, tc2), lambda j, i: (0, j)),
                  pl.BlockSpec((tm, FFN_TC), lambda j, i: (i, j)),
                  pl.BlockSpec((HALO, FFN_TC), lambda j, i: (jnp.minimum((i + 1) * hb, last_halo), j))],
        out_specs=[pl.BlockSpec((tm, tc2), lambda j, i: (i, j)),
                   pl.BlockSpec((CONV_W, tc2), lambda j, i: (0, j)),
                   pl.BlockSpec((1, tc2), lambda j, i: (0, j))],
        out_shape=[jax.ShapeDtypeStruct((L, C2), BF16), jax.ShapeDtypeStruct((CONV_W, C2), F32),
                   jax.ShapeDtypeStruct((1, C2), F32)],
        scratch_shapes=[pltpu.VMEM((tm + 2 * HALO, tc2), F32), pltpu.VMEM((ext, tc2), F32)],
        compiler_params=_params("parallel", "arbitrary"))(u, u, u, cw, cb, dact, dact)


def _ffn_up(name):
    @jax.custom_vjp
    def f(h1, w_in, cw, cb):
        u = _mm(h1, w_in, out_dtype=F32, name=name + "_up")
        return _glu_fwd_call(u, cw, cb, name + "_glu")

    def fwd(h1, w_in, cw, cb):
        u = _mm(h1, w_in, out_dtype=F32, name=name + "_up")
        return _glu_fwd_call(u, cw, cb, name + "_glu"), (h1, w_in, cw, cb, u)

    def bwd(res, dact):
        h1, w_in, cw, cb, u = res
        du, dcw, dcb = _glu_bwd_call(u, cw, cb, dact, name + "_glu_bwd")
        dh1 = _mm(du, w_in, tb=True, out_dtype=F32, name=name + "_up_dx")
        dw = _mm(h1, du, ta=True, out_dtype=w_in.dtype, name=name + "_up_dw")
        return dh1, dw, dcw, dcb

    f.defvjp(fwd, bwd)
    return f


def _pairs(n, kv_major):
    if kv_major:
        pr = [(i, j) for j in range(n) for i in range(j, n)]
    else:
        pr = [(i, j) for i in range(n) for j in range(i + 1)]
    return (jnp.asarray(np.array([p[0] for p in pr], np.int32)), jnp.asarray(np.array([p[1] for p in pr], np.int32)))


def _scores(q, k, cq, ck, scale, i, j, tq, tk, masked):
    s = lax.dot_general(q, k, (((1,), (1,)), ((), ())), preferred_element_type=F32) * scale
    if cq is not None:
        s = s + cq - ck
    if not masked:
        return s, None
    row = i * tq + lax.broadcasted_iota(jnp.int32, (tq, tk), 0)
    col = j * tk + lax.broadcasted_iota(jnp.int32, (tq, tk), 1)
    mask = (col <= row) & (col >= PAD)
    return jnp.where(mask, s, NEG), mask


def _flash_fwd_call(q, k, v, c, scale, name):
    H, L, dk = q.shape
    dv = v.shape[-1]
    tq = tk = _tile(L, ROW_TILE)
    nq = L // tq
    it, jt = _pairs(nq, False)
    has_c = c is not None

    def kern(it_ref, jt_ref, q_ref, k_ref, v_ref, *rest):
        if has_c:
            cq_ref, ck_ref, o_ref, lse_ref, m_s, l_s, acc_s = rest
        else:
            o_ref, lse_ref, m_s, l_s, acc_s = rest
        t = pl.program_id(1)
        i, j = it_ref[t], jt_ref[t]

        @pl.when(j == 0)
        def _():
            m_s[...] = jnp.full_like(m_s, NEG)
            l_s[...] = jnp.zeros_like(l_s)
            acc_s[...] = jnp.zeros_like(acc_s)

        def step(masked):
            s, mask = _scores(q_ref[...], k_ref[...], cq_ref[...] if has_c else None, ck_ref[...] if has_c else None,
                              scale, i, j, tq, tk, masked)
            m_prev = m_s[...]
            m_new = jnp.maximum(m_prev, jnp.max(s, axis=-1, keepdims=True))
            a = jnp.exp(m_prev - m_new)
            p = jnp.exp(s - m_new)
            if masked:
                p = jnp.where(mask, p, 0.0)
            l_s[...] = a * l_s[...] + jnp.sum(p, axis=-1, keepdims=True)
            acc_s[...] = a * acc_s[...] + jnp.dot(p.astype(BF16), v_ref[...], preferred_element_type=F32)
            m_s[...] = m_new

        edge = (j == i) | (j == 0)
        pl.when(edge)(lambda: step(True))
        pl.when(jnp.logical_not(edge))(lambda: step(False))

        @pl.when(j == i)
        def _():
            l = l_s[...]
            l = jnp.where(l == 0.0, 1.0, l)
            o_ref[...] = (acc_s[...] / l).astype(o_ref.dtype)
            lse_ref[...] = m_s[...] + jnp.log(l)

    qmap = lambda h, t, it, jt: (h, it[t], 0)
    kmap = lambda h, t, it, jt: (h, jt[t], 0)
    in_specs = [pl.BlockSpec((None, tq, dk), qmap), pl.BlockSpec((None, tk, dk), kmap), pl.BlockSpec((None, tk, dv), kmap)]
    args = [q, k, v]
    if has_c:
        in_specs += [pl.BlockSpec((None, tq, 1), qmap), pl.BlockSpec((None, 1, tk), lambda h, t, it, jt: (h, 0, jt[t]))]
        args += [c[:, :, None], c[:, None, :]]
    return pl.pallas_call(
        kern, name=name,
        grid_spec=pltpu.PrefetchScalarGridSpec(
            num_scalar_prefetch=2, grid=(H, int(it.shape[0])), in_specs=in_specs,
            out_specs=[pl.BlockSpec((None, tq, dv), qmap), pl.BlockSpec((None, tq, 1), qmap)],
            scratch_shapes=[pltpu.VMEM((tq, 1), F32), pltpu.VMEM((tq, 1), F32), pltpu.VMEM((tq, dv), F32)]),
        out_shape=[jax.ShapeDtypeStruct((H, L, dv), F32), jax.ShapeDtypeStruct((H, L, 1), F32)],
        compiler_params=_params("parallel", "arbitrary"))(it, jt, *args)


def _flash_bwd_call(q, k, v, c, o, lse, do, scale, name):
    H, L, dk = q.shape
    dv = v.shape[-1]
    tq = tk = _tile(L, ROW_TILE)
    nq = L // tq
    it, jt = _pairs(nq, True)
    has_c = c is not None

    def kern(it_ref, jt_ref, q_ref, k_ref, v_ref, o_ref, do_ref, lse_ref, *rest):
        if has_c:
            cq_ref, ck_ref, dq_ref, dk_ref, dv_ref, dc_ref, dcq_ref, dk_s, dv_s, dc_s = rest
        else:
            dq_ref, dk_ref, dv_ref, dk_s, dv_s = rest
        t = pl.program_id(1)
        i, j = it_ref[t], jt_ref[t]

        @pl.when(t == 0)
        def _():
            dq_ref[...] = jnp.zeros_like(dq_ref)
            if has_c:
                dcq_ref[...] = jnp.zeros_like(dcq_ref)

        @pl.when(i == j)
        def _():
            dk_s[...] = jnp.zeros_like(dk_s)
            dv_s[...] = jnp.zeros_like(dv_s)
            if has_c:
                dc_s[...] = jnp.zeros_like(dc_s)

        def step(masked):
            qb, kb = q_ref[...], k_ref[...]
            s, mask = _scores(qb, kb, cq_ref[...] if has_c else None, ck_ref[...] if has_c else None,
                              scale, i, j, tq, tk, masked)
            p = jnp.exp(s - lse_ref[...])
            if masked:
                p = jnp.where(mask, p, 0.0)
            dof = do_ref[...]
            dob = dof.astype(BF16)
            delta = jnp.sum(dof * o_ref[...], axis=-1, keepdims=True)
            dp = lax.dot_general(dob, v_ref[...], (((1,), (1,)), ((), ())), preferred_element_type=F32)
            ds = p * (dp - delta)
            dsb = ds.astype(BF16)
            dv_s[...] += lax.dot_general(p.astype(BF16), dob, (((0,), (0,)), ((), ())), preferred_element_type=F32)
            dk_s[...] += lax.dot_general(dsb, qb, (((0,), (0,)), ((), ())), preferred_element_type=F32)
            rows = pl.ds(pl.multiple_of(i * tq, tq), tq)
            dq_ref[rows, :] += jnp.dot(dsb, kb, preferred_element_type=F32) * scale
            if has_c:
                dc_s[...] += jnp.sum(ds, axis=0, keepdims=True)
                dcq_ref[rows, :] += jnp.sum(ds, axis=1, keepdims=True)

        edge = (j == i) | (j == 0)
        pl.when(edge)(lambda: step(True))
        pl.when(jnp.logical_not(edge))(lambda: step(False))

        @pl.when(i == nq - 1)
        def _():
            dk_ref[...] = (dk_s[...] * scale).astype(dk_ref.dtype)
            dv_ref[...] = dv_s[...].astype(dv_ref.dtype)
            if has_c:
                dc_ref[...] = -dc_s[...]

    qmap = lambda h, t, it, jt: (h, it[t], 0)
    kmap = lambda h, t, it, jt: (h, jt[t], 0)
    rowmap = lambda h, t, it, jt: (h, 0, jt[t])
    in_specs = [pl.BlockSpec((None, tq, dk), qmap), pl.BlockSpec((None, tk, dk), kmap), pl.BlockSpec((None, tk, dv), kmap),
                pl.BlockSpec((None, tq, dv), qmap), pl.BlockSpec((None, tq, dv), qmap), pl.BlockSpec((None, tq, 1), qmap)]
    args = [q, k, v, o, do, lse]
    out_specs = [pl.BlockSpec((None, L, dk), lambda h, t, it, jt: (h, 0, 0)),
                 pl.BlockSpec((None, tk, dk), kmap), pl.BlockSpec((None, tk, dv), kmap)]
    out_shape = [jax.ShapeDtypeStruct((H, L, dk), F32), jax.ShapeDtypeStruct((H, L, dk), BF16),
                 jax.ShapeDtypeStruct((H, L, dv), BF16)]
    scratch = [pltpu.VMEM((tk, dk), F32), pltpu.VMEM((tk, dv), F32)]
    if has_c:
        in_specs += [pl.BlockSpec((None, tq, 1), qmap), pl.BlockSpec((None, 1, tk), rowmap)]
        args += [c[:, :, None], c[:, None, :]]
        out_specs += [pl.BlockSpec((None, 1, tk), rowmap), pl.BlockSpec((None, L, 1), lambda h, t, it, jt: (h, 0, 0))]
        out_shape += [jax.ShapeDtypeStruct((H, 1, L), F32), jax.ShapeDtypeStruct((H, L, 1), F32)]
        scratch.append(pltpu.VMEM((1, tk), F32))
    return pl.pallas_call(
        kern, name=name,
        grid_spec=pltpu.PrefetchScalarGridSpec(
            num_scalar_prefetch=2, grid=(H, int(it.shape[0])), in_specs=in_specs, out_specs=out_specs,
            scratch_shapes=scratch),
        out_shape=out_shape, compiler_params=_params("parallel", "arbitrary"))(it, jt, *args)


def _flash(name, scale, with_decay):
    def run_fwd(q, k, v, c):
        return _flash_fwd_call(q, k, v, c, scale, name + "_fwd")

    def run_bwd(res, do):
        q, k, v, c, o, lse = res
        return _flash_bwd_call(q, k, v, c, o, lse, do, scale, name + "_bwd")

    if with_decay:
        @jax.custom_vjp
        def f(q, k, v, c):
            return run_fwd(q, k, v, c)[0]

        def fwd(q, k, v, c):
            o, lse = run_fwd(q, k, v, c)
            return o, (q, k, v, c, o, lse)

        def bwd(res, do):
            dq, dk, dv, dck, dcq = run_bwd(res, do)
            return dq.astype(BF16), dk, dv, dcq[:, :, 0] + dck[:, 0, :]
    else:
        @jax.custom_vjp
        def f(q, k, v):
            return run_fwd(q, k, v, None)[0]

        def fwd(q, k, v):
            o, lse = run_fwd(q, k, v, None)
            return o, (q, k, v, None, o, lse)

        def bwd(res, do):
            dq, dk, dv = run_bwd(res, do)
            return dq.astype(BF16), dk, dv

    f.defvjp(fwd, bwd)
    return f


def _swa_parts(q_ref, kc_ref, kp_ref, km_ref, sink_ref, i, scale):
    R = SWA_G * BLOCK
    q = q_ref[...].reshape(R, SWA_HD)
    nt = (((1,), (1,)), ((), ()))
    r = lax.broadcasted_iota(jnp.int32, (R, BLOCK), 0) & (BLOCK - 1)
    c = lax.broadcasted_iota(jnp.int32, (R, BLOCK), 1)
    s_c = jnp.where((c <= r) & (i >= 1), lax.dot_general(q, kc_ref[...], nt, preferred_element_type=F32) * scale, NEG)
    s_p = jnp.where((c > r) & (i >= 2), lax.dot_general(q, kp_ref[...], nt, preferred_element_type=F32) * scale, NEG)
    s_m = jnp.where((c >= PAD) & ((i >= 1) | (c <= r)),
                    lax.dot_general(q, km_ref[...], nt, preferred_element_type=F32) * scale, NEG)
    sink = jnp.broadcast_to(sink_ref[...], (SWA_G, BLOCK, 1)).reshape(R, 1)
    return q, s_c, s_p, s_m, sink


def _swa_specs(L):
    nb = L // BLOCK
    qs = pl.BlockSpec((None, SWA_G, BLOCK, SWA_HD), lambda g, i: (g, 0, i, 0))
    kc = pl.BlockSpec((None, BLOCK, SWA_HD), lambda g, i: (g, i, 0))
    kp = pl.BlockSpec((None, BLOCK, SWA_HD), lambda g, i: (g, jnp.maximum(i - 1, 0), 0))
    km = pl.BlockSpec((None, BLOCK, SWA_HD), lambda g, i: (g, 0, 0))
    sk = pl.BlockSpec((None, SWA_G, 1, 1), lambda g, i: (g, 0, 0, 0))
    ls = pl.BlockSpec((None, SWA_G, BLOCK, 1), lambda g, i: (g, 0, i, 0))
    return nb, qs, kc, kp, km, sk, ls


def _swa_fwd_call(q, k, v, sinks, scale, name):
    _, _, L, _ = q.shape
    nb, qs, kc, kp, km, sk, ls = _swa_specs(L)
    R = SWA_G * BLOCK

    def kern(q_ref, kc_ref, kp_ref, km_ref, vc_ref, vp_ref, vm_ref, sink_ref, o_ref, lse_ref):
        i = pl.program_id(1)
        _, s_c, s_p, s_m, sink = _swa_parts(q_ref, kc_ref, kp_ref, km_ref, sink_ref, i, scale)
        mx = lambda s: jnp.max(s, axis=-1, keepdims=True)
        m = jnp.maximum(jnp.maximum(mx(s_c), mx(s_p)), jnp.maximum(mx(s_m), sink))
        p_c, p_p, p_m = jnp.exp(s_c - m), jnp.exp(s_p - m), jnp.exp(s_m - m)
        sm = lambda p: jnp.sum(p, axis=-1, keepdims=True)
        den = sm(p_c) + sm(p_p) + sm(p_m) + jnp.exp(sink - m)
        inv = 1.0 / den
        pv = lambda p, v_ref: jnp.dot((p * inv).astype(BF16), v_ref[...], preferred_element_type=F32)
        o = pv(p_c, vc_ref) + pv(p_p, vp_ref) + pv(p_m, vm_ref)
        o_ref[...] = o.reshape(SWA_G, BLOCK, SWA_HD)
        lse_ref[...] = (m + jnp.log(den)).reshape(SWA_G, BLOCK, 1)

    return pl.pallas_call(
        kern, name=name, grid=(SWA_KVH, nb), in_specs=[qs, kc, kp, km, kc, kp, km, sk], out_specs=[qs, ls],
        out_shape=[jax.ShapeDtypeStruct((SWA_KVH, SWA_G, L, SWA_HD), F32), jax.ShapeDtypeStruct((SWA_KVH, SWA_G, L, 1), F32)],
        compiler_params=_params("parallel", "parallel"))(q, k, k, k, v, v, v, sinks)


def _swa_bwd_call(q, k, v, sinks, o, lse, do, scale, name):
    _, _, L, _ = q.shape
    nb, qs, kc, kp, km, sk, ls = _swa_specs(L)
    R = SWA_G * BLOCK
    full = pl.BlockSpec((None, L, SWA_HD), lambda g, i: (g, 0, 0))

    def kern(q_ref, kc_ref, kp_ref, km_ref, vc_ref, vp_ref, vm_ref, sink_ref, o_ref, do_ref, lse_ref,
             dq_ref, dk_ref, dv_ref, dsink_ref):
        i = pl.program_id(1)

        @pl.when(i == 0)
        def _():
            dk_ref[...] = jnp.zeros_like(dk_ref)
            dv_ref[...] = jnp.zeros_like(dv_ref)
            dsink_ref[...] = jnp.zeros_like(dsink_ref)

        q, s_c, s_p, s_m, sink = _swa_parts(q_ref, kc_ref, kp_ref, km_ref, sink_ref, i, scale)
        lse = lse_ref[...].reshape(R, 1)
        dof = do_ref[...].reshape(R, SWA_HD)
        dob = dof.astype(BF16)
        delta = jnp.sum(dof * o_ref[...].reshape(R, SWA_HD), axis=-1, keepdims=True)
        nt = (((1,), (1,)), ((), ()))
        tn = (((0,), (0,)), ((), ()))
        cur = pl.ds(pl.multiple_of(i * BLOCK, BLOCK), BLOCK)
        prev = pl.ds(pl.multiple_of(jnp.maximum(i - 1, 0) * BLOCK, BLOCK), BLOCK)
        meta = pl.ds(0, BLOCK)
        dq = jnp.zeros((R, SWA_HD), F32)
        for s, k_ref, v_ref, rows in ((s_c, kc_ref, vc_ref, cur), (s_p, kp_ref, vp_ref, prev), (s_m, km_ref, vm_ref, meta)):
            p = jnp.exp(s - lse)
            dp = lax.dot_general(dob, v_ref[...], nt, preferred_element_type=F32)
            ds = (p * (dp - delta)).astype(BF16)
            dq = dq + jnp.dot(ds, k_ref[...], preferred_element_type=F32)
            dv_ref[rows, :] += lax.dot_general(p.astype(BF16), dob, tn, preferred_element_type=F32)
            dk_ref[rows, :] += lax.dot_general(ds, q, tn, preferred_element_type=F32) * scale
        dq_ref[...] = (dq * scale).reshape(SWA_G, BLOCK, SWA_HD).astype(dq_ref.dtype)
        dsk = -jnp.exp(sink - lse) * delta
        dsink_ref[...] += jnp.sum(dsk.reshape(SWA_G, BLOCK, 1), axis=1, keepdims=True)

    return pl.pallas_call(
        kern, name=name, grid=(SWA_KVH, nb), in_specs=[qs, kc, kp, km, kc, kp, km, sk, qs, qs, ls],
        out_specs=[qs, full, full, sk],
        out_shape=[jax.ShapeDtypeStruct((SWA_KVH, SWA_G, L, SWA_HD), BF16), jax.ShapeDtypeStruct((SWA_KVH, L, SWA_HD), F32),
                   jax.ShapeDtypeStruct((SWA_KVH, L, SWA_HD), F32), jax.ShapeDtypeStruct((SWA_KVH, SWA_G, 1, 1), F32)],
        compiler_params=_params("parallel", "arbitrary"))(q, k, k, k, v, v, v, sinks, o, do, lse)


def _swa_attn(name, scale):
    @jax.custom_vjp
    def f(q, k, v, sinks):
        return _swa_fwd_call(q, k, v, sinks, scale, name + "_fwd")[0]

    def fwd(q, k, v, sinks):
        o, lse = _swa_fwd_call(q, k, v, sinks, scale, name + "_fwd")
        return o, (q, k, v, sinks, o, lse)

    def bwd(res, do):
        q, k, v, sinks, o, lse = res
        dq, dk, dv, dsink = _swa_bwd_call(q, k, v, sinks, o, lse, do, scale, name + "_bwd")
        return dq, dk.astype(BF16), dv.astype(BF16), dsink

    f.defvjp(fwd, bwd)
    return f


def _scan_call(x, bias, mul, pre_logsig, name):
    L, W = x.shape
    tb = _tile(L, ROW_TILE)
    has_mul = mul is not None

    def kern(x_ref, b_ref, *rest):
        if has_mul:
            m_ref, o_ref, tot_ref, carry = rest
        else:
            o_ref, tot_ref, carry = rest
        i = pl.program_id(0)

        @pl.when(i == 0)
        def _():
            carry[...] = jnp.zeros_like(carry)
            tot_ref[...] = jnp.zeros_like(tot_ref)

        z = x_ref[...] + b_ref[...]
        if pre_logsig:
            z = jnp.minimum(z, 0.0) - jnp.log(1.0 + jnp.exp(-jnp.abs(z)))
        row = lax.broadcasted_iota(jnp.int32, (tb, W), 0)
        s = 1
        while s < tb:
            z = z + jnp.where(row >= s, pltpu.roll(z, s, 0), 0.0)
            s *= 2
        z = z + carry[...]
        carry[...] = z[tb - 1:tb, :]
        if has_mul:
            z = z * m_ref[...]
        o_ref[...] = z
        tot_ref[...] += jnp.sum(z, axis=0, keepdims=True)

    row = pl.BlockSpec((tb, W), lambda i: (i, 0))
    vec = pl.BlockSpec((1, W), lambda i: (0, 0))
    return pl.pallas_call(
        kern, name=name, grid=(L // tb,), in_specs=[row, vec] + ([row] if has_mul else []), out_specs=[row, vec],
        out_shape=[jax.ShapeDtypeStruct((L, W), F32), jax.ShapeDtypeStruct((1, W), F32)],
        scratch_shapes=[pltpu.VMEM((1, W), F32)],
        compiler_params=_params("arbitrary"))(*([x, bias] + ([mul] if has_mul else [])))


def _sigmoid_neg_call(x, bias, name):
    L, W = x.shape
    tb = _tile(L, ROW_TILE)

    def kern(x_ref, b_ref, o_ref):
        row = pl.program_id(0) * tb + lax.broadcasted_iota(jnp.int32, (tb, 1), 0)
        o_ref[...] = jnp.where(row >= PAD, jax.nn.sigmoid(-(x_ref[...] + b_ref[...])), 0.0)

    row = pl.BlockSpec((tb, W), lambda i: (i, 0))
    vec = pl.BlockSpec((1, W), lambda i: (0, 0))
    return pl.pallas_call(kern, name=name, grid=(L // tb,), in_specs=[row, vec], out_specs=row,
                          out_shape=jax.ShapeDtypeStruct((L, W), F32), compiler_params=_params("parallel"))(x, bias)


def _decay(name):
    @jax.custom_vjp
    def f(fg, b):
        return _scan_call(fg, b, None, True, name + "_fwd")[0]

    def fwd(fg, b):
        return f(fg, b), (fg, b)

    def bwd(res, dc):
        fg, b = res
        sg = _sigmoid_neg_call(fg, b, name + "_dsig")
        dfg_rev, db = _scan_call(dc[::-1], jnp.zeros_like(b), sg[::-1], False, name + "_bwd")
        return dfg_rev[::-1], db

    f.defvjp(fwd, bwd)
    return f


def _loss_call(hf, target, name):
    L, D = hf.shape
    nb = L // BLOCK

    def kern(h_ref, t_ref, loss_ref, dy_ref, acc):
        i = pl.program_id(0)

        @pl.when(i == 0)
        def _():
            acc[...] = jnp.zeros_like(acc)
            dy_ref[...] = jnp.zeros_like(dy_ref)

        @pl.when(i > 0)
        def _():
            e = h_ref[...] - t_ref[...]
            dy_ref[...] = e * (1.0 / D)
            acc[...] += jnp.sum(e * e, axis=0, keepdims=True)

        @pl.when(i == nb - 1)
        def _():
            loss_ref[...] = jnp.broadcast_to(jnp.sum(acc[...], axis=1, keepdims=True) * (0.5 / D), loss_ref.shape)

    return pl.pallas_call(
        kern, name=name, grid=(nb,),
        in_specs=[pl.BlockSpec((BLOCK, D), lambda i: (i, 0)), pl.BlockSpec((BLOCK, D), lambda i: (jnp.maximum(i - 1, 0), 0))],
        out_specs=[pl.BlockSpec((1, LANES), lambda i: (0, 0)), pl.BlockSpec((BLOCK, D), lambda i: (i, 0))],
        out_shape=[jax.ShapeDtypeStruct((1, LANES), F32), jax.ShapeDtypeStruct((L, D), F32)],
        scratch_shapes=[pltpu.VMEM((1, D), F32)],
        compiler_params=_params("arbitrary"))(hf, target)


def _loss_head(name):
    @jax.custom_vjp
    def f(hf, target):
        return _loss_call(hf, target, name)[0][0, 0]

    def fwd(hf, target):
        loss, dy = _loss_call(hf, target, name)
        return loss[0, 0], (dy, jnp.zeros((), F32))

    def bwd(res, g):
        dy, _ = res
        return dy * g, None

    f.defvjp(fwd, bwd)
    return f


def _me_and_peers():
    x, y, c = lax.axis_index("x"), lax.axis_index("y"), lax.axis_index("c")
    me = 4 * x + 2 * y + c
    peers = []
    for k in range(1, N_DEV):
        px = 1 - x if k & 4 else x
        py = 1 - y if k & 2 else y
        pc = 1 - c if k & 1 else c
        peers.append(((px, py, pc), 4 * px + 2 * py + pc))
    return me, peers


def _exchange(buf, gather, name):
    shape = buf.shape if gather else buf.shape[1:]

    def body(src_ref, out_ref, send_sems, recv_sems, local_sem):
        me, peers = _me_and_peers()
        piece = lambda p: src_ref if gather else src_ref.at[p]
        mine = pltpu.make_async_copy(piece(me), out_ref.at[me], local_sem)
        mine.start()
        sends = []
        for k, (dev, pid) in enumerate(peers):
            cp = pltpu.make_async_remote_copy(src_ref=piece(pid), dst_ref=out_ref.at[me], send_sem=send_sems.at[k],
                                              recv_sem=recv_sems.at[k], device_id=dev, device_id_type=MESH)
            cp.start()
            sends.append(cp)
        for k, (dev, pid) in enumerate(peers):
            pltpu.make_async_remote_copy(src_ref=piece(me), dst_ref=out_ref.at[pid], send_sem=send_sems.at[k],
                                         recv_sem=recv_sems.at[k], device_id=dev, device_id_type=MESH).wait_recv()
        for cp in sends:
            cp.wait_send()
        mine.wait()

    return pl.pallas_call(
        body, name=name, out_shape=jax.ShapeDtypeStruct((N_DEV,) + tuple(shape), buf.dtype),
        in_specs=[pl.BlockSpec(memory_space=pl.ANY)], out_specs=pl.BlockSpec(memory_space=pl.ANY),
        scratch_shapes=[pltpu.SemaphoreType.DMA((N_DEV - 1,)), pltpu.SemaphoreType.DMA((N_DEV - 1,)), pltpu.SemaphoreType.DMA],
    )(buf)


def _adamw_call(parts, w, m, v, name):
    R, C = w.shape
    tr = _tile(R, 256)
    c1 = 1.0 / (1.0 - ADAM_B1 ** ADAM_STEP)
    c2 = 1.0 / (1.0 - ADAM_B2 ** ADAM_STEP)

    def kern(p_ref, w_ref, m_ref, v_ref, g_ref, d_ref, nm_ref, nv_ref):
        g = p_ref[0].astype(F32)
        for p in range(1, N_DEV):
            g = g + p_ref[p].astype(F32)
        nm = ADAM_B1 * m_ref[...] + (1.0 - ADAM_B1) * g
        nv = ADAM_B2 * v_ref[...] + (1.0 - ADAM_B2) * (g * g)
        g_ref[...] = g
        nm_ref[...] = nm
        nv_ref[...] = nv
        d_ref[...] = -ADAM_LR * ((nm * c1) / (jnp.sqrt(nv * c2) + ADAM_EPS) + ADAM_WD * w_ref[...])

    slab = pl.BlockSpec((tr, C), lambda i: (i, 0))
    return pl.pallas_call(
        kern, name=name, grid=(R // tr,),
        in_specs=[pl.BlockSpec((N_DEV, tr, C), lambda i: (0, i, 0)), slab, slab, slab], out_specs=[slab] * 4,
        out_shape=[jax.ShapeDtypeStruct((R, C), F32)] * 4, compiler_params=_params("parallel"))(parts, w, m, v)


PACK_W = 1024


def _pack(arrs, dtype, row_mult):
    flat = jnp.concatenate([a.astype(dtype).reshape(-1) for a in arrs])
    n = flat.shape[0]
    unit = row_mult * PACK_W
    tot = -(-n // unit) * unit
    return jnp.pad(flat, (0, tot - n)).reshape(tot // PACK_W, PACK_W)


def _unpack(slab, shapes):
    flat = slab.reshape(-1)
    out, off = [], 0
    for s in shapes:
        n = int(np.prod(s))
        out.append(flat[off:off + n].reshape(s))
        off += n
    return out


def _to_full(stacked, axis):
    moved = jnp.moveaxis(stacked, 0, axis)
    s = list(moved.shape)
    return moved.reshape(s[:axis] + [s[axis] * s[axis + 1]] + s[axis + 2:])


def _to_shards(full, axis):
    s = list(full.shape)
    split = full.reshape(s[:axis] + [N_DEV, s[axis] // N_DEV] + s[axis + 1:])
    return jnp.moveaxis(split, axis, 0)


def _heads(x, h, d):
    return x.reshape(x.shape[0], h, d).transpose(1, 0, 2)


def _unheads(x):
    return x.transpose(1, 0, 2).reshape(x.shape[1], -1)


def _pad_cols(x, width):
    return jnp.pad(x, ((0, 0), (0, width - x.shape[1])))


def _fox_mixer(h, w_in, b_f, w_o, tag):
    hd = FOX_HEADS * FOX_HD
    qkv = _linear(tag + "_qkv", BF16)(h, w_in[:, :3 * hd])
    fg = _linear(tag + "_gate", F32)(h, _pad_cols(w_in[:, 3 * hd:], LANES))
    c = _decay(tag + "_decay")(fg, _pad_cols(b_f[None, :], LANES))
    q, k, v = (_heads(qkv[:, n * hd:(n + 1) * hd], FOX_HEADS, FOX_HD) for n in range(3))
    o = _flash(tag + "_attn", FOX_HD ** -0.5, True)(q, k, v, c[:, :FOX_HEADS].T)
    return _linear(tag + "_out", F32)(_unheads(o), w_o)


def _swa_mixer(h, w_in, sinks, w_o, tabs, tag):
    L = h.shape[0]
    qd, kd = SWA_QH * SWA_HD, SWA_KVH * SWA_HD
    proj = _linear(tag + "_qkv", F32)(h, w_in)
    rope = _rope(tag + "_rope", ROPE_DIM // 2, BF16)
    q = rope(proj[:, :qd], *tabs)
    k = rope(proj[:, qd:qd + kd], *tabs)
    v = proj[:, qd + kd:].astype(BF16)
    qg = _heads(q, SWA_QH, SWA_HD).reshape(SWA_KVH, SWA_G, L, SWA_HD)
    o = _swa_attn(tag + "_attn", SWA_HD ** -0.5)(qg, _heads(k, SWA_KVH, SWA_HD), _heads(v, SWA_KVH, SWA_HD),
                                                 sinks.reshape(SWA_KVH, SWA_G, 1, 1))
    return _linear(tag + "_out", F32)(_unheads(o.reshape(SWA_QH, L, SWA_HD)), w_o)


def _mla_mixer(h, w_a, g_q, g_kv, w_uq, w_ukv, w_o, tabs, tag):
    L = h.shape[0]
    cq = _linear(tag + "_aq", F32)(h, w_a[:, :MLA_QL])
    ckv = _linear(tag + "_akv", F32)(h, w_a[:, MLA_QL:MLA_QL + MLA_KVL])
    kr = _linear(tag + "_akr", F32)(h, _pad_cols(w_a[:, MLA_QL + MLA_KVL:], LANES))
    cq = _rmsnorm(tag + "_nq")(cq, g_q[None, :])
    ckv = _rmsnorm(tag + "_nkv")(ckv, g_kv[None, :])
    q = _linear(tag + "_uq", F32)(cq, w_uq).reshape(L, MLA_HEADS, MLA_NOPE + MLA_ROPE)
    kv = _linear(tag + "_ukv", BF16)(ckv, w_ukv).reshape(L, MLA_HEADS, MLA_NOPE + MLA_V)
    rope = _rope(tag + "_rope", MLA_ROPE // 2, BF16)
    q_rope = rope(q[:, :, MLA_NOPE:].reshape(L, MLA_HEADS * MLA_ROPE), *tabs).reshape(L, MLA_HEADS, MLA_ROPE)
    k_rope = rope(kr, *tabs)[:, :MLA_ROPE]
    qf = jnp.concatenate([q[:, :, :MLA_NOPE].astype(BF16), q_rope], axis=-1).transpose(1, 0, 2)
    kf = jnp.concatenate([kv[:, :, :MLA_NOPE], jnp.broadcast_to(k_rope[:, None, :], (L, MLA_HEADS, MLA_ROPE))],
                         axis=-1).transpose(1, 0, 2)
    v = kv[:, :, MLA_NOPE:].transpose(1, 0, 2)
    o = _flash(tag + "_attn", (MLA_NOPE + MLA_ROPE) ** -0.5, False)(qf, kf, v)
    return _linear(tag + "_out", F32)(_unheads(o), w_o)


def _ffn_perm(x):
    lead = x.shape[:-1]
    nt = D_FF // FFN_TC
    return jnp.swapaxes(x.reshape(lead + (2, nt, FFN_TC)), -3, -2).reshape(lead + (2 * D_FF,))


def _local_loss(wts, x, target):
    S, D = x.shape
    L = S + BLOCK
    h = jnp.concatenate([jnp.zeros((PAD, D), F32), wts["meta_tokens"], x], axis=0)
    cos_p, sin_p = _rope_tables(L, ROPE_DIM, ROPE_THETA)
    tabs_p = _rope_lanes(cos_p, sin_p, SWA_HD)
    cos_m, sin_m = _rope_tables(L, MLA_ROPE, MLA_ROPE_THETA)
    tabs_m = _rope_lanes(cos_m, sin_m, MLA_ROPE)
    for i in range(DEPTH):
        kind, j = i % 3, i // 3
        tag = f"l{i}"
        if kind == 0:
            mix = _fox_mixer(h, wts["fox_w_in"][j], wts["fox_b_f"][j], wts["fox_w_o"][j], tag + "_fox")
        elif kind == 1:
            mix = _swa_mixer(h, wts["swa_w_in"][j], wts["swa_sinks"][j], wts["swa_w_o"][j], tabs_p, tag + "_swa")
        else:
            mix = _mla_mixer(h, wts["mla_w_a"][j], wts["mla_g_q"][j], wts["mla_g_kv"][j], wts["mla_w_uq"][j],
                             wts["mla_w_ukv"][j], wts["mla_w_o"][j], tabs_m, tag + "_mla")
        h = _deepnorm(tag + "_ln1")(h, mix, wts["ln1_g"][i][None, :], wts["ln1_b"][i][None, :])
        act = _ffn_up(tag + "_ffn")(h, _ffn_perm(wts["ffn_w_in"][i]), _ffn_perm(wts["ffn_conv_w"][i]),
                                    _ffn_perm(wts["ffn_conv_b"][i])[None, :])
        ffn = _linear(tag + "_ffn_down", F32)(act, wts["ffn_w_out"][i])
        h = _deepnorm(tag + "_ln2")(h, ffn, wts["ln2_g"][i][None, :], wts["ln2_b"][i][None, :])
    return _loss_head("loss_head")(h, target)


BIG = (("fox_w_in", 2), ("fox_w_o", 1), ("swa_w_in", 2), ("swa_w_o", 1), ("mla_w_a", 1), ("mla_w_uq", 2),
       ("mla_w_ukv", 2), ("mla_w_o", 1), ("ffn_w_in", 2), ("ffn_w_out", 1))
SMALL = (("meta_tokens", 1), ("mla_g_q", 1), ("mla_g_kv", 1), ("ffn_conv_w", 2))
REPL = ("ln1_g", "ln1_b", "ln2_g", "ln2_b", "fox_b_f", "swa_sinks", "ffn_conv_b")
WEIGHTS = ("meta_tokens", "ln1_g", "ln1_b", "ln2_g", "ln2_b", "fox_w_in", "fox_b_f", "fox_w_o", "swa_w_in", "swa_sinks",
           "swa_w_o", "mla_w_a", "mla_g_q", "mla_g_kv", "mla_w_uq", "mla_w_ukv", "mla_w_o", "ffn_w_in", "ffn_conv_w",
           "ffn_conv_b", "ffn_w_out")


def _pack_stacked(arrs, dtype, row_mult):
    flat = jnp.concatenate([a.astype(dtype).reshape(N_DEV, -1) for a in arrs], axis=1)
    n = flat.shape[1]
    unit = row_mult * PACK_W
    tot = -(-n // unit) * unit
    return jnp.pad(flat, ((0, 0), (0, tot - n))).reshape(N_DEV, tot // PACK_W, PACK_W)


def _unpack_stacked(slab, shapes):
    flat = slab.reshape(N_DEV, -1)
    out, off = [], 0
    for s in shapes:
        n = int(np.prod(s))
        out.append(flat[:, off:off + n].reshape((N_DEV,) + tuple(s)))
        off += n
    return out


ADAM_ROWS = 256


def _step(x, target, w, m, v):
    big = _exchange(_pack([w[n] for n, _ in BIG], BF16, SUBLANES_BF16), True, "gather_big")
    small = _exchange(_pack([w[n] for n, _ in SMALL], F32, SUBLANES_F32), True, "gather_small")
    full = {n: w[n] for n in REPL}
    for names, slab in ((BIG, big), (SMALL, small)):
        stacked = _unpack_stacked(slab, [w[n].shape for n, _ in names])
        for (n, ax), a in zip(names, stacked):
            full[n] = _to_full(a, ax)
    loss, (grads, grad_x) = jax.value_and_grad(_local_loss, argnums=(0, 1))(full, x, target)
    sharded = BIG + SMALL
    send = _pack_stacked([_to_shards(grads[n], ax) for n, ax in sharded], BF16, ADAM_ROWS)
    got = _exchange(send, False, "scatter_grads")
    repl = _exchange(_pack([grads[n] for n in REPL], F32, SUBLANES_F32), True, "gather_repl_grads")
    out = {}
    for names, parts, mult, tag in (([n for n, _ in sharded], got, ADAM_ROWS, "adamw_sharded"),
                                    (list(REPL), repl, SUBLANES_F32, "adamw_repl")):
        slabs = [_pack([d[n] for n in names], F32, mult) for d in (w, m, v)]
        res = _adamw_call(parts, *slabs, tag)
        for kind, slab in zip(("grad", "delta", "new_m", "new_v"), res):
            for n, a in zip(names, _unpack(slab, [w[n].shape for n in names])):
                out[(kind, n)] = a
    return loss, grad_x, out


def kernel(x, meta_tokens, ln1_g, ln1_b, ln2_g, ln2_b, fox_w_in, fox_b_f, fox_w_o, swa_w_in, swa_sinks, swa_w_o, mla_w_a, mla_g_q, mla_g_kv, mla_w_uq, mla_w_ukv, mla_w_o, ffn_w_in, ffn_conv_w, ffn_conv_b, ffn_w_out, loss_target, m_meta_tokens, m_ln1_g, m_ln1_b, m_ln2_g, m_ln2_b, m_fox_w_in, m_fox_b_f, m_fox_w_o, m_swa_w_in, m_swa_sinks, m_swa_w_o, m_mla_w_a, m_mla_g_q, m_mla_g_kv, m_mla_w_uq, m_mla_w_ukv, m_mla_w_o, m_ffn_w_in, m_ffn_conv_w, m_ffn_conv_b, m_ffn_w_out, v_meta_tokens, v_ln1_g, v_ln1_b, v_ln2_g, v_ln2_b, v_fox_w_in, v_fox_b_f, v_fox_w_o, v_swa_w_in, v_swa_sinks, v_swa_w_o, v_mla_w_a, v_mla_g_q, v_mla_g_kv, v_mla_w_uq, v_mla_w_ukv, v_mla_w_o, v_ffn_w_in, v_ffn_conv_w, v_ffn_conv_b, v_ffn_w_out):
    args = (meta_tokens, ln1_g, ln1_b, ln2_g, ln2_b, fox_w_in, fox_b_f, fox_w_o, swa_w_in, swa_sinks, swa_w_o, mla_w_a,
            mla_g_q, mla_g_kv, mla_w_uq, mla_w_ukv, mla_w_o, ffn_w_in, ffn_conv_w, ffn_conv_b, ffn_w_out)
    ms = (m_meta_tokens, m_ln1_g, m_ln1_b, m_ln2_g, m_ln2_b, m_fox_w_in, m_fox_b_f, m_fox_w_o, m_swa_w_in, m_swa_sinks,
          m_swa_w_o, m_mla_w_a, m_mla_g_q, m_mla_g_kv, m_mla_w_uq, m_mla_w_ukv, m_mla_w_o, m_ffn_w_in, m_ffn_conv_w,
          m_ffn_conv_b, m_ffn_w_out)
    vs = (v_meta_tokens, v_ln1_g, v_ln1_b, v_ln2_g, v_ln2_b, v_fox_w_in, v_fox_b_f, v_fox_w_o, v_swa_w_in, v_swa_sinks,
          v_swa_w_o, v_mla_w_a, v_mla_g_q, v_mla_g_kv, v_mla_w_uq, v_mla_w_ukv, v_mla_w_o, v_ffn_w_in, v_ffn_conv_w,
          v_ffn_conv_b, v_ffn_w_out)
    w = dict(zip(WEIGHTS, args))
    m = dict(zip(WEIGHTS, ms))
    v = dict(zip(WEIGHTS, vs))
    loss, grad_x, out = _step(x[0], loss_target[0], w, m, v)
    loss = lax.psum(loss, ("x", "y", "c"))
    res = [loss, grad_x[None]]
    for kind in ("grad", "delta", "new_m", "new_v"):
        res += [out[(kind, n)] for n in WEIGHTS]
    return tuple(res)
```

```python
import jax
import jax.numpy as jnp
import numpy as np
from jax import lax
from jax.experimental import pallas as pl
from jax.experimental.pallas import tpu as pltpu

F32 = jnp.float32
BF16 = jnp.bfloat16

D_MODEL = 1024
DEPTH = 4
N_META = 16
BLOCK = 128
PAD = BLOCK - N_META
NEG = -1e30
ALPHA = (2.0 * DEPTH) ** 0.25
LN_EPS = 1e-5
RMS_EPS = 1e-6
FOX_HEADS, FOX_HD = 16, 64
SWA_QH, SWA_KVH, SWA_HD = 16, 2, 64
SWA_G = SWA_QH // SWA_KVH
ROPE_THETA = 500000.0
ROPE_DIM = SWA_HD // 4
MLA_HEADS, MLA_QL, MLA_KVL, MLA_NOPE, MLA_ROPE, MLA_V = 16, 384, 256, 64, 32, 64
MLA_ROPE_THETA = 10000.0
D_FF = 2816
CONV_W = 3
ADAM_LR, ADAM_B1, ADAM_B2, ADAM_EPS, ADAM_WD, ADAM_STEP = 0.001, 0.9, 0.999, 1e-08, 0.01, 10

LANES = 128
SUBLANES_F32 = 8
SUBLANES_BF16 = 16
VMEM_LIMIT = 48 * 1024 * 1024
N_DEV = 8
MESH = pl.DeviceIdType.MESH
ROW_TILE = 640
ADAM_ROWS = 256


def _tile(n, cap):
    if n <= cap:
        return n
    best = 0
    for t in range(LANES, cap + 1, LANES):
        if n % t == 0:
            best = t
    assert best, (n, cap)
    return best


def _params(*sem):
    return pltpu.CompilerParams(dimension_semantics=sem, vmem_limit_bytes=VMEM_LIMIT)


def _mm(a, b, *, la="mk", lb="kn", sa=None, sb=None, so=None, out_dtype, name):
    size, tile, parts = {}, {}, {}
    for x, lay, split in ((a, la, sa), (b, lb, sb)):
        shp = x.shape[1:] if split else x.shape
        for ax, n in zip(lay, shp):
            if ax == split:
                size[ax], tile[ax], parts[ax] = x.shape[0] * n, n, x.shape[0]
            else:
                assert size.setdefault(ax, n) == n, (name, ax, a.shape, b.shape)
    for ax, cap in (("m", 1024), ("n", 1536), ("k", 1536)):
        tile.setdefault(ax, _tile(size[ax], cap))
    grid = tuple(size[ax] // tile[ax] for ax in "mnk")
    nk = grid[2]
    dn = (((la.index("k"),), (lb.index("k"),)), ((), ()))

    def kern(a_ref, b_ref, o_ref, *acc):
        p = lax.dot_general(a_ref[...].astype(BF16), b_ref[...].astype(BF16), dn, preferred_element_type=F32)
        if nk == 1:
            o_ref[...] = p.astype(o_ref.dtype)
            return
        acc_ref, = acc
        k = pl.program_id(2)

        @pl.when(k == 0)
        def _():
            acc_ref[...] = p

        @pl.when(k > 0)
        def _():
            acc_ref[...] += p

        @pl.when(k == nk - 1)
        def _():
            o_ref[...] = acc_ref[...].astype(o_ref.dtype)

    def spec(lay, split):
        blk = tuple(tile[ax] for ax in lay)
        if split is None:
            return pl.BlockSpec(blk, lambda i, j, k: tuple({"m": i, "n": j, "k": k}[ax] for ax in lay))
        return pl.BlockSpec((None,) + blk, lambda i, j, k: ({"m": i, "n": j, "k": k}[split],) + tuple(
            0 if ax == split else {"m": i, "n": j, "k": k}[ax] for ax in lay))

    if so is None:
        out_shape = (size["m"], size["n"])
    else:
        out_shape = (parts[so],) + tuple(tile[ax] if ax == so else size[ax] for ax in "mn")
    return pl.pallas_call(
        kern, name=name, grid=grid,
        in_specs=[spec(la, sa), spec(lb, sb)], out_specs=spec("mn", so),
        out_shape=jax.ShapeDtypeStruct(out_shape, out_dtype),
        scratch_shapes=[] if nk == 1 else [pltpu.VMEM((tile["m"], tile["n"]), F32)],
        compiler_params=_params("parallel", "parallel", "arbitrary"),
    )(a, b)


def _linear(name, out_dtype):
    @jax.custom_vjp
    def f(x, w):
        return _mm(x, w, out_dtype=out_dtype, name=name + "_fwd")

    def fwd(x, w):
        return f(x, w), (x, w)

    def bwd(res, g):
        x, w = res
        dx = _mm(g, w, lb="nk", out_dtype=x.dtype, name=name + "_dx")
        dw = _mm(x, g, la="km", out_dtype=w.dtype, name=name + "_dw")
        return dx, dw

    f.defvjp(fwd, bwd)
    return f


def _ln_stats(z):
    mu = jnp.mean(z, axis=-1, keepdims=True)
    zc = z - mu
    var = jnp.mean(zc * zc, axis=-1, keepdims=True)
    return zc, lax.rsqrt(var + LN_EPS)


def _ln_fwd_call(h, mix, g, b, name):
    L, D = h.shape
    tm = _tile(L, ROW_TILE)

    def kern(h_ref, m_ref, g_ref, b_ref, o_ref):
        zc, rstd = _ln_stats(ALPHA * h_ref[...] + m_ref[...])
        o_ref[...] = zc * rstd * g_ref[...] + b_ref[...]

    row = pl.BlockSpec((tm, D), lambda i: (i, 0))
    vec = pl.BlockSpec((1, D), lambda i: (0, 0))
    return pl.pallas_call(kern, name=name, grid=(L // tm,), in_specs=[row, row, vec, vec], out_specs=row,
                          out_shape=jax.ShapeDtypeStruct((L, D), F32), compiler_params=_params("parallel"))(h, mix, g, b)


def _ln_bwd_call(h, mix, g, dout, name):
    L, D = h.shape
    tm = _tile(L, ROW_TILE)

    def kern(h_ref, m_ref, g_ref, d_ref, dz_ref, dg_ref, db_ref):
        i = pl.program_id(0)
        zc, rstd = _ln_stats(ALPHA * h_ref[...] + m_ref[...])
        xhat = zc * rstd
        d = d_ref[...]
        dxh = d * g_ref[...]
        m1 = jnp.mean(dxh, axis=-1, keepdims=True)
        m2 = jnp.mean(dxh * xhat, axis=-1, keepdims=True)
        row = i * tm + lax.broadcasted_iota(jnp.int32, (tm, 1), 0)
        dz_ref[...] = jnp.where(row >= PAD, rstd * (dxh - m1 - xhat * m2), 0.0)
        pg = jnp.sum(d * xhat, axis=0, keepdims=True)
        pb = jnp.sum(d, axis=0, keepdims=True)

        @pl.when(i == 0)
        def _():
            dg_ref[...] = pg
            db_ref[...] = pb

        @pl.when(i > 0)
        def _():
            dg_ref[...] += pg
            db_ref[...] += pb

    row = pl.BlockSpec((tm, D), lambda i: (i, 0))
    vec = pl.BlockSpec((1, D), lambda i: (0, 0))
    return pl.pallas_call(
        kern, name=name, grid=(L // tm,), in_specs=[row, row, vec, row], out_specs=[row, vec, vec],
        out_shape=[jax.ShapeDtypeStruct((L, D), F32), jax.ShapeDtypeStruct((1, D), F32), jax.ShapeDtypeStruct((1, D), F32)],
        compiler_params=_params("arbitrary"))(h, mix, g, dout)


def _deepnorm(name):
    @jax.custom_vjp
    def f(h, mix, g, b):
        return _ln_fwd_call(h, mix, g, b, name + "_fwd")

    def fwd(h, mix, g, b):
        return f(h, mix, g, b), (h, mix, g)

    def bwd(res, dout):
        h, mix, g = res
        dz, dg, db = _ln_bwd_call(h, mix, g, dout, name + "_bwd")
        return ALPHA * dz, dz, dg, db

    f.defvjp(fwd, bwd)
    return f


def _rms_fwd_call(x, g, name):
    L, n = x.shape
    tm = _tile(L, ROW_TILE)

    def kern(x_ref, g_ref, o_ref):
        x = x_ref[...]
        o_ref[...] = x * lax.rsqrt(jnp.mean(x * x, axis=-1, keepdims=True) + RMS_EPS) * g_ref[...]

    row = pl.BlockSpec((tm, n), lambda i: (i, 0))
    vec = pl.BlockSpec((1, n), lambda i: (0, 0))
    return pl.pallas_call(kern, name=name, grid=(L // tm,), in_specs=[row, vec], out_specs=row,
                          out_shape=jax.ShapeDtypeStruct((L, n), F32), compiler_params=_params("parallel"))(x, g)


def _rms_bwd_call(x, g, dout, name):
    L, n = x.shape
    tm = _tile(L, ROW_TILE)

    def kern(x_ref, g_ref, d_ref, dx_ref, dg_ref):
        i = pl.program_id(0)
        x = x_ref[...]
        rstd = lax.rsqrt(jnp.mean(x * x, axis=-1, keepdims=True) + RMS_EPS)
        xhat = x * rstd
        d = d_ref[...]
        dxh = d * g_ref[...]
        dx_ref[...] = rstd * (dxh - xhat * jnp.mean(dxh * xhat, axis=-1, keepdims=True))
        pg = jnp.sum(d * xhat, axis=0, keepdims=True)

        @pl.when(i == 0)
        def _():
            dg_ref[...] = pg

        @pl.when(i > 0)
        def _():
            dg_ref[...] += pg

    row = pl.BlockSpec((tm, n), lambda i: (i, 0))
    vec = pl.BlockSpec((1, n), lambda i: (0, 0))
    return pl.pallas_call(
        kern, name=name, grid=(L // tm,), in_specs=[row, vec, row], out_specs=[row, vec],
        out_shape=[jax.ShapeDtypeStruct((L, n), F32), jax.ShapeDtypeStruct((1, n), F32)],
        compiler_params=_params("arbitrary"))(x, g, dout)


def _rmsnorm(name):
    @jax.custom_vjp
    def f(x, g):
        return _rms_fwd_call(x, g, name + "_fwd")

    def fwd(x, g):
        return f(x, g), (x, g)

    def bwd(res, dout):
        x, g = res
        dx, dg = _rms_bwd_call(x, g, dout, name + "_bwd")
        return dx, dg

    f.defvjp(fwd, bwd)
    return f


def _rope_call(x, c, s1, s2, r, out_dtype, name):
    L, W = x.shape
    reps = W // LANES
    tm = _tile(L, ROW_TILE)

    def kern(x_ref, c_ref, s1_ref, s2_ref, o_ref):
        x = x_ref[...].astype(F32)
        wide = lambda t: jnp.tile(t[...], (1, reps)) if reps > 1 else t[...]
        out = x * wide(c_ref) + pltpu.roll(x, W - r, 1) * wide(s1_ref) + pltpu.roll(x, r, 1) * wide(s2_ref)
        o_ref[...] = out.astype(o_ref.dtype)

    row = pl.BlockSpec((tm, W), lambda i: (i, 0))
    tab = pl.BlockSpec((tm, LANES), lambda i: (i, 0))
    return pl.pallas_call(kern, name=name, grid=(L // tm,), in_specs=[row, tab, tab, tab], out_specs=row,
                          out_shape=jax.ShapeDtypeStruct((L, W), out_dtype), compiler_params=_params("parallel"))(x, c, s1, s2)


def _rope(name, r, out_dtype):
    @jax.custom_vjp
    def f(x, c, s1, s2):
        return _rope_call(x, c, s1, s2, r, out_dtype, name + "_fwd")

    def fwd(x, c, s1, s2):
        return f(x, c, s1, s2), (c, s1, s2, jnp.zeros((), x.dtype))

    def bwd(res, g):
        c, s1, s2, proto = res
        dx = _rope_call(g, c, -s1, -s2, r, proto.dtype, name + "_bwd")
        return dx, jnp.zeros_like(c), jnp.zeros_like(s1), jnp.zeros_like(s2)

    f.defvjp(fwd, bwd)
    return f


def _rope_tables(L, dim, theta):
    pos = (jnp.arange(L) - PAD).astype(F32)
    inv = theta ** (-jnp.arange(0, dim, 2, dtype=F32) / dim)
    ang = pos[:, None] * inv[None, :]
    return jnp.cos(ang), jnp.sin(ang)


def _rope_lanes(cos, sin, period):
    L, half = cos.shape
    one = jnp.ones((L, period - 2 * half), F32)
    zero = jnp.zeros((L, period - 2 * half), F32)
    z_h = jnp.zeros((L, half), F32)
    c = jnp.concatenate([cos, cos, one], axis=1)
    s1 = jnp.concatenate([-sin, z_h, zero], axis=1)
    s2 = jnp.concatenate([z_h, sin, zero], axis=1)
    reps = LANES // period
    return tuple(jnp.tile(t, (1, reps)) for t in (c, s1, s2))


HALO = SUBLANES_BF16


def _conv_taps(buf_ref, off, rows, w, b):
    return (b + buf_ref[pl.ds(off - 2, rows), :] * w[0:1, :]
            + buf_ref[pl.ds(off - 1, rows), :] * w[1:2, :] + buf_ref[pl.ds(off, rows), :] * w[2:3, :])


def _glu_fwd_call(u4, cw4, cb4, name):
    _, P, L, C = u4.shape
    tm = _tile(L, ROW_TILE)
    hb = tm // HALO

    def kern(u_ref, up_ref, w_ref, b_ref, o_ref, buf):
        i = pl.program_id(1)
        row = i * tm + lax.broadcasted_iota(jnp.int32, (tm, 1), 0)
        prow = i * tm - HALO + lax.broadcasted_iota(jnp.int32, (HALO, 1), 0)
        ys = []
        for g in range(2):
            buf[g, pl.ds(HALO, tm), :] = jnp.where(row >= PAD, u_ref[g], 0.0)
            buf[g, pl.ds(0, HALO), :] = jnp.where(prow >= PAD, up_ref[g], 0.0)
            ys.append(_conv_taps(buf.at[g], HALO, tm, w_ref[g], b_ref[g]))
        gate, val = ys
        o_ref[...] = (gate * jax.nn.sigmoid(gate) * val).astype(o_ref.dtype)

    return pl.pallas_call(
        kern, name=name, grid=(P, L // tm),
        in_specs=[pl.BlockSpec((2, None, tm, C), lambda p, i: (0, p, i, 0)),
                  pl.BlockSpec((2, None, HALO, C), lambda p, i: (0, p, jnp.maximum(i * hb - 1, 0), 0)),
                  pl.BlockSpec((2, None, CONV_W, C), lambda p, i: (0, p, 0, 0)),
                  pl.BlockSpec((2, None, 1, C), lambda p, i: (0, p, 0, 0))],
        out_specs=pl.BlockSpec((None, tm, C), lambda p, i: (p, i, 0)),
        out_shape=jax.ShapeDtypeStruct((P, L, C), BF16),
        scratch_shapes=[pltpu.VMEM((2, tm + HALO, C), F32)],
        compiler_params=_params("parallel", "parallel"))(u4, u4, cw4, cb4)


def _glu_bwd_call(u4, cw4, cb4, dact, name):
    _, P, L, C = u4.shape
    tm = _tile(L, ROW_TILE)
    hb = tm // HALO
    ext = tm + HALO
    last_halo = L // HALO - 1

    def kern(u_ref, up_ref, un_ref, w_ref, b_ref, d_ref, dn_ref, du_ref, dw_ref, db_ref, ubuf, dybuf):
        i = pl.program_id(1)
        r0 = i * tm
        mask = lambda blk, start: jnp.where(
            (start + lax.broadcasted_iota(jnp.int32, (blk.shape[0], 1), 0) >= PAD), blk, 0.0)
        ys = []
        for g in range(2):
            ubuf[g, pl.ds(0, HALO), :] = mask(up_ref[g], r0 - HALO)
            ubuf[g, pl.ds(HALO, tm), :] = mask(u_ref[g], r0)
            ubuf[g, pl.ds(HALO + tm, HALO), :] = un_ref[g]
            ys.append(_conv_taps(ubuf.at[g], HALO, ext, w_ref[g], b_ref[g]))
        gate, val = ys
        sg = jax.nn.sigmoid(gate)
        d = jnp.concatenate([d_ref[...], dn_ref[...]], axis=0).astype(F32)
        erow = r0 + lax.broadcasted_iota(jnp.int32, (ext, 1), 0)
        d = jnp.where(erow < L, d, 0.0)
        dybuf[0] = d * val * (sg * (1.0 + gate * (1.0 - sg)))
        dybuf[1] = d * (gate * sg)
        row = r0 + lax.broadcasted_iota(jnp.int32, (tm, 1), 0)
        for g in range(2):
            w = w_ref[g]
            du = (dybuf[g, pl.ds(0, tm), :] * w[2:3, :] + dybuf[g, pl.ds(1, tm), :] * w[1:2, :]
                  + dybuf[g, pl.ds(2, tm), :] * w[0:1, :])
            du_ref[g] = jnp.where(row >= PAD, du, 0.0).astype(du_ref.dtype)
            dy = dybuf[g, pl.ds(0, tm), :]
            pw = jnp.concatenate([jnp.sum(dy * ubuf[g, pl.ds(HALO - 2 + t, tm), :], axis=0, keepdims=True)
                                  for t in range(CONV_W)], axis=0)
            pb = jnp.sum(dy, axis=0, keepdims=True)

            @pl.when(i == 0)
            def _():
                dw_ref[g] = pw
                db_ref[g] = pb

            @pl.when(i > 0)
            def _():
                dw_ref[g] += pw
                db_ref[g] += pb

    nxt = lambda i: jnp.minimum((i + 1) * hb, last_halo)
    return pl.pallas_call(
        kern, name=name, grid=(P, L // tm),
        in_specs=[pl.BlockSpec((2, None, tm, C), lambda p, i: (0, p, i, 0)),
                  pl.BlockSpec((2, None, HALO, C), lambda p, i: (0, p, jnp.maximum(i * hb - 1, 0), 0)),
                  pl.BlockSpec((2, None, HALO, C), lambda p, i: (0, p, nxt(i), 0)),
                  pl.BlockSpec((2, None, CONV_W, C), lambda p, i: (0, p, 0, 0)),
                  pl.BlockSpec((2, None, 1, C), lambda p, i: (0, p, 0, 0)),
                  pl.BlockSpec((None, tm, C), lambda p, i: (p, i, 0)),
                  pl.BlockSpec((None, HALO, C), lambda p, i: (p, nxt(i), 0))],
        out_specs=[pl.BlockSpec((2, None, tm, C), lambda p, i: (0, p, i, 0)),
                   pl.BlockSpec((2, None, CONV_W, C), lambda p, i: (0, p, 0, 0)),
                   pl.BlockSpec((2, None, 1, C), lambda p, i: (0, p, 0, 0))],
        out_shape=[jax.ShapeDtypeStruct((2, P, L, C), BF16), jax.ShapeDtypeStruct((2, P, CONV_W, C), F32),
                   jax.ShapeDtypeStruct((2, P, 1, C), F32)],
        scratch_shapes=[pltpu.VMEM((2, tm + 2 * HALO, C), F32), pltpu.VMEM((2, ext, C), F32)],
        compiler_params=_params("parallel", "arbitrary"))(u4, u4, u4, cw4, cb4, dact, dact)


def _ffn_up(name):
    def run(h1, w3, cw4, cb4):
        u3 = _mm(h1, w3, sb="n", so="n", out_dtype=F32, name=name + "_up")
        u4 = u3.reshape((2, u3.shape[0] // 2) + u3.shape[1:])
        return _glu_fwd_call(u4, cw4, cb4, name + "_glu"), u4

    @jax.custom_vjp
    def f(h1, w3, cw4, cb4):
        return run(h1, w3, cw4, cb4)[0]

    def fwd(h1, w3, cw4, cb4):
        act, u4 = run(h1, w3, cw4, cb4)
        return act, (h1, w3, cw4, cb4, u4)

    def bwd(res, dact):
        h1, w3, cw4, cb4, u4 = res
        du4, dcw, dcb = _glu_bwd_call(u4, cw4, cb4, dact, name + "_glu_bwd")
        du3 = du4.reshape((du4.shape[0] * du4.shape[1],) + du4.shape[2:])
        dh1 = _mm(du3, w3, sa="k", lb="nk", sb="k", out_dtype=F32, name=name + "_up_dx")
        dw3 = _mm(h1, du3, la="km", sb="n", so="n", out_dtype=w3.dtype, name=name + "_up_dw")
        return dh1, dw3, dcw, dcb

    f.defvjp(fwd, bwd)
    return f


def _ffn_down(name):
    @jax.custom_vjp
    def f(act3, wo3):
        return _mm(act3, wo3, sa="k", sb="k", out_dtype=F32, name=name + "_fwd")

    def fwd(act3, wo3):
        return f(act3, wo3), (act3, wo3)

    def bwd(res, g):
        act3, wo3 = res
        dact = _mm(g, wo3, lb="nk", sb="n", so="n", out_dtype=act3.dtype, name=name + "_dx")
        dwo = _mm(act3, g, la="km", sa="m", so="m", out_dtype=wo3.dtype, name=name + "_dw")
        return dact, dwo

    f.defvjp(fwd, bwd)
    return f


def _pairs(n, kv_major):
    if kv_major:
        pr = [(i, j) for j in range(n) for i in range(j, n)]
    else:
        pr = [(i, j) for i in range(n) for j in range(i + 1)]
    return (jnp.asarray(np.array([p[0] for p in pr], np.int32)), jnp.asarray(np.array([p[1] for p in pr], np.int32)))


def _scores(q, k, scale, i, j, tq, tk, masked):
    s = lax.dot_general(q, k, (((1,), (1,)), ((), ())), preferred_element_type=F32)
    if scale is not None:
        s = s * scale
    if not masked:
        return s, None
    row = i * tq + lax.broadcasted_iota(jnp.int32, (tq, tk), 0)
    col = j * tk + lax.broadcasted_iota(jnp.int32, (tq, tk), 1)
    mask = (col <= row) & (col >= PAD)
    return jnp.where(mask, s, NEG), mask


def _flash_fwd_call(q, k, v, scale, name):
    H, L, dk = q.shape
    dv = v.shape[-1]
    tq = tk = _tile(L, ROW_TILE)
    nq = L // tq
    it, jt = _pairs(nq, False)

    def kern(it_ref, jt_ref, q_ref, k_ref, v_ref, o_ref, lse_ref, m_s, l_s, acc_s):
        t = pl.program_id(1)
        i, j = it_ref[t], jt_ref[t]

        @pl.when(j == 0)
        def _():
            m_s[...] = jnp.full_like(m_s, NEG)
            l_s[...] = jnp.zeros_like(l_s)
            acc_s[...] = jnp.zeros_like(acc_s)

        def step(masked):
            s, mask = _scores(q_ref[...], k_ref[...], scale, i, j, tq, tk, masked)
            m_prev = m_s[...]
            m_new = jnp.maximum(m_prev, jnp.max(s, axis=-1, keepdims=True))
            a = jnp.exp(m_prev - m_new)
            p = jnp.exp(s - m_new)
            if masked:
                p = jnp.where(mask, p, 0.0)
            l_s[...] = a * l_s[...] + jnp.sum(p, axis=-1, keepdims=True)
            acc_s[...] = a * acc_s[...] + jnp.dot(p.astype(BF16), v_ref[...], preferred_element_type=F32)
            m_s[...] = m_new

        edge = (j == i) | (j == 0)
        pl.when(edge)(lambda: step(True))
        pl.when(jnp.logical_not(edge))(lambda: step(False))

        @pl.when(j == i)
        def _():
            l = l_s[...]
            l = jnp.where(l == 0.0, 1.0, l)
            o_ref[...] = (acc_s[...] / l).astype(o_ref.dtype)
            lse_ref[...] = m_s[...] + jnp.log(l)

    qmap = lambda h, t, it, jt: (h, it[t], 0)
    kmap = lambda h, t, it, jt: (h, jt[t], 0)
    return pl.pallas_call(
        kern, name=name,
        grid_spec=pltpu.PrefetchScalarGridSpec(
            num_scalar_prefetch=2, grid=(H, int(it.shape[0])),
            in_specs=[pl.BlockSpec((None, tq, dk), qmap), pl.BlockSpec((None, tk, dk), kmap),
                      pl.BlockSpec((None, tk, dv), kmap)],
            out_specs=[pl.BlockSpec((None, tq, dv), qmap), pl.BlockSpec((None, tq, 1), qmap)],
            scratch_shapes=[pltpu.VMEM((tq, 1), F32), pltpu.VMEM((tq, 1), F32), pltpu.VMEM((tq, dv), F32)]),
        out_shape=[jax.ShapeDtypeStruct((H, L, dv), F32), jax.ShapeDtypeStruct((H, L, 1), F32)],
        compiler_params=_params("parallel", "arbitrary"))(it, jt, q, k, v)


def _flash_bwd_call(q, k, v, o, lse, do, scale, dk_dtype, name):
    H, L, dk = q.shape
    dv = v.shape[-1]
    tq = tk = _tile(L, ROW_TILE)
    nq = L // tq
    it, jt = _pairs(nq, True)

    def kern(it_ref, jt_ref, q_ref, k_ref, v_ref, o_ref, do_ref, lse_ref, dq_ref, dk_ref, dv_ref, dk_s, dv_s):
        t = pl.program_id(1)
        i, j = it_ref[t], jt_ref[t]

        @pl.when(t == 0)
        def _():
            dq_ref[...] = jnp.zeros_like(dq_ref)

        @pl.when(i == j)
        def _():
            dk_s[...] = jnp.zeros_like(dk_s)
            dv_s[...] = jnp.zeros_like(dv_s)

        def step(masked):
            qb, kb = q_ref[...], k_ref[...]
            s, mask = _scores(qb, kb, scale, i, j, tq, tk, masked)
            p = jnp.exp(s - lse_ref[...])
            if masked:
                p = jnp.where(mask, p, 0.0)
            dof = do_ref[...]
            dob = dof.astype(BF16)
            delta = jnp.sum(dof * o_ref[...], axis=-1, keepdims=True)
            dp = lax.dot_general(dob, v_ref[...], (((1,), (1,)), ((), ())), preferred_element_type=F32)
            dsb = (p * (dp - delta)).astype(BF16)
            dv_s[...] += lax.dot_general(p.astype(BF16), dob, (((0,), (0,)), ((), ())), preferred_element_type=F32)
            dk_s[...] += lax.dot_general(dsb, qb, (((0,), (0,)), ((), ())), preferred_element_type=F32)
            rows = pl.ds(pl.multiple_of(i * tq, tq), tq)
            dq_ref[rows, :] += jnp.dot(dsb, kb, preferred_element_type=F32)

        edge = (j == i) | (j == 0)
        pl.when(edge)(lambda: step(True))
        pl.when(jnp.logical_not(edge))(lambda: step(False))

        @pl.when(i == nq - 1)
        def _():
            dks = dk_s[...]
            dk_ref[...] = (dks if scale is None else dks * scale).astype(dk_ref.dtype)
            dv_ref[...] = dv_s[...].astype(dv_ref.dtype)

    qmap = lambda h, t, it, jt: (h, it[t], 0)
    kmap = lambda h, t, it, jt: (h, jt[t], 0)
    return pl.pallas_call(
        kern, name=name,
        grid_spec=pltpu.PrefetchScalarGridSpec(
            num_scalar_prefetch=2, grid=(H, int(it.shape[0])),
            in_specs=[pl.BlockSpec((None, tq, dk), qmap), pl.BlockSpec((None, tk, dk), kmap),
                      pl.BlockSpec((None, tk, dv), kmap), pl.BlockSpec((None, tq, dv), qmap),
                      pl.BlockSpec((None, tq, dv), qmap), pl.BlockSpec((None, tq, 1), qmap)],
            out_specs=[pl.BlockSpec((None, L, dk), lambda h, t, it, jt: (h, 0, 0)),
                       pl.BlockSpec((None, tk, dk), kmap), pl.BlockSpec((None, tk, dv), kmap)],
            scratch_shapes=[pltpu.VMEM((tk, dk), F32), pltpu.VMEM((tk, dv), F32)]),
        out_shape=[jax.ShapeDtypeStruct((H, L, dk), F32), jax.ShapeDtypeStruct((H, L, dk), dk_dtype),
                   jax.ShapeDtypeStruct((H, L, dv), BF16)],
        compiler_params=_params("parallel", "arbitrary"))(it, jt, q, k, v, o, do, lse)


def _mla_attention(name):
    scale = (MLA_NOPE + MLA_ROPE) ** -0.5

    @jax.custom_vjp
    def f(q, k, v):
        return _flash_fwd_call(q, k, v, scale, name + "_fwd")[0]

    def fwd(q, k, v):
        o, lse = _flash_fwd_call(q, k, v, scale, name + "_fwd")
        return o, (q, k, v, o, lse)

    def bwd(res, do):
        q, k, v, o, lse = res
        dq, dk, dv = _flash_bwd_call(q, k, v, o, lse, do, scale, BF16, name + "_bwd")
        return (dq * scale).astype(BF16), dk, dv

    f.defvjp(fwd, bwd)
    return f


FOX_SCALE = 0.125
C_PARTS = 3


def _fox_operands(q, k, c):
    to_bf16 = lambda t: lax.reduce_precision(t, exponent_bits=8, mantissa_bits=7)
    hi = to_bf16(c)
    lo = to_bf16(c - hi)
    ll = to_bf16(c - hi - lo)
    terms = jnp.stack([hi, lo, ll], axis=-1).astype(BF16)
    ones = jnp.ones_like(terms)
    fill = jnp.zeros(q.shape[:2] + (LANES - FOX_HD - 2 * C_PARTS,), BF16)
    qa = jnp.concatenate([q * jnp.asarray(FOX_SCALE, BF16), terms, ones, fill], axis=-1)
    ka = jnp.concatenate([k, ones, -terms, fill], axis=-1)
    return qa, ka


def _fox_attention(name):
    assert FOX_SCALE == FOX_HD ** -0.5

    def run(q, k, v, c):
        qa, ka = _fox_operands(q, k, c)
        o, lse = _flash_fwd_call(qa, ka, v, None, name + "_fwd")
        return o, (qa, ka, v, o, lse)

    @jax.custom_vjp
    def f(q, k, v, c):
        return run(q, k, v, c)[0]

    def fwd(q, k, v, c):
        return run(q, k, v, c)

    def bwd(res, do):
        qa, ka, v, o, lse = res
        dqa, dka, dv = _flash_bwd_call(qa, ka, v, o, lse, do, None, F32, name + "_bwd")
        dc = dqa[:, :, FOX_HD] - dka[:, :, FOX_HD + C_PARTS]
        return (dqa[:, :, :FOX_HD] * FOX_SCALE).astype(BF16), dka[:, :, :FOX_HD].astype(BF16), dv, dc

    f.defvjp(fwd, bwd)
    return f


def _swa_parts(q_ref, kc_ref, kp_ref, km_ref, sink_ref, i, scale):
    R = SWA_G * BLOCK
    q = q_ref[...].reshape(R, SWA_HD)
    nt = (((1,), (1,)), ((), ()))
    r = lax.broadcasted_iota(jnp.int32, (R, BLOCK), 0) & (BLOCK - 1)
    c = lax.broadcasted_iota(jnp.int32, (R, BLOCK), 1)
    s_c = jnp.where((c <= r) & (i >= 1), lax.dot_general(q, kc_ref[...], nt, preferred_element_type=F32) * scale, NEG)
    s_p = jnp.where((c > r) & (i >= 2), lax.dot_general(q, kp_ref[...], nt, preferred_element_type=F32) * scale, NEG)
    s_m = jnp.where((c >= PAD) & ((i >= 1) | (c <= r)),
                    lax.dot_general(q, km_ref[...], nt, preferred_element_type=F32) * scale, NEG)
    sink = jnp.broadcast_to(sink_ref[...], (SWA_G, BLOCK, 1)).reshape(R, 1)
    return q, s_c, s_p, s_m, sink


def _swa_specs(L):
    nb = L // BLOCK
    qs = pl.BlockSpec((None, SWA_G, BLOCK, SWA_HD), lambda g, i: (g, 0, i, 0))
    kc = pl.BlockSpec((None, BLOCK, SWA_HD), lambda g, i: (g, i, 0))
    kp = pl.BlockSpec((None, BLOCK, SWA_HD), lambda g, i: (g, jnp.maximum(i - 1, 0), 0))
    km = pl.BlockSpec((None, BLOCK, SWA_HD), lambda g, i: (g, 0, 0))
    sk = pl.BlockSpec((None, SWA_G, 1, 1), lambda g, i: (g, 0, 0, 0))
    ls = pl.BlockSpec((None, SWA_G, BLOCK, 1), lambda g, i: (g, 0, i, 0))
    return nb, qs, kc, kp, km, sk, ls


def _swa_fwd_call(q, k, v, sinks, scale, name):
    _, _, L, _ = q.shape
    nb, qs, kc, kp, km, sk, ls = _swa_specs(L)

    def kern(q_ref, kc_ref, kp_ref, km_ref, vc_ref, vp_ref, vm_ref, sink_ref, o_ref, lse_ref):
        i = pl.program_id(1)
        _, s_c, s_p, s_m, sink = _swa_parts(q_ref, kc_ref, kp_ref, km_ref, sink_ref, i, scale)
        mx = lambda s: jnp.max(s, axis=-1, keepdims=True)
        m = jnp.maximum(jnp.maximum(mx(s_c), mx(s_p)), jnp.maximum(mx(s_m), sink))
        p_c, p_p, p_m = jnp.exp(s_c - m), jnp.exp(s_p - m), jnp.exp(s_m - m)
        sm = lambda p: jnp.sum(p, axis=-1, keepdims=True)
        den = sm(p_c) + sm(p_p) + sm(p_m) + jnp.exp(sink - m)
        inv = 1.0 / den
        pv = lambda p, v_ref: jnp.dot((p * inv).astype(BF16), v_ref[...], preferred_element_type=F32)
        o = pv(p_c, vc_ref) + pv(p_p, vp_ref) + pv(p_m, vm_ref)
        o_ref[...] = o.reshape(SWA_G, BLOCK, SWA_HD)
        lse_ref[...] = (m + jnp.log(den)).reshape(SWA_G, BLOCK, 1)

    return pl.pallas_call(
        kern, name=name, grid=(SWA_KVH, nb), in_specs=[qs, kc, kp, km, kc, kp, km, sk], out_specs=[qs, ls],
        out_shape=[jax.ShapeDtypeStruct((SWA_KVH, SWA_G, L, SWA_HD), F32), jax.ShapeDtypeStruct((SWA_KVH, SWA_G, L, 1), F32)],
        compiler_params=_params("parallel", "parallel"))(q, k, k, k, v, v, v, sinks)


def _swa_bwd_call(q, k, v, sinks, o, lse, do, scale, name):
    _, _, L, _ = q.shape
    nb, qs, kc, kp, km, sk, ls = _swa_specs(L)
    R = SWA_G * BLOCK
    full = pl.BlockSpec((None, L, SWA_HD), lambda g, i: (g, 0, 0))

    def kern(q_ref, kc_ref, kp_ref, km_ref, vc_ref, vp_ref, vm_ref, sink_ref, o_ref, do_ref, lse_ref,
             dq_ref, dk_ref, dv_ref, dsink_ref):
        i = pl.program_id(1)

        @pl.when(i == 0)
        def _():
            dk_ref[...] = jnp.zeros_like(dk_ref)
            dv_ref[...] = jnp.zeros_like(dv_ref)
            dsink_ref[...] = jnp.zeros_like(dsink_ref)

        q, s_c, s_p, s_m, sink = _swa_parts(q_ref, kc_ref, kp_ref, km_ref, sink_ref, i, scale)
        lse = lse_ref[...].reshape(R, 1)
        dof = do_ref[...].reshape(R, SWA_HD)
        dob = dof.astype(BF16)
        delta = jnp.sum(dof * o_ref[...].reshape(R, SWA_HD), axis=-1, keepdims=True)
        nt = (((1,), (1,)), ((), ()))
        tn = (((0,), (0,)), ((), ()))
        cur = pl.ds(pl.multiple_of(i * BLOCK, BLOCK), BLOCK)
        prev = pl.ds(pl.multiple_of(jnp.maximum(i - 1, 0) * BLOCK, BLOCK), BLOCK)
        meta = pl.ds(0, BLOCK)
        dq = jnp.zeros((R, SWA_HD), F32)
        for s, k_ref, v_ref, rows in ((s_c, kc_ref, vc_ref, cur), (s_p, kp_ref, vp_ref, prev), (s_m, km_ref, vm_ref, meta)):
            p = jnp.exp(s - lse)
            dp = lax.dot_general(dob, v_ref[...], nt, preferred_element_type=F32)
            ds = (p * (dp - delta)).astype(BF16)
            dq = dq + jnp.dot(ds, k_ref[...], preferred_element_type=F32)
            dv_ref[rows, :] += lax.dot_general(p.astype(BF16), dob, tn, preferred_element_type=F32)
            dk_ref[rows, :] += lax.dot_general(ds, q, tn, preferred_element_type=F32) * scale
        dq_ref[...] = (dq * scale).reshape(SWA_G, BLOCK, SWA_HD).astype(dq_ref.dtype)
        dsk = -jnp.exp(sink - lse) * delta
        dsink_ref[...] += jnp.sum(dsk.reshape(SWA_G, BLOCK, 1), axis=1, keepdims=True)

    return pl.pallas_call(
        kern, name=name, grid=(SWA_KVH, nb), in_specs=[qs, kc, kp, km, kc, kp, km, sk, qs, qs, ls],
        out_specs=[qs, full, full, sk],
        out_shape=[jax.ShapeDtypeStruct((SWA_KVH, SWA_G, L, SWA_HD), BF16), jax.ShapeDtypeStruct((SWA_KVH, L, SWA_HD), F32),
                   jax.ShapeDtypeStruct((SWA_KVH, L, SWA_HD), F32), jax.ShapeDtypeStruct((SWA_KVH, SWA_G, 1, 1), F32)],
        compiler_params=_params("parallel", "arbitrary"))(q, k, k, k, v, v, v, sinks, o, do, lse)


def _swa_attn(name, scale):
    @jax.custom_vjp
    def f(q, k, v, sinks):
        return _swa_fwd_call(q, k, v, sinks, scale, name + "_fwd")[0]

    def fwd(q, k, v, sinks):
        o, lse = _swa_fwd_call(q, k, v, sinks, scale, name + "_fwd")
        return o, (q, k, v, sinks, o, lse)

    def bwd(res, do):
        q, k, v, sinks, o, lse = res
        dq, dk, dv, dsink = _swa_bwd_call(q, k, v, sinks, o, lse, do, scale, name + "_bwd")
        return dq, dk.astype(BF16), dv.astype(BF16), dsink

    f.defvjp(fwd, bwd)
    return f


def _scan_call(x, bias, mul, pre_logsig, name):
    L, W = x.shape
    tb = _tile(L, ROW_TILE)
    has_mul = mul is not None

    def kern(x_ref, b_ref, *rest):
        if has_mul:
            m_ref, o_ref, tot_ref, carry = rest
        else:
            o_ref, tot_ref, carry = rest
        i = pl.program_id(0)

        @pl.when(i == 0)
        def _():
            carry[...] = jnp.zeros_like(carry)
            tot_ref[...] = jnp.zeros_like(tot_ref)

        z = x_ref[...] + b_ref[...]
        if pre_logsig:
            z = jnp.minimum(z, 0.0) - jnp.log(1.0 + jnp.exp(-jnp.abs(z)))
        row = lax.broadcasted_iota(jnp.int32, (tb, W), 0)
        s = 1
        while s < tb:
            z = z + jnp.where(row >= s, pltpu.roll(z, s, 0), 0.0)
            s *= 2
        z = z + carry[...]
        carry[...] = z[tb - 1:tb, :]
        if has_mul:
            z = z * m_ref[...]
        o_ref[...] = z
        tot_ref[...] += jnp.sum(z, axis=0, keepdims=True)

    row = pl.BlockSpec((tb, W), lambda i: (i, 0))
    vec = pl.BlockSpec((1, W), lambda i: (0, 0))
    return pl.pallas_call(
        kern, name=name, grid=(L // tb,), in_specs=[row, vec] + ([row] if has_mul else []), out_specs=[row, vec],
        out_shape=[jax.ShapeDtypeStruct((L, W), F32), jax.ShapeDtypeStruct((1, W), F32)],
        scratch_shapes=[pltpu.VMEM((1, W), F32)],
        compiler_params=_params("arbitrary"))(*([x, bias] + ([mul] if has_mul else [])))


def _sigmoid_neg_call(x, bias, name):
    L, W = x.shape
    tb = _tile(L, ROW_TILE)

    def kern(x_ref, b_ref, o_ref):
        row = pl.program_id(0) * tb + lax.broadcasted_iota(jnp.int32, (tb, 1), 0)
        o_ref[...] = jnp.where(row >= PAD, jax.nn.sigmoid(-(x_ref[...] + b_ref[...])), 0.0)

    row = pl.BlockSpec((tb, W), lambda i: (i, 0))
    vec = pl.BlockSpec((1, W), lambda i: (0, 0))
    return pl.pallas_call(kern, name=name, grid=(L // tb,), in_specs=[row, vec], out_specs=row,
                          out_shape=jax.ShapeDtypeStruct((L, W), F32), compiler_params=_params("parallel"))(x, bias)


def _decay(name):
    @jax.custom_vjp
    def f(fg, b):
        return _scan_call(fg, b, None, True, name + "_fwd")[0]

    def fwd(fg, b):
        return f(fg, b), (fg, b)

    def bwd(res, dc):
        fg, b = res
        sg = _sigmoid_neg_call(fg, b, name + "_dsig")
        dfg_rev, db = _scan_call(dc[::-1], jnp.zeros_like(b), sg[::-1], False, name + "_bwd")
        return dfg_rev[::-1], db

    f.defvjp(fwd, bwd)
    return f


def _loss_call(hf, target, name):
    L, D = hf.shape
    nb = L // BLOCK

    def kern(h_ref, t_ref, loss_ref, dy_ref, acc):
        i = pl.program_id(0)

        @pl.when(i == 0)
        def _():
            acc[...] = jnp.zeros_like(acc)
            dy_ref[...] = jnp.zeros_like(dy_ref)

        @pl.when(i > 0)
        def _():
            e = h_ref[...] - t_ref[...]
            dy_ref[...] = e * (1.0 / D)
            acc[...] += jnp.sum(e * e, axis=0, keepdims=True)

        @pl.when(i == nb - 1)
        def _():
            loss_ref[...] = jnp.broadcast_to(jnp.sum(acc[...], axis=1, keepdims=True) * (0.5 / D), loss_ref.shape)

    return pl.pallas_call(
        kern, name=name, grid=(nb,),
        in_specs=[pl.BlockSpec((BLOCK, D), lambda i: (i, 0)), pl.BlockSpec((BLOCK, D), lambda i: (jnp.maximum(i - 1, 0), 0))],
        out_specs=[pl.BlockSpec((1, LANES), lambda i: (0, 0)), pl.BlockSpec((BLOCK, D), lambda i: (i, 0))],
        out_shape=[jax.ShapeDtypeStruct((1, LANES), F32), jax.ShapeDtypeStruct((L, D), F32)],
        scratch_shapes=[pltpu.VMEM((1, D), F32)],
        compiler_params=_params("arbitrary"))(hf, target)


def _loss_head(name):
    @jax.custom_vjp
    def f(hf, target):
        return _loss_call(hf, target, name)[0][0, 0]

    def fwd(hf, target):
        loss, dy = _loss_call(hf, target, name)
        return loss[0, 0], (dy, jnp.zeros((), F32))

    def bwd(res, g):
        dy, _ = res
        return dy * g, None

    f.defvjp(fwd, bwd)
    return f


def _me_and_peers():
    x, y, c = lax.axis_index("x"), lax.axis_index("y"), lax.axis_index("c")
    me = 4 * x + 2 * y + c
    peers = []
    for k in range(1, N_DEV):
        px = 1 - x if k & 4 else x
        py = 1 - y if k & 2 else y
        pc = 1 - c if k & 1 else c
        peers.append(((px, py, pc), 4 * px + 2 * py + pc))
    return me, peers


def _exchange(arrs, gather, name):
    n = len(arrs)
    shapes = [tuple(a.shape) if gather else tuple(a.shape[1:]) for a in arrs]

    def body(*refs):
        srcs, outs = refs[:n], refs[n:2 * n]
        send_sems, recv_sems, local_sems = refs[2 * n:]
        me, peers = _me_and_peers()
        piece = lambda a, p: srcs[a] if gather else srcs[a].at[p]
        started = []
        for a in range(n):
            mine = pltpu.make_async_copy(piece(a, me), outs[a].at[me], local_sems.at[a])
            mine.start()
            started.append(mine.wait)
        for k, (dev, pid) in enumerate(peers):
            for a in range(n):
                cp = pltpu.make_async_remote_copy(src_ref=piece(a, pid), dst_ref=outs[a].at[me], send_sem=send_sems.at[a, k],
                                                  recv_sem=recv_sems.at[a, k], device_id=dev, device_id_type=MESH)
                cp.start()
                started.append(cp.wait_send)
        for k, (dev, pid) in enumerate(peers):
            for a in range(n):
                pltpu.make_async_remote_copy(src_ref=piece(a, me), dst_ref=outs[a].at[pid], send_sem=send_sems.at[a, k],
                                             recv_sem=recv_sems.at[a, k], device_id=dev, device_id_type=MESH).wait_recv()
        for wait in started:
            wait()

    hbm = pl.BlockSpec(memory_space=pl.ANY)
    return pl.pallas_call(
        body, name=name, out_shape=[jax.ShapeDtypeStruct((N_DEV,) + s, a.dtype) for s, a in zip(shapes, arrs)],
        in_specs=[hbm] * n, out_specs=[hbm] * n,
        scratch_shapes=[pltpu.SemaphoreType.DMA((n, N_DEV - 1)), pltpu.SemaphoreType.DMA((n, N_DEV - 1)),
                        pltpu.SemaphoreType.DMA((n,))],
    )(*arrs)


def _adamw_call(parts, w, m, v, name):
    shape = w.shape
    cols = shape[-1]
    rows = int(np.prod(shape[:-1]))
    tr = rows
    if rows > ADAM_ROWS:
        tr = max(t for t in range(SUBLANES_F32, ADAM_ROWS + 1, SUBLANES_F32) if rows % t == 0)
    c1 = 1.0 / (1.0 - ADAM_B1 ** ADAM_STEP)
    c2 = 1.0 / (1.0 - ADAM_B2 ** ADAM_STEP)

    def kern(p_ref, w_ref, m_ref, v_ref, g_ref, d_ref, nm_ref, nv_ref):
        g = p_ref[0].astype(F32)
        for p in range(1, N_DEV):
            g = g + p_ref[p].astype(F32)
        nm = ADAM_B1 * m_ref[...] + (1.0 - ADAM_B1) * g
        nv = ADAM_B2 * v_ref[...] + (1.0 - ADAM_B2) * (g * g)
        g_ref[...] = g
        nm_ref[...] = nm
        nv_ref[...] = nv
        d_ref[...] = -ADAM_LR * ((nm * c1) / (jnp.sqrt(nv * c2) + ADAM_EPS) + ADAM_WD * w_ref[...])

    slab = pl.BlockSpec((tr, cols), lambda i: (i, 0))
    flat = lambda t: t.reshape(rows, cols)
    res = pl.pallas_call(
        kern, name=name, grid=(rows // tr,),
        in_specs=[pl.BlockSpec((N_DEV, tr, cols), lambda i: (0, i, 0)), slab, slab, slab], out_specs=[slab] * 4,
        out_shape=[jax.ShapeDtypeStruct((rows, cols), F32)] * 4,
        compiler_params=_params("parallel"))(parts.reshape(N_DEV, rows, cols), flat(w), flat(m), flat(v))
    return [r.reshape(shape) for r in res]


def _to_full(stacked, axis):
    moved = jnp.moveaxis(stacked, 0, axis)
    s = list(moved.shape)
    return moved.reshape(s[:axis] + [s[axis] * s[axis + 1]] + s[axis + 2:])


def _heads(x, h, d):
    return x.reshape(x.shape[0], h, d).transpose(1, 0, 2)


def _unheads(x):
    return x.transpose(1, 0, 2).reshape(x.shape[1], -1)


def _pad_cols(x, width):
    return jnp.pad(x, ((0, 0), (0, width - x.shape[1])))


def _fox_mixer(h, w_in, b_f, w_o, tag):
    hd = FOX_HEADS * FOX_HD
    qkv = _linear(tag + "_qkv", BF16)(h, w_in[:, :3 * hd])
    fg = _linear(tag + "_gate", F32)(h, _pad_cols(w_in[:, 3 * hd:], LANES))
    c = _decay(tag + "_decay")(fg, _pad_cols(b_f[None, :], LANES))
    q, k, v = (_heads(qkv[:, n * hd:(n + 1) * hd], FOX_HEADS, FOX_HD) for n in range(3))
    o = _fox_attention(tag + "_attn")(q, k, v, c[:, :FOX_HEADS].T)
    return _linear(tag + "_out", F32)(_unheads(o), w_o)


def _swa_mixer(h, w_in, sinks, w_o, tabs, tag):
    L = h.shape[0]
    qd, kd = SWA_QH * SWA_HD, SWA_KVH * SWA_HD
    proj = _linear(tag + "_qkv", F32)(h, w_in)
    rope = _rope(tag + "_rope", ROPE_DIM // 2, BF16)
    q = rope(proj[:, :qd], *tabs)
    k = rope(proj[:, qd:qd + kd], *tabs)
    v = proj[:, qd + kd:].astype(BF16)
    qg = _heads(q, SWA_QH, SWA_HD).reshape(SWA_KVH, SWA_G, L, SWA_HD)
    o = _swa_attn(tag + "_attn", SWA_HD ** -0.5)(qg, _heads(k, SWA_KVH, SWA_HD), _heads(v, SWA_KVH, SWA_HD),
                                                 sinks.reshape(SWA_KVH, SWA_G, 1, 1))
    return _linear(tag + "_out", F32)(_unheads(o.reshape(SWA_QH, L, SWA_HD)), w_o)


def _mla_mixer(h, w_a, g_q, g_kv, w_uq, w_ukv, w_o, tabs, tag):
    L = h.shape[0]
    cq = _linear(tag + "_aq", F32)(h, w_a[:, :MLA_QL])
    ckv = _linear(tag + "_akv", F32)(h, w_a[:, MLA_QL:MLA_QL + MLA_KVL])
    kr = _linear(tag + "_akr", F32)(h, _pad_cols(w_a[:, MLA_QL + MLA_KVL:], LANES))
    cq = _rmsnorm(tag + "_nq")(cq, g_q[None, :])
    ckv = _rmsnorm(tag + "_nkv")(ckv, g_kv[None, :])
    q = _linear(tag + "_uq", F32)(cq, w_uq).reshape(L, MLA_HEADS, MLA_NOPE + MLA_ROPE)
    kv = _linear(tag + "_ukv", BF16)(ckv, w_ukv).reshape(L, MLA_HEADS, MLA_NOPE + MLA_V)
    rope = _rope(tag + "_rope", MLA_ROPE // 2, BF16)
    q_rope = rope(q[:, :, MLA_NOPE:].reshape(L, MLA_HEADS * MLA_ROPE), *tabs).reshape(L, MLA_HEADS, MLA_ROPE)
    k_rope = rope(kr, *tabs)[:, :MLA_ROPE]
    qf = jnp.concatenate([q[:, :, :MLA_NOPE].astype(BF16), q_rope], axis=-1).transpose(1, 0, 2)
    kf = jnp.concatenate([kv[:, :, :MLA_NOPE], jnp.broadcast_to(k_rope[:, None, :], (L, MLA_HEADS, MLA_ROPE))],
                         axis=-1).transpose(1, 0, 2)
    v = kv[:, :, MLA_NOPE:].transpose(1, 0, 2)
    o = _mla_attention(tag + "_attn")(qf, kf, v)
    return _linear(tag + "_out", F32)(_unheads(o), w_o)


BIG = (("fox_w_in", 2), ("fox_w_o", 1), ("swa_w_in", 2), ("swa_w_o", 1), ("mla_w_a", 1), ("mla_w_uq", 2),
       ("mla_w_ukv", 2), ("mla_w_o", 1), ("ffn_w_in", 2), ("ffn_w_out", 1))
SMALL = (("meta_tokens", 1), ("mla_g_q", 1), ("mla_g_kv", 1), ("ffn_conv_w", 2))
SHARDED = BIG + SMALL
REPL = ("ln1_g", "ln1_b", "ln2_g", "ln2_b", "fox_b_f", "swa_sinks", "ffn_conv_b")
FFN_STACKED = ("ffn_w_in", "ffn_w_out", "ffn_conv_w")
WEIGHTS = ("meta_tokens", "ln1_g", "ln1_b", "ln2_g", "ln2_b", "fox_w_in", "fox_b_f", "fox_w_o", "swa_w_in", "swa_sinks",
           "swa_w_o", "mla_w_a", "mla_g_q", "mla_g_kv", "mla_w_uq", "mla_w_ukv", "mla_w_o", "ffn_w_in", "ffn_conv_w",
           "ffn_conv_b", "ffn_w_out")


def _local_loss(stacked, repl, x, target):
    S, D = x.shape
    L = S + BLOCK
    wts = {n: _to_full(stacked[n], ax) for n, ax in SHARDED if n not in FFN_STACKED}
    wts.update(repl)
    half = N_DEV // 2
    h = jnp.concatenate([jnp.zeros((PAD, D), F32), wts["meta_tokens"], x], axis=0)
    cos_p, sin_p = _rope_tables(L, ROPE_DIM, ROPE_THETA)
    tabs_p = _rope_lanes(cos_p, sin_p, SWA_HD)
    cos_m, sin_m = _rope_tables(L, MLA_ROPE, MLA_ROPE_THETA)
    tabs_m = _rope_lanes(cos_m, sin_m, MLA_ROPE)
    for i in range(DEPTH):
        kind, j = i % 3, i // 3
        tag = f"l{i}"
        if kind == 0:
            mix = _fox_mixer(h, wts["fox_w_in"][j], wts["fox_b_f"][j], wts["fox_w_o"][j], tag + "_fox")
        elif kind == 1:
            mix = _swa_mixer(h, wts["swa_w_in"][j], wts["swa_sinks"][j], wts["swa_w_o"][j], tabs_p, tag + "_swa")
        else:
            mix = _mla_mixer(h, wts["mla_w_a"][j], wts["mla_g_q"][j], wts["mla_g_kv"][j], wts["mla_w_uq"][j],
                             wts["mla_w_ukv"][j], wts["mla_w_o"][j], tabs_m, tag + "_mla")
        h = _deepnorm(tag + "_ln1")(h, mix, wts["ln1_g"][i][None, :], wts["ln1_b"][i][None, :])
        w3 = stacked["ffn_w_in"][:, i]
        shard = w3.shape[-1]
        cw4 = stacked["ffn_conv_w"][:, i].reshape(2, half, CONV_W, shard)
        cb4 = wts["ffn_conv_b"][i].reshape(2, half, 1, shard)
        wo3 = stacked["ffn_w_out"][:, i].reshape(half, shard, D)
        act3 = _ffn_up(tag + "_ffn")(h, w3, cw4, cb4)
        ffn = _ffn_down(tag + "_ffn_down")(act3, wo3)
        h = _deepnorm(tag + "_ln2")(h, ffn, wts["ln2_g"][i][None, :], wts["ln2_b"][i][None, :])
    return _loss_head("loss_head")(h, target)


def _step(x, target, w, m, v):
    names = [n for n, _ in SHARDED]
    big = {n for n, _ in BIG}
    stacked = dict(zip(names, _exchange([w[n].astype(BF16) if n in big else w[n] for n in names], True, "gather_weights")))
    repl = {n: w[n] for n in REPL}
    loss, (g_st, g_repl, grad_x) = jax.value_and_grad(_local_loss, argnums=(0, 1, 2))(stacked, repl, x, target)
    got = dict(zip(names, _exchange([g_st[n] for n in names], False, "scatter_grads")))
    got.update(zip(REPL, _exchange([g_repl[n] for n in REPL], True, "gather_repl_grads")))
    out = {}
    for n in WEIGHTS:
        for kind, a in zip(("grad", "delta", "new_m", "new_v"), _adamw_call(got[n], w[n], m[n], v[n], "adamw_" + n)):
            out[(kind, n)] = a
    return loss, grad_x, out


def kernel(x, meta_tokens, ln1_g, ln1_b, ln2_g, ln2_b, fox_w_in, fox_b_f, fox_w_o, swa_w_in, swa_sinks, swa_w_o, mla_w_a, mla_g_q, mla_g_kv, mla_w_uq, mla_w_ukv, mla_w_o, ffn_w_in, ffn_conv_w, ffn_conv_b, ffn_w_out, loss_target, m_meta_tokens, m_ln1_g, m_ln1_b, m_ln2_g, m_ln2_b, m_fox_w_in, m_fox_b_f, m_fox_w_o, m_swa_w_in, m_swa_sinks, m_swa_w_o, m_mla_w_a, m_mla_g_q, m_mla_g_kv, m_mla_w_uq, m_mla_w_ukv, m_mla_w_o, m_ffn_w_in, m_ffn_conv_w, m_ffn_conv_b, m_ffn_w_out, v_meta_tokens, v_ln1_g, v_ln1_b, v_ln2_g, v_ln2_b, v_fox_w_in, v_fox_b_f, v_fox_w_o, v_swa_w_in, v_swa_sinks, v_swa_w_o, v_mla_w_a, v_mla_g_q, v_mla_g_kv, v_mla_w_uq, v_mla_w_ukv, v_mla_w_o, v_ffn_w_in, v_ffn_conv_w, v_ffn_conv_b, v_ffn_w_out):
    args = (meta_tokens, ln1_g, ln1_b, ln2_g, ln2_b, fox_w_in, fox_b_f, fox_w_o, swa_w_in, swa_sinks, swa_w_o, mla_w_a,
            mla_g_q, mla_g_kv, mla_w_uq, mla_w_ukv, mla_w_o, ffn_w_in, ffn_conv_w, ffn_conv_b, ffn_w_out)
    ms = (m_meta_tokens, m_ln1_g, m_ln1_b, m_ln2_g, m_ln2_b, m_fox_w_in, m_fox_b_f, m_fox_w_o, m_swa_w_in, m_swa_sinks,
          m_swa_w_o, m_mla_w_a, m_mla_g_q, m_mla_g_kv, m_mla_w_uq, m_mla_w_ukv, m_mla_w_o, m_ffn_w_in, m_ffn_conv_w,
          m_ffn_conv_b, m_ffn_w_out)
    vs = (v_meta_tokens, v_ln1_g, v_ln1_b, v_ln2_g, v_ln2_b, v_fox_w_in, v_fox_b_f, v_fox_w_o, v_swa_w_in, v_swa_sinks,
          v_swa_w_o, v_mla_w_a, v_mla_g_q, v_mla_g_kv, v_mla_w_uq, v_mla_w_ukv, v_mla_w_o, v_ffn_w_in, v_ffn_conv_w,
          v_ffn_conv_b, v_ffn_w_out)
    w = dict(zip(WEIGHTS, args))
    m = dict(zip(WEIGHTS, ms))
    v = dict(zip(WEIGHTS, vs))
    loss, grad_x, out = _step(x[0], loss_target[0], w, m, v)
    loss = lax.psum(loss, ("x", "y", "c"))
    res = [loss, grad_x[None]]
    for kind in ("grad", "delta", "new_m", "new_v"):
        res += [out[(kind, n)] for n in WEIGHTS]
    return tuple(res)
```

```python
import jax
import jax.numpy as jnp
import numpy as np
from jax import lax
from jax.experimental import pallas as pl
from jax.experimental.pallas import tpu as pltpu

F32 = jnp.float32
BF16 = jnp.bfloat16

D_MODEL = 1024
DEPTH = 4
N_META = 16
BLOCK = 128
PAD = BLOCK - N_META
NEG = -1e30
ALPHA = (2.0 * DEPTH) ** 0.25
LN_EPS = 1e-5
RMS_EPS = 1e-6
FOX_HEADS, FOX_HD = 16, 64
SWA_QH, SWA_KVH, SWA_HD = 16, 2, 64
SWA_G = SWA_QH // SWA_KVH
ROPE_THETA = 500000.0
ROPE_DIM = SWA_HD // 4
MLA_HEADS, MLA_QL, MLA_KVL, MLA_NOPE, MLA_ROPE, MLA_V = 16, 384, 256, 64, 32, 64
MLA_ROPE_THETA = 10000.0
D_FF = 2816
CONV_W = 3
ADAM_LR, ADAM_B1, ADAM_B2, ADAM_EPS, ADAM_WD, ADAM_STEP = 0.001, 0.9, 0.999, 1e-08, 0.01, 10

LANES = 128
SUBLANES_F32 = 8
SUBLANES_BF16 = 16
VMEM_LIMIT = 48 * 1024 * 1024
N_DEV = 8
MESH = pl.DeviceIdType.MESH
ROW_TILE = 640
ADAM_ROWS = 256
MM_TILE_M, MM_TILE_N, MM_TILE_K = 1664, 1024, 1664


def _tile(n, cap):
    if n <= cap:
        return n
    best = 0
    for t in range(LANES, cap + 1, LANES):
        if n % t == 0:
            best = t
    assert best, (n, cap)
    return best


def _params(*sem):
    return pltpu.CompilerParams(dimension_semantics=sem, vmem_limit_bytes=VMEM_LIMIT)


def _mm(a, b, *, la="mk", lb="kn", sa=None, sb=None, so=None, out_dtype, name):
    size, tile, parts = {}, {}, {}
    for x, lay, split in ((a, la, sa), (b, lb, sb)):
        shp = x.shape[1:] if split else x.shape
        for ax, n in zip(lay, shp):
            if ax == split:
                size[ax], tile[ax], parts[ax] = x.shape[0] * n, n, x.shape[0]
            else:
                assert size.setdefault(ax, n) == n, (name, ax, a.shape, b.shape)
    for ax, cap in (("m", MM_TILE_M), ("n", MM_TILE_N), ("k", MM_TILE_K)):
        tile.setdefault(ax, _tile(size[ax], cap))
    grid = tuple(size[ax] // tile[ax] for ax in "mnk")
    nk = grid[2]
    dn = (((la.index("k"),), (lb.index("k"),)), ((), ()))

    def kern(a_ref, b_ref, o_ref, *acc):
        p = lax.dot_general(a_ref[...].astype(BF16), b_ref[...].astype(BF16), dn, preferred_element_type=F32)
        if nk == 1:
            o_ref[...] = p.astype(o_ref.dtype)
            return
        acc_ref, = acc
        k = pl.program_id(2)

        @pl.when(k == 0)
        def _():
            acc_ref[...] = p

        @pl.when(k > 0)
        def _():
            acc_ref[...] += p

        @pl.when(k == nk - 1)
        def _():
            o_ref[...] = acc_ref[...].astype(o_ref.dtype)

    def spec(lay, split):
        blk = tuple(tile[ax] for ax in lay)
        if split is None:
            return pl.BlockSpec(blk, lambda i, j, k: tuple({"m": i, "n": j, "k": k}[ax] for ax in lay))
        return pl.BlockSpec((None,) + blk, lambda i, j, k: ({"m": i, "n": j, "k": k}[split],) + tuple(
            0 if ax == split else {"m": i, "n": j, "k": k}[ax] for ax in lay))

    if so is None:
        out_shape = (size["m"], size["n"])
    else:
        out_shape = (parts[so],) + tuple(tile[ax] if ax == so else size[ax] for ax in "mn")
    return pl.pallas_call(
        kern, name=name, grid=grid,
        in_specs=[spec(la, sa), spec(lb, sb)], out_specs=spec("mn", so),
        out_shape=jax.ShapeDtypeStruct(out_shape, out_dtype),
        scratch_shapes=[] if nk == 1 else [pltpu.VMEM((tile["m"], tile["n"]), F32)],
        compiler_params=_params("parallel", "parallel", "arbitrary"),
    )(a, b)


def _linear(name, out_dtype):
    @jax.custom_vjp
    def f(x, w):
        return _mm(x, w, out_dtype=out_dtype, name=name + "_fwd")

    def fwd(x, w):
        return f(x, w), (x, w)

    def bwd(res, g):
        x, w = res
        dx = _mm(g, w, lb="nk", out_dtype=x.dtype, name=name + "_dx")
        dw = _mm(x, g, la="km", out_dtype=w.dtype, name=name + "_dw")
        return dx, dw

    f.defvjp(fwd, bwd)
    return f


def _ln_stats(z):
    mu = jnp.mean(z, axis=-1, keepdims=True)
    zc = z - mu
    var = jnp.mean(zc * zc, axis=-1, keepdims=True)
    return zc, lax.rsqrt(var + LN_EPS)


def _ln_fwd_call(h, mix, g, b, name):
    L, D = h.shape
    tm = _tile(L, ROW_TILE)

    def kern(h_ref, m_ref, g_ref, b_ref, o_ref):
        zc, rstd = _ln_stats(ALPHA * h_ref[...] + m_ref[...])
        o_ref[...] = zc * rstd * g_ref[...] + b_ref[...]

    row = pl.BlockSpec((tm, D), lambda i: (i, 0))
    vec = pl.BlockSpec((1, D), lambda i: (0, 0))
    return pl.pallas_call(kern, name=name, grid=(L // tm,), in_specs=[row, row, vec, vec], out_specs=row,
                          out_shape=jax.ShapeDtypeStruct((L, D), F32), compiler_params=_params("parallel"))(h, mix, g, b)


def _ln_bwd_call(h, mix, g, dout, name):
    L, D = h.shape
    tm = _tile(L, ROW_TILE)

    def kern(h_ref, m_ref, g_ref, d_ref, dz_ref, dg_ref, db_ref):
        i = pl.program_id(0)
        zc, rstd = _ln_stats(ALPHA * h_ref[...] + m_ref[...])
        xhat = zc * rstd
        d = d_ref[...]
        dxh = d * g_ref[...]
        m1 = jnp.mean(dxh, axis=-1, keepdims=True)
        m2 = jnp.mean(dxh * xhat, axis=-1, keepdims=True)
        row = i * tm + lax.broadcasted_iota(jnp.int32, (tm, 1), 0)
        dz_ref[...] = jnp.where(row >= PAD, rstd * (dxh - m1 - xhat * m2), 0.0)
        pg = jnp.sum(d * xhat, axis=0, keepdims=True)
        pb = jnp.sum(d, axis=0, keepdims=True)

        @pl.when(i == 0)
        def _():
            dg_ref[...] = pg
            db_ref[...] = pb

        @pl.when(i > 0)
        def _():
            dg_ref[...] += pg
            db_ref[...] += pb

    row = pl.BlockSpec((tm, D), lambda i: (i, 0))
    vec = pl.BlockSpec((1, D), lambda i: (0, 0))
    return pl.pallas_call(
        kern, name=name, grid=(L // tm,), in_specs=[row, row, vec, row], out_specs=[row, vec, vec],
        out_shape=[jax.ShapeDtypeStruct((L, D), F32), jax.ShapeDtypeStruct((1, D), F32), jax.ShapeDtypeStruct((1, D), F32)],
        compiler_params=_params("arbitrary"))(h, mix, g, dout)


def _deepnorm(name):
    @jax.custom_vjp
    def f(h, mix, g, b):
        return _ln_fwd_call(h, mix, g, b, name + "_fwd")

    def fwd(h, mix, g, b):
        return f(h, mix, g, b), (h, mix, g)

    def bwd(res, dout):
        h, mix, g = res
        dz, dg, db = _ln_bwd_call(h, mix, g, dout, name + "_bwd")
        return ALPHA * dz, dz, dg, db

    f.defvjp(fwd, bwd)
    return f


def _rms_fwd_call(x, g, name):
    L, n = x.shape
    tm = _tile(L, ROW_TILE)

    def kern(x_ref, g_ref, o_ref):
        x = x_ref[...]
        o_ref[...] = x * lax.rsqrt(jnp.mean(x * x, axis=-1, keepdims=True) + RMS_EPS) * g_ref[...]

    row = pl.BlockSpec((tm, n), lambda i: (i, 0))
    vec = pl.BlockSpec((1, n), lambda i: (0, 0))
    return pl.pallas_call(kern, name=name, grid=(L // tm,), in_specs=[row, vec], out_specs=row,
                          out_shape=jax.ShapeDtypeStruct((L, n), F32), compiler_params=_params("parallel"))(x, g)


def _rms_bwd_call(x, g, dout, name):
    L, n = x.shape
    tm = _tile(L, ROW_TILE)

    def kern(x_ref, g_ref, d_ref, dx_ref, dg_ref):
        i = pl.program_id(0)
        x = x_ref[...]
        rstd = lax.rsqrt(jnp.mean(x * x, axis=-1, keepdims=True) + RMS_EPS)
        xhat = x * rstd
        d = d_ref[...]
        dxh = d * g_ref[...]
        dx_ref[...] = rstd * (dxh - xhat * jnp.mean(dxh * xhat, axis=-1, keepdims=True))
        pg = jnp.sum(d * xhat, axis=0, keepdims=True)

        @pl.when(i == 0)
        def _():
            dg_ref[...] = pg

        @pl.when(i > 0)
        def _():
            dg_ref[...] += pg

    row = pl.BlockSpec((tm, n), lambda i: (i, 0))
    vec = pl.BlockSpec((1, n), lambda i: (0, 0))
    return pl.pallas_call(
        kern, name=name, grid=(L // tm,), in_specs=[row, vec, row], out_specs=[row, vec],
        out_shape=[jax.ShapeDtypeStruct((L, n), F32), jax.ShapeDtypeStruct((1, n), F32)],
        compiler_params=_params("arbitrary"))(x, g, dout)


def _rmsnorm(name):
    @jax.custom_vjp
    def f(x, g):
        return _rms_fwd_call(x, g, name + "_fwd")

    def fwd(x, g):
        return f(x, g), (x, g)

    def bwd(res, dout):
        x, g = res
        dx, dg = _rms_bwd_call(x, g, dout, name + "_bwd")
        return dx, dg

    f.defvjp(fwd, bwd)
    return f


def _rope_call(x, c, s1, s2, r, out_dtype, name):
    L, W = x.shape
    reps = W // LANES
    tm = _tile(L, ROW_TILE)

    def kern(x_ref, c_ref, s1_ref, s2_ref, o_ref):
        x = x_ref[...].astype(F32)
        wide = lambda t: jnp.tile(t[...], (1, reps)) if reps > 1 else t[...]
        out = x * wide(c_ref) + pltpu.roll(x, W - r, 1) * wide(s1_ref) + pltpu.roll(x, r, 1) * wide(s2_ref)
        o_ref[...] = out.astype(o_ref.dtype)

    row = pl.BlockSpec((tm, W), lambda i: (i, 0))
    tab = pl.BlockSpec((tm, LANES), lambda i: (i, 0))
    return pl.pallas_call(kern, name=name, grid=(L // tm,), in_specs=[row, tab, tab, tab], out_specs=row,
                          out_shape=jax.ShapeDtypeStruct((L, W), out_dtype), compiler_params=_params("parallel"))(x, c, s1, s2)


def _rope(name, r, out_dtype):
    @jax.custom_vjp
    def f(x, c, s1, s2):
        return _rope_call(x, c, s1, s2, r, out_dtype, name + "_fwd")

    def fwd(x, c, s1, s2):
        return f(x, c, s1, s2), (c, s1, s2, jnp.zeros((), x.dtype))

    def bwd(res, g):
        c, s1, s2, proto = res
        dx = _rope_call(g, c, -s1, -s2, r, proto.dtype, name + "_bwd")
        return dx, jnp.zeros_like(c), jnp.zeros_like(s1), jnp.zeros_like(s2)

    f.defvjp(fwd, bwd)
    return f


def _rope_tables(L, dim, theta):
    pos = (jnp.arange(L) - PAD).astype(F32)
    inv = theta ** (-jnp.arange(0, dim, 2, dtype=F32) / dim)
    ang = pos[:, None] * inv[None, :]
    return jnp.cos(ang), jnp.sin(ang)


def _rope_lanes(cos, sin, period):
    L, half = cos.shape
    one = jnp.ones((L, period - 2 * half), F32)
    zero = jnp.zeros((L, period - 2 * half), F32)
    z_h = jnp.zeros((L, half), F32)
    c = jnp.concatenate([cos, cos, one], axis=1)
    s1 = jnp.concatenate([-sin, z_h, zero], axis=1)
    s2 = jnp.concatenate([z_h, sin, zero], axis=1)
    reps = LANES // period
    return tuple(jnp.tile(t, (1, reps)) for t in (c, s1, s2))


HALO = SUBLANES_BF16


def _conv_taps(buf_ref, off, rows, w, b):
    return (b + buf_ref[pl.ds(off - 2, rows), :] * w[0:1, :]
            + buf_ref[pl.ds(off - 1, rows), :] * w[1:2, :] + buf_ref[pl.ds(off, rows), :] * w[2:3, :])


def _glu_fwd_call(u4, cw4, cb4, name):
    _, P, L, C = u4.shape
    tm = _tile(L, ROW_TILE)
    hb = tm // HALO

    def kern(u_ref, up_ref, w_ref, b_ref, o_ref, buf):
        i = pl.program_id(1)
        row = i * tm + lax.broadcasted_iota(jnp.int32, (tm, 1), 0)
        prow = i * tm - HALO + lax.broadcasted_iota(jnp.int32, (HALO, 1), 0)
        ys = []
        for g in range(2):
            buf[g, pl.ds(HALO, tm), :] = jnp.where(row >= PAD, u_ref[g], 0.0)
            buf[g, pl.ds(0, HALO), :] = jnp.where(prow >= PAD, up_ref[g], 0.0)
            ys.append(_conv_taps(buf.at[g], HALO, tm, w_ref[g], b_ref[g]))
        gate, val = ys
        o_ref[...] = (gate * jax.nn.sigmoid(gate) * val).astype(o_ref.dtype)

    return pl.pallas_call(
        kern, name=name, grid=(P, L // tm),
        in_specs=[pl.BlockSpec((2, None, tm, C), lambda p, i: (0, p, i, 0)),
                  pl.BlockSpec((2, None, HALO, C), lambda p, i: (0, p, jnp.maximum(i * hb - 1, 0), 0)),
                  pl.BlockSpec((2, None, CONV_W, C), lambda p, i: (0, p, 0, 0)),
                  pl.BlockSpec((2, None, 1, C), lambda p, i: (0, p, 0, 0))],
        out_specs=pl.BlockSpec((None, tm, C), lambda p, i: (p, i, 0)),
        out_shape=jax.ShapeDtypeStruct((P, L, C), BF16),
        scratch_shapes=[pltpu.VMEM((2, tm + HALO, C), F32)],
        compiler_params=_params("parallel", "parallel"))(u4, u4, cw4, cb4)


def _glu_bwd_call(u4, cw4, cb4, dact, name):
    _, P, L, C = u4.shape
    tm = _tile(L, ROW_TILE)
    hb = tm // HALO
    ext = tm + HALO
    last_halo = L // HALO - 1

    def kern(u_ref, up_ref, un_ref, w_ref, b_ref, d_ref, dn_ref, du_ref, dw_ref, db_ref, ubuf, dybuf):
        i = pl.program_id(1)
        r0 = i * tm
        mask = lambda blk, start: jnp.where(
            (start + lax.broadcasted_iota(jnp.int32, (blk.shape[0], 1), 0) >= PAD), blk, 0.0)
        ys = []
        for g in range(2):
            ubuf[g, pl.ds(0, HALO), :] = mask(up_ref[g], r0 - HALO)
            ubuf[g, pl.ds(HALO, tm), :] = mask(u_ref[g], r0)
            ubuf[g, pl.ds(HALO + tm, HALO), :] = un_ref[g]
            ys.append(_conv_taps(ubuf.at[g], HALO, ext, w_ref[g], b_ref[g]))
        gate, val = ys
        sg = jax.nn.sigmoid(gate)
        d = jnp.concatenate([d_ref[...], dn_ref[...]], axis=0).astype(F32)
        erow = r0 + lax.broadcasted_iota(jnp.int32, (ext, 1), 0)
        d = jnp.where(erow < L, d, 0.0)
        dybuf[0] = d * val * (sg * (1.0 + gate * (1.0 - sg)))
        dybuf[1] = d * (gate * sg)
        row = r0 + lax.broadcasted_iota(jnp.int32, (tm, 1), 0)
        for g in range(2):
            w = w_ref[g]
            du = (dybuf[g, pl.ds(0, tm), :] * w[2:3, :] + dybuf[g, pl.ds(1, tm), :] * w[1:2, :]
                  + dybuf[g, pl.ds(2, tm), :] * w[0:1, :])
            du_ref[g] = jnp.where(row >= PAD, du, 0.0).astype(du_ref.dtype)
            dy = dybuf[g, pl.ds(0, tm), :]
            pw = jnp.concatenate([jnp.sum(dy * ubuf[g, pl.ds(HALO - 2 + t, tm), :], axis=0, keepdims=True)
                                  for t in range(CONV_W)], axis=0)
            pb = jnp.sum(dy, axis=0, keepdims=True)

            @pl.when(i == 0)
            def _():
                dw_ref[g] = pw
                db_ref[g] = pb

            @pl.when(i > 0)
            def _():
                dw_ref[g] += pw
                db_ref[g] += pb

    nxt = lambda i: jnp.minimum((i + 1) * hb, last_halo)
    return pl.pallas_call(
        kern, name=name, grid=(P, L // tm),
        in_specs=[pl.BlockSpec((2, None, tm, C), lambda p, i: (0, p, i, 0)),
                  pl.BlockSpec((2, None, HALO, C), lambda p, i: (0, p, jnp.maximum(i * hb - 1, 0), 0)),
                  pl.BlockSpec((2, None, HALO, C), lambda p, i: (0, p, nxt(i), 0)),
                  pl.BlockSpec((2, None, CONV_W, C), lambda p, i: (0, p, 0, 0)),
                  pl.BlockSpec((2, None, 1, C), lambda p, i: (0, p, 0, 0)),
                  pl.BlockSpec((None, tm, C), lambda p, i: (p, i, 0)),
                  pl.BlockSpec((None, HALO, C), lambda p, i: (p, nxt(i), 0))],
        out_specs=[pl.BlockSpec((2, None, tm, C), lambda p, i: (0, p, i, 0)),
                   pl.BlockSpec((2, None, CONV_W, C), lambda p, i: (0, p, 0, 0)),
                   pl.BlockSpec((2, None, 1, C), lambda p, i: (0, p, 0, 0))],
        out_shape=[jax.ShapeDtypeStruct((2, P, L, C), BF16), jax.ShapeDtypeStruct((2, P, CONV_W, C), F32),
                   jax.ShapeDtypeStruct((2, P, 1, C), F32)],
        scratch_shapes=[pltpu.VMEM((2, tm + 2 * HALO, C), F32), pltpu.VMEM((2, ext, C), F32)],
        compiler_params=_params("parallel", "arbitrary"))(u4, u4, u4, cw4, cb4, dact, dact)


def _ffn_up(name):
    def run(h1, w3, cw4, cb4):
        u3 = _mm(h1, w3, sb="n", so="n", out_dtype=F32, name=name + "_up")
        u4 = u3.reshape((2, u3.shape[0] // 2) + u3.shape[1:])
        return _glu_fwd_call(u4, cw4, cb4, name + "_glu"), u4

    @jax.custom_vjp
    def f(h1, w3, cw4, cb4):
        return run(h1, w3, cw4, cb4)[0]

    def fwd(h1, w3, cw4, cb4):
        act, u4 = run(h1, w3, cw4, cb4)
        return act, (h1, w3, cw4, cb4, u4)

    def bwd(res, dact):
        h1, w3, cw4, cb4, u4 = res
        du4, dcw, dcb = _glu_bwd_call(u4, cw4, cb4, dact, name + "_glu_bwd")
        du3 = du4.reshape((du4.shape[0] * du4.shape[1],) + du4.shape[2:])
        dh1 = _mm(du3, w3, sa="k", lb="nk", sb="k", out_dtype=F32, name=name + "_up_dx")
        dw3 = _mm(h1, du3, la="km", sb="n", so="n", out_dtype=w3.dtype, name=name + "_up_dw")
        return dh1, dw3, dcw, dcb

    f.defvjp(fwd, bwd)
    return f


def _ffn_down(name):
    @jax.custom_vjp
    def f(act3, wo3):
        return _mm(act3, wo3, sa="k", sb="k", out_dtype=F32, name=name + "_fwd")

    def fwd(act3, wo3):
        return f(act3, wo3), (act3, wo3)

    def bwd(res, g):
        act3, wo3 = res
        dact = _mm(g, wo3, lb="nk", sb="n", so="n", out_dtype=act3.dtype, name=name + "_dx")
        dwo = _mm(act3, g, la="km", sa="m", so="m", out_dtype=wo3.dtype, name=name + "_dw")
        return dact, dwo

    f.defvjp(fwd, bwd)
    return f


def _pairs(n, kv_major):
    if kv_major:
        pr = [(i, j) for j in range(n) for i in range(j, n)]
    else:
        pr = [(i, j) for i in range(n) for j in range(i + 1)]
    return (jnp.asarray(np.array([p[0] for p in pr], np.int32)), jnp.asarray(np.array([p[1] for p in pr], np.int32)))


TN = (((0,), (0,)), ((), ()))
NT = (((1,), (1,)), ((), ()))


def _scores(kT, qT, i, j, tq, tk, masked):
    s = lax.dot_general(kT, qT, TN, preferred_element_type=F32)
    if not masked:
        return s, None
    col = j * tk + lax.broadcasted_iota(jnp.int32, (tk, tq), 0)
    row = i * tq + lax.broadcasted_iota(jnp.int32, (tk, tq), 1)
    mask = (col <= row) & (col >= PAD)
    return jnp.where(mask, s, NEG), mask


def _flash_fwd_call(qT, kT, vT, name):
    H, dk, L = qT.shape
    dv = vT.shape[1]
    tq = tk = _tile(L, ROW_TILE)
    nq = L // tq
    it, jt = _pairs(nq, False)

    def kern(it_ref, jt_ref, q_ref, k_ref, v_ref, o_ref, lse_ref, m_s, l_s, acc_s):
        t = pl.program_id(1)
        i, j = it_ref[t], jt_ref[t]

        @pl.when(j == 0)
        def _():
            m_s[...] = jnp.full_like(m_s, NEG)
            l_s[...] = jnp.zeros_like(l_s)
            acc_s[...] = jnp.zeros_like(acc_s)

        def step(masked):
            s, mask = _scores(k_ref[...], q_ref[...], i, j, tq, tk, masked)
            m_prev = m_s[...]
            m_new = jnp.maximum(m_prev, jnp.max(s, axis=0, keepdims=True))
            a = jnp.exp(m_prev - m_new)
            p = jnp.exp(s - m_new)
            if masked:
                p = jnp.where(mask, p, 0.0)
            l_s[...] = a * l_s[...] + jnp.sum(p, axis=0, keepdims=True)
            acc_s[...] = a * acc_s[...] + jnp.dot(v_ref[...], p.astype(BF16), preferred_element_type=F32)
            m_s[...] = m_new

        edge = (j == i) | (j == 0)
        pl.when(edge)(lambda: step(True))
        pl.when(jnp.logical_not(edge))(lambda: step(False))

        @pl.when(j == i)
        def _():
            l = l_s[...]
            l = jnp.where(l == 0.0, 1.0, l)
            o_ref[...] = (acc_s[...] / l).astype(o_ref.dtype)
            lse_ref[...] = m_s[...] + jnp.log(l)

    qmap = lambda h, t, it, jt: (h, 0, it[t])
    kmap = lambda h, t, it, jt: (h, 0, jt[t])
    return pl.pallas_call(
        kern, name=name,
        grid_spec=pltpu.PrefetchScalarGridSpec(
            num_scalar_prefetch=2, grid=(H, int(it.shape[0])),
            in_specs=[pl.BlockSpec((None, dk, tq), qmap), pl.BlockSpec((None, dk, tk), kmap),
                      pl.BlockSpec((None, dv, tk), kmap)],
            out_specs=[pl.BlockSpec((None, dv, tq), qmap), pl.BlockSpec((None, 1, tq), qmap)],
            scratch_shapes=[pltpu.VMEM((1, tq), F32), pltpu.VMEM((1, tq), F32), pltpu.VMEM((dv, tq), F32)]),
        out_shape=[jax.ShapeDtypeStruct((H, dv, L), F32), jax.ShapeDtypeStruct((H, 1, L), F32)],
        compiler_params=_params("parallel", "arbitrary"))(it, jt, qT, kT, vT)


def _flash_bwd_call(qT, kT, vT, oT, lse, doT, name):
    H, dk, L = qT.shape
    dv = vT.shape[1]
    tq = tk = _tile(L, ROW_TILE)
    nq = L // tq
    it, jt = _pairs(nq, True)

    def kern(it_ref, jt_ref, q_ref, k_ref, v_ref, o_ref, do_ref, lse_ref, dq_ref, dk_ref, dv_ref, dk_s, dv_s):
        t = pl.program_id(1)
        i, j = it_ref[t], jt_ref[t]

        @pl.when(t == 0)
        def _():
            dq_ref[...] = jnp.zeros_like(dq_ref)

        @pl.when(i == j)
        def _():
            dk_s[...] = jnp.zeros_like(dk_s)
            dv_s[...] = jnp.zeros_like(dv_s)

        def step(masked):
            qb, kb = q_ref[...], k_ref[...]
            s, mask = _scores(kb, qb, i, j, tq, tk, masked)
            p = jnp.exp(s - lse_ref[...])
            if masked:
                p = jnp.where(mask, p, 0.0)
            dof = do_ref[...]
            dob = dof.astype(BF16)
            delta = jnp.sum(dof * o_ref[...], axis=0, keepdims=True)
            dp = lax.dot_general(v_ref[...], dob, TN, preferred_element_type=F32)
            dsb = (p * (dp - delta)).astype(BF16)
            dv_s[...] += lax.dot_general(dob, p.astype(BF16), NT, preferred_element_type=F32)
            dk_s[...] += lax.dot_general(qb, dsb, NT, preferred_element_type=F32)
            cols = pl.ds(pl.multiple_of(i * tq, tq), tq)
            dq_ref[:, cols] += jnp.dot(kb, dsb, preferred_element_type=F32)

        edge = (j == i) | (j == 0)
        pl.when(edge)(lambda: step(True))
        pl.when(jnp.logical_not(edge))(lambda: step(False))

        @pl.when(i == nq - 1)
        def _():
            dk_ref[...] = dk_s[...]
            dv_ref[...] = dv_s[...].astype(dv_ref.dtype)

    qmap = lambda h, t, it, jt: (h, 0, it[t])
    kmap = lambda h, t, it, jt: (h, 0, jt[t])
    return pl.pallas_call(
        kern, name=name,
        grid_spec=pltpu.PrefetchScalarGridSpec(
            num_scalar_prefetch=2, grid=(H, int(it.shape[0])),
            in_specs=[pl.BlockSpec((None, dk, tq), qmap), pl.BlockSpec((None, dk, tk), kmap),
                      pl.BlockSpec((None, dv, tk), kmap), pl.BlockSpec((None, dv, tq), qmap),
                      pl.BlockSpec((None, dv, tq), qmap), pl.BlockSpec((None, 1, tq), qmap)],
            out_specs=[pl.BlockSpec((None, dk, L), lambda h, t, it, jt: (h, 0, 0)),
                       pl.BlockSpec((None, dk, tk), kmap), pl.BlockSpec((None, dv, tk), kmap)],
            scratch_shapes=[pltpu.VMEM((dk, tk), F32), pltpu.VMEM((dv, tk), F32)]),
        out_shape=[jax.ShapeDtypeStruct((H, dk, L), F32), jax.ShapeDtypeStruct((H, dk, L), F32),
                   jax.ShapeDtypeStruct((H, dv, L), BF16)],
        compiler_params=_params("parallel", "arbitrary"))(it, jt, qT, kT, vT, oT, doT, lse)


def _attention(name):
    @jax.custom_vjp
    def f(qT, kT, vT):
        return _flash_fwd_call(qT, kT, vT, name + "_fwd")[0]

    def fwd(qT, kT, vT):
        oT, lse = _flash_fwd_call(qT, kT, vT, name + "_fwd")
        return oT, (qT, kT, vT, oT, lse)

    def bwd(res, doT):
        qT, kT, vT, oT, lse = res
        dqT, dkT, dvT = _flash_bwd_call(qT, kT, vT, oT, lse, doT, name + "_bwd")
        return dqT.astype(BF16), dkT.astype(BF16), dvT

    f.defvjp(fwd, bwd)
    return f


FOX_SCALE = 0.125
C_PARTS = 3


FOX_ROWS = FOX_HD + SUBLANES_BF16


def _fox_operands(qkvT, cT):
    hd = FOX_HEADS * FOX_HD
    L = qkvT.shape[1]
    qT, kT, vT = (qkvT[n * hd:(n + 1) * hd].reshape(FOX_HEADS, FOX_HD, L) for n in range(3))
    to_bf16 = lambda t: lax.reduce_precision(t, exponent_bits=8, mantissa_bits=7)
    hi = to_bf16(cT)
    lo = to_bf16(cT - hi)
    ll = to_bf16(cT - hi - lo)
    terms = jnp.stack([hi, lo, ll], axis=1).astype(BF16)
    ones = jnp.ones_like(terms)
    fill = jnp.zeros((FOX_HEADS, FOX_ROWS - FOX_HD - 2 * C_PARTS, L), BF16)
    qa = jnp.concatenate([qT * jnp.asarray(FOX_SCALE, BF16), terms, ones, fill], axis=1)
    ka = jnp.concatenate([kT, ones, -terms, fill], axis=1)
    return qa, ka, vT


def _fox_attention(name):
    assert FOX_SCALE == FOX_HD ** -0.5

    def run(qkvT, cT):
        qa, ka, vT = _fox_operands(qkvT, cT)
        oT, lse = _flash_fwd_call(qa, ka, vT, name + "_fwd")
        return oT.reshape(-1, oT.shape[-1]), (qa, ka, vT, oT, lse)

    @jax.custom_vjp
    def f(qkvT, cT):
        return run(qkvT, cT)[0]

    def fwd(qkvT, cT):
        return run(qkvT, cT)

    def bwd(res, doT):
        qa, ka, vT, oT, lse = res
        dqa, dka, dvT = _flash_bwd_call(qa, ka, vT, oT, lse, doT.reshape(oT.shape), name + "_bwd")
        dc = dqa[:, FOX_HD, :] - dka[:, FOX_HD + C_PARTS, :]
        flat = lambda t: t.reshape(-1, t.shape[-1])
        dqkvT = jnp.concatenate([flat((dqa[:, :FOX_HD] * FOX_SCALE).astype(BF16)), flat(dka[:, :FOX_HD].astype(BF16)),
                                 flat(dvT)], axis=0)
        return dqkvT, dc

    f.defvjp(fwd, bwd)
    return f


def _linear_t(name, out_dtype):
    @jax.custom_vjp
    def f(wT, x):
        return _mm(wT, x, lb="nk", out_dtype=out_dtype, name=name + "_fwd")

    def fwd(wT, x):
        return f(wT, x), (wT, x)

    def bwd(res, g):
        wT, x = res
        dwT = _mm(g, x, out_dtype=wT.dtype, name=name + "_dw")
        dx = _mm(g, wT, la="km", out_dtype=x.dtype, name=name + "_dx")
        return dwT, dx

    f.defvjp(fwd, bwd)
    return f


def _linear_km(name):
    @jax.custom_vjp
    def f(aT, w):
        return _mm(aT, w, la="km", out_dtype=F32, name=name + "_fwd")

    def fwd(aT, w):
        return f(aT, w), (aT, w)

    def bwd(res, g):
        aT, w = res
        daT = _mm(w, g, lb="nk", out_dtype=aT.dtype, name=name + "_dx")
        dw = _mm(aT, g, out_dtype=w.dtype, name=name + "_dw")
        return daT, dw

    f.defvjp(fwd, bwd)
    return f


def _swa_parts(q_ref, kc_ref, kp_ref, km_ref, sink_ref, i, scale):
    R = SWA_G * BLOCK
    q = q_ref[...].reshape(R, SWA_HD)
    nt = (((1,), (1,)), ((), ()))
    r = lax.broadcasted_iota(jnp.int32, (R, BLOCK), 0) & (BLOCK - 1)
    c = lax.broadcasted_iota(jnp.int32, (R, BLOCK), 1)
    s_c = jnp.where((c <= r) & (i >= 1), lax.dot_general(q, kc_ref[...], nt, preferred_element_type=F32) * scale, NEG)
    s_p = jnp.where((c > r) & (i >= 2), lax.dot_general(q, kp_ref[...], nt, preferred_element_type=F32) * scale, NEG)
    s_m = jnp.where((c >= PAD) & ((i >= 1) | (c <= r)),
                    lax.dot_general(q, km_ref[...], nt, preferred_element_type=F32) * scale, NEG)
    sink = jnp.broadcast_to(sink_ref[...], (SWA_G, BLOCK, 1)).reshape(R, 1)
    return q, s_c, s_p, s_m, sink


def _swa_specs(L):
    nb = L // BLOCK
    qs = pl.BlockSpec((None, SWA_G, BLOCK, SWA_HD), lambda g, i: (g, 0, i, 0))
    kc = pl.BlockSpec((None, BLOCK, SWA_HD), lambda g, i: (g, i, 0))
    kp = pl.BlockSpec((None, BLOCK, SWA_HD), lambda g, i: (g, jnp.maximum(i - 1, 0), 0))
    km = pl.BlockSpec((None, BLOCK, SWA_HD), lambda g, i: (g, 0, 0))
    sk = pl.BlockSpec((None, SWA_G, 1, 1), lambda g, i: (g, 0, 0, 0))
    ls = pl.BlockSpec((None, SWA_G, BLOCK, 1), lambda g, i: (g, 0, i, 0))
    return nb, qs, kc, kp, km, sk, ls


def _swa_fwd_call(q, k, v, sinks, scale, name):
    _, _, L, _ = q.shape
    nb, qs, kc, kp, km, sk, ls = _swa_specs(L)

    def kern(q_ref, kc_ref, kp_ref, km_ref, vc_ref, vp_ref, vm_ref, sink_ref, o_ref, lse_ref):
        i = pl.program_id(1)
        _, s_c, s_p, s_m, sink = _swa_parts(q_ref, kc_ref, kp_ref, km_ref, sink_ref, i, scale)
        mx = lambda s: jnp.max(s, axis=-1, keepdims=True)
        m = jnp.maximum(jnp.maximum(mx(s_c), mx(s_p)), jnp.maximum(mx(s_m), sink))
        p_c, p_p, p_m = jnp.exp(s_c - m), jnp.exp(s_p - m), jnp.exp(s_m - m)
        sm = lambda p: jnp.sum(p, axis=-1, keepdims=True)
        den = sm(p_c) + sm(p_p) + sm(p_m) + jnp.exp(sink - m)
        inv = 1.0 / den
        pv = lambda p, v_ref: jnp.dot((p * inv).astype(BF16), v_ref[...], preferred_element_type=F32)
        o = pv(p_c, vc_ref) + pv(p_p, vp_ref) + pv(p_m, vm_ref)
        o_ref[...] = o.reshape(SWA_G, BLOCK, SWA_HD)
        lse_ref[...] = (m + jnp.log(den)).reshape(SWA_G, BLOCK, 1)

    return pl.pallas_call(
        kern, name=name, grid=(SWA_KVH, nb), in_specs=[qs, kc, kp, km, kc, kp, km, sk], out_specs=[qs, ls],
        out_shape=[jax.ShapeDtypeStruct((SWA_KVH, SWA_G, L, SWA_HD), F32), jax.ShapeDtypeStruct((SWA_KVH, SWA_G, L, 1), F32)],
        compiler_params=_params("parallel", "parallel"))(q, k, k, k, v, v, v, sinks)


def _swa_bwd_call(q, k, v, sinks, o, lse, do, scale, name):
    _, _, L, _ = q.shape
    nb, qs, kc, kp, km, sk, ls = _swa_specs(L)
    R = SWA_G * BLOCK
    full = pl.BlockSpec((None, L, SWA_HD), lambda g, i: (g, 0, 0))

    def kern(q_ref, kc_ref, kp_ref, km_ref, vc_ref, vp_ref, vm_ref, sink_ref, o_ref, do_ref, lse_ref,
             dq_ref, dk_ref, dv_ref, dsink_ref):
        i = pl.program_id(1)

        @pl.when(i == 0)
        def _():
            dk_ref[...] = jnp.zeros_like(dk_ref)
            dv_ref[...] = jnp.zeros_like(dv_ref)
            dsink_ref[...] = jnp.zeros_like(dsink_ref)

        q, s_c, s_p, s_m, sink = _swa_parts(q_ref, kc_ref, kp_ref, km_ref, sink_ref, i, scale)
        lse = lse_ref[...].reshape(R, 1)
        dof = do_ref[...].reshape(R, SWA_HD)
        dob = dof.astype(BF16)
        delta = jnp.sum(dof * o_ref[...].reshape(R, SWA_HD), axis=-1, keepdims=True)
        nt = (((1,), (1,)), ((), ()))
        tn = (((0,), (0,)), ((), ()))
        cur = pl.ds(pl.multiple_of(i * BLOCK, BLOCK), BLOCK)
        prev = pl.ds(pl.multiple_of(jnp.maximum(i - 1, 0) * BLOCK, BLOCK), BLOCK)
        meta = pl.ds(0, BLOCK)
        dq = jnp.zeros((R, SWA_HD), F32)
        for s, k_ref, v_ref, rows in ((s_c, kc_ref, vc_ref, cur), (s_p, kp_ref, vp_ref, prev), (s_m, km_ref, vm_ref, meta)):
            p = jnp.exp(s - lse)
            dp = lax.dot_general(dob, v_ref[...], nt, preferred_element_type=F32)
            ds = (p * (dp - delta)).astype(BF16)
            dq = dq + jnp.dot(ds, k_ref[...], preferred_element_type=F32)
            dv_ref[rows, :] += lax.dot_general(p.astype(BF16), dob, tn, preferred_element_type=F32)
            dk_ref[rows, :] += lax.dot_general(ds, q, tn, preferred_element_type=F32) * scale
        dq_ref[...] = (dq * scale).reshape(SWA_G, BLOCK, SWA_HD).astype(dq_ref.dtype)
        dsk = -jnp.exp(sink - lse) * delta
        dsink_ref[...] += jnp.sum(dsk.reshape(SWA_G, BLOCK, 1), axis=1, keepdims=True)

    return pl.pallas_call(
        kern, name=name, grid=(SWA_KVH, nb), in_specs=[qs, kc, kp, km, kc, kp, km, sk, qs, qs, ls],
        out_specs=[qs, full, full, sk],
        out_shape=[jax.ShapeDtypeStruct((SWA_KVH, SWA_G, L, SWA_HD), BF16), jax.ShapeDtypeStruct((SWA_KVH, L, SWA_HD), F32),
                   jax.ShapeDtypeStruct((SWA_KVH, L, SWA_HD), F32), jax.ShapeDtypeStruct((SWA_KVH, SWA_G, 1, 1), F32)],
        compiler_params=_params("parallel", "arbitrary"))(q, k, k, k, v, v, v, sinks, o, do, lse)


def _swa_attn(name, scale):
    @jax.custom_vjp
    def f(q, k, v, sinks):
        return _swa_fwd_call(q, k, v, sinks, scale, name + "_fwd")[0]

    def fwd(q, k, v, sinks):
        o, lse = _swa_fwd_call(q, k, v, sinks, scale, name + "_fwd")
        return o, (q, k, v, sinks, o, lse)

    def bwd(res, do):
        q, k, v, sinks, o, lse = res
        dq, dk, dv, dsink = _swa_bwd_call(q, k, v, sinks, o, lse, do, scale, name + "_bwd")
        return dq, dk.astype(BF16), dv.astype(BF16), dsink

    f.defvjp(fwd, bwd)
    return f


def _scan_call(x, bias, mul, pre_logsig, name):
    L, W = x.shape
    tb = _tile(L, ROW_TILE)
    has_mul = mul is not None

    def kern(x_ref, b_ref, *rest):
        if has_mul:
            m_ref, o_ref, tot_ref, carry = rest
        else:
            o_ref, tot_ref, carry = rest
        i = pl.program_id(0)

        @pl.when(i == 0)
        def _():
            carry[...] = jnp.zeros_like(carry)
            tot_ref[...] = jnp.zeros_like(tot_ref)

        z = x_ref[...] + b_ref[...]
        if pre_logsig:
            z = jnp.minimum(z, 0.0) - jnp.log(1.0 + jnp.exp(-jnp.abs(z)))
        row = lax.broadcasted_iota(jnp.int32, (tb, W), 0)
        s = 1
        while s < tb:
            z = z + jnp.where(row >= s, pltpu.roll(z, s, 0), 0.0)
            s *= 2
        z = z + carry[...]
        carry[...] = z[tb - 1:tb, :]
        if has_mul:
            z = z * m_ref[...]
        o_ref[...] = z
        tot_ref[...] += jnp.sum(z, axis=0, keepdims=True)

    row = pl.BlockSpec((tb, W), lambda i: (i, 0))
    vec = pl.BlockSpec((1, W), lambda i: (0, 0))
    return pl.pallas_call(
        kern, name=name, grid=(L // tb,), in_specs=[row, vec] + ([row] if has_mul else []), out_specs=[row, vec],
        out_shape=[jax.ShapeDtypeStruct((L, W), F32), jax.ShapeDtypeStruct((1, W), F32)],
        scratch_shapes=[pltpu.VMEM((1, W), F32)],
        compiler_params=_params("arbitrary"))(*([x, bias] + ([mul] if has_mul else [])))


def _sigmoid_neg_call(x, bias, name):
    L, W = x.shape
    tb = _tile(L, ROW_TILE)

    def kern(x_ref, b_ref, o_ref):
        row = pl.program_id(0) * tb + lax.broadcasted_iota(jnp.int32, (tb, 1), 0)
        o_ref[...] = jnp.where(row >= PAD, jax.nn.sigmoid(-(x_ref[...] + b_ref[...])), 0.0)

    row = pl.BlockSpec((tb, W), lambda i: (i, 0))
    vec = pl.BlockSpec((1, W), lambda i: (0, 0))
    return pl.pallas_call(kern, name=name, grid=(L // tb,), in_specs=[row, vec], out_specs=row,
                          out_shape=jax.ShapeDtypeStruct((L, W), F32), compiler_params=_params("parallel"))(x, bias)


def _decay(name):
    @jax.custom_vjp
    def f(fg, b):
        return _scan_call(fg, b, None, True, name + "_fwd")[0]

    def fwd(fg, b):
        return f(fg, b), (fg, b)

    def bwd(res, dc):
        fg, b = res
        sg = _sigmoid_neg_call(fg, b, name + "_dsig")
        dfg_rev, db = _scan_call(dc[::-1], jnp.zeros_like(b), sg[::-1], False, name + "_bwd")
        return dfg_rev[::-1], db

    f.defvjp(fwd, bwd)
    return f


def _loss_call(hf, target, name):
    L, D = hf.shape
    nb = L // BLOCK

    def kern(h_ref, t_ref, loss_ref, dy_ref, acc):
        i = pl.program_id(0)

        @pl.when(i == 0)
        def _():
            acc[...] = jnp.zeros_like(acc)
            dy_ref[...] = jnp.zeros_like(dy_ref)

        @pl.when(i > 0)
        def _():
            e = h_ref[...] - t_ref[...]
            dy_ref[...] = e * (1.0 / D)
            acc[...] += jnp.sum(e * e, axis=0, keepdims=True)

        @pl.when(i == nb - 1)
        def _():
            loss_ref[...] = jnp.broadcast_to(jnp.sum(acc[...], axis=1, keepdims=True) * (0.5 / D), loss_ref.shape)

    return pl.pallas_call(
        kern, name=name, grid=(nb,),
        in_specs=[pl.BlockSpec((BLOCK, D), lambda i: (i, 0)), pl.BlockSpec((BLOCK, D), lambda i: (jnp.maximum(i - 1, 0), 0))],
        out_specs=[pl.BlockSpec((1, LANES), lambda i: (0, 0)), pl.BlockSpec((BLOCK, D), lambda i: (i, 0))],
        out_shape=[jax.ShapeDtypeStruct((1, LANES), F32), jax.ShapeDtypeStruct((L, D), F32)],
        scratch_shapes=[pltpu.VMEM((1, D), F32)],
        compiler_params=_params("arbitrary"))(hf, target)


def _loss_head(name):
    @jax.custom_vjp
    def f(hf, target):
        return _loss_call(hf, target, name)[0][0, 0]

    def fwd(hf, target):
        loss, dy = _loss_call(hf, target, name)
        return loss[0, 0], (dy, jnp.zeros((), F32))

    def bwd(res, g):
        dy, _ = res
        return dy * g, None

    f.defvjp(fwd, bwd)
    return f


def _me_and_peers():
    x, y, c = lax.axis_index("x"), lax.axis_index("y"), lax.axis_index("c")
    me = 4 * x + 2 * y + c
    peers = []
    for k in range(1, N_DEV):
        px = 1 - x if k & 4 else x
        py = 1 - y if k & 2 else y
        pc = 1 - c if k & 1 else c
        peers.append(((px, py, pc), 4 * px + 2 * py + pc))
    return me, peers


def _exchange(arrs, gather, name):
    n = len(arrs)
    shapes = [tuple(a.shape) if gather else tuple(a.shape[1:]) for a in arrs]

    def body(*refs):
        srcs, outs = refs[:n], refs[n:2 * n]
        send_sems, recv_sems, local_sems = refs[2 * n:]
        me, peers = _me_and_peers()
        piece = lambda a, p: srcs[a] if gather else srcs[a].at[p]
        started = []
        for a in range(n):
            mine = pltpu.make_async_copy(piece(a, me), outs[a].at[me], local_sems.at[a])
            mine.start()
            started.append(mine.wait)
        for k, (dev, pid) in enumerate(peers):
            for a in range(n):
                cp = pltpu.make_async_remote_copy(src_ref=piece(a, pid), dst_ref=outs[a].at[me], send_sem=send_sems.at[a, k],
                                                  recv_sem=recv_sems.at[a, k], device_id=dev, device_id_type=MESH)
                cp.start()
                started.append(cp.wait_send)
        for k, (dev, pid) in enumerate(peers):
            for a in range(n):
                pltpu.make_async_remote_copy(src_ref=piece(a, me), dst_ref=outs[a].at[pid], send_sem=send_sems.at[a, k],
                                             recv_sem=recv_sems.at[a, k], device_id=dev, device_id_type=MESH).wait_recv()
        for wait in started:
            wait()

    hbm = pl.BlockSpec(memory_space=pl.ANY)
    return pl.pallas_call(
        body, name=name, out_shape=[jax.ShapeDtypeStruct((N_DEV,) + s, a.dtype) for s, a in zip(shapes, arrs)],
        in_specs=[hbm] * n, out_specs=[hbm] * n,
        scratch_shapes=[pltpu.SemaphoreType.DMA((n, N_DEV - 1)), pltpu.SemaphoreType.DMA((n, N_DEV - 1)),
                        pltpu.SemaphoreType.DMA((n,))],
    )(*arrs)


def _adamw_call(parts, w, m, v, name):
    shape = w.shape
    cols = shape[-1]
    rows = int(np.prod(shape[:-1]))
    tr = rows
    if rows > ADAM_ROWS:
        tr = max(t for t in range(SUBLANES_F32, ADAM_ROWS + 1, SUBLANES_F32) if rows % t == 0)
    c1 = 1.0 / (1.0 - ADAM_B1 ** ADAM_STEP)
    c2 = 1.0 / (1.0 - ADAM_B2 ** ADAM_STEP)

    def kern(p_ref, w_ref, m_ref, v_ref, g_ref, d_ref, nm_ref, nv_ref):
        g = p_ref[0].astype(F32)
        for p in range(1, N_DEV):
            g = g + p_ref[p].astype(F32)
        nm = ADAM_B1 * m_ref[...] + (1.0 - ADAM_B1) * g
        nv = ADAM_B2 * v_ref[...] + (1.0 - ADAM_B2) * (g * g)
        g_ref[...] = g
        nm_ref[...] = nm
        nv_ref[...] = nv
        d_ref[...] = -ADAM_LR * ((nm * c1) / (jnp.sqrt(nv * c2) + ADAM_EPS) + ADAM_WD * w_ref[...])

    slab = pl.BlockSpec((tr, cols), lambda i: (i, 0))
    flat = lambda t: t.reshape(rows, cols)
    res = pl.pallas_call(
        kern, name=name, grid=(rows // tr,),
        in_specs=[pl.BlockSpec((N_DEV, tr, cols), lambda i: (0, i, 0)), slab, slab, slab], out_specs=[slab] * 4,
        out_shape=[jax.ShapeDtypeStruct((rows, cols), F32)] * 4,
        compiler_params=_params("parallel"))(parts.reshape(N_DEV, rows, cols), flat(w), flat(m), flat(v))
    return [r.reshape(shape) for r in res]


def _to_full(stacked, axis):
    moved = jnp.moveaxis(stacked, 0, axis)
    s = list(moved.shape)
    return moved.reshape(s[:axis] + [s[axis] * s[axis + 1]] + s[axis + 2:])


def _heads(x, h, d):
    return x.reshape(x.shape[0], h, d).transpose(1, 0, 2)


def _unheads(x):
    return x.transpose(1, 0, 2).reshape(x.shape[1], -1)


def _pad_cols(x, width):
    return jnp.pad(x, ((0, 0), (0, width - x.shape[1])))


def _fox_mixer(h, w_in, b_f, w_o, tag):
    hd = FOX_HEADS * FOX_HD
    qkvT = _linear_t(tag + "_qkv", BF16)(w_in[:, :3 * hd].T, h)
    fg = _linear(tag + "_gate", F32)(h, _pad_cols(w_in[:, 3 * hd:], LANES))
    c = _decay(tag + "_decay")(fg, _pad_cols(b_f[None, :], LANES))
    oT = _fox_attention(tag + "_attn")(qkvT, c[:, :FOX_HEADS].T)
    return _linear_km(tag + "_out")(oT, w_o)


def _swa_mixer(h, w_in, sinks, w_o, tabs, tag):
    L = h.shape[0]
    qd, kd = SWA_QH * SWA_HD, SWA_KVH * SWA_HD
    proj = _linear(tag + "_qkv", F32)(h, w_in)
    rope = _rope(tag + "_rope", ROPE_DIM // 2, BF16)
    q = rope(proj[:, :qd], *tabs)
    k = rope(proj[:, qd:qd + kd], *tabs)
    v = proj[:, qd + kd:].astype(BF16)
    qg = _heads(q, SWA_QH, SWA_HD).reshape(SWA_KVH, SWA_G, L, SWA_HD)
    o = _swa_attn(tag + "_attn", SWA_HD ** -0.5)(qg, _heads(k, SWA_KVH, SWA_HD), _heads(v, SWA_KVH, SWA_HD),
                                                 sinks.reshape(SWA_KVH, SWA_G, 1, 1))
    return _linear(tag + "_out", F32)(_unheads(o.reshape(SWA_QH, L, SWA_HD)), w_o)


def _mla_mixer(h, w_a, g_q, g_kv, w_uq, w_ukv, w_o, tabs, tag):
    L = h.shape[0]
    cq = _linear(tag + "_aq", F32)(h, w_a[:, :MLA_QL])
    ckv = _linear(tag + "_akv", F32)(h, w_a[:, MLA_QL:MLA_QL + MLA_KVL])
    kr = _linear(tag + "_akr", F32)(h, _pad_cols(w_a[:, MLA_QL + MLA_KVL:], LANES))
    cq = _rmsnorm(tag + "_nq")(cq, g_q[None, :])
    ckv = _rmsnorm(tag + "_nkv")(ckv, g_kv[None, :])
    q = _linear(tag + "_uq", F32)(cq, w_uq).reshape(L, MLA_HEADS, MLA_NOPE + MLA_ROPE)
    kv = _linear(tag + "_ukv", BF16)(ckv, w_ukv).reshape(L, MLA_HEADS, MLA_NOPE + MLA_V)
    scale = (MLA_NOPE + MLA_ROPE) ** -0.5
    rope = _rope(tag + "_rope", MLA_ROPE // 2, BF16)
    q_rope = rope(q[:, :, MLA_NOPE:].reshape(L, MLA_HEADS * MLA_ROPE), *(t * scale for t in tabs))
    k_rope = rope(kr, *tabs)[:, :MLA_ROPE]
    qf = jnp.concatenate([(q[:, :, :MLA_NOPE] * scale).astype(BF16), q_rope.reshape(L, MLA_HEADS, MLA_ROPE)], axis=-1)
    kf = jnp.concatenate([kv[:, :, :MLA_NOPE], jnp.broadcast_to(k_rope[:, None, :], (L, MLA_HEADS, MLA_ROPE))], axis=-1)
    seq_last = lambda t: t.transpose(1, 2, 0)
    oT = _attention(tag + "_attn")(seq_last(qf), seq_last(kf), seq_last(kv[:, :, MLA_NOPE:]))
    return _linear_km(tag + "_out")(oT.reshape(MLA_HEADS * MLA_V, L), w_o)


BIG = (("fox_w_in", 2), ("fox_w_o", 1), ("swa_w_in", 2), ("swa_w_o", 1), ("mla_w_a", 1), ("mla_w_uq", 2),
       ("mla_w_ukv", 2), ("mla_w_o", 1), ("ffn_w_in", 2), ("ffn_w_out", 1))
SMALL = (("meta_tokens", 1), ("mla_g_q", 1), ("mla_g_kv", 1), ("ffn_conv_w", 2))
SHARDED = BIG + SMALL
REPL = ("ln1_g", "ln1_b", "ln2_g", "ln2_b", "fox_b_f", "swa_sinks", "ffn_conv_b")
FFN_STACKED = ("ffn_w_in", "ffn_w_out", "ffn_conv_w")
WEIGHTS = ("meta_tokens", "ln1_g", "ln1_b", "ln2_g", "ln2_b", "fox_w_in", "fox_b_f", "fox_w_o", "swa_w_in", "swa_sinks",
           "swa_w_o", "mla_w_a", "mla_g_q", "mla_g_kv", "mla_w_uq", "mla_w_ukv", "mla_w_o", "ffn_w_in", "ffn_conv_w",
           "ffn_conv_b", "ffn_w_out")


def _local_loss(stacked, repl, x, target):
    S, D = x.shape
    L = S + BLOCK
    wts = {n: _to_full(stacked[n], ax) for n, ax in SHARDED if n not in FFN_STACKED}
    wts.update(repl)
    half = N_DEV // 2
    h = jnp.concatenate([jnp.zeros((PAD, D), F32), wts["meta_tokens"], x], axis=0)
    cos_p, sin_p = _rope_tables(L, ROPE_DIM, ROPE_THETA)
    tabs_p = _rope_lanes(cos_p, sin_p, SWA_HD)
    cos_m, sin_m = _rope_tables(L, MLA_ROPE, MLA_ROPE_THETA)
    tabs_m = _rope_lanes(cos_m, sin_m, MLA_ROPE)
    for i in range(DEPTH):
        kind, j = i % 3, i // 3
        tag = f"l{i}"
        if kind == 0:
            mix = _fox_mixer(h, wts["fox_w_in"][j], wts["fox_b_f"][j], wts["fox_w_o"][j], tag + "_fox")
        elif kind == 1:
            mix = _swa_mixer(h, wts["swa_w_in"][j], wts["swa_sinks"][j], wts["swa_w_o"][j], tabs_p, tag + "_swa")
        else:
            mix = _mla_mixer(h, wts["mla_w_a"][j], wts["mla_g_q"][j], wts["mla_g_kv"][j], wts["mla_w_uq"][j],
                             wts["mla_w_ukv"][j], wts["mla_w_o"][j], tabs_m, tag + "_mla")
        h = _deepnorm(tag + "_ln1")(h, mix, wts["ln1_g"][i][None, :], wts["ln1_b"][i][None, :])
        w3 = stacked["ffn_w_in"][:, i]
        shard = w3.shape[-1]
        cw4 = stacked["ffn_conv_w"][:, i].reshape(2, half, CONV_W, shard)
        cb4 = wts["ffn_conv_b"][i].reshape(2, half, 1, shard)
        wo3 = stacked["ffn_w_out"][:, i].reshape(half, shard, D)
        act3 = _ffn_up(tag + "_ffn")(h, w3, cw4, cb4)
        ffn = _ffn_down(tag + "_ffn_down")(act3, wo3)
        h = _deepnorm(tag + "_ln2")(h, ffn, wts["ln2_g"][i][None, :], wts["ln2_b"][i][None, :])
    return _loss_head("loss_head")(h, target)


def _step(x, target, w, m, v):
    names = [n for n, _ in SHARDED]
    big = {n for n, _ in BIG}
    stacked = dict(zip(names, _exchange([w[n].astype(BF16) if n in big else w[n] for n in names], True, "gather_weights")))
    repl = {n: w[n] for n in REPL}
    loss, (g_st, g_repl, grad_x) = jax.value_and_grad(_local_loss, argnums=(0, 1, 2))(stacked, repl, x, target)
    got = dict(zip(names, _exchange([g_st[n] for n in names], False, "scatter_grads")))
    got.update(zip(REPL, _exchange([g_repl[n] for n in REPL], True, "gather_repl_grads")))
    out = {}
    for n in WEIGHTS:
        for kind, a in zip(("grad", "delta", "new_m", "new_v"), _adamw_call(got[n], w[n], m[n], v[n], "adamw_" + n)):
            out[(kind, n)] = a
    return loss, grad_x, out


def kernel(x, meta_tokens, ln1_g, ln1_b, ln2_g, ln2_b, fox_w_in, fox_b_f, fox_w_o, swa_w_in, swa_sinks, swa_w_o, mla_w_a, mla_g_q, mla_g_kv, mla_w_uq, mla_w_ukv, mla_w_o, ffn_w_in, ffn_conv_w, ffn_conv_b, ffn_w_out, loss_target, m_meta_tokens, m_ln1_g, m_ln1_b, m_ln2_g, m_ln2_b, m_fox_w_in, m_fox_b_f, m_fox_w_o, m_swa_w_in, m_swa_sinks, m_swa_w_o, m_mla_w_a, m_mla_g_q, m_mla_g_kv, m_mla_w_uq, m_mla_w_ukv, m_mla_w_o, m_ffn_w_in, m_ffn_conv_w, m_ffn_conv_b, m_ffn_w_out, v_meta_tokens, v_ln1_g, v_ln1_b, v_ln2_g, v_ln2_b, v_fox_w_in, v_fox_b_f, v_fox_w_o, v_swa_w_in, v_swa_sinks, v_swa_w_o, v_mla_w_a, v_mla_g_q, v_mla_g_kv, v_mla_w_uq, v_mla_w_ukv, v_mla_w_o, v_ffn_w_in, v_ffn_conv_w, v_ffn_conv_b, v_ffn_w_out):
    args = (meta_tokens, ln1_g, ln1_b, ln2_g, ln2_b, fox_w_in, fox_b_f, fox_w_o, swa_w_in, swa_sinks, swa_w_o, mla_w_a,
            mla_g_q, mla_g_kv, mla_w_uq, mla_w_ukv, mla_w_o, ffn_w_in, ffn_conv_w, ffn_conv_b, ffn_w_out)
    ms = (m_meta_tokens, m_ln1_g, m_ln1_b, m_ln2_g, m_ln2_b, m_fox_w_in, m_fox_b_f, m_fox_w_o, m_swa_w_in, m_swa_sinks,
          m_swa_w_o, m_mla_w_a, m_mla_g_q, m_mla_g_kv, m_mla_w_uq, m_mla_w_ukv, m_mla_w_o, m_ffn_w_in, m_ffn_conv_w,
          m_ffn_conv_b, m_ffn_w_out)
    vs = (v_meta_tokens, v_ln1_g, v_ln1_b, v_ln2_g, v_ln2_b, v_fox_w_in, v_fox_b_f, v_fox_w_o, v_swa_w_in, v_swa_sinks,
          v_swa_w_o, v_mla_w_a, v_mla_g_q, v_mla_g_kv, v_mla_w_uq, v_mla_w_ukv, v_mla_w_o, v_ffn_w_in, v_ffn_conv_w,
          v_ffn_conv_b, v_ffn_w_out)
    w = dict(zip(WEIGHTS, args))
    m = dict(zip(WEIGHTS, ms))
    v = dict(zip(WEIGHTS, vs))
    loss, grad_x, out = _step(x[0], loss_target[0], w, m, v)
    loss = lax.psum(loss, ("x", "y", "c"))
    res = [loss, grad_x[None]]
    for kind in ("grad", "delta", "new_m", "new_v"):
        res += [out[(kind, n)] for n in WEIGHTS]
    return tuple(res)
```

```python
import jax
import jax.numpy as jnp
import numpy as np
from jax import lax
from jax.experimental import pallas as pl
from jax.experimental.pallas import tpu as pltpu

F32 = jnp.float32
BF16 = jnp.bfloat16

D_MODEL = 1024
DEPTH = 4
N_META = 16
BLOCK = 128
PAD = BLOCK - N_META
NEG = -1e30
ALPHA = (2.0 * DEPTH) ** 0.25
LN_EPS = 1e-5
RMS_EPS = 1e-6
FOX_HEADS, FOX_HD = 16, 64
SWA_QH, SWA_KVH, SWA_HD = 16, 2, 64
SWA_G = SWA_QH // SWA_KVH
ROPE_THETA = 500000.0
ROPE_DIM = SWA_HD // 4
MLA_HEADS, MLA_QL, MLA_KVL, MLA_NOPE, MLA_ROPE, MLA_V = 16, 384, 256, 64, 32, 64
MLA_ROPE_THETA = 10000.0
D_FF = 2816
CONV_W = 3
ADAM_LR, ADAM_B1, ADAM_B2, ADAM_EPS, ADAM_WD, ADAM_STEP = 0.001, 0.9, 0.999, 1e-08, 0.01, 10

LANES = 128
SUBLANES_F32 = 8
SUBLANES_BF16 = 16
VMEM_LIMIT = 48 * 1024 * 1024
N_DEV = 8
MESH = pl.DeviceIdType.MESH
ROW_TILE = 640
ADAM_ROWS = 256
MM_TILE_M, MM_TILE_N, MM_TILE_K = 1664, 1024, 1664


def _tile(n, cap):
    if n <= cap:
        return n
    best = 0
    for t in range(LANES, cap + 1, LANES):
        if n % t == 0:
            best = t
    assert best, (n, cap)
    return best


def _params(*sem):
    return pltpu.CompilerParams(dimension_semantics=sem, vmem_limit_bytes=VMEM_LIMIT)


def _mm(a, b, *, la="mk", lb="kn", sa=None, sb=None, so=None, out_dtype, name):
    size, tile, parts = {}, {}, {}
    for x, lay, split in ((a, la, sa), (b, lb, sb)):
        shp = x.shape[1:] if split else x.shape
        for ax, n in zip(lay, shp):
            if ax == split:
                size[ax], tile[ax], parts[ax] = x.shape[0] * n, n, x.shape[0]
            else:
                assert size.setdefault(ax, n) == n, (name, ax, a.shape, b.shape)
    for ax, cap in (("m", MM_TILE_M), ("n", MM_TILE_N), ("k", MM_TILE_K)):
        tile.setdefault(ax, _tile(size[ax], cap))
    grid = tuple(size[ax] // tile[ax] for ax in "mnk")
    nk = grid[2]
    dn = (((la.index("k"),), (lb.index("k"),)), ((), ()))

    def kern(a_ref, b_ref, o_ref, *acc):
        p = lax.dot_general(a_ref[...].astype(BF16), b_ref[...].astype(BF16), dn, preferred_element_type=F32)
        if nk == 1:
            o_ref[...] = p.astype(o_ref.dtype)
            return
        acc_ref, = acc
        k = pl.program_id(2)

        @pl.when(k == 0)
        def _():
            acc_ref[...] = p

        @pl.when(k > 0)
        def _():
            acc_ref[...] += p

        @pl.when(k == nk - 1)
        def _():
            o_ref[...] = acc_ref[...].astype(o_ref.dtype)

    def spec(lay, split):
        blk = tuple(tile[ax] for ax in lay)
        if split is None:
            return pl.BlockSpec(blk, lambda i, j, k: tuple({"m": i, "n": j, "k": k}[ax] for ax in lay))
        return pl.BlockSpec((None,) + blk, lambda i, j, k: ({"m": i, "n": j, "k": k}[split],) + tuple(
            0 if ax == split else {"m": i, "n": j, "k": k}[ax] for ax in lay))

    if so is None:
        out_shape = (size["m"], size["n"])
    else:
        out_shape = (parts[so],) + tuple(tile[ax] if ax == so else size[ax] for ax in "mn")
    return pl.pallas_call(
        kern, name=name, grid=grid,
        in_specs=[spec(la, sa), spec(lb, sb)], out_specs=spec("mn", so),
        out_shape=jax.ShapeDtypeStruct(out_shape, out_dtype),
        scratch_shapes=[] if nk == 1 else [pltpu.VMEM((tile["m"], tile["n"]), F32)],
        compiler_params=_params("parallel", "parallel", "arbitrary"),
    )(a, b)


def _linear(name, out_dtype):
    @jax.custom_vjp
    def f(x, w):
        return _mm(x, w, out_dtype=out_dtype, name=name + "_fwd")

    def fwd(x, w):
        return f(x, w), (x, w)

    def bwd(res, g):
        x, w = res
        dx = _mm(g, w, lb="nk", out_dtype=x.dtype, name=name + "_dx")
        dw = _mm(x, g, la="km", out_dtype=w.dtype, name=name + "_dw")
        return dx, dw

    f.defvjp(fwd, bwd)
    return f


def _ln_stats(z):
    mu = jnp.mean(z, axis=-1, keepdims=True)
    zc = z - mu
    var = jnp.mean(zc * zc, axis=-1, keepdims=True)
    return zc, lax.rsqrt(var + LN_EPS)


def _ln_fwd_call(h, mix, g, b, name):
    L, D = h.shape
    tm = _tile(L, ROW_TILE)

    def kern(h_ref, m_ref, g_ref, b_ref, o_ref):
        zc, rstd = _ln_stats(ALPHA * h_ref[...] + m_ref[...])
        o_ref[...] = zc * rstd * g_ref[...] + b_ref[...]

    row = pl.BlockSpec((tm, D), lambda i: (i, 0))
    vec = pl.BlockSpec((1, D), lambda i: (0, 0))
    return pl.pallas_call(kern, name=name, grid=(L // tm,), in_specs=[row, row, vec, vec], out_specs=row,
                          out_shape=jax.ShapeDtypeStruct((L, D), F32), compiler_params=_params("parallel"))(h, mix, g, b)


def _ln_bwd_call(h, mix, g, dout, name):
    L, D = h.shape
    tm = _tile(L, ROW_TILE)

    def kern(h_ref, m_ref, g_ref, d_ref, dz_ref, dg_ref, db_ref):
        i = pl.program_id(0)
        zc, rstd = _ln_stats(ALPHA * h_ref[...] + m_ref[...])
        xhat = zc * rstd
        d = d_ref[...]
        dxh = d * g_ref[...]
        m1 = jnp.mean(dxh, axis=-1, keepdims=True)
        m2 = jnp.mean(dxh * xhat, axis=-1, keepdims=True)
        row = i * tm + lax.broadcasted_iota(jnp.int32, (tm, 1), 0)
        dz_ref[...] = jnp.where(row >= PAD, rstd * (dxh - m1 - xhat * m2), 0.0)
        pg = jnp.sum(d * xhat, axis=0, keepdims=True)
        pb = jnp.sum(d, axis=0, keepdims=True)

        @pl.when(i == 0)
        def _():
            dg_ref[...] = pg
            db_ref[...] = pb

        @pl.when(i > 0)
        def _():
            dg_ref[...] += pg
            db_ref[...] += pb

    row = pl.BlockSpec((tm, D), lambda i: (i, 0))
    vec = pl.BlockSpec((1, D), lambda i: (0, 0))
    return pl.pallas_call(
        kern, name=name, grid=(L // tm,), in_specs=[row, row, vec, row], out_specs=[row, vec, vec],
        out_shape=[jax.ShapeDtypeStruct((L, D), F32), jax.ShapeDtypeStruct((1, D), F32), jax.ShapeDtypeStruct((1, D), F32)],
        compiler_params=_params("arbitrary"))(h, mix, g, dout)


def _deepnorm(name):
    @jax.custom_vjp
    def f(h, mix, g, b):
        return _ln_fwd_call(h, mix, g, b, name + "_fwd")

    def fwd(h, mix, g, b):
        return f(h, mix, g, b), (h, mix, g)

    def bwd(res, dout):
        h, mix, g = res
        dz, dg, db = _ln_bwd_call(h, mix, g, dout, name + "_bwd")
        return ALPHA * dz, dz, dg, db

    f.defvjp(fwd, bwd)
    return f


def _rms_fwd_call(x, g, name):
    L, n = x.shape
    tm = _tile(L, ROW_TILE)

    def kern(x_ref, g_ref, o_ref):
        x = x_ref[...]
        o_ref[...] = x * lax.rsqrt(jnp.mean(x * x, axis=-1, keepdims=True) + RMS_EPS) * g_ref[...]

    row = pl.BlockSpec((tm, n), lambda i: (i, 0))
    vec = pl.BlockSpec((1, n), lambda i: (0, 0))
    return pl.pallas_call(kern, name=name, grid=(L // tm,), in_specs=[row, vec], out_specs=row,
                          out_shape=jax.ShapeDtypeStruct((L, n), F32), compiler_params=_params("parallel"))(x, g)


def _rms_bwd_call(x, g, dout, name):
    L, n = x.shape
    tm = _tile(L, ROW_TILE)

    def kern(x_ref, g_ref, d_ref, dx_ref, dg_ref):
        i = pl.program_id(0)
        x = x_ref[...]
        rstd = lax.rsqrt(jnp.mean(x * x, axis=-1, keepdims=True) + RMS_EPS)
        xhat = x * rstd
        d = d_ref[...]
        dxh = d * g_ref[...]
        dx_ref[...] = rstd * (dxh - xhat * jnp.mean(dxh * xhat, axis=-1, keepdims=True))
        pg = jnp.sum(d * xhat, axis=0, keepdims=True)

        @pl.when(i == 0)
        def _():
            dg_ref[...] = pg

        @pl.when(i > 0)
        def _():
            dg_ref[...] += pg

    row = pl.BlockSpec((tm, n), lambda i: (i, 0))
    vec = pl.BlockSpec((1, n), lambda i: (0, 0))
    return pl.pallas_call(
        kern, name=name, grid=(L // tm,), in_specs=[row, vec, row], out_specs=[row, vec],
        out_shape=[jax.ShapeDtypeStruct((L, n), F32), jax.ShapeDtypeStruct((1, n), F32)],
        compiler_params=_params("arbitrary"))(x, g, dout)


def _rmsnorm(name):
    @jax.custom_vjp
    def f(x, g):
        return _rms_fwd_call(x, g, name + "_fwd")

    def fwd(x, g):
        return f(x, g), (x, g)

    def bwd(res, dout):
        x, g = res
        dx, dg = _rms_bwd_call(x, g, dout, name + "_bwd")
        return dx, dg

    f.defvjp(fwd, bwd)
    return f


def _rope_call(x, c, s1, s2, r, out_dtype, name):
    L, W = x.shape
    reps = W // LANES
    tm = _tile(L, ROW_TILE)

    def kern(x_ref, c_ref, s1_ref, s2_ref, o_ref):
        x = x_ref[...].astype(F32)
        wide = lambda t: jnp.tile(t[...], (1, reps)) if reps > 1 else t[...]
        out = x * wide(c_ref) + pltpu.roll(x, W - r, 1) * wide(s1_ref) + pltpu.roll(x, r, 1) * wide(s2_ref)
        o_ref[...] = out.astype(o_ref.dtype)

    row = pl.BlockSpec((tm, W), lambda i: (i, 0))
    tab = pl.BlockSpec((tm, LANES), lambda i: (i, 0))
    return pl.pallas_call(kern, name=name, grid=(L // tm,), in_specs=[row, tab, tab, tab], out_specs=row,
                          out_shape=jax.ShapeDtypeStruct((L, W), out_dtype), compiler_params=_params("parallel"))(x, c, s1, s2)


def _rope(name, r, out_dtype):
    @jax.custom_vjp
    def f(x, c, s1, s2):
        return _rope_call(x, c, s1, s2, r, out_dtype, name + "_fwd")

    def fwd(x, c, s1, s2):
        return f(x, c, s1, s2), (c, s1, s2, jnp.zeros((), x.dtype))

    def bwd(res, g):
        c, s1, s2, proto = res
        dx = _rope_call(g, c, -s1, -s2, r, proto.dtype, name + "_bwd")
        return dx, jnp.zeros_like(c), jnp.zeros_like(s1), jnp.zeros_like(s2)

    f.defvjp(fwd, bwd)
    return f


def _rope_tables(L, dim, theta):
    pos = (jnp.arange(L) - PAD).astype(F32)
    inv = theta ** (-jnp.arange(0, dim, 2, dtype=F32) / dim)
    ang = pos[:, None] * inv[None, :]
    return jnp.cos(ang), jnp.sin(ang)


def _rope_lanes(cos, sin, period):
    L, half = cos.shape
    one = jnp.ones((L, period - 2 * half), F32)
    zero = jnp.zeros((L, period - 2 * half), F32)
    z_h = jnp.zeros((L, half), F32)
    c = jnp.concatenate([cos, cos, one], axis=1)
    s1 = jnp.concatenate([-sin, z_h, zero], axis=1)
    s2 = jnp.concatenate([z_h, sin, zero], axis=1)
    reps = LANES // period
    return tuple(jnp.tile(t, (1, reps)) for t in (c, s1, s2))


HALO = SUBLANES_BF16


GLU_ROWS = 128


def _strips(C):
    return [(c0, min(LANES, C - c0)) for c0 in range(0, C, LANES)]


def _shifted_down(win):
    return [pltpu.roll(win, CONV_W - 1 - t, 0)[SUBLANES_F32:, :] if t < CONV_W - 1 else win[SUBLANES_F32:, :]
            for t in range(CONV_W)]


def _conv_taps(buf_ref, g, off, rows, cols, w, b):
    u = _shifted_down(buf_ref[g, pl.ds(off - SUBLANES_F32, rows + SUBLANES_F32), cols])
    return b + u[0] * w[0:1, :] + u[1] * w[1:2, :] + u[2] * w[2:3, :]


def _glu_fwd_call(u4, cw4, cb4, name):
    _, P, L, C = u4.shape
    tm = _tile(L, ROW_TILE)
    hb = tm // HALO

    def kern(u_ref, up_ref, w_ref, b_ref, o_ref, buf):
        i = pl.program_id(1)
        row = i * tm + lax.broadcasted_iota(jnp.int32, (tm, 1), 0)
        prow = i * tm - HALO + lax.broadcasted_iota(jnp.int32, (HALO, 1), 0)
        for g in range(2):
            buf[g, pl.ds(HALO, tm), :] = jnp.where(row >= PAD, u_ref[g], 0.0)
            buf[g, pl.ds(0, HALO), :] = jnp.where(prow >= PAD, up_ref[g], 0.0)
        for c0, wd in _strips(C):
            cols = pl.ds(c0, wd)
            ws = [w_ref[g][:, c0:c0 + wd] for g in range(2)]
            bs = [b_ref[g][:, c0:c0 + wd] for g in range(2)]
            for r in range(0, tm, GLU_ROWS):
                n = min(GLU_ROWS, tm - r)
                gate, val = (_conv_taps(buf, g, HALO + r, n, cols, ws[g], bs[g]) for g in range(2))
                o_ref[pl.ds(r, n), cols] = (gate * jax.nn.sigmoid(gate) * val).astype(o_ref.dtype)

    return pl.pallas_call(
        kern, name=name, grid=(P, L // tm),
        in_specs=[pl.BlockSpec((2, None, tm, C), lambda p, i: (0, p, i, 0)),
                  pl.BlockSpec((2, None, HALO, C), lambda p, i: (0, p, jnp.maximum(i * hb - 1, 0), 0)),
                  pl.BlockSpec((2, None, CONV_W, C), lambda p, i: (0, p, 0, 0)),
                  pl.BlockSpec((2, None, 1, C), lambda p, i: (0, p, 0, 0))],
        out_specs=pl.BlockSpec((None, tm, C), lambda p, i: (p, i, 0)),
        out_shape=jax.ShapeDtypeStruct((P, L, C), BF16),
        scratch_shapes=[pltpu.VMEM((2, tm + HALO, C), F32)],
        compiler_params=_params("parallel", "parallel"))(u4, u4, cw4, cb4)


def _glu_bwd_call(u4, cw4, cb4, dact, name):
    _, P, L, C = u4.shape
    tm = _tile(L, ROW_TILE)
    hb = tm // HALO
    ext = tm + HALO
    last_halo = L // HALO - 1

    def kern(u_ref, up_ref, un_ref, w_ref, b_ref, d_ref, dn_ref, du_ref, dw_ref, db_ref, ubuf, dbuf):
        i = pl.program_id(1)
        r0 = i * tm
        mask = lambda blk, start: jnp.where(
            (start + lax.broadcasted_iota(jnp.int32, (blk.shape[0], 1), 0) >= PAD), blk, 0.0)
        for g in range(2):
            ubuf[g, pl.ds(0, HALO), :] = mask(up_ref[g], r0 - HALO)
            ubuf[g, pl.ds(HALO, tm), :] = mask(u_ref[g], r0)
            ubuf[g, pl.ds(HALO + tm, HALO), :] = un_ref[g]
        dbuf[pl.ds(0, tm), :] = d_ref[...].astype(F32)
        dbuf[pl.ds(tm, HALO), :] = jnp.where(r0 + tm < L, dn_ref[...].astype(F32), 0.0)
        for c0, wd in _strips(C):
            cols = pl.ds(c0, wd)
            ws = [w_ref[g][:, c0:c0 + wd] for g in range(2)]
            bs = [b_ref[g][:, c0:c0 + wd] for g in range(2)]
            pw = [[jnp.zeros((1, wd), F32) for _ in range(CONV_W)] for _ in range(2)]
            pb = [jnp.zeros((1, wd), F32) for _ in range(2)]
            for r in range(0, tm, GLU_ROWS):
                n = min(GLU_ROWS, tm - r)
                ne = n + SUBLANES_F32
                us = [_shifted_down(ubuf[g, pl.ds(HALO + r - SUBLANES_F32, ne + SUBLANES_F32), cols]) for g in range(2)]
                gate, val = (bs[g] + us[g][0] * ws[g][0:1, :] + us[g][1] * ws[g][1:2, :] + us[g][2] * ws[g][2:3, :]
                             for g in range(2))
                sg = jax.nn.sigmoid(gate)
                d = dbuf[pl.ds(r, ne), cols]
                dys = (d * val * (sg * (1.0 + gate * (1.0 - sg))), d * (gate * sg))
                row = r0 + r + lax.broadcasted_iota(jnp.int32, (n, 1), 0)
                for g in range(2):
                    w = ws[g]
                    dy = dys[g][:n, :]
                    du = (dy * w[2:3, :] + pltpu.roll(dys[g], ne - 1, 0)[:n, :] * w[1:2, :]
                          + pltpu.roll(dys[g], ne - 2, 0)[:n, :] * w[0:1, :])
                    du_ref[g, pl.ds(r, n), cols] = jnp.where(row >= PAD, du, 0.0).astype(du_ref.dtype)
                    for t in range(CONV_W):
                        pw[g][t] = pw[g][t] + jnp.sum(dy * us[g][t][:n, :], axis=0, keepdims=True)
                    pb[g] = pb[g] + jnp.sum(dy, axis=0, keepdims=True)
            for g in range(2):
                pwg = jnp.concatenate(pw[g], axis=0)

                @pl.when(i == 0)
                def _():
                    dw_ref[g, :, cols] = pwg
                    db_ref[g, :, cols] = pb[g]

                @pl.when(i > 0)
                def _():
                    dw_ref[g, :, cols] += pwg
                    db_ref[g, :, cols] += pb[g]

    nxt = lambda i: jnp.minimum((i + 1) * hb, last_halo)
    return pl.pallas_call(
        kern, name=name, grid=(P, L // tm),
        in_specs=[pl.BlockSpec((2, None, tm, C), lambda p, i: (0, p, i, 0)),
                  pl.BlockSpec((2, None, HALO, C), lambda p, i: (0, p, jnp.maximum(i * hb - 1, 0), 0)),
                  pl.BlockSpec((2, None, HALO, C), lambda p, i: (0, p, nxt(i), 0)),
                  pl.BlockSpec((2, None, CONV_W, C), lambda p, i: (0, p, 0, 0)),
                  pl.BlockSpec((2, None, 1, C), lambda p, i: (0, p, 0, 0)),
                  pl.BlockSpec((None, tm, C), lambda p, i: (p, i, 0)),
                  pl.BlockSpec((None, HALO, C), lambda p, i: (p, nxt(i), 0))],
        out_specs=[pl.BlockSpec((2, None, tm, C), lambda p, i: (0, p, i, 0)),
                   pl.BlockSpec((2, None, CONV_W, C), lambda p, i: (0, p, 0, 0)),
                   pl.BlockSpec((2, None, 1, C), lambda p, i: (0, p, 0, 0))],
        out_shape=[jax.ShapeDtypeStruct((2, P, L, C), BF16), jax.ShapeDtypeStruct((2, P, CONV_W, C), F32),
                   jax.ShapeDtypeStruct((2, P, 1, C), F32)],
        scratch_shapes=[pltpu.VMEM((2, tm + 2 * HALO, C), F32), pltpu.VMEM((ext, C), F32)],
        compiler_params=_params("parallel", "arbitrary"))(u4, u4, u4, cw4, cb4, dact, dact)


def _ffn_up(name):
    def run(h1, w3, cw4, cb4):
        u3 = _mm(h1, w3, sb="n", so="n", out_dtype=F32, name=name + "_up")
        u4 = u3.reshape((2, u3.shape[0] // 2) + u3.shape[1:])
        return _glu_fwd_call(u4, cw4, cb4, name + "_glu"), u4

    @jax.custom_vjp
    def f(h1, w3, cw4, cb4):
        return run(h1, w3, cw4, cb4)[0]

    def fwd(h1, w3, cw4, cb4):
        act, u4 = run(h1, w3, cw4, cb4)
        return act, (h1, w3, cw4, cb4, u4)

    def bwd(res, dact):
        h1, w3, cw4, cb4, u4 = res
        du4, dcw, dcb = _glu_bwd_call(u4, cw4, cb4, dact, name + "_glu_bwd")
        du3 = du4.reshape((du4.shape[0] * du4.shape[1],) + du4.shape[2:])
        dh1 = _mm(du3, w3, sa="k", lb="nk", sb="k", out_dtype=F32, name=name + "_up_dx")
        dw3 = _mm(h1, du3, la="km", sb="n", so="n", out_dtype=w3.dtype, name=name + "_up_dw")
        return dh1, dw3, dcw, dcb

    f.defvjp(fwd, bwd)
    return f


def _ffn_down(name):
    @jax.custom_vjp
    def f(act3, wo3):
        return _mm(act3, wo3, sa="k", sb="k", out_dtype=F32, name=name + "_fwd")

    def fwd(act3, wo3):
        return f(act3, wo3), (act3, wo3)

    def bwd(res, g):
        act3, wo3 = res
        dact = _mm(g, wo3, lb="nk", sb="n", so="n", out_dtype=act3.dtype, name=name + "_dx")
        dwo = _mm(act3, g, la="km", sa="m", so="m", out_dtype=wo3.dtype, name=name + "_dw")
        return dact, dwo

    f.defvjp(fwd, bwd)
    return f


def _pairs(n, kv_major):
    if kv_major:
        pr = [(i, j) for j in range(n) for i in range(j, n)]
    else:
        pr = [(i, j) for i in range(n) for j in range(i + 1)]
    return (jnp.asarray(np.array([p[0] for p in pr], np.int32)), jnp.asarray(np.array([p[1] for p in pr], np.int32)))


TN = (((0,), (0,)), ((), ()))
NT = (((1,), (1,)), ((), ()))


def _scores(kT, qT, i, j, tq, tk, masked):
    s = lax.dot_general(kT, qT, TN, preferred_element_type=F32)
    if not masked:
        return s, None
    col = j * tk + lax.broadcasted_iota(jnp.int32, (tk, tq), 0)
    row = i * tq + lax.broadcasted_iota(jnp.int32, (tk, tq), 1)
    mask = (col <= row) & (col >= PAD)
    return jnp.where(mask, s, NEG), mask


FOX_SCALE = 0.125
C_PARTS = 3


class _AttnLayout:
    def __init__(self, heads, q_rows, v_rows, bases=(0, 0, 0), extra=0, q_scale=None, grad_rows=None):
        self.heads, self.q_rows, self.v_rows, self.bases, self.extra, self.q_scale = heads, q_rows, v_rows, bases, extra, q_scale
        self.dk = q_rows + extra
        self.grad_rows = q_rows if grad_rows is None else grad_rows

    def specs(self, tq, tk):
        qb, kb, vb = self.bases
        qmap = lambda h, t, it, jt: (qb + h, it[t])
        kmap = lambda h, t, it, jt: (kb + h, jt[t])
        vmap = lambda h, t, it, jt: (vb + h, jt[t])
        s = [pl.BlockSpec((self.q_rows, tq), qmap), pl.BlockSpec((self.q_rows, tk), kmap),
             pl.BlockSpec((self.v_rows, tk), vmap)]
        if self.extra:
            s += [pl.BlockSpec((self.extra, tq), lambda h, t, it, jt: (h, it[t])),
                  pl.BlockSpec((self.extra, tk), lambda h, t, it, jt: (h, jt[t]))]
        return s

    def operands(self, refs):
        n = 5 if self.extra else 3
        q, k = refs[0][...], refs[1][...]
        if self.q_scale is not None:
            q = q * jnp.asarray(self.q_scale, q.dtype)
        if self.extra:
            q = jnp.concatenate([q, refs[3][...]], axis=0)
            k = jnp.concatenate([k, refs[4][...]], axis=0)
        return q, k, refs[2], refs[n:]


def _attn_fwd_call(lay, arrs, name):
    L = arrs[0].shape[1]
    H, dv = lay.heads, lay.v_rows
    tq = tk = _tile(L, ROW_TILE)
    it, jt = _pairs(L // tq, False)

    def kern(it_ref, jt_ref, *refs):
        t = pl.program_id(1)
        i, j = it_ref[t], jt_ref[t]
        q, k, v_ref, (o_ref, lse_ref, m_s, l_s, acc_s) = lay.operands(refs)

        @pl.when(j == 0)
        def _():
            m_s[...] = jnp.full_like(m_s, NEG)
            l_s[...] = jnp.zeros_like(l_s)
            acc_s[...] = jnp.zeros_like(acc_s)

        def step(masked):
            s, mask = _scores(k, q, i, j, tq, tk, masked)
            m_prev = m_s[...]
            m_new = jnp.maximum(m_prev, jnp.max(s, axis=0, keepdims=True))
            a = jnp.exp(m_prev - m_new)
            p = jnp.exp(s - m_new)
            if masked:
                p = jnp.where(mask, p, 0.0)
            l_s[...] = a * l_s[...] + jnp.sum(p, axis=0, keepdims=True)
            acc_s[...] = a * acc_s[...] + jnp.dot(v_ref[...], p.astype(BF16), preferred_element_type=F32)
            m_s[...] = m_new

        pl.when(j == i)(lambda: step(True))
        pl.when(j != i)(lambda: step(False))

        @pl.when(j == i)
        def _():
            l = l_s[...]
            l = jnp.where(l == 0.0, 1.0, l)
            o_ref[...] = (acc_s[...] / l).astype(o_ref.dtype)
            lse_ref[...] = m_s[...] + jnp.log(l)

    return pl.pallas_call(
        kern, name=name,
        grid_spec=pltpu.PrefetchScalarGridSpec(
            num_scalar_prefetch=2, grid=(H, int(it.shape[0])), in_specs=lay.specs(tq, tk),
            out_specs=[pl.BlockSpec((dv, tq), lambda h, t, it, jt: (h, it[t])),
                       pl.BlockSpec((None, 1, tq), lambda h, t, it, jt: (h, 0, it[t]))],
            scratch_shapes=[pltpu.VMEM((1, tq), F32), pltpu.VMEM((1, tq), F32), pltpu.VMEM((dv, tq), F32)]),
        out_shape=[jax.ShapeDtypeStruct((H * dv, L), F32), jax.ShapeDtypeStruct((H, 1, L), F32)],
        compiler_params=_params("parallel", "arbitrary"))(it, jt, *arrs)


C_ROWS = SUBLANES_F32


def _attn_bwd_call(lay, arrs, oT, lse, doT, name):
    L = arrs[0].shape[1]
    H, dv, dk, gr = lay.heads, lay.v_rows, lay.dk, lay.grad_rows
    tq = tk = _tile(L, ROW_TILE)
    nq = L // tq
    it, jt = _pairs(nq, True)
    npairs = int(it.shape[0])
    with_c = lay.extra > 0

    def kern(it_ref, jt_ref, *refs):
        t = pl.program_id(1)
        i, j = it_ref[t], jt_ref[t]
        q, k, v_ref, rest = lay.operands(refs)
        o_ref, do_ref, lse_ref = rest[:3]
        if with_c:
            dq_ref, dk_ref, dv_ref, dqc_ref, dkc_ref, dq_s, dk_s, dv_s = rest[3:]
        else:
            dq_ref, dk_ref, dv_ref, dq_s, dk_s, dv_s = rest[3:]

        @pl.when(t == 0)
        def _():
            dq_s[...] = jnp.zeros_like(dq_s)

        @pl.when(i == j)
        def _():
            dk_s[...] = jnp.zeros_like(dk_s)
            dv_s[...] = jnp.zeros_like(dv_s)

        def step(masked):
            s, mask = _scores(k, q, i, j, tq, tk, masked)
            p = jnp.exp(s - lse_ref[...])
            if masked:
                p = jnp.where(mask, p, 0.0)
            dof = do_ref[...]
            dob = dof.astype(BF16)
            delta = jnp.sum(dof * o_ref[...], axis=0, keepdims=True)
            dp = lax.dot_general(v_ref[...], dob, TN, preferred_element_type=F32)
            dsb = (p * (dp - delta)).astype(BF16)
            dv_s[...] += lax.dot_general(dob, p.astype(BF16), NT, preferred_element_type=F32)
            dk_s[...] += lax.dot_general(q, dsb, NT, preferred_element_type=F32)
            cols = pl.ds(pl.multiple_of(i * tq, tq), tq)
            dq_s[:, cols] += jnp.dot(k, dsb, preferred_element_type=F32)

        pl.when(j == i)(lambda: step(True))
        pl.when(j != i)(lambda: step(False))

        @pl.when(i == nq - 1)
        def _():
            dk_ref[...] = dk_s[:gr, :].astype(dk_ref.dtype)
            dv_ref[...] = dv_s[...].astype(dv_ref.dtype)
            if with_c:
                dkc_ref[...] = dk_s[lay.q_rows:lay.q_rows + C_ROWS, :]

        @pl.when(t == npairs - 1)
        def _():
            dq = dq_s[:gr, :]
            dq_ref[...] = (dq if lay.q_scale is None else dq * lay.q_scale).astype(dq_ref.dtype)
            if with_c:
                dqc_ref[...] = dq_s[lay.q_rows:lay.q_rows + C_ROWS, :]

    qcol = lambda h, t, it, jt: (h, it[t])
    kcol = lambda h, t, it, jt: (h, jt[t])
    whole = lambda h, t, it, jt: (h, 0)
    out_specs = [pl.BlockSpec((gr, L), whole), pl.BlockSpec((gr, tk), kcol), pl.BlockSpec((dv, tk), kcol)]
    out_shape = [jax.ShapeDtypeStruct((H * gr, L), BF16), jax.ShapeDtypeStruct((H * gr, L), BF16),
                 jax.ShapeDtypeStruct((H * dv, L), BF16)]
    if with_c:
        out_specs += [pl.BlockSpec((C_ROWS, L), whole), pl.BlockSpec((C_ROWS, tk), kcol)]
        out_shape += [jax.ShapeDtypeStruct((H * C_ROWS, L), F32)] * 2
    return pl.pallas_call(
        kern, name=name,
        grid_spec=pltpu.PrefetchScalarGridSpec(
            num_scalar_prefetch=2, grid=(H, npairs),
            in_specs=lay.specs(tq, tk) + [pl.BlockSpec((dv, tq), qcol), pl.BlockSpec((dv, tq), qcol),
                                          pl.BlockSpec((None, 1, tq), lambda h, t, it, jt: (h, 0, it[t]))],
            out_specs=out_specs,
            scratch_shapes=[pltpu.VMEM((dk, L), F32), pltpu.VMEM((dk, tk), F32), pltpu.VMEM((dv, tk), F32)]),
        out_shape=out_shape, compiler_params=_params("parallel", "arbitrary"))(it, jt, *arrs, oT, doT, lse)


POISON = 1e30


def _fox_extras(cT):
    H, L = cT.shape
    to_bf16 = lambda t: lax.reduce_precision(t, exponent_bits=8, mantissa_bits=7)
    hi = to_bf16(cT)
    lo = to_bf16(cT - hi)
    ll = to_bf16(cT - hi - lo)
    terms = jnp.stack([hi, lo, ll], axis=1).astype(BF16)
    ones = jnp.ones_like(terms)
    fill = jnp.zeros((H, SUBLANES_BF16 - 2 * C_PARTS, L), BF16)
    pad_key = (jnp.arange(L) < PAD)[None, None, :]
    kterms = jnp.where(pad_key, jnp.asarray([POISON, 0.0, 0.0], BF16)[None, :, None], terms)
    eq = jnp.concatenate([terms, ones, fill], axis=1)
    ek = jnp.concatenate([ones, -kterms, fill], axis=1)
    return eq.reshape(H * SUBLANES_BF16, L), ek.reshape(H * SUBLANES_BF16, L)


def _fox_attention(name):
    assert FOX_SCALE == FOX_HD ** -0.5
    lay = _AttnLayout(FOX_HEADS, FOX_HD, FOX_HD, bases=(0, FOX_HEADS, 2 * FOX_HEADS), extra=SUBLANES_BF16,
                      q_scale=FOX_SCALE)

    def run(qkvT, cT):
        arrs = (qkvT, qkvT, qkvT) + _fox_extras(cT)
        oT, lse = _attn_fwd_call(lay, arrs, name + "_fwd")
        return oT, (arrs, oT, lse)

    @jax.custom_vjp
    def f(qkvT, cT):
        return run(qkvT, cT)[0]

    def fwd(qkvT, cT):
        return run(qkvT, cT)

    def bwd(res, doT):
        arrs, oT, lse = res
        dq, dk, dv, dqc, dkc = _attn_bwd_call(lay, arrs, oT, lse, doT, name + "_bwd")
        L = dq.shape[1]
        dc = dqc.reshape(FOX_HEADS, C_ROWS, L)[:, 0, :] - dkc.reshape(FOX_HEADS, C_ROWS, L)[:, C_PARTS, :]
        return jnp.concatenate([dq, dk, dv], axis=0), dc

    f.defvjp(fwd, bwd)
    return f


MLA_XROWS = SUBLANES_BF16


def _mla_attention(name):
    rows = MLA_NOPE + MLA_ROPE + MLA_XROWS
    lay = _AttnLayout(MLA_HEADS, rows, MLA_V, grad_rows=MLA_NOPE + MLA_ROPE)

    @jax.custom_vjp
    def f(qT, kT, vT):
        return _attn_fwd_call(lay, (qT, kT, vT), name + "_fwd")[0]

    def fwd(qT, kT, vT):
        oT, lse = _attn_fwd_call(lay, (qT, kT, vT), name + "_fwd")
        return oT, (qT, kT, vT, oT, lse)

    def bwd(res, doT):
        qT, kT, vT, oT, lse = res
        dq, dk, dv = _attn_bwd_call(lay, (qT, kT, vT), oT, lse, doT, name + "_bwd")
        L = dq.shape[1]
        widen = lambda t: jnp.pad(t.reshape(MLA_HEADS, -1, L), ((0, 0), (0, MLA_XROWS), (0, 0))).reshape(-1, L)
        return widen(dq), widen(dk), dv

    f.defvjp(fwd, bwd)
    return f


def _linear_t(name, out_dtype):
    @jax.custom_vjp
    def f(wT, x):
        return _mm(wT, x, lb="nk", out_dtype=out_dtype, name=name + "_fwd")

    def fwd(wT, x):
        return f(wT, x), (wT, x)

    def bwd(res, g):
        wT, x = res
        dwT = _mm(g, x, out_dtype=wT.dtype, name=name + "_dw")
        dx = _mm(g, wT, la="km", out_dtype=x.dtype, name=name + "_dx")
        return dwT, dx

    f.defvjp(fwd, bwd)
    return f


def _linear_km(name):
    @jax.custom_vjp
    def f(aT, w):
        return _mm(aT, w, la="km", out_dtype=F32, name=name + "_fwd")

    def fwd(aT, w):
        return f(aT, w), (aT, w)

    def bwd(res, g):
        aT, w = res
        daT = _mm(w, g, lb="nk", out_dtype=aT.dtype, name=name + "_dx")
        dw = _mm(aT, g, out_dtype=w.dtype, name=name + "_dw")
        return daT, dw

    f.defvjp(fwd, bwd)
    return f


def _swa_parts(q_ref, kc_ref, kp_ref, km_ref, sink_ref, i, scale):
    R = SWA_G * BLOCK
    q = q_ref[...].reshape(R, SWA_HD)
    nt = (((1,), (1,)), ((), ()))
    r = lax.broadcasted_iota(jnp.int32, (R, BLOCK), 0) & (BLOCK - 1)
    c = lax.broadcasted_iota(jnp.int32, (R, BLOCK), 1)
    s_c = jnp.where((c <= r) & (i >= 1), lax.dot_general(q, kc_ref[...], nt, preferred_element_type=F32) * scale, NEG)
    s_p = jnp.where((c > r) & (i >= 2), lax.dot_general(q, kp_ref[...], nt, preferred_element_type=F32) * scale, NEG)
    s_m = jnp.where((c >= PAD) & ((i >= 1) | (c <= r)),
                    lax.dot_general(q, km_ref[...], nt, preferred_element_type=F32) * scale, NEG)
    sink = jnp.broadcast_to(sink_ref[...], (SWA_G, BLOCK, 1)).reshape(R, 1)
    return q, s_c, s_p, s_m, sink


def _swa_specs(L):
    nb = L // BLOCK
    qs = pl.BlockSpec((None, SWA_G, BLOCK, SWA_HD), lambda g, i: (g, 0, i, 0))
    kc = pl.BlockSpec((None, BLOCK, SWA_HD), lambda g, i: (g, i, 0))
    kp = pl.BlockSpec((None, BLOCK, SWA_HD), lambda g, i: (g, jnp.maximum(i - 1, 0), 0))
    km = pl.BlockSpec((None, BLOCK, SWA_HD), lambda g, i: (g, 0, 0))
    sk = pl.BlockSpec((None, SWA_G, 1, 1), lambda g, i: (g, 0, 0, 0))
    ls = pl.BlockSpec((None, SWA_G, BLOCK, 1), lambda g, i: (g, 0, i, 0))
    return nb, qs, kc, kp, km, sk, ls


def _swa_fwd_call(q, k, v, sinks, scale, name):
    _, _, L, _ = q.shape
    nb, qs, kc, kp, km, sk, ls = _swa_specs(L)

    def kern(q_ref, kc_ref, kp_ref, km_ref, vc_ref, vp_ref, vm_ref, sink_ref, o_ref, lse_ref):
        i = pl.program_id(1)
        _, s_c, s_p, s_m, sink = _swa_parts(q_ref, kc_ref, kp_ref, km_ref, sink_ref, i, scale)
        mx = lambda s: jnp.max(s, axis=-1, keepdims=True)
        m = jnp.maximum(jnp.maximum(mx(s_c), mx(s_p)), jnp.maximum(mx(s_m), sink))
        p_c, p_p, p_m = jnp.exp(s_c - m), jnp.exp(s_p - m), jnp.exp(s_m - m)
        sm = lambda p: jnp.sum(p, axis=-1, keepdims=True)
        den = sm(p_c) + sm(p_p) + sm(p_m) + jnp.exp(sink - m)
        inv = 1.0 / den
        pv = lambda p, v_ref: jnp.dot((p * inv).astype(BF16), v_ref[...], preferred_element_type=F32)
        o = pv(p_c, vc_ref) + pv(p_p, vp_ref) + pv(p_m, vm_ref)
        o_ref[...] = o.reshape(SWA_G, BLOCK, SWA_HD)
        lse_ref[...] = (m + jnp.log(den)).reshape(SWA_G, BLOCK, 1)

    return pl.pallas_call(
        kern, name=name, grid=(SWA_KVH, nb), in_specs=[qs, kc, kp, km, kc, kp, km, sk], out_specs=[qs, ls],
        out_shape=[jax.ShapeDtypeStruct((SWA_KVH, SWA_G, L, SWA_HD), F32), jax.ShapeDtypeStruct((SWA_KVH, SWA_G, L, 1), F32)],
        compiler_params=_params("parallel", "parallel"))(q, k, k, k, v, v, v, sinks)


def _swa_bwd_call(q, k, v, sinks, o, lse, do, scale, name):
    _, _, L, _ = q.shape
    nb, qs, kc, kp, km, sk, ls = _swa_specs(L)
    R = SWA_G * BLOCK
    full = pl.BlockSpec((None, L, SWA_HD), lambda g, i: (g, 0, 0))

    def kern(q_ref, kc_ref, kp_ref, km_ref, vc_ref, vp_ref, vm_ref, sink_ref, o_ref, do_ref, lse_ref,
             dq_ref, dk_ref, dv_ref, dsink_ref):
        i = pl.program_id(1)

        @pl.when(i == 0)
        def _():
            dk_ref[...] = jnp.zeros_like(dk_ref)
            dv_ref[...] = jnp.zeros_like(dv_ref)
            dsink_ref[...] = jnp.zeros_like(dsink_ref)

        q, s_c, s_p, s_m, sink = _swa_parts(q_ref, kc_ref, kp_ref, km_ref, sink_ref, i, scale)
        lse = lse_ref[...].reshape(R, 1)
        dof = do_ref[...].reshape(R, SWA_HD)
        dob = dof.astype(BF16)
        delta = jnp.sum(dof * o_ref[...].reshape(R, SWA_HD), axis=-1, keepdims=True)
        nt = (((1,), (1,)), ((), ()))
        tn = (((0,), (0,)), ((), ()))
        cur = pl.ds(pl.multiple_of(i * BLOCK, BLOCK), BLOCK)
        prev = pl.ds(pl.multiple_of(jnp.maximum(i - 1, 0) * BLOCK, BLOCK), BLOCK)
        meta = pl.ds(0, BLOCK)
        dq = jnp.zeros((R, SWA_HD), F32)
        for s, k_ref, v_ref, rows in ((s_c, kc_ref, vc_ref, cur), (s_p, kp_ref, vp_ref, prev), (s_m, km_ref, vm_ref, meta)):
            p = jnp.exp(s - lse)
            dp = lax.dot_general(dob, v_ref[...], nt, preferred_element_type=F32)
            ds = (p * (dp - delta)).astype(BF16)
            dq = dq + jnp.dot(ds, k_ref[...], preferred_element_type=F32)
            dv_ref[rows, :] += lax.dot_general(p.astype(BF16), dob, tn, preferred_element_type=F32)
            dk_ref[rows, :] += lax.dot_general(ds, q, tn, preferred_element_type=F32) * scale
        dq_ref[...] = (dq * scale).reshape(SWA_G, BLOCK, SWA_HD).astype(dq_ref.dtype)
        dsk = -jnp.exp(sink - lse) * delta
        dsink_ref[...] += jnp.sum(dsk.reshape(SWA_G, BLOCK, 1), axis=1, keepdims=True)

    return pl.pallas_call(
        kern, name=name, grid=(SWA_KVH, nb), in_specs=[qs, kc, kp, km, kc, kp, km, sk, qs, qs, ls],
        out_specs=[qs, full, full, sk],
        out_shape=[jax.ShapeDtypeStruct((SWA_KVH, SWA_G, L, SWA_HD), BF16), jax.ShapeDtypeStruct((SWA_KVH, L, SWA_HD), F32),
                   jax.ShapeDtypeStruct((SWA_KVH, L, SWA_HD), F32), jax.ShapeDtypeStruct((SWA_KVH, SWA_G, 1, 1), F32)],
        compiler_params=_params("parallel", "arbitrary"))(q, k, k, k, v, v, v, sinks, o, do, lse)


def _swa_attn(name, scale):
    @jax.custom_vjp
    def f(q, k, v, sinks):
        return _swa_fwd_call(q, k, v, sinks, scale, name + "_fwd")[0]

    def fwd(q, k, v, sinks):
        o, lse = _swa_fwd_call(q, k, v, sinks, scale, name + "_fwd")
        return o, (q, k, v, sinks, o, lse)

    def bwd(res, do):
        q, k, v, sinks, o, lse = res
        dq, dk, dv, dsink = _swa_bwd_call(q, k, v, sinks, o, lse, do, scale, name + "_bwd")
        return dq, dk.astype(BF16), dv.astype(BF16), dsink

    f.defvjp(fwd, bwd)
    return f


def _scan_call(x, bias, mul, pre_logsig, name):
    L, W = x.shape
    tb = _tile(L, ROW_TILE)
    has_mul = mul is not None

    def kern(x_ref, b_ref, *rest):
        if has_mul:
            m_ref, o_ref, tot_ref, carry = rest
        else:
            o_ref, tot_ref, carry = rest
        i = pl.program_id(0)

        @pl.when(i == 0)
        def _():
            carry[...] = jnp.zeros_like(carry)
            tot_ref[...] = jnp.zeros_like(tot_ref)

        z = x_ref[...] + b_ref[...]
        if pre_logsig:
            z = jnp.minimum(z, 0.0) - jnp.log(1.0 + jnp.exp(-jnp.abs(z)))
        row = lax.broadcasted_iota(jnp.int32, (tb, W), 0)
        s = 1
        while s < tb:
            z = z + jnp.where(row >= s, pltpu.roll(z, s, 0), 0.0)
            s *= 2
        z = z + carry[...]
        carry[...] = z[tb - 1:tb, :]
        if has_mul:
            z = z * m_ref[...]
        o_ref[...] = z
        tot_ref[...] += jnp.sum(z, axis=0, keepdims=True)

    row = pl.BlockSpec((tb, W), lambda i: (i, 0))
    vec = pl.BlockSpec((1, W), lambda i: (0, 0))
    return pl.pallas_call(
        kern, name=name, grid=(L // tb,), in_specs=[row, vec] + ([row] if has_mul else []), out_specs=[row, vec],
        out_shape=[jax.ShapeDtypeStruct((L, W), F32), jax.ShapeDtypeStruct((1, W), F32)],
        scratch_shapes=[pltpu.VMEM((1, W), F32)],
        compiler_params=_params("arbitrary"))(*([x, bias] + ([mul] if has_mul else [])))


def _sigmoid_neg_call(x, bias, name):
    L, W = x.shape
    tb = _tile(L, ROW_TILE)

    def kern(x_ref, b_ref, o_ref):
        row = pl.program_id(0) * tb + lax.broadcasted_iota(jnp.int32, (tb, 1), 0)
        o_ref[...] = jnp.where(row >= PAD, jax.nn.sigmoid(-(x_ref[...] + b_ref[...])), 0.0)

    row = pl.BlockSpec((tb, W), lambda i: (i, 0))
    vec = pl.BlockSpec((1, W), lambda i: (0, 0))
    return pl.pallas_call(kern, name=name, grid=(L // tb,), in_specs=[row, vec], out_specs=row,
                          out_shape=jax.ShapeDtypeStruct((L, W), F32), compiler_params=_params("parallel"))(x, bias)


def _decay(name):
    @jax.custom_vjp
    def f(fg, b):
        return _scan_call(fg, b, None, True, name + "_fwd")[0]

    def fwd(fg, b):
        return f(fg, b), (fg, b)

    def bwd(res, dc):
        fg, b = res
        sg = _sigmoid_neg_call(fg, b, name + "_dsig")
        dfg_rev, db = _scan_call(dc[::-1], jnp.zeros_like(b), sg[::-1], False, name + "_bwd")
        return dfg_rev[::-1], db

    f.defvjp(fwd, bwd)
    return f


def _loss_call(hf, target, name):
    L, D = hf.shape
    nb = L // BLOCK

    def kern(h_ref, t_ref, loss_ref, dy_ref, acc):
        i = pl.program_id(0)

        @pl.when(i == 0)
        def _():
            acc[...] = jnp.zeros_like(acc)
            dy_ref[...] = jnp.zeros_like(dy_ref)

        @pl.when(i > 0)
        def _():
            e = h_ref[...] - t_ref[...]
            dy_ref[...] = e * (1.0 / D)
            acc[...] += jnp.sum(e * e, axis=0, keepdims=True)

        @pl.when(i == nb - 1)
        def _():
            loss_ref[...] = jnp.broadcast_to(jnp.sum(acc[...], axis=1, keepdims=True) * (0.5 / D), loss_ref.shape)

    return pl.pallas_call(
        kern, name=name, grid=(nb,),
        in_specs=[pl.BlockSpec((BLOCK, D), lambda i: (i, 0)), pl.BlockSpec((BLOCK, D), lambda i: (jnp.maximum(i - 1, 0), 0))],
        out_specs=[pl.BlockSpec((1, LANES), lambda i: (0, 0)), pl.BlockSpec((BLOCK, D), lambda i: (i, 0))],
        out_shape=[jax.ShapeDtypeStruct((1, LANES), F32), jax.ShapeDtypeStruct((L, D), F32)],
        scratch_shapes=[pltpu.VMEM((1, D), F32)],
        compiler_params=_params("arbitrary"))(hf, target)


def _loss_head(name):
    @jax.custom_vjp
    def f(hf, target):
        return _loss_call(hf, target, name)[0][0, 0]

    def fwd(hf, target):
        loss, dy = _loss_call(hf, target, name)
        return loss[0, 0], (dy, jnp.zeros((), F32))

    def bwd(res, g):
        dy, _ = res
        return dy * g, None

    f.defvjp(fwd, bwd)
    return f


def _me_and_peers():
    x, y, c = lax.axis_index("x"), lax.axis_index("y"), lax.axis_index("c")
    me = 4 * x + 2 * y + c
    peers = []
    for k in range(1, N_DEV):
        px = 1 - x if k & 4 else x
        py = 1 - y if k & 2 else y
        pc = 1 - c if k & 1 else c
        peers.append(((px, py, pc), 4 * px + 2 * py + pc))
    return me, peers


def _exchange(arrs, gather, name):
    n = len(arrs)
    shapes = [tuple(a.shape) if gather else tuple(a.shape[1:]) for a in arrs]

    def body(*refs):
        srcs, outs = refs[:n], refs[n:2 * n]
        send_sems, recv_sems, local_sems = refs[2 * n:]
        me, peers = _me_and_peers()
        piece = lambda a, p: srcs[a] if gather else srcs[a].at[p]
        started = []
        for a in range(n):
            mine = pltpu.make_async_copy(piece(a, me), outs[a].at[me], local_sems.at[a])
            mine.start()
            started.append(mine.wait)
        for k, (dev, pid) in enumerate(peers):
            for a in range(n):
                cp = pltpu.make_async_remote_copy(src_ref=piece(a, pid), dst_ref=outs[a].at[me], send_sem=send_sems.at[a, k],
                                                  recv_sem=recv_sems.at[a, k], device_id=dev, device_id_type=MESH)
                cp.start()
                started.append(cp.wait_send)
        for k, (dev, pid) in enumerate(peers):
            for a in range(n):
                pltpu.make_async_remote_copy(src_ref=piece(a, me), dst_ref=outs[a].at[pid], send_sem=send_sems.at[a, k],
                                             recv_sem=recv_sems.at[a, k], device_id=dev, device_id_type=MESH).wait_recv()
        for wait in started:
            wait()

    hbm = pl.BlockSpec(memory_space=pl.ANY)
    return pl.pallas_call(
        body, name=name, out_shape=[jax.ShapeDtypeStruct((N_DEV,) + s, a.dtype) for s, a in zip(shapes, arrs)],
        in_specs=[hbm] * n, out_specs=[hbm] * n,
        scratch_shapes=[pltpu.SemaphoreType.DMA((n, N_DEV - 1)), pltpu.SemaphoreType.DMA((n, N_DEV - 1)),
                        pltpu.SemaphoreType.DMA((n,))],
    )(*arrs)


def _adamw_call(parts, w, m, v, name):
    shape = w.shape
    cols = shape[-1]
    rows = int(np.prod(shape[:-1]))
    tr = rows
    if rows > ADAM_ROWS:
        tr = max(t for t in range(SUBLANES_F32, ADAM_ROWS + 1, SUBLANES_F32) if rows % t == 0)
    c1 = 1.0 / (1.0 - ADAM_B1 ** ADAM_STEP)
    c2 = 1.0 / (1.0 - ADAM_B2 ** ADAM_STEP)

    def kern(p_ref, w_ref, m_ref, v_ref, g_ref, d_ref, nm_ref, nv_ref):
        g = p_ref[0].astype(F32)
        for p in range(1, N_DEV):
            g = g + p_ref[p].astype(F32)
        nm = ADAM_B1 * m_ref[...] + (1.0 - ADAM_B1) * g
        nv = ADAM_B2 * v_ref[...] + (1.0 - ADAM_B2) * (g * g)
        g_ref[...] = g
        nm_ref[...] = nm
        nv_ref[...] = nv
        d_ref[...] = -ADAM_LR * ((nm * c1) / (jnp.sqrt(nv * c2) + ADAM_EPS) + ADAM_WD * w_ref[...])

    slab = pl.BlockSpec((tr, cols), lambda i: (i, 0))
    flat = lambda t: t.reshape(rows, cols)
    res = pl.pallas_call(
        kern, name=name, grid=(rows // tr,),
        in_specs=[pl.BlockSpec((N_DEV, tr, cols), lambda i: (0, i, 0)), slab, slab, slab], out_specs=[slab] * 4,
        out_shape=[jax.ShapeDtypeStruct((rows, cols), F32)] * 4,
        compiler_params=_params("parallel"))(parts.reshape(N_DEV, rows, cols), flat(w), flat(m), flat(v))
    return [r.reshape(shape) for r in res]


def _to_full(stacked, axis):
    moved = jnp.moveaxis(stacked, 0, axis)
    s = list(moved.shape)
    return moved.reshape(s[:axis] + [s[axis] * s[axis + 1]] + s[axis + 2:])


def _heads(x, h, d):
    return x.reshape(x.shape[0], h, d).transpose(1, 0, 2)


def _unheads(x):
    return x.transpose(1, 0, 2).reshape(x.shape[1], -1)


def _pad_cols(x, width):
    return jnp.pad(x, ((0, 0), (0, width - x.shape[1])))


def _fox_mixer(h, w_in, b_f, w_o, tag):
    hd = FOX_HEADS * FOX_HD
    qkvT = _linear_t(tag + "_qkv", BF16)(w_in[:, :3 * hd].T, h)
    fg = _linear(tag + "_gate", F32)(h, _pad_cols(w_in[:, 3 * hd:], LANES))
    c = _decay(tag + "_decay")(fg, _pad_cols(b_f[None, :], LANES))
    oT = _fox_attention(tag + "_attn")(qkvT, c[:, :FOX_HEADS].T)
    return _linear_km(tag + "_out")(oT, w_o)


def _swa_mixer(h, w_in, sinks, w_o, tabs, tag):
    L = h.shape[0]
    qd, kd = SWA_QH * SWA_HD, SWA_KVH * SWA_HD
    proj = _linear(tag + "_qkv", F32)(h, w_in)
    rope = _rope(tag + "_rope", ROPE_DIM // 2, BF16)
    q = rope(proj[:, :qd], *tabs)
    k = rope(proj[:, qd:qd + kd], *tabs)
    v = proj[:, qd + kd:].astype(BF16)
    qg = _heads(q, SWA_QH, SWA_HD).reshape(SWA_KVH, SWA_G, L, SWA_HD)
    o = _swa_attn(tag + "_attn", SWA_HD ** -0.5)(qg, _heads(k, SWA_KVH, SWA_HD), _heads(v, SWA_KVH, SWA_HD),
                                                 sinks.reshape(SWA_KVH, SWA_G, 1, 1))
    return _linear(tag + "_out", F32)(_unheads(o.reshape(SWA_QH, L, SWA_HD)), w_o)


def _mla_mixer(h, w_a, g_q, g_kv, w_uq, w_ukv, w_o, tabs, tag):
    L = h.shape[0]
    cq = _linear(tag + "_aq", F32)(h, w_a[:, :MLA_QL])
    ckv = _linear(tag + "_akv", F32)(h, w_a[:, MLA_QL:MLA_QL + MLA_KVL])
    kr = _linear(tag + "_akr", F32)(h, _pad_cols(w_a[:, MLA_QL + MLA_KVL:], LANES))
    cq = _rmsnorm(tag + "_nq")(cq, g_q[None, :])
    ckv = _rmsnorm(tag + "_nkv")(ckv, g_kv[None, :])
    q = _linear(tag + "_uq", F32)(cq, w_uq).reshape(L, MLA_HEADS, MLA_NOPE + MLA_ROPE)
    kv = _linear(tag + "_ukv", BF16)(ckv, w_ukv).reshape(L, MLA_HEADS, MLA_NOPE + MLA_V)
    scale = (MLA_NOPE + MLA_ROPE) ** -0.5
    rope = _rope(tag + "_rope", MLA_ROPE // 2, BF16)
    q_rope = rope(q[:, :, MLA_NOPE:].reshape(L, MLA_HEADS * MLA_ROPE), *(t * scale for t in tabs))
    k_rope = rope(kr, *tabs)[:, :MLA_ROPE]
    qf = jnp.concatenate([(q[:, :, :MLA_NOPE] * scale).astype(BF16), q_rope.reshape(L, MLA_HEADS, MLA_ROPE)], axis=-1)
    kf = jnp.concatenate([kv[:, :, :MLA_NOPE], jnp.broadcast_to(k_rope[:, None, :], (L, MLA_HEADS, MLA_ROPE))], axis=-1)
    seq_last = lambda t: t.transpose(1, 2, 0)
    unit = jnp.zeros((MLA_HEADS, MLA_XROWS, L), BF16).at[:, 0, :].set(1.0)
    poison = jnp.zeros((MLA_HEADS, MLA_XROWS, L), BF16).at[:, 0, :].set(jnp.where(jnp.arange(L) < PAD, -POISON, 0.0))
    qT = jnp.concatenate([seq_last(qf), unit], axis=1).reshape(-1, L)
    kT = jnp.concatenate([seq_last(kf), poison], axis=1).reshape(-1, L)
    oT = _mla_attention(tag + "_attn")(qT, kT, seq_last(kv[:, :, MLA_NOPE:]).reshape(-1, L))
    return _linear_km(tag + "_out")(oT, w_o)


BIG = (("fox_w_in", 2), ("fox_w_o", 1), ("swa_w_in", 2), ("swa_w_o", 1), ("mla_w_a", 1), ("mla_w_uq", 2),
       ("mla_w_ukv", 2), ("mla_w_o", 1), ("ffn_w_in", 2), ("ffn_w_out", 1))
SMALL = (("meta_tokens", 1), ("mla_g_q", 1), ("mla_g_kv", 1), ("ffn_conv_w", 2))
SHARDED = BIG + SMALL
REPL = ("ln1_g", "ln1_b", "ln2_g", "ln2_b", "fox_b_f", "swa_sinks", "ffn_conv_b")
FFN_STACKED = ("ffn_w_in", "ffn_w_out", "ffn_conv_w")
WEIGHTS = ("meta_tokens", "ln1_g", "ln1_b", "ln2_g", "ln2_b", "fox_w_in", "fox_b_f", "fox_w_o", "swa_w_in", "swa_sinks",
           "swa_w_o", "mla_w_a", "mla_g_q", "mla_g_kv", "mla_w_uq", "mla_w_ukv", "mla_w_o", "ffn_w_in", "ffn_conv_w",
           "ffn_conv_b", "ffn_w_out")


def _local_loss(stacked, repl, x, target):
    S, D = x.shape
    L = S + BLOCK
    wts = {n: _to_full(stacked[n], ax) for n, ax in SHARDED if n not in FFN_STACKED}
    wts.update(repl)
    half = N_DEV // 2
    h = jnp.concatenate([jnp.zeros((PAD, D), F32), wts["meta_tokens"], x], axis=0)
    cos_p, sin_p = _rope_tables(L, ROPE_DIM, ROPE_THETA)
    tabs_p = _rope_lanes(cos_p, sin_p, SWA_HD)
    cos_m, sin_m = _rope_tables(L, MLA_ROPE, MLA_ROPE_THETA)
    tabs_m = _rope_lanes(cos_m, sin_m, MLA_ROPE)
    for i in range(DEPTH):
        kind, j = i % 3, i // 3
        tag = f"l{i}"
        if kind == 0:
            mix = _fox_mixer(h, wts["fox_w_in"][j], wts["fox_b_f"][j], wts["fox_w_o"][j], tag + "_fox")
        elif kind == 1:
            mix = _swa_mixer(h, wts["swa_w_in"][j], wts["swa_sinks"][j], wts["swa_w_o"][j], tabs_p, tag + "_swa")
        else:
            mix = _mla_mixer(h, wts["mla_w_a"][j], wts["mla_g_q"][j], wts["mla_g_kv"][j], wts["mla_w_uq"][j],
                             wts["mla_w_ukv"][j], wts["mla_w_o"][j], tabs_m, tag + "_mla")
        h = _deepnorm(tag + "_ln1")(h, mix, wts["ln1_g"][i][None, :], wts["ln1_b"][i][None, :])
        w3 = stacked["ffn_w_in"][:, i]
        shard = w3.shape[-1]
        cw4 = stacked["ffn_conv_w"][:, i].reshape(2, half, CONV_W, shard)
        cb4 = wts["ffn_conv_b"][i].reshape(2, half, 1, shard)
        wo3 = stacked["ffn_w_out"][:, i].reshape(half, shard, D)
        act3 = _ffn_up(tag + "_ffn")(h, w3, cw4, cb4)
        ffn = _ffn_down(tag + "_ffn_down")(act3, wo3)
        h = _deepnorm(tag + "_ln2")(h, ffn, wts["ln2_g"][i][None, :], wts["ln2_b"][i][None, :])
    return _loss_head("loss_head")(h, target)


def _step(x, target, w, m, v):
    names = [n for n, _ in SHARDED]
    big = {n for n, _ in BIG}
    stacked = dict(zip(names, _exchange([w[n].astype(BF16) if n in big else w[n] for n in names], True, "gather_weights")))
    repl = {n: w[n] for n in REPL}
    loss, (g_st, g_repl, grad_x) = jax.value_and_grad(_local_loss, argnums=(0, 1, 2))(stacked, repl, x, target)
    got = dict(zip(names, _exchange([g_st[n] for n in names], False, "scatter_grads")))
    got.update(zip(REPL, _exchange([g_repl[n] for n in REPL], True, "gather_repl_grads")))
    out = {}
    for n in WEIGHTS:
        for kind, a in zip(("grad", "delta", "new_m", "new_v"), _adamw_call(got[n], w[n], m[n], v[n], "adamw_" + n)):
            out[(kind, n)] = a
    return loss, grad_x, out


def kernel(x, meta_tokens, ln1_g, ln1_b, ln2_g, ln2_b, fox_w_in, fox_b_f, fox_w_o, swa_w_in, swa_sinks, swa_w_o, mla_w_a, mla_g_q, mla_g_kv, mla_w_uq, mla_w_ukv, mla_w_o, ffn_w_in, ffn_conv_w, ffn_conv_b, ffn_w_out, loss_target, m_meta_tokens, m_ln1_g, m_ln1_b, m_ln2_g, m_ln2_b, m_fox_w_in, m_fox_b_f, m_fox_w_o, m_swa_w_in, m_swa_sinks, m_swa_w_o, m_mla_w_a, m_mla_g_q, m_mla_g_kv, m_mla_w_uq, m_mla_w_ukv, m_mla_w_o, m_ffn_w_in, m_ffn_conv_w, m_ffn_conv_b, m_ffn_w_out, v_meta_tokens, v_ln1_g, v_ln1_b, v_ln2_g, v_ln2_b, v_fox_w_in, v_fox_b_f, v_fox_w_o, v_swa_w_in, v_swa_sinks, v_swa_w_o, v_mla_w_a, v_mla_g_q, v_mla_g_kv, v_mla_w_uq, v_mla_w_ukv, v_mla_w_o, v_ffn_w_in, v_ffn_conv_w, v_ffn_conv_b, v_ffn_w_out):
    args = (meta_tokens, ln1_g, ln1_b, ln2_g, ln2_b, fox_w_in, fox_b_f, fox_w_o, swa_w_in, swa_sinks, swa_w_o, mla_w_a,
            mla_g_q, mla_g_kv, mla_w_uq, mla_w_ukv, mla_w_o, ffn_w_in, ffn_conv_w, ffn_conv_b, ffn_w_out)
    ms = (m_meta_tokens, m_ln1_g, m_ln1_b, m_ln2_g, m_ln2_b, m_fox_w_in, m_fox_b_f, m_fox_w_o, m_swa_w_in, m_swa_sinks,
          m_swa_w_o, m_mla_w_a, m_mla_g_q, m_mla_g_kv, m_mla_w_uq, m_mla_w_ukv, m_mla_w_o, m_ffn_w_in, m_ffn_conv_w,
          m_ffn_conv_b, m_ffn_w_out)
    vs = (v_meta_tokens, v_ln1_g, v_ln1_b, v_ln2_g, v_ln2_b, v_fox_w_in, v_fox_b_f, v_fox_w_o, v_swa_w_in, v_swa_sinks,
          v_swa_w_o, v_mla_w_a, v_mla_g_q, v_mla_g_kv, v_mla_w_uq, v_mla_w_ukv, v_mla_w_o, v_ffn_w_in, v_ffn_conv_w,
          v_ffn_conv_b, v_ffn_w_out)
    w = dict(zip(WEIGHTS, args))
    m = dict(zip(WEIGHTS, ms))
    v = dict(zip(WEIGHTS, vs))
    loss, grad_x, out = _step(x[0], loss_target[0], w, m, v)
    loss = lax.psum(loss, ("x", "y", "c"))
    res = [loss, grad_x[None]]
    for kind in ("grad", "delta", "new_m", "new_v"):
        res += [out[(kind, n)] for n in WEIGHTS]
    return tuple(res)
```

```python
import jax
import jax.numpy as jnp
import numpy as np
from jax import lax
from jax.experimental import pallas as pl
from jax.experimental.pallas import tpu as pltpu

F32 = jnp.float32
BF16 = jnp.bfloat16

D_MODEL = 1024
DEPTH = 4
N_META = 16
BLOCK = 128
PAD = BLOCK - N_META
NEG = -1e30
ALPHA = (2.0 * DEPTH) ** 0.25
LN_EPS = 1e-5
RMS_EPS = 1e-6
FOX_HEADS, FOX_HD = 16, 64
SWA_QH, SWA_KVH, SWA_HD = 16, 2, 64
SWA_G = SWA_QH // SWA_KVH
ROPE_THETA = 500000.0
ROPE_DIM = SWA_HD // 4
MLA_HEADS, MLA_QL, MLA_KVL, MLA_NOPE, MLA_ROPE, MLA_V = 16, 384, 256, 64, 32, 64
MLA_ROPE_THETA = 10000.0
D_FF = 2816
CONV_W = 3
ADAM_LR, ADAM_B1, ADAM_B2, ADAM_EPS, ADAM_WD, ADAM_STEP = 0.001, 0.9, 0.999, 1e-08, 0.01, 10

LANES = 128
SUBLANES_F32 = 8
SUBLANES_BF16 = 16
VMEM_LIMIT = 48 * 1024 * 1024
N_DEV = 8
MESH = pl.DeviceIdType.MESH
ROW_TILE = 640
ADAM_ROWS = 256
MM_TILE_M, MM_TILE_N, MM_TILE_K = 1664, 1024, 1664
ATTN_HEADS = 2


def _tile(n, cap):
    if n <= cap:
        return n
    best = 0
    for t in range(LANES, cap + 1, LANES):
        if n % t == 0:
            best = t
    assert best, (n, cap)
    return best


def _params(*sem):
    return pltpu.CompilerParams(dimension_semantics=sem, vmem_limit_bytes=VMEM_LIMIT)


def _mm(a, b, *, la="mk", lb="kn", sa=None, sb=None, so=None, out_dtype, name):
    size, tile, parts = {}, {}, {}
    for x, lay, split in ((a, la, sa), (b, lb, sb)):
        shp = x.shape[1:] if split else x.shape
        for ax, n in zip(lay, shp):
            if ax == split:
                size[ax], tile[ax], parts[ax] = x.shape[0] * n, n, x.shape[0]
            else:
                assert size.setdefault(ax, n) == n, (name, ax, a.shape, b.shape)
    for ax, cap in (("m", MM_TILE_M), ("n", MM_TILE_N), ("k", MM_TILE_K)):
        tile.setdefault(ax, _tile(size[ax], cap))
    grid = tuple(size[ax] // tile[ax] for ax in "mnk")
    nk = grid[2]
    dn = (((la.index("k"),), (lb.index("k"),)), ((), ()))

    def kern(a_ref, b_ref, o_ref, *acc):
        p = lax.dot_general(a_ref[...].astype(BF16), b_ref[...].astype(BF16), dn, preferred_element_type=F32)
        if nk == 1:
            o_ref[...] = p.astype(o_ref.dtype)
            return
        acc_ref, = acc
        k = pl.program_id(2)

        @pl.when(k == 0)
        def _():
            acc_ref[...] = p

        @pl.when(k > 0)
        def _():
            acc_ref[...] += p

        @pl.when(k == nk - 1)
        def _():
            o_ref[...] = acc_ref[...].astype(o_ref.dtype)

    def spec(lay, split):
        blk = tuple(tile[ax] for ax in lay)
        if split is None:
            return pl.BlockSpec(blk, lambda i, j, k: tuple({"m": i, "n": j, "k": k}[ax] for ax in lay))
        return pl.BlockSpec((None,) + blk, lambda i, j, k: ({"m": i, "n": j, "k": k}[split],) + tuple(
            0 if ax == split else {"m": i, "n": j, "k": k}[ax] for ax in lay))

    if so is None:
        out_shape = (size["m"], size["n"])
    else:
        out_shape = (parts[so],) + tuple(tile[ax] if ax == so else size[ax] for ax in "mn")
    return pl.pallas_call(
        kern, name=name, grid=grid,
        in_specs=[spec(la, sa), spec(lb, sb)], out_specs=spec("mn", so),
        out_shape=jax.ShapeDtypeStruct(out_shape, out_dtype),
        scratch_shapes=[] if nk == 1 else [pltpu.VMEM((tile["m"], tile["n"]), F32)],
        compiler_params=_params("parallel", "parallel", "arbitrary"),
    )(a, b)


def _linear(name, out_dtype):
    @jax.custom_vjp
    def f(x, w):
        return _mm(x, w, out_dtype=out_dtype, name=name + "_fwd")

    def fwd(x, w):
        return f(x, w), (x, w)

    def bwd(res, g):
        x, w = res
        dx = _mm(g, w, lb="nk", out_dtype=x.dtype, name=name + "_dx")
        dw = _mm(x, g, la="km", out_dtype=w.dtype, name=name + "_dw")
        return dx, dw

    f.defvjp(fwd, bwd)
    return f


def _ln_stats(z):
    mu = jnp.mean(z, axis=-1, keepdims=True)
    zc = z - mu
    var = jnp.mean(zc * zc, axis=-1, keepdims=True)
    return zc, lax.rsqrt(var + LN_EPS)


def _ln_fwd_call(h, mix, g, b, name):
    L, D = h.shape
    tm = _tile(L, ROW_TILE)

    def kern(h_ref, m_ref, g_ref, b_ref, o_ref):
        zc, rstd = _ln_stats(ALPHA * h_ref[...] + m_ref[...])
        o_ref[...] = zc * rstd * g_ref[...] + b_ref[...]

    row = pl.BlockSpec((tm, D), lambda i: (i, 0))
    vec = pl.BlockSpec((1, D), lambda i: (0, 0))
    return pl.pallas_call(kern, name=name, grid=(L // tm,), in_specs=[row, row, vec, vec], out_specs=row,
                          out_shape=jax.ShapeDtypeStruct((L, D), F32), compiler_params=_params("parallel"))(h, mix, g, b)


def _ln_bwd_call(h, mix, g, dout, name):
    L, D = h.shape
    tm = _tile(L, ROW_TILE)

    def kern(h_ref, m_ref, g_ref, d_ref, dz_ref, dg_ref, db_ref):
        i = pl.program_id(0)
        zc, rstd = _ln_stats(ALPHA * h_ref[...] + m_ref[...])
        xhat = zc * rstd
        d = d_ref[...]
        dxh = d * g_ref[...]
        m1 = jnp.mean(dxh, axis=-1, keepdims=True)
        m2 = jnp.mean(dxh * xhat, axis=-1, keepdims=True)
        row = i * tm + lax.broadcasted_iota(jnp.int32, (tm, 1), 0)
        dz_ref[...] = jnp.where(row >= PAD, rstd * (dxh - m1 - xhat * m2), 0.0)
        pg = jnp.sum(d * xhat, axis=0, keepdims=True)
        pb = jnp.sum(d, axis=0, keepdims=True)

        @pl.when(i == 0)
        def _():
            dg_ref[...] = pg
            db_ref[...] = pb

        @pl.when(i > 0)
        def _():
            dg_ref[...] += pg
            db_ref[...] += pb

    row = pl.BlockSpec((tm, D), lambda i: (i, 0))
    vec = pl.BlockSpec((1, D), lambda i: (0, 0))
    return pl.pallas_call(
        kern, name=name, grid=(L // tm,), in_specs=[row, row, vec, row], out_specs=[row, vec, vec],
        out_shape=[jax.ShapeDtypeStruct((L, D), F32), jax.ShapeDtypeStruct((1, D), F32), jax.ShapeDtypeStruct((1, D), F32)],
        compiler_params=_params("arbitrary"))(h, mix, g, dout)


def _deepnorm(name):
    @jax.custom_vjp
    def f(h, mix, g, b):
        return _ln_fwd_call(h, mix, g, b, name + "_fwd")

    def fwd(h, mix, g, b):
        return f(h, mix, g, b), (h, mix, g)

    def bwd(res, dout):
        h, mix, g = res
        dz, dg, db = _ln_bwd_call(h, mix, g, dout, name + "_bwd")
        return ALPHA * dz, dz, dg, db

    f.defvjp(fwd, bwd)
    return f


def _rms_fwd_call(x, g, name):
    L, n = x.shape
    tm = _tile(L, ROW_TILE)

    def kern(x_ref, g_ref, o_ref):
        x = x_ref[...]
        o_ref[...] = x * lax.rsqrt(jnp.mean(x * x, axis=-1, keepdims=True) + RMS_EPS) * g_ref[...]

    row = pl.BlockSpec((tm, n), lambda i: (i, 0))
    vec = pl.BlockSpec((1, n), lambda i: (0, 0))
    return pl.pallas_call(kern, name=name, grid=(L // tm,), in_specs=[row, vec], out_specs=row,
                          out_shape=jax.ShapeDtypeStruct((L, n), F32), compiler_params=_params("parallel"))(x, g)


def _rms_bwd_call(x, g, dout, name):
    L, n = x.shape
    tm = _tile(L, ROW_TILE)

    def kern(x_ref, g_ref, d_ref, dx_ref, dg_ref):
        i = pl.program_id(0)
        x = x_ref[...]
        rstd = lax.rsqrt(jnp.mean(x * x, axis=-1, keepdims=True) + RMS_EPS)
        xhat = x * rstd
        d = d_ref[...]
        dxh = d * g_ref[...]
        dx_ref[...] = rstd * (dxh - xhat * jnp.mean(dxh * xhat, axis=-1, keepdims=True))
        pg = jnp.sum(d * xhat, axis=0, keepdims=True)

        @pl.when(i == 0)
        def _():
            dg_ref[...] = pg

        @pl.when(i > 0)
        def _():
            dg_ref[...] += pg

    row = pl.BlockSpec((tm, n), lambda i: (i, 0))
    vec = pl.BlockSpec((1, n), lambda i: (0, 0))
    return pl.pallas_call(
        kern, name=name, grid=(L // tm,), in_specs=[row, vec, row], out_specs=[row, vec],
        out_shape=[jax.ShapeDtypeStruct((L, n), F32), jax.ShapeDtypeStruct((1, n), F32)],
        compiler_params=_params("arbitrary"))(x, g, dout)


def _rmsnorm(name):
    @jax.custom_vjp
    def f(x, g):
        return _rms_fwd_call(x, g, name + "_fwd")

    def fwd(x, g):
        return f(x, g), (x, g)

    def bwd(res, dout):
        x, g = res
        dx, dg = _rms_bwd_call(x, g, dout, name + "_bwd")
        return dx, dg

    f.defvjp(fwd, bwd)
    return f


def _rope_call(x, c, s1, s2, r, out_dtype, name):
    L, W = x.shape
    reps = W // LANES
    tm = _tile(L, ROW_TILE)

    def kern(x_ref, c_ref, s1_ref, s2_ref, o_ref):
        x = x_ref[...].astype(F32)
        wide = lambda t: jnp.tile(t[...], (1, reps)) if reps > 1 else t[...]
        out = x * wide(c_ref) + pltpu.roll(x, W - r, 1) * wide(s1_ref) + pltpu.roll(x, r, 1) * wide(s2_ref)
        o_ref[...] = out.astype(o_ref.dtype)

    row = pl.BlockSpec((tm, W), lambda i: (i, 0))
    tab = pl.BlockSpec((tm, LANES), lambda i: (i, 0))
    return pl.pallas_call(kern, name=name, grid=(L // tm,), in_specs=[row, tab, tab, tab], out_specs=row,
                          out_shape=jax.ShapeDtypeStruct((L, W), out_dtype), compiler_params=_params("parallel"))(x, c, s1, s2)


def _rope(name, r, out_dtype):
    @jax.custom_vjp
    def f(x, c, s1, s2):
        return _rope_call(x, c, s1, s2, r, out_dtype, name + "_fwd")

    def fwd(x, c, s1, s2):
        return f(x, c, s1, s2), (c, s1, s2, jnp.zeros((), x.dtype))

    def bwd(res, g):
        c, s1, s2, proto = res
        dx = _rope_call(g, c, -s1, -s2, r, proto.dtype, name + "_bwd")
        return dx, jnp.zeros_like(c), jnp.zeros_like(s1), jnp.zeros_like(s2)

    f.defvjp(fwd, bwd)
    return f


def _rope_tables(L, dim, theta):
    pos = (jnp.arange(L) - PAD).astype(F32)
    inv = theta ** (-jnp.arange(0, dim, 2, dtype=F32) / dim)
    ang = pos[:, None] * inv[None, :]
    return jnp.cos(ang), jnp.sin(ang)


def _rope_lanes(cos, sin, period):
    L, half = cos.shape
    one = jnp.ones((L, period - 2 * half), F32)
    zero = jnp.zeros((L, period - 2 * half), F32)
    z_h = jnp.zeros((L, half), F32)
    c = jnp.concatenate([cos, cos, one], axis=1)
    s1 = jnp.concatenate([-sin, z_h, zero], axis=1)
    s2 = jnp.concatenate([z_h, sin, zero], axis=1)
    reps = LANES // period
    return tuple(jnp.tile(t, (1, reps)) for t in (c, s1, s2))


HALO = SUBLANES_BF16


GLU_ROWS = 128


def _strips(C):
    return [(c0, min(LANES, C - c0)) for c0 in range(0, C, LANES)]


def _shifted_down(win):
    return [pltpu.roll(win, CONV_W - 1 - t, 0)[SUBLANES_F32:, :] if t < CONV_W - 1 else win[SUBLANES_F32:, :]
            for t in range(CONV_W)]


def _conv_taps(buf_ref, g, off, rows, cols, w, b):
    u = _shifted_down(buf_ref[g, pl.ds(off - SUBLANES_F32, rows + SUBLANES_F32), cols])
    return b + u[0] * w[0:1, :] + u[1] * w[1:2, :] + u[2] * w[2:3, :]


def _glu_fwd_call(u4, cw4, cb4, name):
    _, P, L, C = u4.shape
    tm = _tile(L, ROW_TILE)
    hb = tm // HALO

    def kern(u_ref, up_ref, w_ref, b_ref, o_ref, buf):
        i = pl.program_id(1)
        row = i * tm + lax.broadcasted_iota(jnp.int32, (tm, 1), 0)
        prow = i * tm - HALO + lax.broadcasted_iota(jnp.int32, (HALO, 1), 0)
        for g in range(2):
            buf[g, pl.ds(HALO, tm), :] = jnp.where(row >= PAD, u_ref[g], 0.0)
            buf[g, pl.ds(0, HALO), :] = jnp.where(prow >= PAD, up_ref[g], 0.0)
        for c0, wd in _strips(C):
            cols = pl.ds(c0, wd)
            ws = [w_ref[g][:, c0:c0 + wd] for g in range(2)]
            bs = [b_ref[g][:, c0:c0 + wd] for g in range(2)]
            for r in range(0, tm, GLU_ROWS):
                n = min(GLU_ROWS, tm - r)
                gate, val = (_conv_taps(buf, g, HALO + r, n, cols, ws[g], bs[g]) for g in range(2))
                o_ref[pl.ds(r, n), cols] = (gate * jax.nn.sigmoid(gate) * val).astype(o_ref.dtype)

    return pl.pallas_call(
        kern, name=name, grid=(P, L // tm),
        in_specs=[pl.BlockSpec((2, None, tm, C), lambda p, i: (0, p, i, 0)),
                  pl.BlockSpec((2, None, HALO, C), lambda p, i: (0, p, jnp.maximum(i * hb - 1, 0), 0)),
                  pl.BlockSpec((2, None, CONV_W, C), lambda p, i: (0, p, 0, 0)),
                  pl.BlockSpec((2, None, 1, C), lambda p, i: (0, p, 0, 0))],
        out_specs=pl.BlockSpec((None, tm, C), lambda p, i: (p, i, 0)),
        out_shape=jax.ShapeDtypeStruct((P, L, C), BF16),
        scratch_shapes=[pltpu.VMEM((2, tm + HALO, C), F32)],
        compiler_params=_params("parallel", "parallel"))(u4, u4, cw4, cb4)


def _glu_bwd_call(u4, cw4, cb4, dact, name):
    _, P, L, C = u4.shape
    tm = _tile(L, ROW_TILE)
    hb = tm // HALO
    ext = tm + HALO
    last_halo = L // HALO - 1

    def kern(u_ref, up_ref, un_ref, w_ref, b_ref, d_ref, dn_ref, du_ref, dw_ref, db_ref, ubuf, dbuf):
        i = pl.program_id(1)
        r0 = i * tm
        mask = lambda blk, start: jnp.where(
            (start + lax.broadcasted_iota(jnp.int32, (blk.shape[0], 1), 0) >= PAD), blk, 0.0)
        for g in range(2):
            ubuf[g, pl.ds(0, HALO), :] = mask(up_ref[g], r0 - HALO)
            ubuf[g, pl.ds(HALO, tm), :] = mask(u_ref[g], r0)
            ubuf[g, pl.ds(HALO + tm, HALO), :] = un_ref[g]
        dbuf[pl.ds(0, tm), :] = d_ref[...].astype(F32)
        dbuf[pl.ds(tm, HALO), :] = jnp.where(r0 + tm < L, dn_ref[...].astype(F32), 0.0)
        for c0, wd in _strips(C):
            cols = pl.ds(c0, wd)
            ws = [w_ref[g][:, c0:c0 + wd] for g in range(2)]
            bs = [b_ref[g][:, c0:c0 + wd] for g in range(2)]
            pw = [[jnp.zeros((1, wd), F32) for _ in range(CONV_W)] for _ in range(2)]
            pb = [jnp.zeros((1, wd), F32) for _ in range(2)]
            for r in range(0, tm, GLU_ROWS):
                n = min(GLU_ROWS, tm - r)
                ne = n + SUBLANES_F32
                us = [_shifted_down(ubuf[g, pl.ds(HALO + r - SUBLANES_F32, ne + SUBLANES_F32), cols]) for g in range(2)]
                gate, val = (bs[g] + us[g][0] * ws[g][0:1, :] + us[g][1] * ws[g][1:2, :] + us[g][2] * ws[g][2:3, :]
                             for g in range(2))
                sg = jax.nn.sigmoid(gate)
                d = dbuf[pl.ds(r, ne), cols]
                dys = (d * val * (sg * (1.0 + gate * (1.0 - sg))), d * (gate * sg))
                row = r0 + r + lax.broadcasted_iota(jnp.int32, (n, 1), 0)
                for g in range(2):
                    w = ws[g]
                    dy = dys[g][:n, :]
                    du = (dy * w[2:3, :] + pltpu.roll(dys[g], ne - 1, 0)[:n, :] * w[1:2, :]
                          + pltpu.roll(dys[g], ne - 2, 0)[:n, :] * w[0:1, :])
                    du_ref[g, pl.ds(r, n), cols] = jnp.where(row >= PAD, du, 0.0).astype(du_ref.dtype)
                    for t in range(CONV_W):
                        pw[g][t] = pw[g][t] + jnp.sum(dy * us[g][t][:n, :], axis=0, keepdims=True)
                    pb[g] = pb[g] + jnp.sum(dy, axis=0, keepdims=True)
            for g in range(2):
                pwg = jnp.concatenate(pw[g], axis=0)

                @pl.when(i == 0)
                def _():
                    dw_ref[g, :, cols] = pwg
                    db_ref[g, :, cols] = pb[g]

                @pl.when(i > 0)
                def _():
                    dw_ref[g, :, cols] += pwg
                    db_ref[g, :, cols] += pb[g]

    nxt = lambda i: jnp.minimum((i + 1) * hb, last_halo)
    return pl.pallas_call(
        kern, name=name, grid=(P, L // tm),
        in_specs=[pl.BlockSpec((2, None, tm, C), lambda p, i: (0, p, i, 0)),
                  pl.BlockSpec((2, None, HALO, C), lambda p, i: (0, p, jnp.maximum(i * hb - 1, 0), 0)),
                  pl.BlockSpec((2, None, HALO, C), lambda p, i: (0, p, nxt(i), 0)),
                  pl.BlockSpec((2, None, CONV_W, C), lambda p, i: (0, p, 0, 0)),
                  pl.BlockSpec((2, None, 1, C), lambda p, i: (0, p, 0, 0)),
                  pl.BlockSpec((None, tm, C), lambda p, i: (p, i, 0)),
                  pl.BlockSpec((None, HALO, C), lambda p, i: (p, nxt(i), 0))],
        out_specs=[pl.BlockSpec((2, None, tm, C), lambda p, i: (0, p, i, 0)),
                   pl.BlockSpec((2, None, CONV_W, C), lambda p, i: (0, p, 0, 0)),
                   pl.BlockSpec((2, None, 1, C), lambda p, i: (0, p, 0, 0))],
        out_shape=[jax.ShapeDtypeStruct((2, P, L, C), BF16), jax.ShapeDtypeStruct((2, P, CONV_W, C), F32),
                   jax.ShapeDtypeStruct((2, P, 1, C), F32)],
        scratch_shapes=[pltpu.VMEM((2, tm + 2 * HALO, C), F32), pltpu.VMEM((ext, C), F32)],
        compiler_params=_params("parallel", "arbitrary"))(u4, u4, u4, cw4, cb4, dact, dact)


def _ffn_up(name):
    def run(h1, w3, cw4, cb4):
        u3 = _mm(h1, w3, sb="n", so="n", out_dtype=F32, name=name + "_up")
        u4 = u3.reshape((2, u3.shape[0] // 2) + u3.shape[1:])
        return _glu_fwd_call(u4, cw4, cb4, name + "_glu"), u4

    @jax.custom_vjp
    def f(h1, w3, cw4, cb4):
        return run(h1, w3, cw4, cb4)[0]

    def fwd(h1, w3, cw4, cb4):
        act, u4 = run(h1, w3, cw4, cb4)
        return act, (h1, w3, cw4, cb4, u4)

    def bwd(res, dact):
        h1, w3, cw4, cb4, u4 = res
        du4, dcw, dcb = _glu_bwd_call(u4, cw4, cb4, dact, name + "_glu_bwd")
        du3 = du4.reshape((du4.shape[0] * du4.shape[1],) + du4.shape[2:])
        dh1 = _mm(du3, w3, sa="k", lb="nk", sb="k", out_dtype=F32, name=name + "_up_dx")
        dw3 = _mm(h1, du3, la="km", sb="n", so="n", out_dtype=w3.dtype, name=name + "_up_dw")
        return dh1, dw3, dcw, dcb

    f.defvjp(fwd, bwd)
    return f


def _ffn_down(name):
    @jax.custom_vjp
    def f(act3, wo3):
        return _mm(act3, wo3, sa="k", sb="k", out_dtype=F32, name=name + "_fwd")

    def fwd(act3, wo3):
        return f(act3, wo3), (act3, wo3)

    def bwd(res, g):
        act3, wo3 = res
        dact = _mm(g, wo3, lb="nk", sb="n", so="n", out_dtype=act3.dtype, name=name + "_dx")
        dwo = _mm(act3, g, la="km", sa="m", so="m", out_dtype=wo3.dtype, name=name + "_dw")
        return dact, dwo

    f.defvjp(fwd, bwd)
    return f


def _pairs(n, kv_major):
    if kv_major:
        pr = [(i, j) for j in range(n) for i in range(j, n)]
    else:
        pr = [(i, j) for i in range(n) for j in range(i + 1)]
    return (jnp.asarray(np.array([p[0] for p in pr], np.int32)), jnp.asarray(np.array([p[1] for p in pr], np.int32)))


TN = (((0,), (0,)), ((), ()))
NT = (((1,), (1,)), ((), ()))


def _scores(kT, qT, i, j, tq, tk, masked):
    s = lax.dot_general(kT, qT, TN, preferred_element_type=F32)
    if not masked:
        return s, None
    col = j * tk + lax.broadcasted_iota(jnp.int32, (tk, tq), 0)
    row = i * tq + lax.broadcasted_iota(jnp.int32, (tk, tq), 1)
    mask = (col <= row) & (col >= PAD)
    return jnp.where(mask, s, NEG), mask


FOX_SCALE = 0.125
C_PARTS = 3


class _AttnLayout:
    def __init__(self, heads, q_rows, v_rows, bases=(0, 0, 0), extra=0, q_scale=None, grad_rows=None):
        self.heads, self.q_rows, self.v_rows, self.bases, self.extra, self.q_scale = heads, q_rows, v_rows, bases, extra, q_scale
        self.dk = q_rows + extra
        self.grad_rows = q_rows if grad_rows is None else grad_rows
        self.n_in = 5 if extra else 3
        assert heads % ATTN_HEADS == 0 and all(b % ATTN_HEADS == 0 for b in bases)

    def specs(self, tq, tk):
        hp = ATTN_HEADS
        qb, kb, vb = (b // hp for b in self.bases)
        qmap = lambda g, t, it, jt: (qb + g, it[t])
        kmap = lambda g, t, it, jt: (kb + g, jt[t])
        vmap = lambda g, t, it, jt: (vb + g, jt[t])
        s = [pl.BlockSpec((hp * self.q_rows, tq), qmap), pl.BlockSpec((hp * self.q_rows, tk), kmap),
             pl.BlockSpec((hp * self.v_rows, tk), vmap)]
        if self.extra:
            s += [pl.BlockSpec((hp * self.extra, tq), lambda g, t, it, jt: (g, it[t])),
                  pl.BlockSpec((hp * self.extra, tk), lambda g, t, it, jt: (g, jt[t]))]
        return s

    def operands(self, refs, a):
        rows = lambda ref, n: ref[a * n:(a + 1) * n, :]
        q, k = rows(refs[0], self.q_rows), rows(refs[1], self.q_rows)
        if self.q_scale is not None:
            q = q * jnp.asarray(self.q_scale, q.dtype)
        if self.extra:
            q = jnp.concatenate([q, rows(refs[3], self.extra)], axis=0)
            k = jnp.concatenate([k, rows(refs[4], self.extra)], axis=0)
        return q, k, rows(refs[2], self.v_rows)


def _attn_fwd_call(lay, arrs, name):
    L = arrs[0].shape[1]
    H, dv, hp = lay.heads, lay.v_rows, ATTN_HEADS
    tq = tk = _tile(L, ROW_TILE)
    it, jt = _pairs(L // tq, False)

    def kern(it_ref, jt_ref, *refs):
        t = pl.program_id(1)
        i, j = it_ref[t], jt_ref[t]
        o_ref, lse_ref, m_s, l_s, acc_s = refs[lay.n_in:]

        @pl.when(j == 0)
        def _():
            m_s[...] = jnp.full_like(m_s, NEG)
            l_s[...] = jnp.zeros_like(l_s)
            acc_s[...] = jnp.zeros_like(acc_s)

        def step(masked):
            ops = [lay.operands(refs, a) for a in range(hp)]
            scores = [_scores(k, q, i, j, tq, tk, masked) for q, k, _ in ops]
            weights = []
            for a, (s, mask) in enumerate(scores):
                m_prev = m_s[a]
                m_new = jnp.maximum(m_prev, jnp.max(s, axis=0, keepdims=True))
                scale = jnp.exp(m_prev - m_new)
                p = jnp.exp(s - m_new)
                if masked:
                    p = jnp.where(mask, p, 0.0)
                l_s[a] = scale * l_s[a] + jnp.sum(p, axis=0, keepdims=True)
                m_s[a] = m_new
                weights.append((scale, p.astype(BF16)))
            for a, (scale, pb) in enumerate(weights):
                acc_s[a] = scale * acc_s[a] + jnp.dot(ops[a][2], pb, preferred_element_type=F32)

        pl.when(j == i)(lambda: step(True))
        pl.when(j != i)(lambda: step(False))

        @pl.when(j == i)
        def _():
            for a in range(hp):
                l = l_s[a]
                l = jnp.where(l == 0.0, 1.0, l)
                o_ref[a * dv:(a + 1) * dv, :] = (acc_s[a] / l).astype(o_ref.dtype)
                lse_ref[a] = m_s[a] + jnp.log(l)

    return pl.pallas_call(
        kern, name=name,
        grid_spec=pltpu.PrefetchScalarGridSpec(
            num_scalar_prefetch=2, grid=(H // hp, int(it.shape[0])), in_specs=lay.specs(tq, tk),
            out_specs=[pl.BlockSpec((hp * dv, tq), lambda g, t, it, jt: (g, it[t])),
                       pl.BlockSpec((hp, 1, tq), lambda g, t, it, jt: (g, 0, it[t]))],
            scratch_shapes=[pltpu.VMEM((hp, 1, tq), F32), pltpu.VMEM((hp, 1, tq), F32), pltpu.VMEM((hp, dv, tq), F32)]),
        out_shape=[jax.ShapeDtypeStruct((H * dv, L), F32), jax.ShapeDtypeStruct((H, 1, L), F32)],
        compiler_params=_params("parallel", "arbitrary"))(it, jt, *arrs)


C_ROWS = SUBLANES_F32


def _attn_bwd_call(lay, arrs, oT, lse, doT, name):
    L = arrs[0].shape[1]
    H, dv, dk, gr, hp = lay.heads, lay.v_rows, lay.dk, lay.grad_rows, ATTN_HEADS
    tq = tk = _tile(L, ROW_TILE)
    nq = L // tq
    it, jt = _pairs(nq, True)
    npairs = int(it.shape[0])
    with_c = lay.extra > 0
    crows = pl.ds(lay.q_rows, C_ROWS)

    def kern(it_ref, jt_ref, *refs):
        t = pl.program_id(1)
        i, j = it_ref[t], jt_ref[t]
        rest = refs[lay.n_in:]
        o_ref, do_ref, lse_ref = rest[:3]
        if with_c:
            dq_ref, dk_ref, dv_ref, dqc_ref, dkc_ref, dq_s, dk_s, dv_s = rest[3:]
        else:
            dq_ref, dk_ref, dv_ref, dq_s, dk_s, dv_s = rest[3:]

        @pl.when(t == 0)
        def _():
            dq_s[...] = jnp.zeros_like(dq_s)

        @pl.when(i == j)
        def _():
            dk_s[...] = jnp.zeros_like(dk_s)
            dv_s[...] = jnp.zeros_like(dv_s)

        def step(masked):
            first = []
            for a in range(hp):
                q, k, v = lay.operands(refs, a)
                s, mask = _scores(k, q, i, j, tq, tk, masked)
                dof = do_ref[a * dv:(a + 1) * dv, :]
                dob = dof.astype(BF16)
                delta = jnp.sum(dof * o_ref[a * dv:(a + 1) * dv, :], axis=0, keepdims=True)
                dp = lax.dot_general(v, dob, TN, preferred_element_type=F32)
                first.append((q, k, s, mask, dob, delta, dp))
            second = []
            for a, (q, k, s, mask, dob, delta, dp) in enumerate(first):
                p = jnp.exp(s - lse_ref[a])
                if masked:
                    p = jnp.where(mask, p, 0.0)
                second.append((p.astype(BF16), (p * (dp - delta)).astype(BF16)))
            cols = pl.ds(pl.multiple_of(i * tq, tq), tq)
            for a, (pb, dsb) in enumerate(second):
                q, k, _, _, dob, _, _ = first[a]
                dv_s[a] += lax.dot_general(dob, pb, NT, preferred_element_type=F32)
                dk_s[a] += lax.dot_general(q, dsb, NT, preferred_element_type=F32)
                dq_s[a, :, cols] += jnp.dot(k, dsb, preferred_element_type=F32)

        pl.when(j == i)(lambda: step(True))
        pl.when(j != i)(lambda: step(False))

        @pl.when(i == nq - 1)
        def _():
            for a in range(hp):
                dk_ref[a * gr:(a + 1) * gr, :] = dk_s[a, :gr, :].astype(dk_ref.dtype)
                dv_ref[a * dv:(a + 1) * dv, :] = dv_s[a].astype(dv_ref.dtype)
                if with_c:
                    dkc_ref[a * C_ROWS:(a + 1) * C_ROWS, :] = dk_s[a, crows, :]

        @pl.when(t == npairs - 1)
        def _():
            for a in range(hp):
                dq = dq_s[a, :gr, :]
                dq_ref[a * gr:(a + 1) * gr, :] = (dq if lay.q_scale is None else dq * lay.q_scale).astype(dq_ref.dtype)
                if with_c:
                    dqc_ref[a * C_ROWS:(a + 1) * C_ROWS, :] = dq_s[a, crows, :]

    qcol = lambda g, t, it, jt: (g, it[t])
    kcol = lambda g, t, it, jt: (g, jt[t])
    whole = lambda g, t, it, jt: (g, 0)
    out_specs = [pl.BlockSpec((hp * gr, L), whole), pl.BlockSpec((hp * gr, tk), kcol), pl.BlockSpec((hp * dv, tk), kcol)]
    out_shape = [jax.ShapeDtypeStruct((H * gr, L), BF16), jax.ShapeDtypeStruct((H * gr, L), BF16),
                 jax.ShapeDtypeStruct((H * dv, L), BF16)]
    if with_c:
        out_specs += [pl.BlockSpec((hp * C_ROWS, L), whole), pl.BlockSpec((hp * C_ROWS, tk), kcol)]
        out_shape += [jax.ShapeDtypeStruct((H * C_ROWS, L), F32)] * 2
    return pl.pallas_call(
        kern, name=name,
        grid_spec=pltpu.PrefetchScalarGridSpec(
            num_scalar_prefetch=2, grid=(H // hp, npairs),
            in_specs=lay.specs(tq, tk) + [pl.BlockSpec((hp * dv, tq), qcol), pl.BlockSpec((hp * dv, tq), qcol),
                                          pl.BlockSpec((hp, 1, tq), lambda g, t, it, jt: (g, 0, it[t]))],
            out_specs=out_specs,
            scratch_shapes=[pltpu.VMEM((hp, dk, L), F32), pltpu.VMEM((hp, dk, tk), F32), pltpu.VMEM((hp, dv, tk), F32)]),
        out_shape=out_shape, compiler_params=_params("parallel", "arbitrary"))(it, jt, *arrs, oT, doT, lse)


POISON = 1e30


def _fox_extras(cT):
    H, L = cT.shape
    to_bf16 = lambda t: lax.reduce_precision(t, exponent_bits=8, mantissa_bits=7)
    hi = to_bf16(cT)
    lo = to_bf16(cT - hi)
    ll = to_bf16(cT - hi - lo)
    terms = jnp.stack([hi, lo, ll], axis=1).astype(BF16)
    ones = jnp.ones_like(terms)
    fill = jnp.zeros((H, SUBLANES_BF16 - 2 * C_PARTS, L), BF16)
    pad_key = (jnp.arange(L) < PAD)[None, None, :]
    kterms = jnp.where(pad_key, jnp.asarray([POISON, 0.0, 0.0], BF16)[None, :, None], terms)
    eq = jnp.concatenate([terms, ones, fill], axis=1)
    ek = jnp.concatenate([ones, -kterms, fill], axis=1)
    return eq.reshape(H * SUBLANES_BF16, L), ek.reshape(H * SUBLANES_BF16, L)


def _fox_attention(name):
    assert FOX_SCALE == FOX_HD ** -0.5
    lay = _AttnLayout(FOX_HEADS, FOX_HD, FOX_HD, bases=(0, FOX_HEADS, 2 * FOX_HEADS), extra=SUBLANES_BF16,
                      q_scale=FOX_SCALE)

    def run(qkvT, cT):
        arrs = (qkvT, qkvT, qkvT) + _fox_extras(cT)
        oT, lse = _attn_fwd_call(lay, arrs, name + "_fwd")
        return oT, (arrs, oT, lse)

    @jax.custom_vjp
    def f(qkvT, cT):
        return run(qkvT, cT)[0]

    def fwd(qkvT, cT):
        return run(qkvT, cT)

    def bwd(res, doT):
        arrs, oT, lse = res
        dq, dk, dv, dqc, dkc = _attn_bwd_call(lay, arrs, oT, lse, doT, name + "_bwd")
        L = dq.shape[1]
        dc = dqc.reshape(FOX_HEADS, C_ROWS, L)[:, 0, :] - dkc.reshape(FOX_HEADS, C_ROWS, L)[:, C_PARTS, :]
        return jnp.concatenate([dq, dk, dv], axis=0), dc

    f.defvjp(fwd, bwd)
    return f


MLA_XROWS = SUBLANES_BF16


def _mla_attention(name):
    rows = MLA_NOPE + MLA_ROPE + MLA_XROWS
    lay = _AttnLayout(MLA_HEADS, rows, MLA_V, grad_rows=MLA_NOPE + MLA_ROPE)

    @jax.custom_vjp
    def f(qT, kT, vT):
        return _attn_fwd_call(lay, (qT, kT, vT), name + "_fwd")[0]

    def fwd(qT, kT, vT):
        oT, lse = _attn_fwd_call(lay, (qT, kT, vT), name + "_fwd")
        return oT, (qT, kT, vT, oT, lse)

    def bwd(res, doT):
        qT, kT, vT, oT, lse = res
        dq, dk, dv = _attn_bwd_call(lay, (qT, kT, vT), oT, lse, doT, name + "_bwd")
        L = dq.shape[1]
        widen = lambda t: jnp.pad(t.reshape(MLA_HEADS, -1, L), ((0, 0), (0, MLA_XROWS), (0, 0))).reshape(-1, L)
        return widen(dq), widen(dk), dv

    f.defvjp(fwd, bwd)
    return f


def _linear_t(name, out_dtype):
    @jax.custom_vjp
    def f(wT, x):
        return _mm(wT, x, lb="nk", out_dtype=out_dtype, name=name + "_fwd")

    def fwd(wT, x):
        return f(wT, x), (wT, x)

    def bwd(res, g):
        wT, x = res
        dwT = _mm(g, x, out_dtype=wT.dtype, name=name + "_dw")
        dx = _mm(g, wT, la="km", out_dtype=x.dtype, name=name + "_dx")
        return dwT, dx

    f.defvjp(fwd, bwd)
    return f


def _linear_km(name):
    @jax.custom_vjp
    def f(aT, w):
        return _mm(aT, w, la="km", out_dtype=F32, name=name + "_fwd")

    def fwd(aT, w):
        return f(aT, w), (aT, w)

    def bwd(res, g):
        aT, w = res
        daT = _mm(w, g, lb="nk", out_dtype=aT.dtype, name=name + "_dx")
        dw = _mm(aT, g, out_dtype=w.dtype, name=name + "_dw")
        return daT, dw

    f.defvjp(fwd, bwd)
    return f


def _swa_parts(q_ref, kc_ref, kp_ref, km_ref, sink_ref, i, scale):
    R = SWA_G * BLOCK
    q = q_ref[...].reshape(R, SWA_HD)
    nt = (((1,), (1,)), ((), ()))
    r = lax.broadcasted_iota(jnp.int32, (R, BLOCK), 0) & (BLOCK - 1)
    c = lax.broadcasted_iota(jnp.int32, (R, BLOCK), 1)
    s_c = jnp.where((c <= r) & (i >= 1), lax.dot_general(q, kc_ref[...], nt, preferred_element_type=F32) * scale, NEG)
    s_p = jnp.where((c > r) & (i >= 2), lax.dot_general(q, kp_ref[...], nt, preferred_element_type=F32) * scale, NEG)
    s_m = jnp.where((c >= PAD) & ((i >= 1) | (c <= r)),
                    lax.dot_general(q, km_ref[...], nt, preferred_element_type=F32) * scale, NEG)
    sink = jnp.broadcast_to(sink_ref[...], (SWA_G, BLOCK, 1)).reshape(R, 1)
    return q, s_c, s_p, s_m, sink


def _swa_specs(L):
    nb = L // BLOCK
    qs = pl.BlockSpec((None, SWA_G, BLOCK, SWA_HD), lambda g, i: (g, 0, i, 0))
    kc = pl.BlockSpec((None, BLOCK, SWA_HD), lambda g, i: (g, i, 0))
    kp = pl.BlockSpec((None, BLOCK, SWA_HD), lambda g, i: (g, jnp.maximum(i - 1, 0), 0))
    km = pl.BlockSpec((None, BLOCK, SWA_HD), lambda g, i: (g, 0, 0))
    sk = pl.BlockSpec((None, SWA_G, 1, 1), lambda g, i: (g, 0, 0, 0))
    ls = pl.BlockSpec((None, SWA_G, BLOCK, 1), lambda g, i: (g, 0, i, 0))
    return nb, qs, kc, kp, km, sk, ls


def _swa_fwd_call(q, k, v, sinks, scale, name):
    _, _, L, _ = q.shape
    nb, qs, kc, kp, km, sk, ls = _swa_specs(L)

    def kern(q_ref, kc_ref, kp_ref, km_ref, vc_ref, vp_ref, vm_ref, sink_ref, o_ref, lse_ref):
        i = pl.program_id(1)
        _, s_c, s_p, s_m, sink = _swa_parts(q_ref, kc_ref, kp_ref, km_ref, sink_ref, i, scale)
        mx = lambda s: jnp.max(s, axis=-1, keepdims=True)
        m = jnp.maximum(jnp.maximum(mx(s_c), mx(s_p)), jnp.maximum(mx(s_m), sink))
        p_c, p_p, p_m = jnp.exp(s_c - m), jnp.exp(s_p - m), jnp.exp(s_m - m)
        sm = lambda p: jnp.sum(p, axis=-1, keepdims=True)
        den = sm(p_c) + sm(p_p) + sm(p_m) + jnp.exp(sink - m)
        inv = 1.0 / den
        pv = lambda p, v_ref: jnp.dot((p * inv).astype(BF16), v_ref[...], preferred_element_type=F32)
        o = pv(p_c, vc_ref) + pv(p_p, vp_ref) + pv(p_m, vm_ref)
        o_ref[...] = o.reshape(SWA_G, BLOCK, SWA_HD)
        lse_ref[...] = (m + jnp.log(den)).reshape(SWA_G, BLOCK, 1)

    return pl.pallas_call(
        kern, name=name, grid=(SWA_KVH, nb), in_specs=[qs, kc, kp, km, kc, kp, km, sk], out_specs=[qs, ls],
        out_shape=[jax.ShapeDtypeStruct((SWA_KVH, SWA_G, L, SWA_HD), F32), jax.ShapeDtypeStruct((SWA_KVH, SWA_G, L, 1), F32)],
        compiler_params=_params("parallel", "parallel"))(q, k, k, k, v, v, v, sinks)


def _swa_bwd_call(q, k, v, sinks, o, lse, do, scale, name):
    _, _, L, _ = q.shape
    nb, qs, kc, kp, km, sk, ls = _swa_specs(L)
    R = SWA_G * BLOCK
    full = pl.BlockSpec((None, L, SWA_HD), lambda g, i: (g, 0, 0))

    def kern(q_ref, kc_ref, kp_ref, km_ref, vc_ref, vp_ref, vm_ref, sink_ref, o_ref, do_ref, lse_ref,
             dq_ref, dk_ref, dv_ref, dsink_ref):
        i = pl.program_id(1)

        @pl.when(i == 0)
        def _():
            dk_ref[...] = jnp.zeros_like(dk_ref)
            dv_ref[...] = jnp.zeros_like(dv_ref)
            dsink_ref[...] = jnp.zeros_like(dsink_ref)

        q, s_c, s_p, s_m, sink = _swa_parts(q_ref, kc_ref, kp_ref, km_ref, sink_ref, i, scale)
        lse = lse_ref[...].reshape(R, 1)
        dof = do_ref[...].reshape(R, SWA_HD)
        dob = dof.astype(BF16)
        delta = jnp.sum(dof * o_ref[...].reshape(R, SWA_HD), axis=-1, keepdims=True)
        nt = (((1,), (1,)), ((), ()))
        tn = (((0,), (0,)), ((), ()))
        cur = pl.ds(pl.multiple_of(i * BLOCK, BLOCK), BLOCK)
        prev = pl.ds(pl.multiple_of(jnp.maximum(i - 1, 0) * BLOCK, BLOCK), BLOCK)
        meta = pl.ds(0, BLOCK)
        dq = jnp.zeros((R, SWA_HD), F32)
        for s, k_ref, v_ref, rows in ((s_c, kc_ref, vc_ref, cur), (s_p, kp_ref, vp_ref, prev), (s_m, km_ref, vm_ref, meta)):
            p = jnp.exp(s - lse)
            dp = lax.dot_general(dob, v_ref[...], nt, preferred_element_type=F32)
            ds = (p * (dp - delta)).astype(BF16)
            dq = dq + jnp.dot(ds, k_ref[...], preferred_element_type=F32)
            dv_ref[rows, :] += lax.dot_general(p.astype(BF16), dob, tn, preferred_element_type=F32)
            dk_ref[rows, :] += lax.dot_general(ds, q, tn, preferred_element_type=F32) * scale
        dq_ref[...] = (dq * scale).reshape(SWA_G, BLOCK, SWA_HD).astype(dq_ref.dtype)
        dsk = -jnp.exp(sink - lse) * delta
        dsink_ref[...] += jnp.sum(dsk.reshape(SWA_G, BLOCK, 1), axis=1, keepdims=True)

    return pl.pallas_call(
        kern, name=name, grid=(SWA_KVH, nb), in_specs=[qs, kc, kp, km, kc, kp, km, sk, qs, qs, ls],
        out_specs=[qs, full, full, sk],
        out_shape=[jax.ShapeDtypeStruct((SWA_KVH, SWA_G, L, SWA_HD), BF16), jax.ShapeDtypeStruct((SWA_KVH, L, SWA_HD), F32),
                   jax.ShapeDtypeStruct((SWA_KVH, L, SWA_HD), F32), jax.ShapeDtypeStruct((SWA_KVH, SWA_G, 1, 1), F32)],
        compiler_params=_params("parallel", "arbitrary"))(q, k, k, k, v, v, v, sinks, o, do, lse)


def _swa_attn(name, scale):
    @jax.custom_vjp
    def f(q, k, v, sinks):
        return _swa_fwd_call(q, k, v, sinks, scale, name + "_fwd")[0]

    def fwd(q, k, v, sinks):
        o, lse = _swa_fwd_call(q, k, v, sinks, scale, name + "_fwd")
        return o, (q, k, v, sinks, o, lse)

    def bwd(res, do):
        q, k, v, sinks, o, lse = res
        dq, dk, dv, dsink = _swa_bwd_call(q, k, v, sinks, o, lse, do, scale, name + "_bwd")
        return dq, dk.astype(BF16), dv.astype(BF16), dsink

    f.defvjp(fwd, bwd)
    return f


def _scan_call(x, bias, mul, pre_logsig, name):
    L, W = x.shape
    tb = _tile(L, ROW_TILE)
    has_mul = mul is not None

    def kern(x_ref, b_ref, *rest):
        if has_mul:
            m_ref, o_ref, tot_ref, carry = rest
        else:
            o_ref, tot_ref, carry = rest
        i = pl.program_id(0)

        @pl.when(i == 0)
        def _():
            carry[...] = jnp.zeros_like(carry)
            tot_ref[...] = jnp.zeros_like(tot_ref)

        z = x_ref[...] + b_ref[...]
        if pre_logsig:
            z = jnp.minimum(z, 0.0) - jnp.log(1.0 + jnp.exp(-jnp.abs(z)))
        row = lax.broadcasted_iota(jnp.int32, (tb, W), 0)
        s = 1
        while s < tb:
            z = z + jnp.where(row >= s, pltpu.roll(z, s, 0), 0.0)
            s *= 2
        z = z + carry[...]
        carry[...] = z[tb - 1:tb, :]
        if has_mul:
            z = z * m_ref[...]
        o_ref[...] = z
        tot_ref[...] += jnp.sum(z, axis=0, keepdims=True)

    row = pl.BlockSpec((tb, W), lambda i: (i, 0))
    vec = pl.BlockSpec((1, W), lambda i: (0, 0))
    return pl.pallas_call(
        kern, name=name, grid=(L // tb,), in_specs=[row, vec] + ([row] if has_mul else []), out_specs=[row, vec],
        out_shape=[jax.ShapeDtypeStruct((L, W), F32), jax.ShapeDtypeStruct((1, W), F32)],
        scratch_shapes=[pltpu.VMEM((1, W), F32)],
        compiler_params=_params("arbitrary"))(*([x, bias] + ([mul] if has_mul else [])))


def _sigmoid_neg_call(x, bias, name):
    L, W = x.shape
    tb = _tile(L, ROW_TILE)

    def kern(x_ref, b_ref, o_ref):
        row = pl.program_id(0) * tb + lax.broadcasted_iota(jnp.int32, (tb, 1), 0)
        o_ref[...] = jnp.where(row >= PAD, jax.nn.sigmoid(-(x_ref[...] + b_ref[...])), 0.0)

    row = pl.BlockSpec((tb, W), lambda i: (i, 0))
    vec = pl.BlockSpec((1, W), lambda i: (0, 0))
    return pl.pallas_call(kern, name=name, grid=(L // tb,), in_specs=[row, vec], out_specs=row,
                          out_shape=jax.ShapeDtypeStruct((L, W), F32), compiler_params=_params("parallel"))(x, bias)


def _decay(name):
    @jax.custom_vjp
    def f(fg, b):
        return _scan_call(fg, b, None, True, name + "_fwd")[0]

    def fwd(fg, b):
        return f(fg, b), (fg, b)

    def bwd(res, dc):
        fg, b = res
        sg = _sigmoid_neg_call(fg, b, name + "_dsig")
        dfg_rev, db = _scan_call(dc[::-1], jnp.zeros_like(b), sg[::-1], False, name + "_bwd")
        return dfg_rev[::-1], db

    f.defvjp(fwd, bwd)
    return f


def _loss_call(hf, target, name):
    L, D = hf.shape
    nb = L // BLOCK

    def kern(h_ref, t_ref, loss_ref, dy_ref, acc):
        i = pl.program_id(0)

        @pl.when(i == 0)
        def _():
            acc[...] = jnp.zeros_like(acc)
            dy_ref[...] = jnp.zeros_like(dy_ref)

        @pl.when(i > 0)
        def _():
            e = h_ref[...] - t_ref[...]
            dy_ref[...] = e * (1.0 / D)
            acc[...] += jnp.sum(e * e, axis=0, keepdims=True)

        @pl.when(i == nb - 1)
        def _():
            loss_ref[...] = jnp.broadcast_to(jnp.sum(acc[...], axis=1, keepdims=True) * (0.5 / D), loss_ref.shape)

    return pl.pallas_call(
        kern, name=name, grid=(nb,),
        in_specs=[pl.BlockSpec((BLOCK, D), lambda i: (i, 0)), pl.BlockSpec((BLOCK, D), lambda i: (jnp.maximum(i - 1, 0), 0))],
        out_specs=[pl.BlockSpec((1, LANES), lambda i: (0, 0)), pl.BlockSpec((BLOCK, D), lambda i: (i, 0))],
        out_shape=[jax.ShapeDtypeStruct((1, LANES), F32), jax.ShapeDtypeStruct((L, D), F32)],
        scratch_shapes=[pltpu.VMEM((1, D), F32)],
        compiler_params=_params("arbitrary"))(hf, target)


def _loss_head(name):
    @jax.custom_vjp
    def f(hf, target):
        return _loss_call(hf, target, name)[0][0, 0]

    def fwd(hf, target):
        loss, dy = _loss_call(hf, target, name)
        return loss[0, 0], (dy, jnp.zeros((), F32))

    def bwd(res, g):
        dy, _ = res
        return dy * g, None

    f.defvjp(fwd, bwd)
    return f


def _me_and_peers():
    x, y, c = lax.axis_index("x"), lax.axis_index("y"), lax.axis_index("c")
    me = 4 * x + 2 * y + c
    peers = []
    for k in range(1, N_DEV):
        px = 1 - x if k & 4 else x
        py = 1 - y if k & 2 else y
        pc = 1 - c if k & 1 else c
        peers.append(((px, py, pc), 4 * px + 2 * py + pc))
    return me, peers


def _scatter(arrs, name):
    n = len(arrs)

    def body(*refs):
        srcs, outs = refs[:n], refs[n:2 * n]
        send_sems, recv_sems, local_sems = refs[2 * n:]
        me, peers = _me_and_peers()
        piece = lambda a, p: srcs[a].at[p]
        started = []
        for a in range(n):
            mine = pltpu.make_async_copy(piece(a, me), outs[a].at[me], local_sems.at[a])
            mine.start()
            started.append(mine.wait)
        for k, (dev, pid) in enumerate(peers):
            for a in range(n):
                cp = pltpu.make_async_remote_copy(src_ref=piece(a, pid), dst_ref=outs[a].at[me], send_sem=send_sems.at[a, k],
                                                  recv_sem=recv_sems.at[a, k], device_id=dev, device_id_type=MESH)
                cp.start()
                started.append(cp.wait_send)
        for k, (dev, pid) in enumerate(peers):
            for a in range(n):
                pltpu.make_async_remote_copy(src_ref=piece(a, me), dst_ref=outs[a].at[pid], send_sem=send_sems.at[a, k],
                                             recv_sem=recv_sems.at[a, k], device_id=dev, device_id_type=MESH).wait_recv()
        for wait in started:
            wait()

    hbm = pl.BlockSpec(memory_space=pl.ANY)
    return pl.pallas_call(
        body, name=name, out_shape=[jax.ShapeDtypeStruct(a.shape, a.dtype) for a in arrs],
        in_specs=[hbm] * n, out_specs=[hbm] * n,
        scratch_shapes=[pltpu.SemaphoreType.DMA((n, N_DEV - 1)), pltpu.SemaphoreType.DMA((n, N_DEV - 1)),
                        pltpu.SemaphoreType.DMA((n,))],
    )(*arrs)


def _gather(arrs, name):
    n = len(arrs)
    hops = N_DEV - 1

    def body(*refs):
        srcs, outs = refs[:n], refs[n:2 * n]
        send_sems, recv_sems, local_sems = refs[2 * n:]
        x, y, c = lax.axis_index("x"), lax.axis_index("y"), lax.axis_index("c")
        ident = lambda px, py, pc: 4 * px + 2 * py + pc
        me, sibling = ident(x, y, c), (x, y, 1 - c)
        chips = [(1 - x, y), (x, 1 - y), (1 - x, 1 - y)]

        def copy(a, k, block, to, src=None):
            return pltpu.make_async_remote_copy(
                src_ref=outs[a].at[block] if src is None else src, dst_ref=outs[a].at[block],
                send_sem=send_sems.at[a, k], recv_sem=recv_sems.at[a, k], device_id=to, device_id_type=MESH)

        started = []
        for a in range(n):
            mine = pltpu.make_async_copy(srcs[a], outs[a].at[me], local_sems.at[a])
            mine.start()
            started.append(mine.wait)
            for k, to in enumerate([sibling] + [(px, py, c) for px, py in chips]):
                cp = copy(a, k, me, to, src=srcs[a])
                cp.start()
                started.append(cp.wait_send)
        for j, (px, py) in enumerate(chips):
            for a in range(n):
                copy(a, 1 + j, ident(px, py, c), (x, y, c)).wait_recv()
                cp = copy(a, 4 + j, ident(px, py, c), sibling)
                cp.start()
                started.append(cp.wait_send)
        for a in range(n):
            copy(a, 0, ident(x, y, 1 - c), (x, y, c)).wait_recv()
            for j, (px, py) in enumerate(chips):
                copy(a, 4 + j, ident(px, py, 1 - c), (x, y, c)).wait_recv()
        for wait in started:
            wait()

    hbm = pl.BlockSpec(memory_space=pl.ANY)
    return pl.pallas_call(
        body, name=name, out_shape=[jax.ShapeDtypeStruct((N_DEV,) + tuple(a.shape), a.dtype) for a in arrs],
        in_specs=[hbm] * n, out_specs=[hbm] * n,
        scratch_shapes=[pltpu.SemaphoreType.DMA((n, hops)), pltpu.SemaphoreType.DMA((n, hops)),
                        pltpu.SemaphoreType.DMA((n,))],
    )(*arrs)


def _adamw_call(parts, w, m, v, name):
    shape = w.shape
    cols = shape[-1]
    rows = int(np.prod(shape[:-1]))
    tr = rows
    if rows > ADAM_ROWS:
        tr = max(t for t in range(SUBLANES_F32, ADAM_ROWS + 1, SUBLANES_F32) if rows % t == 0)
    c1 = 1.0 / (1.0 - ADAM_B1 ** ADAM_STEP)
    c2 = 1.0 / (1.0 - ADAM_B2 ** ADAM_STEP)

    def kern(p_ref, w_ref, m_ref, v_ref, g_ref, d_ref, nm_ref, nv_ref):
        g = p_ref[0].astype(F32)
        for p in range(1, N_DEV):
            g = g + p_ref[p].astype(F32)
        nm = ADAM_B1 * m_ref[...] + (1.0 - ADAM_B1) * g
        nv = ADAM_B2 * v_ref[...] + (1.0 - ADAM_B2) * (g * g)
        g_ref[...] = g
        nm_ref[...] = nm
        nv_ref[...] = nv
        d_ref[...] = -ADAM_LR * ((nm * c1) / (jnp.sqrt(nv * c2) + ADAM_EPS) + ADAM_WD * w_ref[...])

    slab = pl.BlockSpec((tr, cols), lambda i: (i, 0))
    flat = lambda t: t.reshape(rows, cols)
    res = pl.pallas_call(
        kern, name=name, grid=(rows // tr,),
        in_specs=[pl.BlockSpec((N_DEV, tr, cols), lambda i: (0, i, 0)), slab, slab, slab], out_specs=[slab] * 4,
        out_shape=[jax.ShapeDtypeStruct((rows, cols), F32)] * 4,
        compiler_params=_params("parallel"))(parts.reshape(N_DEV, rows, cols), flat(w), flat(m), flat(v))
    return [r.reshape(shape) for r in res]


def _to_full(stacked, axis):
    moved = jnp.moveaxis(stacked, 0, axis)
    s = list(moved.shape)
    return moved.reshape(s[:axis] + [s[axis] * s[axis + 1]] + s[axis + 2:])


def _heads(x, h, d):
    return x.reshape(x.shape[0], h, d).transpose(1, 0, 2)


def _unheads(x):
    return x.transpose(1, 0, 2).reshape(x.shape[1], -1)


def _pad_cols(x, width):
    return jnp.pad(x, ((0, 0), (0, width - x.shape[1])))


def _fox_mixer(h, w_in, b_f, w_o, tag):
    hd = FOX_HEADS * FOX_HD
    qkvT = _linear_t(tag + "_qkv", BF16)(w_in[:, :3 * hd].T, h)
    fg = _linear(tag + "_gate", F32)(h, _pad_cols(w_in[:, 3 * hd:], LANES))
    c = _decay(tag + "_decay")(fg, _pad_cols(b_f[None, :], LANES))
    oT = _fox_attention(tag + "_attn")(qkvT, c[:, :FOX_HEADS].T)
    return _linear_km(tag + "_out")(oT, w_o)


def _swa_mixer(h, w_in, sinks, w_o, tabs, tag):
    L = h.shape[0]
    qd, kd = SWA_QH * SWA_HD, SWA_KVH * SWA_HD
    proj = _linear(tag + "_qkv", F32)(h, w_in)
    rope = _rope(tag + "_rope", ROPE_DIM // 2, BF16)
    q = rope(proj[:, :qd], *tabs)
    k = rope(proj[:, qd:qd + kd], *tabs)
    v = proj[:, qd + kd:].astype(BF16)
    qg = _heads(q, SWA_QH, SWA_HD).reshape(SWA_KVH, SWA_G, L, SWA_HD)
    o = _swa_attn(tag + "_attn", SWA_HD ** -0.5)(qg, _heads(k, SWA_KVH, SWA_HD), _heads(v, SWA_KVH, SWA_HD),
                                                 sinks.reshape(SWA_KVH, SWA_G, 1, 1))
    return _linear(tag + "_out", F32)(_unheads(o.reshape(SWA_QH, L, SWA_HD)), w_o)


def _mla_mixer(h, w_a, g_q, g_kv, w_uq, w_ukv, w_o, tabs, tag):
    L = h.shape[0]
    cq = _linear(tag + "_aq", F32)(h, w_a[:, :MLA_QL])
    ckv = _linear(tag + "_akv", F32)(h, w_a[:, MLA_QL:MLA_QL + MLA_KVL])
    kr = _linear(tag + "_akr", F32)(h, _pad_cols(w_a[:, MLA_QL + MLA_KVL:], LANES))
    cq = _rmsnorm(tag + "_nq")(cq, g_q[None, :])
    ckv = _rmsnorm(tag + "_nkv")(ckv, g_kv[None, :])
    q = _linear(tag + "_uq", F32)(cq, w_uq).reshape(L, MLA_HEADS, MLA_NOPE + MLA_ROPE)
    kv = _linear(tag + "_ukv", BF16)(ckv, w_ukv).reshape(L, MLA_HEADS, MLA_NOPE + MLA_V)
    scale = (MLA_NOPE + MLA_ROPE) ** -0.5
    rope = _rope(tag + "_rope", MLA_ROPE // 2, BF16)
    q_rope = rope(q[:, :, MLA_NOPE:].reshape(L, MLA_HEADS * MLA_ROPE), *(t * scale for t in tabs))
    k_rope = rope(kr, *tabs)[:, :MLA_ROPE]
    qf = jnp.concatenate([(q[:, :, :MLA_NOPE] * scale).astype(BF16), q_rope.reshape(L, MLA_HEADS, MLA_ROPE)], axis=-1)
    kf = jnp.concatenate([kv[:, :, :MLA_NOPE], jnp.broadcast_to(k_rope[:, None, :], (L, MLA_HEADS, MLA_ROPE))], axis=-1)
    seq_last = lambda t: t.transpose(1, 2, 0)
    unit = jnp.zeros((MLA_HEADS, MLA_XROWS, L), BF16).at[:, 0, :].set(1.0)
    poison = jnp.zeros((MLA_HEADS, MLA_XROWS, L), BF16).at[:, 0, :].set(jnp.where(jnp.arange(L) < PAD, -POISON, 0.0))
    qT = jnp.concatenate([seq_last(qf), unit], axis=1).reshape(-1, L)
    kT = jnp.concatenate([seq_last(kf), poison], axis=1).reshape(-1, L)
    oT = _mla_attention(tag + "_attn")(qT, kT, seq_last(kv[:, :, MLA_NOPE:]).reshape(-1, L))
    return _linear_km(tag + "_out")(oT, w_o)


BIG = (("fox_w_in", 2), ("fox_w_o", 1), ("swa_w_in", 2), ("swa_w_o", 1), ("mla_w_a", 1), ("mla_w_uq", 2),
       ("mla_w_ukv", 2), ("mla_w_o", 1), ("ffn_w_in", 2), ("ffn_w_out", 1))
SMALL = (("meta_tokens", 1), ("mla_g_q", 1), ("mla_g_kv", 1), ("ffn_conv_w", 2))
SHARDED = BIG + SMALL
REPL = ("ln1_g", "ln1_b", "ln2_g", "ln2_b", "fox_b_f", "swa_sinks", "ffn_conv_b")
FFN_STACKED = ("ffn_w_in", "ffn_w_out", "ffn_conv_w")
WEIGHTS = ("meta_tokens", "ln1_g", "ln1_b", "ln2_g", "ln2_b", "fox_w_in", "fox_b_f", "fox_w_o", "swa_w_in", "swa_sinks",
           "swa_w_o", "mla_w_a", "mla_g_q", "mla_g_kv", "mla_w_uq", "mla_w_ukv", "mla_w_o", "ffn_w_in", "ffn_conv_w",
           "ffn_conv_b", "ffn_w_out")


def _local_loss(stacked, repl, x, target):
    S, D = x.shape
    L = S + BLOCK
    wts = {n: _to_full(stacked[n], ax) for n, ax in SHARDED if n not in FFN_STACKED}
    wts.update(repl)
    half = N_DEV // 2
    h = jnp.concatenate([jnp.zeros((PAD, D), F32), wts["meta_tokens"], x], axis=0)
    cos_p, sin_p = _rope_tables(L, ROPE_DIM, ROPE_THETA)
    tabs_p = _rope_lanes(cos_p, sin_p, SWA_HD)
    cos_m, sin_m = _rope_tables(L, MLA_ROPE, MLA_ROPE_THETA)
    tabs_m = _rope_lanes(cos_m, sin_m, MLA_ROPE)
    for i in range(DEPTH):
        kind, j = i % 3, i // 3
        tag = f"l{i}"
        if kind == 0:
            mix = _fox_mixer(h, wts["fox_w_in"][j], wts["fox_b_f"][j], wts["fox_w_o"][j], tag + "_fox")
        elif kind == 1:
            mix = _swa_mixer(h, wts["swa_w_in"][j], wts["swa_sinks"][j], wts["swa_w_o"][j], tabs_p, tag + "_swa")
        else:
            mix = _mla_mixer(h, wts["mla_w_a"][j], wts["mla_g_q"][j], wts["mla_g_kv"][j], wts["mla_w_uq"][j],
                             wts["mla_w_ukv"][j], wts["mla_w_o"][j], tabs_m, tag + "_mla")
        h = _deepnorm(tag + "_ln1")(h, mix, wts["ln1_g"][i][None, :], wts["ln1_b"][i][None, :])
        w3 = stacked["ffn_w_in"][:, i]
        shard = w3.shape[-1]
        cw4 = stacked["ffn_conv_w"][:, i].reshape(2, half, CONV_W, shard)
        cb4 = wts["ffn_conv_b"][i].reshape(2, half, 1, shard)
        wo3 = stacked["ffn_w_out"][:, i].reshape(half, shard, D)
        act3 = _ffn_up(tag + "_ffn")(h, w3, cw4, cb4)
        ffn = _ffn_down(tag + "_ffn_down")(act3, wo3)
        h = _deepnorm(tag + "_ln2")(h, ffn, wts["ln2_g"][i][None, :], wts["ln2_b"][i][None, :])
    return _loss_head("loss_head")(h, target)


def _step(x, target, w, m, v):
    names = [n for n, _ in SHARDED]
    big = {n for n, _ in BIG}
    stacked = dict(zip(names, _gather([w[n].astype(BF16) if n in big else w[n] for n in names], "gather_weights")))
    repl = {n: w[n] for n in REPL}
    loss, (g_st, g_repl, grad_x) = jax.value_and_grad(_local_loss, argnums=(0, 1, 2))(stacked, repl, x, target)
    got = dict(zip(names, _scatter([g_st[n] for n in names], "scatter_grads")))
    got.update(zip(REPL, _gather([g_repl[n] for n in REPL], "gather_repl_grads")))
    out = {}
    for n in WEIGHTS:
        for kind, a in zip(("grad", "delta", "new_m", "new_v"), _adamw_call(got[n], w[n], m[n], v[n], "adamw_" + n)):
            out[(kind, n)] = a
    return loss, grad_x, out


def kernel(x, meta_tokens, ln1_g, ln1_b, ln2_g, ln2_b, fox_w_in, fox_b_f, fox_w_o, swa_w_in, swa_sinks, swa_w_o, mla_w_a, mla_g_q, mla_g_kv, mla_w_uq, mla_w_ukv, mla_w_o, ffn_w_in, ffn_conv_w, ffn_conv_b, ffn_w_out, loss_target, m_meta_tokens, m_ln1_g, m_ln1_b, m_ln2_g, m_ln2_b, m_fox_w_in, m_fox_b_f, m_fox_w_o, m_swa_w_in, m_swa_sinks, m_swa_w_o, m_mla_w_a, m_mla_g_q, m_mla_g_kv, m_mla_w_uq, m_mla_w_ukv, m_mla_w_o, m_ffn_w_in, m_ffn_conv_w, m_ffn_conv_b, m_ffn_w_out, v_meta_tokens, v_ln1_g, v_ln1_b, v_ln2_g, v_ln2_b, v_fox_w_in, v_fox_b_f, v_fox_w_o, v_swa_w_in, v_swa_sinks, v_swa_w_o, v_mla_w_a, v_mla_g_q, v_mla_g_kv, v_mla_w_uq, v_mla_w_ukv, v_mla_w_o, v_ffn_w_in, v_ffn_conv_w, v_ffn_conv_b, v_ffn_w_out):
    args = (meta_tokens, ln1_g, ln1_b, ln2_g, ln2_b, fox_w_in, fox_b_f, fox_w_o, swa_w_in, swa_sinks, swa_w_o, mla_w_a,
            mla_g_q, mla_g_kv, mla_w_uq, mla_w_ukv, mla_w_o, ffn_w_in, ffn_conv_w, ffn_conv_b, ffn_w_out)
    ms = (m_meta_tokens, m_ln1_g, m_ln1_b, m_ln2_g, m_ln2_b, m_fox_w_in, m_fox_b_f, m_fox_w_o, m_swa_w_in, m_swa_sinks,
          m_swa_w_o, m_mla_w_a, m_mla_g_q, m_mla_g_kv, m_mla_w_uq, m_mla_w_ukv, m_mla_w_o, m_ffn_w_in, m_ffn_conv_w,
          m_ffn_conv_b, m_ffn_w_out)
    vs = (v_meta_tokens, v_ln1_g, v_ln1_b, v_ln2_g, v_ln2_b, v_fox_w_in, v_fox_b_f, v_fox_w_o, v_swa_w_in, v_swa_sinks,
          v_swa_w_o, v_mla_w_a, v_mla_g_q, v_mla_g_kv, v_mla_w_uq, v_mla_w_ukv, v_mla_w_o, v_ffn_w_in, v_ffn_conv_w,
          v_ffn_conv_b, v_ffn_w_out)
    w = dict(zip(WEIGHTS, args))
    m = dict(zip(WEIGHTS, ms))
    v = dict(zip(WEIGHTS, vs))
    loss, grad_x, out = _step(x[0], loss_target[0], w, m, v)
    loss = lax.psum(loss, ("x", "y", "c"))
    res = [loss, grad_x[None]]
    for kind in ("grad", "delta", "new_m", "new_v"):
        res += [out[(kind, n)] for n in WEIGHTS]
    return tuple(res)
```

```python
import jax
import jax.numpy as jnp
import numpy as np
from jax import lax
from jax.experimental import pallas as pl
from jax.experimental.pallas import tpu as pltpu

F32 = jnp.float32
BF16 = jnp.bfloat16

D_MODEL = 1024
DEPTH = 4
N_META = 16
BLOCK = 128
PAD = BLOCK - N_META
NEG = -1e30
ALPHA = (2.0 * DEPTH) ** 0.25
LN_EPS = 1e-5
RMS_EPS = 1e-6
FOX_HEADS, FOX_HD = 16, 64
SWA_QH, SWA_KVH, SWA_HD = 16, 2, 64
SWA_G = SWA_QH // SWA_KVH
ROPE_THETA = 500000.0
ROPE_DIM = SWA_HD // 4
MLA_HEADS, MLA_QL, MLA_KVL, MLA_NOPE, MLA_ROPE, MLA_V = 16, 384, 256, 64, 32, 64
MLA_ROPE_THETA = 10000.0
D_FF = 2816
CONV_W = 3
ADAM_LR, ADAM_B1, ADAM_B2, ADAM_EPS, ADAM_WD, ADAM_STEP = 0.001, 0.9, 0.999, 1e-08, 0.01, 10

LANES = 128
SUBLANES_F32 = 8
SUBLANES_BF16 = 16
VMEM_LIMIT = 48 * 1024 * 1024
N_DEV = 8
MESH = pl.DeviceIdType.MESH
ROW_TILE = 640
ADAM_ROWS = 256
MM_TILE_M, MM_TILE_N, MM_TILE_K = 1664, 1024, 1664
ATTN_HEADS_FWD, ATTN_HEADS_BWD = 4, 2


def _tile(n, cap):
    if n <= cap:
        return n
    best = 0
    for t in range(LANES, cap + 1, LANES):
        if n % t == 0:
            best = t
    assert best, (n, cap)
    return best


def _params(*sem):
    return pltpu.CompilerParams(dimension_semantics=sem, vmem_limit_bytes=VMEM_LIMIT)


def _mm(a, b, *, la="mk", lb="kn", sa=None, sb=None, so=None, out_dtype, name):
    size, tile, parts = {}, {}, {}
    for x, lay, split in ((a, la, sa), (b, lb, sb)):
        shp = x.shape[1:] if split else x.shape
        for ax, n in zip(lay, shp):
            if ax == split:
                size[ax], tile[ax], parts[ax] = x.shape[0] * n, n, x.shape[0]
            else:
                assert size.setdefault(ax, n) == n, (name, ax, a.shape, b.shape)
    for ax, cap in (("m", MM_TILE_M), ("n", MM_TILE_N), ("k", MM_TILE_K)):
        tile.setdefault(ax, _tile(size[ax], cap))
    grid = tuple(size[ax] // tile[ax] for ax in "mnk")
    nk = grid[2]
    dn = (((la.index("k"),), (lb.index("k"),)), ((), ()))

    def kern(a_ref, b_ref, o_ref, *acc):
        p = lax.dot_general(a_ref[...].astype(BF16), b_ref[...].astype(BF16), dn, preferred_element_type=F32)
        if nk == 1:
            o_ref[...] = p.astype(o_ref.dtype)
            return
        acc_ref, = acc
        k = pl.program_id(2)

        @pl.when(k == 0)
        def _():
            acc_ref[...] = p

        @pl.when(k > 0)
        def _():
            acc_ref[...] += p

        @pl.when(k == nk - 1)
        def _():
            o_ref[...] = acc_ref[...].astype(o_ref.dtype)

    def spec(lay, split):
        blk = tuple(tile[ax] for ax in lay)
        if split is None:
            return pl.BlockSpec(blk, lambda i, j, k: tuple({"m": i, "n": j, "k": k}[ax] for ax in lay))
        return pl.BlockSpec((None,) + blk, lambda i, j, k: ({"m": i, "n": j, "k": k}[split],) + tuple(
            0 if ax == split else {"m": i, "n": j, "k": k}[ax] for ax in lay))

    if so is None:
        out_shape = (size["m"], size["n"])
    else:
        out_shape = (parts[so],) + tuple(tile[ax] if ax == so else size[ax] for ax in "mn")
    return pl.pallas_call(
        kern, name=name, grid=grid,
        in_specs=[spec(la, sa), spec(lb, sb)], out_specs=spec("mn", so),
        out_shape=jax.ShapeDtypeStruct(out_shape, out_dtype),
        scratch_shapes=[] if nk == 1 else [pltpu.VMEM((tile["m"], tile["n"]), F32)],
        compiler_params=_params("parallel", "parallel", "arbitrary"),
    )(a, b)


def _linear(name, out_dtype):
    @jax.custom_vjp
    def f(x, w):
        return _mm(x, w, out_dtype=out_dtype, name=name + "_fwd")

    def fwd(x, w):
        return f(x, w), (x, w)

    def bwd(res, g):
        x, w = res
        dx = _mm(g, w, lb="nk", out_dtype=x.dtype, name=name + "_dx")
        dw = _mm(x, g, la="km", out_dtype=w.dtype, name=name + "_dw")
        return dx, dw

    f.defvjp(fwd, bwd)
    return f


def _ln_stats(z):
    mu = jnp.mean(z, axis=-1, keepdims=True)
    zc = z - mu
    var = jnp.mean(zc * zc, axis=-1, keepdims=True)
    return zc, lax.rsqrt(var + LN_EPS)


def _ln_fwd_call(h, mix, g, b, name):
    L, D = h.shape
    tm = _tile(L, ROW_TILE)

    def kern(h_ref, m_ref, g_ref, b_ref, o_ref):
        zc, rstd = _ln_stats(ALPHA * h_ref[...] + m_ref[...])
        o_ref[...] = zc * rstd * g_ref[...] + b_ref[...]

    row = pl.BlockSpec((tm, D), lambda i: (i, 0))
    vec = pl.BlockSpec((1, D), lambda i: (0, 0))
    return pl.pallas_call(kern, name=name, grid=(L // tm,), in_specs=[row, row, vec, vec], out_specs=row,
                          out_shape=jax.ShapeDtypeStruct((L, D), F32), compiler_params=_params("parallel"))(h, mix, g, b)


def _ln_bwd_call(h, mix, g, dout, name):
    L, D = h.shape
    tm = _tile(L, ROW_TILE)

    def kern(h_ref, m_ref, g_ref, d_ref, dz_ref, dg_ref, db_ref):
        i = pl.program_id(0)
        zc, rstd = _ln_stats(ALPHA * h_ref[...] + m_ref[...])
        xhat = zc * rstd
        d = d_ref[...]
        dxh = d * g_ref[...]
        m1 = jnp.mean(dxh, axis=-1, keepdims=True)
        m2 = jnp.mean(dxh * xhat, axis=-1, keepdims=True)
        row = i * tm + lax.broadcasted_iota(jnp.int32, (tm, 1), 0)
        dz_ref[...] = jnp.where(row >= PAD, rstd * (dxh - m1 - xhat * m2), 0.0)
        pg = jnp.sum(d * xhat, axis=0, keepdims=True)
        pb = jnp.sum(d, axis=0, keepdims=True)

        @pl.when(i == 0)
        def _():
            dg_ref[...] = pg
            db_ref[...] = pb

        @pl.when(i > 0)
        def _():
            dg_ref[...] += pg
            db_ref[...] += pb

    row = pl.BlockSpec((tm, D), lambda i: (i, 0))
    vec = pl.BlockSpec((1, D), lambda i: (0, 0))
    return pl.pallas_call(
        kern, name=name, grid=(L // tm,), in_specs=[row, row, vec, row], out_specs=[row, vec, vec],
        out_shape=[jax.ShapeDtypeStruct((L, D), F32), jax.ShapeDtypeStruct((1, D), F32), jax.ShapeDtypeStruct((1, D), F32)],
        compiler_params=_params("arbitrary"))(h, mix, g, dout)


def _deepnorm(name):
    @jax.custom_vjp
    def f(h, mix, g, b):
        return _ln_fwd_call(h, mix, g, b, name + "_fwd")

    def fwd(h, mix, g, b):
        return f(h, mix, g, b), (h, mix, g)

    def bwd(res, dout):
        h, mix, g = res
        dz, dg, db = _ln_bwd_call(h, mix, g, dout, name + "_bwd")
        return ALPHA * dz, dz, dg, db

    f.defvjp(fwd, bwd)
    return f


def _rms_fwd_call(x, g, name):
    L, n = x.shape
    tm = _tile(L, ROW_TILE)

    def kern(x_ref, g_ref, o_ref):
        x = x_ref[...]
        o_ref[...] = x * lax.rsqrt(jnp.mean(x * x, axis=-1, keepdims=True) + RMS_EPS) * g_ref[...]

    row = pl.BlockSpec((tm, n), lambda i: (i, 0))
    vec = pl.BlockSpec((1, n), lambda i: (0, 0))
    return pl.pallas_call(kern, name=name, grid=(L // tm,), in_specs=[row, vec], out_specs=row,
                          out_shape=jax.ShapeDtypeStruct((L, n), F32), compiler_params=_params("parallel"))(x, g)


def _rms_bwd_call(x, g, dout, name):
    L, n = x.shape
    tm = _tile(L, ROW_TILE)

    def kern(x_ref, g_ref, d_ref, dx_ref, dg_ref):
        i = pl.program_id(0)
        x = x_ref[...]
        rstd = lax.rsqrt(jnp.mean(x * x, axis=-1, keepdims=True) + RMS_EPS)
        xhat = x * rstd
        d = d_ref[...]
        dxh = d * g_ref[...]
        dx_ref[...] = rstd * (dxh - xhat * jnp.mean(dxh * xhat, axis=-1, keepdims=True))
        pg = jnp.sum(d * xhat, axis=0, keepdims=True)

        @pl.when(i == 0)
        def _():
            dg_ref[...] = pg

        @pl.when(i > 0)
        def _():
            dg_ref[...] += pg

    row = pl.BlockSpec((tm, n), lambda i: (i, 0))
    vec = pl.BlockSpec((1, n), lambda i: (0, 0))
    return pl.pallas_call(
        kern, name=name, grid=(L // tm,), in_specs=[row, vec, row], out_specs=[row, vec],
        out_shape=[jax.ShapeDtypeStruct((L, n), F32), jax.ShapeDtypeStruct((1, n), F32)],
        compiler_params=_params("arbitrary"))(x, g, dout)


def _rmsnorm(name):
    @jax.custom_vjp
    def f(x, g):
        return _rms_fwd_call(x, g, name + "_fwd")

    def fwd(x, g):
        return f(x, g), (x, g)

    def bwd(res, dout):
        x, g = res
        dx, dg = _rms_bwd_call(x, g, dout, name + "_bwd")
        return dx, dg

    f.defvjp(fwd, bwd)
    return f


def _rope_call(x, c, s1, s2, r, out_dtype, name):
    L, W = x.shape
    reps = W // LANES
    tm = _tile(L, ROW_TILE)

    def kern(x_ref, c_ref, s1_ref, s2_ref, o_ref):
        x = x_ref[...].astype(F32)
        wide = lambda t: jnp.tile(t[...], (1, reps)) if reps > 1 else t[...]
        out = x * wide(c_ref) + pltpu.roll(x, W - r, 1) * wide(s1_ref) + pltpu.roll(x, r, 1) * wide(s2_ref)
        o_ref[...] = out.astype(o_ref.dtype)

    row = pl.BlockSpec((tm, W), lambda i: (i, 0))
    tab = pl.BlockSpec((tm, LANES), lambda i: (i, 0))
    return pl.pallas_call(kern, name=name, grid=(L // tm,), in_specs=[row, tab, tab, tab], out_specs=row,
                          out_shape=jax.ShapeDtypeStruct((L, W), out_dtype), compiler_params=_params("parallel"))(x, c, s1, s2)


def _rope(name, r, out_dtype):
    @jax.custom_vjp
    def f(x, c, s1, s2):
        return _rope_call(x, c, s1, s2, r, out_dtype, name + "_fwd")

    def fwd(x, c, s1, s2):
        return f(x, c, s1, s2), (c, s1, s2, jnp.zeros((), x.dtype))

    def bwd(res, g):
        c, s1, s2, proto = res
        dx = _rope_call(g, c, -s1, -s2, r, proto.dtype, name + "_bwd")
        return dx, jnp.zeros_like(c), jnp.zeros_like(s1), jnp.zeros_like(s2)

    f.defvjp(fwd, bwd)
    return f


def _rope_tables(L, dim, theta):
    pos = (jnp.arange(L) - PAD).astype(F32)
    inv = theta ** (-jnp.arange(0, dim, 2, dtype=F32) / dim)
    ang = pos[:, None] * inv[None, :]
    return jnp.cos(ang), jnp.sin(ang)


def _rope_lanes(cos, sin, period):
    L, half = cos.shape
    one = jnp.ones((L, period - 2 * half), F32)
    zero = jnp.zeros((L, period - 2 * half), F32)
    z_h = jnp.zeros((L, half), F32)
    c = jnp.concatenate([cos, cos, one], axis=1)
    s1 = jnp.concatenate([-sin, z_h, zero], axis=1)
    s2 = jnp.concatenate([z_h, sin, zero], axis=1)
    reps = LANES // period
    return tuple(jnp.tile(t, (1, reps)) for t in (c, s1, s2))


HALO = SUBLANES_BF16


GLU_ROWS = 128


def _strips(C):
    return [(c0, min(LANES, C - c0)) for c0 in range(0, C, LANES)]


def _shifted_down(win):
    return [pltpu.roll(win, CONV_W - 1 - t, 0)[SUBLANES_F32:, :] if t < CONV_W - 1 else win[SUBLANES_F32:, :]
            for t in range(CONV_W)]


def _conv_taps(buf_ref, g, off, rows, cols, w, b):
    u = _shifted_down(buf_ref[g, pl.ds(off - SUBLANES_F32, rows + SUBLANES_F32), cols])
    return b + u[0] * w[0:1, :] + u[1] * w[1:2, :] + u[2] * w[2:3, :]


def _glu_fwd_call(u4, cw4, cb4, name):
    _, P, L, C = u4.shape
    tm = _tile(L, ROW_TILE)
    hb = tm // HALO

    def kern(u_ref, up_ref, w_ref, b_ref, o_ref, buf):
        i = pl.program_id(1)
        row = i * tm + lax.broadcasted_iota(jnp.int32, (tm, 1), 0)
        prow = i * tm - HALO + lax.broadcasted_iota(jnp.int32, (HALO, 1), 0)
        for g in range(2):
            buf[g, pl.ds(HALO, tm), :] = jnp.where(row >= PAD, u_ref[g], 0.0)
            buf[g, pl.ds(0, HALO), :] = jnp.where(prow >= PAD, up_ref[g], 0.0)
        for c0, wd in _strips(C):
            cols = pl.ds(c0, wd)
            ws = [w_ref[g][:, c0:c0 + wd] for g in range(2)]
            bs = [b_ref[g][:, c0:c0 + wd] for g in range(2)]
            for r in range(0, tm, GLU_ROWS):
                n = min(GLU_ROWS, tm - r)
                gate, val = (_conv_taps(buf, g, HALO + r, n, cols, ws[g], bs[g]) for g in range(2))
                o_ref[pl.ds(r, n), cols] = (gate * jax.nn.sigmoid(gate) * val).astype(o_ref.dtype)

    return pl.pallas_call(
        kern, name=name, grid=(P, L // tm),
        in_specs=[pl.BlockSpec((2, None, tm, C), lambda p, i: (0, p, i, 0)),
                  pl.BlockSpec((2, None, HALO, C), lambda p, i: (0, p, jnp.maximum(i * hb - 1, 0), 0)),
                  pl.BlockSpec((2, None, CONV_W, C), lambda p, i: (0, p, 0, 0)),
                  pl.BlockSpec((2, None, 1, C), lambda p, i: (0, p, 0, 0))],
        out_specs=pl.BlockSpec((None, tm, C), lambda p, i: (p, i, 0)),
        out_shape=jax.ShapeDtypeStruct((P, L, C), BF16),
        scratch_shapes=[pltpu.VMEM((2, tm + HALO, C), F32)],
        compiler_params=_params("parallel", "parallel"))(u4, u4, cw4, cb4)


def _glu_bwd_call(u4, cw4, cb4, dact, name):
    _, P, L, C = u4.shape
    tm = _tile(L, ROW_TILE)
    hb = tm // HALO
    ext = tm + HALO
    last_halo = L // HALO - 1

    def kern(u_ref, up_ref, un_ref, w_ref, b_ref, d_ref, dn_ref, du_ref, dw_ref, db_ref, ubuf, dbuf):
        i = pl.program_id(1)
        r0 = i * tm
        mask = lambda blk, start: jnp.where(
            (start + lax.broadcasted_iota(jnp.int32, (blk.shape[0], 1), 0) >= PAD), blk, 0.0)
        for g in range(2):
            ubuf[g, pl.ds(0, HALO), :] = mask(up_ref[g], r0 - HALO)
            ubuf[g, pl.ds(HALO, tm), :] = mask(u_ref[g], r0)
            ubuf[g, pl.ds(HALO + tm, HALO), :] = un_ref[g]
        dbuf[pl.ds(0, tm), :] = d_ref[...].astype(F32)
        dbuf[pl.ds(tm, HALO), :] = jnp.where(r0 + tm < L, dn_ref[...].astype(F32), 0.0)
        for c0, wd in _strips(C):
            cols = pl.ds(c0, wd)
            ws = [w_ref[g][:, c0:c0 + wd] for g in range(2)]
            bs = [b_ref[g][:, c0:c0 + wd] for g in range(2)]
            pw = [[jnp.zeros((1, wd), F32) for _ in range(CONV_W)] for _ in range(2)]
            pb = [jnp.zeros((1, wd), F32) for _ in range(2)]
            for r in range(0, tm, GLU_ROWS):
                n = min(GLU_ROWS, tm - r)
                ne = n + SUBLANES_F32
                us = [_shifted_down(ubuf[g, pl.ds(HALO + r - SUBLANES_F32, ne + SUBLANES_F32), cols]) for g in range(2)]
                gate, val = (bs[g] + us[g][0] * ws[g][0:1, :] + us[g][1] * ws[g][1:2, :] + us[g][2] * ws[g][2:3, :]
                             for g in range(2))
                sg = jax.nn.sigmoid(gate)
                d = dbuf[pl.ds(r, ne), cols]
                dys = (d * val * (sg * (1.0 + gate * (1.0 - sg))), d * (gate * sg))
                row = r0 + r + lax.broadcasted_iota(jnp.int32, (n, 1), 0)
                for g in range(2):
                    w = ws[g]
                    dy = dys[g][:n, :]
                    du = (dy * w[2:3, :] + pltpu.roll(dys[g], ne - 1, 0)[:n, :] * w[1:2, :]
                          + pltpu.roll(dys[g], ne - 2, 0)[:n, :] * w[0:1, :])
                    du_ref[g, pl.ds(r, n), cols] = jnp.where(row >= PAD, du, 0.0).astype(du_ref.dtype)
                    for t in range(CONV_W):
                        pw[g][t] = pw[g][t] + jnp.sum(dy * us[g][t][:n, :], axis=0, keepdims=True)
                    pb[g] = pb[g] + jnp.sum(dy, axis=0, keepdims=True)
            for g in range(2):
                pwg = jnp.concatenate(pw[g], axis=0)

                @pl.when(i == 0)
                def _():
                    dw_ref[g, :, cols] = pwg
                    db_ref[g, :, cols] = pb[g]

                @pl.when(i > 0)
                def _():
                    dw_ref[g, :, cols] += pwg
                    db_ref[g, :, cols] += pb[g]

    nxt = lambda i: jnp.minimum((i + 1) * hb, last_halo)
    return pl.pallas_call(
        kern, name=name, grid=(P, L // tm),
        in_specs=[pl.BlockSpec((2, None, tm, C), lambda p, i: (0, p, i, 0)),
                  pl.BlockSpec((2, None, HALO, C), lambda p, i: (0, p, jnp.maximum(i * hb - 1, 0), 0)),
                  pl.BlockSpec((2, None, HALO, C), lambda p, i: (0, p, nxt(i), 0)),
                  pl.BlockSpec((2, None, CONV_W, C), lambda p, i: (0, p, 0, 0)),
                  pl.BlockSpec((2, None, 1, C), lambda p, i: (0, p, 0, 0)),
                  pl.BlockSpec((None, tm, C), lambda p, i: (p, i, 0)),
                  pl.BlockSpec((None, HALO, C), lambda p, i: (p, nxt(i), 0))],
        out_specs=[pl.BlockSpec((2, None, tm, C), lambda p, i: (0, p, i, 0)),
                   pl.BlockSpec((2, None, CONV_W, C), lambda p, i: (0, p, 0, 0)),
                   pl.BlockSpec((2, None, 1, C), lambda p, i: (0, p, 0, 0))],
        out_shape=[jax.ShapeDtypeStruct((2, P, L, C), BF16), jax.ShapeDtypeStruct((2, P, CONV_W, C), F32),
                   jax.ShapeDtypeStruct((2, P, 1, C), F32)],
        scratch_shapes=[pltpu.VMEM((2, tm + 2 * HALO, C), F32), pltpu.VMEM((ext, C), F32)],
        compiler_params=_params("parallel", "arbitrary"))(u4, u4, u4, cw4, cb4, dact, dact)


def _ffn_up(name):
    def run(h1, w3, cw4, cb4):
        u3 = _mm(h1, w3, sb="n", so="n", out_dtype=F32, name=name + "_up")
        u4 = u3.reshape((2, u3.shape[0] // 2) + u3.shape[1:])
        return _glu_fwd_call(u4, cw4, cb4, name + "_glu"), u4

    @jax.custom_vjp
    def f(h1, w3, cw4, cb4):
        return run(h1, w3, cw4, cb4)[0]

    def fwd(h1, w3, cw4, cb4):
        act, u4 = run(h1, w3, cw4, cb4)
        return act, (h1, w3, cw4, cb4, u4)

    def bwd(res, dact):
        h1, w3, cw4, cb4, u4 = res
        du4, dcw, dcb = _glu_bwd_call(u4, cw4, cb4, dact, name + "_glu_bwd")
        du3 = du4.reshape((du4.shape[0] * du4.shape[1],) + du4.shape[2:])
        dh1 = _mm(du3, w3, sa="k", lb="nk", sb="k", out_dtype=F32, name=name + "_up_dx")
        dw3 = _mm(h1, du3, la="km", sb="n", so="n", out_dtype=w3.dtype, name=name + "_up_dw")
        return dh1, dw3, dcw, dcb

    f.defvjp(fwd, bwd)
    return f


def _ffn_down(name):
    @jax.custom_vjp
    def f(act3, wo3):
        return _mm(act3, wo3, sa="k", sb="k", out_dtype=F32, name=name + "_fwd")

    def fwd(act3, wo3):
        return f(act3, wo3), (act3, wo3)

    def bwd(res, g):
        act3, wo3 = res
        dact = _mm(g, wo3, lb="nk", sb="n", so="n", out_dtype=act3.dtype, name=name + "_dx")
        dwo = _mm(act3, g, la="km", sa="m", so="m", out_dtype=wo3.dtype, name=name + "_dw")
        return dact, dwo

    f.defvjp(fwd, bwd)
    return f


def _pairs(n, kv_major):
    if kv_major:
        pr = [(i, j) for j in range(n) for i in range(j, n)]
    else:
        pr = [(i, j) for i in range(n) for j in range(i + 1)]
    return (jnp.asarray(np.array([p[0] for p in pr], np.int32)), jnp.asarray(np.array([p[1] for p in pr], np.int32)))


TN = (((0,), (0,)), ((), ()))
NT = (((1,), (1,)), ((), ()))


def _scores(kT, qT, i, j, tq, tk, masked):
    s = lax.dot_general(kT, qT, TN, preferred_element_type=F32)
    if not masked:
        return s, None
    col = j * tk + lax.broadcasted_iota(jnp.int32, (tk, tq), 0)
    row = i * tq + lax.broadcasted_iota(jnp.int32, (tk, tq), 1)
    mask = (col <= row) & (col >= PAD)
    return jnp.where(mask, s, NEG), mask


FOX_SCALE = 0.125
C_PARTS = 3


class _AttnLayout:
    def __init__(self, heads, q_rows, v_rows, bases=(0, 0, 0), extra=0, q_scale=None, grad_rows=None):
        self.heads, self.q_rows, self.v_rows, self.bases, self.extra, self.q_scale = heads, q_rows, v_rows, bases, extra, q_scale
        self.dk = q_rows + extra
        self.grad_rows = q_rows if grad_rows is None else grad_rows
        self.n_in = 5 if extra else 3

    def specs(self, tq, tk, hp):
        assert self.heads % hp == 0 and all(b % hp == 0 for b in self.bases)
        qb, kb, vb = (b // hp for b in self.bases)
        qmap = lambda g, t, it, jt: (qb + g, it[t])
        kmap = lambda g, t, it, jt: (kb + g, jt[t])
        vmap = lambda g, t, it, jt: (vb + g, jt[t])
        s = [pl.BlockSpec((hp * self.q_rows, tq), qmap), pl.BlockSpec((hp * self.q_rows, tk), kmap),
             pl.BlockSpec((hp * self.v_rows, tk), vmap)]
        if self.extra:
            s += [pl.BlockSpec((hp * self.extra, tq), lambda g, t, it, jt: (g, it[t])),
                  pl.BlockSpec((hp * self.extra, tk), lambda g, t, it, jt: (g, jt[t]))]
        return s

    def operands(self, refs, a):
        rows = lambda ref, n: ref[a * n:(a + 1) * n, :]
        q, k = rows(refs[0], self.q_rows), rows(refs[1], self.q_rows)
        if self.q_scale is not None:
            q = q * jnp.asarray(self.q_scale, q.dtype)
        if self.extra:
            q = jnp.concatenate([q, rows(refs[3], self.extra)], axis=0)
            k = jnp.concatenate([k, rows(refs[4], self.extra)], axis=0)
        return q, k, rows(refs[2], self.v_rows)


def _attn_fwd_call(lay, arrs, name):
    L = arrs[0].shape[1]
    H, dv, hp = lay.heads, lay.v_rows, ATTN_HEADS_FWD
    tq = tk = _tile(L, ROW_TILE)
    it, jt = _pairs(L // tq, False)

    def kern(it_ref, jt_ref, *refs):
        t = pl.program_id(1)
        i, j = it_ref[t], jt_ref[t]
        o_ref, lse_ref, m_s, l_s, acc_s = refs[lay.n_in:]

        @pl.when(j == 0)
        def _():
            m_s[...] = jnp.full_like(m_s, NEG)
            l_s[...] = jnp.zeros_like(l_s)
            acc_s[...] = jnp.zeros_like(acc_s)

        def step(masked):
            ops = [lay.operands(refs, a) for a in range(hp)]
            scores = [_scores(k, q, i, j, tq, tk, masked) for q, k, _ in ops]
            weights = []
            for a, (s, mask) in enumerate(scores):
                m_prev = m_s[a]
                m_new = jnp.maximum(m_prev, jnp.max(s, axis=0, keepdims=True))
                scale = jnp.exp(m_prev - m_new)
                p = jnp.exp(s - m_new)
                if masked:
                    p = jnp.where(mask, p, 0.0)
                l_s[a] = scale * l_s[a] + jnp.sum(p, axis=0, keepdims=True)
                m_s[a] = m_new
                weights.append((scale, p.astype(BF16)))
            for a, (scale, pb) in enumerate(weights):
                acc_s[a] = scale * acc_s[a] + jnp.dot(ops[a][2], pb, preferred_element_type=F32)

        pl.when(j == i)(lambda: step(True))
        pl.when(j != i)(lambda: step(False))

        @pl.when(j == i)
        def _():
            for a in range(hp):
                l = l_s[a]
                l = jnp.where(l == 0.0, 1.0, l)
                o_ref[a * dv:(a + 1) * dv, :] = (acc_s[a] / l).astype(o_ref.dtype)
                lse_ref[a] = m_s[a] + jnp.log(l)

    return pl.pallas_call(
        kern, name=name,
        grid_spec=pltpu.PrefetchScalarGridSpec(
            num_scalar_prefetch=2, grid=(H // hp, int(it.shape[0])), in_specs=lay.specs(tq, tk, hp),
            out_specs=[pl.BlockSpec((hp * dv, tq), lambda g, t, it, jt: (g, it[t])),
                       pl.BlockSpec((hp, 1, tq), lambda g, t, it, jt: (g, 0, it[t]))],
            scratch_shapes=[pltpu.VMEM((hp, 1, tq), F32), pltpu.VMEM((hp, 1, tq), F32), pltpu.VMEM((hp, dv, tq), F32)]),
        out_shape=[jax.ShapeDtypeStruct((H * dv, L), F32), jax.ShapeDtypeStruct((H, 1, L), F32)],
        compiler_params=_params("parallel", "arbitrary"))(it, jt, *arrs)


C_ROWS = SUBLANES_F32


def _attn_bwd_call(lay, arrs, oT, lse, doT, name):
    L = arrs[0].shape[1]
    H, dv, dk, gr, hp = lay.heads, lay.v_rows, lay.dk, lay.grad_rows, ATTN_HEADS_BWD
    tq = tk = _tile(L, ROW_TILE)
    nq = L // tq
    it, jt = _pairs(nq, True)
    npairs = int(it.shape[0])
    with_c = lay.extra > 0
    crows = pl.ds(lay.q_rows, C_ROWS)

    def kern(it_ref, jt_ref, *refs):
        t = pl.program_id(1)
        i, j = it_ref[t], jt_ref[t]
        rest = refs[lay.n_in:]
        o_ref, do_ref, lse_ref = rest[:3]
        if with_c:
            dq_ref, dk_ref, dv_ref, dqc_ref, dkc_ref, dq_s, dk_s, dv_s = rest[3:]
        else:
            dq_ref, dk_ref, dv_ref, dq_s, dk_s, dv_s = rest[3:]

        @pl.when(t == 0)
        def _():
            dq_s[...] = jnp.zeros_like(dq_s)

        @pl.when(i == j)
        def _():
            dk_s[...] = jnp.zeros_like(dk_s)
            dv_s[...] = jnp.zeros_like(dv_s)

        def step(masked):
            first = []
            for a in range(hp):
                q, k, v = lay.operands(refs, a)
                s, mask = _scores(k, q, i, j, tq, tk, masked)
                dof = do_ref[a * dv:(a + 1) * dv, :]
                dob = dof.astype(BF16)
                delta = jnp.sum(dof * o_ref[a * dv:(a + 1) * dv, :], axis=0, keepdims=True)
                dp = lax.dot_general(v, dob, TN, preferred_element_type=F32)
                first.append((q, k, s, mask, dob, delta, dp))
            second = []
            for a, (q, k, s, mask, dob, delta, dp) in enumerate(first):
                p = jnp.exp(s - lse_ref[a])
                if masked:
                    p = jnp.where(mask, p, 0.0)
                second.append((p.astype(BF16), (p * (dp - delta)).astype(BF16)))
            cols = pl.ds(pl.multiple_of(i * tq, tq), tq)
            for a, (pb, dsb) in enumerate(second):
                q, k, _, _, dob, _, _ = first[a]
                dv_s[a] += lax.dot_general(dob, pb, NT, preferred_element_type=F32)
                dk_s[a] += lax.dot_general(q, dsb, NT, preferred_element_type=F32)
                dq_s[a, :, cols] += jnp.dot(k, dsb, preferred_element_type=F32)

        pl.when(j == i)(lambda: step(True))
        pl.when(j != i)(lambda: step(False))

        @pl.when(i == nq - 1)
        def _():
            for a in range(hp):
                dk_ref[a * gr:(a + 1) * gr, :] = dk_s[a, :gr, :].astype(dk_ref.dtype)
                dv_ref[a * dv:(a + 1) * dv, :] = dv_s[a].astype(dv_ref.dtype)
                if with_c:
                    dkc_ref[a * C_ROWS:(a + 1) * C_ROWS, :] = dk_s[a, crows, :]

        @pl.when(t == npairs - 1)
        def _():
            for a in range(hp):
                dq = dq_s[a, :gr, :]
                dq_ref[a * gr:(a + 1) * gr, :] = (dq if lay.q_scale is None else dq * lay.q_scale).astype(dq_ref.dtype)
                if with_c:
                    dqc_ref[a * C_ROWS:(a + 1) * C_ROWS, :] = dq_s[a, crows, :]

    qcol = lambda g, t, it, jt: (g, it[t])
    kcol = lambda g, t, it, jt: (g, jt[t])
    whole = lambda g, t, it, jt: (g, 0)
    out_specs = [pl.BlockSpec((hp * gr, L), whole), pl.BlockSpec((hp * gr, tk), kcol), pl.BlockSpec((hp * dv, tk), kcol)]
    out_shape = [jax.ShapeDtypeStruct((H * gr, L), BF16), jax.ShapeDtypeStruct((H * gr, L), BF16),
                 jax.ShapeDtypeStruct((H * dv, L), BF16)]
    if with_c:
        out_specs += [pl.BlockSpec((hp * C_ROWS, L), whole), pl.BlockSpec((hp * C_ROWS, tk), kcol)]
        out_shape += [jax.ShapeDtypeStruct((H * C_ROWS, L), F32)] * 2
    return pl.pallas_call(
        kern, name=name,
        grid_spec=pltpu.PrefetchScalarGridSpec(
            num_scalar_prefetch=2, grid=(H // hp, npairs),
            in_specs=lay.specs(tq, tk, hp) + [pl.BlockSpec((hp * dv, tq), qcol), pl.BlockSpec((hp * dv, tq), qcol),
                                          pl.BlockSpec((hp, 1, tq), lambda g, t, it, jt: (g, 0, it[t]))],
            out_specs=out_specs,
            scratch_shapes=[pltpu.VMEM((hp, dk, L), F32), pltpu.VMEM((hp, dk, tk), F32), pltpu.VMEM((hp, dv, tk), F32)]),
        out_shape=out_shape, compiler_params=_params("parallel", "arbitrary"))(it, jt, *arrs, oT, doT, lse)


POISON = 1e30


def _fox_extras(cT):
    H, L = cT.shape
    to_bf16 = lambda t: lax.reduce_precision(t, exponent_bits=8, mantissa_bits=7)
    hi = to_bf16(cT)
    lo = to_bf16(cT - hi)
    ll = to_bf16(cT - hi - lo)
    terms = jnp.stack([hi, lo, ll], axis=1).astype(BF16)
    ones = jnp.ones_like(terms)
    fill = jnp.zeros((H, SUBLANES_BF16 - 2 * C_PARTS, L), BF16)
    pad_key = (jnp.arange(L) < PAD)[None, None, :]
    kterms = jnp.where(pad_key, jnp.asarray([POISON, 0.0, 0.0], BF16)[None, :, None], terms)
    eq = jnp.concatenate([terms, ones, fill], axis=1)
    ek = jnp.concatenate([ones, -kterms, fill], axis=1)
    return eq.reshape(H * SUBLANES_BF16, L), ek.reshape(H * SUBLANES_BF16, L)


def _fox_attention(name):
    assert FOX_SCALE == FOX_HD ** -0.5
    lay = _AttnLayout(FOX_HEADS, FOX_HD, FOX_HD, bases=(0, FOX_HEADS, 2 * FOX_HEADS), extra=SUBLANES_BF16,
                      q_scale=FOX_SCALE)

    def run(qkvT, cT):
        arrs = (qkvT, qkvT, qkvT) + _fox_extras(cT)
        oT, lse = _attn_fwd_call(lay, arrs, name + "_fwd")
        return oT, (arrs, oT, lse)

    @jax.custom_vjp
    def f(qkvT, cT):
        return run(qkvT, cT)[0]

    def fwd(qkvT, cT):
        return run(qkvT, cT)

    def bwd(res, doT):
        arrs, oT, lse = res
        dq, dk, dv, dqc, dkc = _attn_bwd_call(lay, arrs, oT, lse, doT, name + "_bwd")
        L = dq.shape[1]
        dc = dqc.reshape(FOX_HEADS, C_ROWS, L)[:, 0, :] - dkc.reshape(FOX_HEADS, C_ROWS, L)[:, C_PARTS, :]
        return jnp.concatenate([dq, dk, dv], axis=0), dc

    f.defvjp(fwd, bwd)
    return f


MLA_XROWS = SUBLANES_BF16


def _mla_attention(name):
    rows = MLA_NOPE + MLA_ROPE + MLA_XROWS
    lay = _AttnLayout(MLA_HEADS, rows, MLA_V, grad_rows=MLA_NOPE + MLA_ROPE)

    @jax.custom_vjp
    def f(qT, kT, vT):
        return _attn_fwd_call(lay, (qT, kT, vT), name + "_fwd")[0]

    def fwd(qT, kT, vT):
        oT, lse = _attn_fwd_call(lay, (qT, kT, vT), name + "_fwd")
        return oT, (qT, kT, vT, oT, lse)

    def bwd(res, doT):
        qT, kT, vT, oT, lse = res
        dq, dk, dv = _attn_bwd_call(lay, (qT, kT, vT), oT, lse, doT, name + "_bwd")
        L = dq.shape[1]
        widen = lambda t: jnp.pad(t.reshape(MLA_HEADS, -1, L), ((0, 0), (0, MLA_XROWS), (0, 0))).reshape(-1, L)
        return widen(dq), widen(dk), dv

    f.defvjp(fwd, bwd)
    return f


def _linear_t(name, out_dtype):
    @jax.custom_vjp
    def f(wT, x):
        return _mm(wT, x, lb="nk", out_dtype=out_dtype, name=name + "_fwd")

    def fwd(wT, x):
        return f(wT, x), (wT, x)

    def bwd(res, g):
        wT, x = res
        dwT = _mm(g, x, out_dtype=wT.dtype, name=name + "_dw")
        dx = _mm(g, wT, la="km", out_dtype=x.dtype, name=name + "_dx")
        return dwT, dx

    f.defvjp(fwd, bwd)
    return f


def _linear_km(name):
    @jax.custom_vjp
    def f(aT, w):
        return _mm(aT, w, la="km", out_dtype=F32, name=name + "_fwd")

    def fwd(aT, w):
        return f(aT, w), (aT, w)

    def bwd(res, g):
        aT, w = res
        daT = _mm(w, g, lb="nk", out_dtype=aT.dtype, name=name + "_dx")
        dw = _mm(aT, g, out_dtype=w.dtype, name=name + "_dw")
        return daT, dw

    f.defvjp(fwd, bwd)
    return f


def _swa_parts(q_ref, kc_ref, kp_ref, km_ref, sink_ref, i, scale):
    R = SWA_G * BLOCK
    q = q_ref[...].reshape(R, SWA_HD)
    nt = (((1,), (1,)), ((), ()))
    r = lax.broadcasted_iota(jnp.int32, (R, BLOCK), 0) & (BLOCK - 1)
    c = lax.broadcasted_iota(jnp.int32, (R, BLOCK), 1)
    s_c = jnp.where((c <= r) & (i >= 1), lax.dot_general(q, kc_ref[...], nt, preferred_element_type=F32) * scale, NEG)
    s_p = jnp.where((c > r) & (i >= 2), lax.dot_general(q, kp_ref[...], nt, preferred_element_type=F32) * scale, NEG)
    s_m = jnp.where((c >= PAD) & ((i >= 1) | (c <= r)),
                    lax.dot_general(q, km_ref[...], nt, preferred_element_type=F32) * scale, NEG)
    sink = jnp.broadcast_to(sink_ref[...], (SWA_G, BLOCK, 1)).reshape(R, 1)
    return q, s_c, s_p, s_m, sink


def _swa_specs(L):
    nb = L // BLOCK
    qs = pl.BlockSpec((None, SWA_G, BLOCK, SWA_HD), lambda g, i: (g, 0, i, 0))
    kc = pl.BlockSpec((None, BLOCK, SWA_HD), lambda g, i: (g, i, 0))
    kp = pl.BlockSpec((None, BLOCK, SWA_HD), lambda g, i: (g, jnp.maximum(i - 1, 0), 0))
    km = pl.BlockSpec((None, BLOCK, SWA_HD), lambda g, i: (g, 0, 0))
    sk = pl.BlockSpec((None, SWA_G, 1, 1), lambda g, i: (g, 0, 0, 0))
    ls = pl.BlockSpec((None, SWA_G, BLOCK, 1), lambda g, i: (g, 0, i, 0))
    return nb, qs, kc, kp, km, sk, ls


def _swa_fwd_call(q, k, v, sinks, scale, name):
    _, _, L, _ = q.shape
    nb, qs, kc, kp, km, sk, ls = _swa_specs(L)

    def kern(q_ref, kc_ref, kp_ref, km_ref, vc_ref, vp_ref, vm_ref, sink_ref, o_ref, lse_ref):
        i = pl.program_id(1)
        _, s_c, s_p, s_m, sink = _swa_parts(q_ref, kc_ref, kp_ref, km_ref, sink_ref, i, scale)
        mx = lambda s: jnp.max(s, axis=-1, keepdims=True)
        m = jnp.maximum(jnp.maximum(mx(s_c), mx(s_p)), jnp.maximum(mx(s_m), sink))
        p_c, p_p, p_m = jnp.exp(s_c - m), jnp.exp(s_p - m), jnp.exp(s_m - m)
        sm = lambda p: jnp.sum(p, axis=-1, keepdims=True)
        den = sm(p_c) + sm(p_p) + sm(p_m) + jnp.exp(sink - m)
        inv = 1.0 / den
        pv = lambda p, v_ref: jnp.dot((p * inv).astype(BF16), v_ref[...], preferred_element_type=F32)
        o = pv(p_c, vc_ref) + pv(p_p, vp_ref) + pv(p_m, vm_ref)
        o_ref[...] = o.reshape(SWA_G, BLOCK, SWA_HD)
        lse_ref[...] = (m + jnp.log(den)).reshape(SWA_G, BLOCK, 1)

    return pl.pallas_call(
        kern, name=name, grid=(SWA_KVH, nb), in_specs=[qs, kc, kp, km, kc, kp, km, sk], out_specs=[qs, ls],
        out_shape=[jax.ShapeDtypeStruct((SWA_KVH, SWA_G, L, SWA_HD), F32), jax.ShapeDtypeStruct((SWA_KVH, SWA_G, L, 1), F32)],
        compiler_params=_params("parallel", "parallel"))(q, k, k, k, v, v, v, sinks)


def _swa_bwd_call(q, k, v, sinks, o, lse, do, scale, name):
    _, _, L, _ = q.shape
    nb, qs, kc, kp, km, sk, ls = _swa_specs(L)
    R = SWA_G * BLOCK
    full = pl.BlockSpec((None, L, SWA_HD), lambda g, i: (g, 0, 0))

    def kern(q_ref, kc_ref, kp_ref, km_ref, vc_ref, vp_ref, vm_ref, sink_ref, o_ref, do_ref, lse_ref,
             dq_ref, dk_ref, dv_ref, dsink_ref):
        i = pl.program_id(1)

        @pl.when(i == 0)
        def _():
            dk_ref[...] = jnp.zeros_like(dk_ref)
            dv_ref[...] = jnp.zeros_like(dv_ref)
            dsink_ref[...] = jnp.zeros_like(dsink_ref)

        q, s_c, s_p, s_m, sink = _swa_parts(q_ref, kc_ref, kp_ref, km_ref, sink_ref, i, scale)
        lse = lse_ref[...].reshape(R, 1)
        dof = do_ref[...].reshape(R, SWA_HD)
        dob = dof.astype(BF16)
        delta = jnp.sum(dof * o_ref[...].reshape(R, SWA_HD), axis=-1, keepdims=True)
        nt = (((1,), (1,)), ((), ()))
        tn = (((0,), (0,)), ((), ()))
        cur = pl.ds(pl.multiple_of(i * BLOCK, BLOCK), BLOCK)
        prev = pl.ds(pl.multiple_of(jnp.maximum(i - 1, 0) * BLOCK, BLOCK), BLOCK)
        meta = pl.ds(0, BLOCK)
        dq = jnp.zeros((R, SWA_HD), F32)
        for s, k_ref, v_ref, rows in ((s_c, kc_ref, vc_ref, cur), (s_p, kp_ref, vp_ref, prev), (s_m, km_ref, vm_ref, meta)):
            p = jnp.exp(s - lse)
            dp = lax.dot_general(dob, v_ref[...], nt, preferred_element_type=F32)
            ds = (p * (dp - delta)).astype(BF16)
            dq = dq + jnp.dot(ds, k_ref[...], preferred_element_type=F32)
            dv_ref[rows, :] += lax.dot_general(p.astype(BF16), dob, tn, preferred_element_type=F32)
            dk_ref[rows, :] += lax.dot_general(ds, q, tn, preferred_element_type=F32) * scale
        dq_ref[...] = (dq * scale).reshape(SWA_G, BLOCK, SWA_HD).astype(dq_ref.dtype)
        dsk = -jnp.exp(sink - lse) * delta
        dsink_ref[...] += jnp.sum(dsk.reshape(SWA_G, BLOCK, 1), axis=1, keepdims=True)

    return pl.pallas_call(
        kern, name=name, grid=(SWA_KVH, nb), in_specs=[qs, kc, kp, km, kc, kp, km, sk, qs, qs, ls],
        out_specs=[qs, full, full, sk],
        out_shape=[jax.ShapeDtypeStruct((SWA_KVH, SWA_G, L, SWA_HD), BF16), jax.ShapeDtypeStruct((SWA_KVH, L, SWA_HD), F32),
                   jax.ShapeDtypeStruct((SWA_KVH, L, SWA_HD), F32), jax.ShapeDtypeStruct((SWA_KVH, SWA_G, 1, 1), F32)],
        compiler_params=_params("parallel", "arbitrary"))(q, k, k, k, v, v, v, sinks, o, do, lse)


def _swa_attn(name, scale):
    @jax.custom_vjp
    def f(q, k, v, sinks):
        return _swa_fwd_call(q, k, v, sinks, scale, name + "_fwd")[0]

    def fwd(q, k, v, sinks):
        o, lse = _swa_fwd_call(q, k, v, sinks, scale, name + "_fwd")
        return o, (q, k, v, sinks, o, lse)

    def bwd(res, do):
        q, k, v, sinks, o, lse = res
        dq, dk, dv, dsink = _swa_bwd_call(q, k, v, sinks, o, lse, do, scale, name + "_bwd")
        return dq, dk.astype(BF16), dv.astype(BF16), dsink

    f.defvjp(fwd, bwd)
    return f


def _scan_call(x, bias, mul, pre_logsig, name):
    L, W = x.shape
    tb = _tile(L, ROW_TILE)
    has_mul = mul is not None

    def kern(x_ref, b_ref, *rest):
        if has_mul:
            m_ref, o_ref, tot_ref, carry = rest
        else:
            o_ref, tot_ref, carry = rest
        i = pl.program_id(0)

        @pl.when(i == 0)
        def _():
            carry[...] = jnp.zeros_like(carry)
            tot_ref[...] = jnp.zeros_like(tot_ref)

        z = x_ref[...] + b_ref[...]
        if pre_logsig:
            z = jnp.minimum(z, 0.0) - jnp.log(1.0 + jnp.exp(-jnp.abs(z)))
        row = lax.broadcasted_iota(jnp.int32, (tb, W), 0)
        s = 1
        while s < tb:
            z = z + jnp.where(row >= s, pltpu.roll(z, s, 0), 0.0)
            s *= 2
        z = z + carry[...]
        carry[...] = z[tb - 1:tb, :]
        if has_mul:
            z = z * m_ref[...]
        o_ref[...] = z
        tot_ref[...] += jnp.sum(z, axis=0, keepdims=True)

    row = pl.BlockSpec((tb, W), lambda i: (i, 0))
    vec = pl.BlockSpec((1, W), lambda i: (0, 0))
    return pl.pallas_call(
        kern, name=name, grid=(L // tb,), in_specs=[row, vec] + ([row] if has_mul else []), out_specs=[row, vec],
        out_shape=[jax.ShapeDtypeStruct((L, W), F32), jax.ShapeDtypeStruct((1, W), F32)],
        scratch_shapes=[pltpu.VMEM((1, W), F32)],
        compiler_params=_params("arbitrary"))(*([x, bias] + ([mul] if has_mul else [])))


def _sigmoid_neg_call(x, bias, name):
    L, W = x.shape
    tb = _tile(L, ROW_TILE)

    def kern(x_ref, b_ref, o_ref):
        row = pl.program_id(0) * tb + lax.broadcasted_iota(jnp.int32, (tb, 1), 0)
        o_ref[...] = jnp.where(row >= PAD, jax.nn.sigmoid(-(x_ref[...] + b_ref[...])), 0.0)

    row = pl.BlockSpec((tb, W), lambda i: (i, 0))
    vec = pl.BlockSpec((1, W), lambda i: (0, 0))
    return pl.pallas_call(kern, name=name, grid=(L // tb,), in_specs=[row, vec], out_specs=row,
                          out_shape=jax.ShapeDtypeStruct((L, W), F32), compiler_params=_params("parallel"))(x, bias)


def _decay(name):
    @jax.custom_vjp
    def f(fg, b):
        return _scan_call(fg, b, None, True, name + "_fwd")[0]

    def fwd(fg, b):
        return f(fg, b), (fg, b)

    def bwd(res, dc):
        fg, b = res
        sg = _sigmoid_neg_call(fg, b, name + "_dsig")
        dfg_rev, db = _scan_call(dc[::-1], jnp.zeros_like(b), sg[::-1], False, name + "_bwd")
        return dfg_rev[::-1], db

    f.defvjp(fwd, bwd)
    return f


def _loss_call(hf, target, name):
    L, D = hf.shape
    nb = L // BLOCK

    def kern(h_ref, t_ref, loss_ref, dy_ref, acc):
        i = pl.program_id(0)

        @pl.when(i == 0)
        def _():
            acc[...] = jnp.zeros_like(acc)
            dy_ref[...] = jnp.zeros_like(dy_ref)

        @pl.when(i > 0)
        def _():
            e = h_ref[...] - t_ref[...]
            dy_ref[...] = e * (1.0 / D)
            acc[...] += jnp.sum(e * e, axis=0, keepdims=True)

        @pl.when(i == nb - 1)
        def _():
            loss_ref[...] = jnp.broadcast_to(jnp.sum(acc[...], axis=1, keepdims=True) * (0.5 / D), loss_ref.shape)

    return pl.pallas_call(
        kern, name=name, grid=(nb,),
        in_specs=[pl.BlockSpec((BLOCK, D), lambda i: (i, 0)), pl.BlockSpec((BLOCK, D), lambda i: (jnp.maximum(i - 1, 0), 0))],
        out_specs=[pl.BlockSpec((1, LANES), lambda i: (0, 0)), pl.BlockSpec((BLOCK, D), lambda i: (i, 0))],
        out_shape=[jax.ShapeDtypeStruct((1, LANES), F32), jax.ShapeDtypeStruct((L, D), F32)],
        scratch_shapes=[pltpu.VMEM((1, D), F32)],
        compiler_params=_params("arbitrary"))(hf, target)


def _loss_head(name):
    @jax.custom_vjp
    def f(hf, target):
        return _loss_call(hf, target, name)[0][0, 0]

    def fwd(hf, target):
        loss, dy = _loss_call(hf, target, name)
        return loss[0, 0], (dy, jnp.zeros((), F32))

    def bwd(res, g):
        dy, _ = res
        return dy * g, None

    f.defvjp(fwd, bwd)
    return f


N_CHIPS = N_DEV // 2


def _sibling_swap(arrs, name):
    n = len(arrs)

    def body(*refs):
        srcs, outs = refs[:n], refs[n:2 * n]
        send_sems, recv_sems = refs[2 * n:]
        x, y, c = lax.axis_index("x"), lax.axis_index("y"), lax.axis_index("c")
        copies = [[pltpu.make_async_remote_copy(
            src_ref=srcs[a].at[2 * q + (1 - c)], dst_ref=outs[a].at[q], send_sem=send_sems.at[a, q],
            recv_sem=recv_sems.at[a, q], device_id=(x, y, 1 - c), device_id_type=MESH) for q in range(N_CHIPS)]
            for a in range(n)]
        for row in copies:
            for cp in row:
                cp.start()
        for row in copies:
            for cp in row:
                cp.wait()

    hbm = pl.BlockSpec(memory_space=pl.ANY)
    return pl.pallas_call(
        body, name=name, out_shape=[jax.ShapeDtypeStruct((N_CHIPS,) + tuple(a.shape[1:]), a.dtype) for a in arrs],
        in_specs=[hbm] * n, out_specs=[hbm] * n,
        scratch_shapes=[pltpu.SemaphoreType.DMA((n, N_CHIPS)), pltpu.SemaphoreType.DMA((n, N_CHIPS))],
    )(*arrs)


def _pair_sum_call(a, b, name):
    shape = a.shape
    cols = shape[-1]
    rows = int(np.prod(shape[:-1]))
    tr = rows
    if rows > 2 * ADAM_ROWS:
        tr = max(t for t in range(SUBLANES_BF16, 2 * ADAM_ROWS + 1, SUBLANES_BF16) if rows % t == 0)

    def kern(a_ref, b_ref, o_ref):
        o_ref[...] = (a_ref[...].astype(F32) + b_ref[...].astype(F32)).astype(o_ref.dtype)

    slab = pl.BlockSpec((tr, cols), lambda i: (i, 0))
    return pl.pallas_call(kern, name=name, grid=(rows // tr,), in_specs=[slab, slab], out_specs=slab,
                          out_shape=jax.ShapeDtypeStruct((rows, cols), a.dtype),
                          compiler_params=_params("parallel"))(a.reshape(rows, cols), b.reshape(rows, cols)).reshape(shape)


def _chip_exchange(arrs, name):
    n = len(arrs)

    def body(*refs):
        srcs, outs = refs[:n], refs[n:2 * n]
        send_sems, recv_sems, local_sems = refs[2 * n:]
        x, y, c = lax.axis_index("x"), lax.axis_index("y"), lax.axis_index("c")
        mine = 2 * x + y
        chips = [(1 - x, y), (x, 1 - y), (1 - x, 1 - y)]
        started = []
        for a in range(n):
            own = pltpu.make_async_copy(srcs[a].at[mine], outs[a].at[mine], local_sems.at[a])
            own.start()
            started.append(own.wait)
        for k, (px, py) in enumerate(chips):
            for a in range(n):
                cp = pltpu.make_async_remote_copy(src_ref=srcs[a].at[2 * px + py], dst_ref=outs[a].at[mine],
                                                  send_sem=send_sems.at[a, k], recv_sem=recv_sems.at[a, k],
                                                  device_id=(px, py, c), device_id_type=MESH)
                cp.start()
                started.append(cp.wait_send)
        for k, (px, py) in enumerate(chips):
            for a in range(n):
                pltpu.make_async_remote_copy(src_ref=srcs[a].at[mine], dst_ref=outs[a].at[2 * px + py],
                                             send_sem=send_sems.at[a, k], recv_sem=recv_sems.at[a, k],
                                             device_id=(px, py, c), device_id_type=MESH).wait_recv()
        for wait in started:
            wait()

    hbm = pl.BlockSpec(memory_space=pl.ANY)
    return pl.pallas_call(
        body, name=name, out_shape=[jax.ShapeDtypeStruct(a.shape, a.dtype) for a in arrs],
        in_specs=[hbm] * n, out_specs=[hbm] * n,
        scratch_shapes=[pltpu.SemaphoreType.DMA((n, N_CHIPS - 1)), pltpu.SemaphoreType.DMA((n, N_CHIPS - 1)),
                        pltpu.SemaphoreType.DMA((n,))],
    )(*arrs)


def _scatter(arrs, names):
    c = lax.axis_index("c")
    theirs = _sibling_swap(arrs, "scatter_swap")
    halves = [lax.dynamic_index_in_dim(a.reshape((N_CHIPS, 2) + a.shape[1:]), c, axis=1, keepdims=False) for a in arrs]
    sums = [_pair_sum_call(h, t, "pair_sum_" + nm) for h, t, nm in zip(halves, theirs, names)]
    return _chip_exchange(sums, "scatter_chips")


def _gather(arrs, name):
    n = len(arrs)
    hops = N_DEV - 1

    def body(*refs):
        srcs, outs = refs[:n], refs[n:2 * n]
        send_sems, recv_sems, local_sems = refs[2 * n:]
        x, y, c = lax.axis_index("x"), lax.axis_index("y"), lax.axis_index("c")
        ident = lambda px, py, pc: 4 * px + 2 * py + pc
        me, sibling = ident(x, y, c), (x, y, 1 - c)
        chips = [(1 - x, y), (x, 1 - y), (1 - x, 1 - y)]

        def copy(a, k, block, to, src=None):
            return pltpu.make_async_remote_copy(
                src_ref=outs[a].at[block] if src is None else src, dst_ref=outs[a].at[block],
                send_sem=send_sems.at[a, k], recv_sem=recv_sems.at[a, k], device_id=to, device_id_type=MESH)

        started = []
        for a in range(n):
            mine = pltpu.make_async_copy(srcs[a], outs[a].at[me], local_sems.at[a])
            mine.start()
            started.append(mine.wait)
            for k, to in enumerate([sibling] + [(px, py, c) for px, py in chips]):
                cp = copy(a, k, me, to, src=srcs[a])
                cp.start()
                started.append(cp.wait_send)
        for j, (px, py) in enumerate(chips):
            for a in range(n):
                copy(a, 1 + j, ident(px, py, c), (x, y, c)).wait_recv()
                cp = copy(a, 4 + j, ident(px, py, c), sibling)
                cp.start()
                started.append(cp.wait_send)
        for a in range(n):
            copy(a, 0, ident(x, y, 1 - c), (x, y, c)).wait_recv()
            for j, (px, py) in enumerate(chips):
                copy(a, 4 + j, ident(px, py, 1 - c), (x, y, c)).wait_recv()
        for wait in started:
            wait()

    hbm = pl.BlockSpec(memory_space=pl.ANY)
    return pl.pallas_call(
        body, name=name, out_shape=[jax.ShapeDtypeStruct((N_DEV,) + tuple(a.shape), a.dtype) for a in arrs],
        in_specs=[hbm] * n, out_specs=[hbm] * n,
        scratch_shapes=[pltpu.SemaphoreType.DMA((n, hops)), pltpu.SemaphoreType.DMA((n, hops)),
                        pltpu.SemaphoreType.DMA((n,))],
    )(*arrs)


def _adamw_call(parts, w, m, v, name):
    shape = w.shape
    nparts = parts.shape[0]
    cols = shape[-1]
    rows = int(np.prod(shape[:-1]))
    tr = rows
    if rows > ADAM_ROWS:
        tr = max(t for t in range(SUBLANES_F32, ADAM_ROWS + 1, SUBLANES_F32) if rows % t == 0)
    c1 = 1.0 / (1.0 - ADAM_B1 ** ADAM_STEP)
    c2 = 1.0 / (1.0 - ADAM_B2 ** ADAM_STEP)

    def kern(p_ref, w_ref, m_ref, v_ref, g_ref, d_ref, nm_ref, nv_ref):
        g = p_ref[0].astype(F32)
        for p in range(1, nparts):
            g = g + p_ref[p].astype(F32)
        nm = ADAM_B1 * m_ref[...] + (1.0 - ADAM_B1) * g
        nv = ADAM_B2 * v_ref[...] + (1.0 - ADAM_B2) * (g * g)
        g_ref[...] = g
        nm_ref[...] = nm
        nv_ref[...] = nv
        d_ref[...] = -ADAM_LR * ((nm * c1) / (jnp.sqrt(nv * c2) + ADAM_EPS) + ADAM_WD * w_ref[...])

    slab = pl.BlockSpec((tr, cols), lambda i: (i, 0))
    flat = lambda t: t.reshape(rows, cols)
    res = pl.pallas_call(
        kern, name=name, grid=(rows // tr,),
        in_specs=[pl.BlockSpec((nparts, tr, cols), lambda i: (0, i, 0)), slab, slab, slab], out_specs=[slab] * 4,
        out_shape=[jax.ShapeDtypeStruct((rows, cols), F32)] * 4,
        compiler_params=_params("parallel"))(parts.reshape(nparts, rows, cols), flat(w), flat(m), flat(v))
    return [r.reshape(shape) for r in res]


def _to_full(stacked, axis):
    moved = jnp.moveaxis(stacked, 0, axis)
    s = list(moved.shape)
    return moved.reshape(s[:axis] + [s[axis] * s[axis + 1]] + s[axis + 2:])


def _heads(x, h, d):
    return x.reshape(x.shape[0], h, d).transpose(1, 0, 2)


def _unheads(x):
    return x.transpose(1, 0, 2).reshape(x.shape[1], -1)


def _pad_cols(x, width):
    return jnp.pad(x, ((0, 0), (0, width - x.shape[1])))


def _fox_mixer(h, w_in, b_f, w_o, tag):
    hd = FOX_HEADS * FOX_HD
    qkvT = _linear_t(tag + "_qkv", BF16)(w_in[:, :3 * hd].T, h)
    fg = _linear(tag + "_gate", F32)(h, _pad_cols(w_in[:, 3 * hd:], LANES))
    c = _decay(tag + "_decay")(fg, _pad_cols(b_f[None, :], LANES))
    oT = _fox_attention(tag + "_attn")(qkvT, c[:, :FOX_HEADS].T)
    return _linear_km(tag + "_out")(oT, w_o)


def _swa_mixer(h, w_in, sinks, w_o, tabs, tag):
    L = h.shape[0]
    qd, kd = SWA_QH * SWA_HD, SWA_KVH * SWA_HD
    proj = _linear(tag + "_qkv", F32)(h, w_in)
    rope = _rope(tag + "_rope", ROPE_DIM // 2, BF16)
    q = rope(proj[:, :qd], *tabs)
    k = rope(proj[:, qd:qd + kd], *tabs)
    v = proj[:, qd + kd:].astype(BF16)
    qg = _heads(q, SWA_QH, SWA_HD).reshape(SWA_KVH, SWA_G, L, SWA_HD)
    o = _swa_attn(tag + "_attn", SWA_HD ** -0.5)(qg, _heads(k, SWA_KVH, SWA_HD), _heads(v, SWA_KVH, SWA_HD),
                                                 sinks.reshape(SWA_KVH, SWA_G, 1, 1))
    return _linear(tag + "_out", F32)(_unheads(o.reshape(SWA_QH, L, SWA_HD)), w_o)


def _mla_mixer(h, w_a, g_q, g_kv, w_uq, w_ukv, w_o, tabs, tag):
    L = h.shape[0]
    cq = _linear(tag + "_aq", F32)(h, w_a[:, :MLA_QL])
    ckv = _linear(tag + "_akv", F32)(h, w_a[:, MLA_QL:MLA_QL + MLA_KVL])
    kr = _linear(tag + "_akr", F32)(h, _pad_cols(w_a[:, MLA_QL + MLA_KVL:], LANES))
    cq = _rmsnorm(tag + "_nq")(cq, g_q[None, :])
    ckv = _rmsnorm(tag + "_nkv")(ckv, g_kv[None, :])
    q = _linear(tag + "_uq", F32)(cq, w_uq).reshape(L, MLA_HEADS, MLA_NOPE + MLA_ROPE)
    kv = _linear(tag + "_ukv", BF16)(ckv, w_ukv).reshape(L, MLA_HEADS, MLA_NOPE + MLA_V)
    scale = (MLA_NOPE + MLA_ROPE) ** -0.5
    rope = _rope(tag + "_rope", MLA_ROPE // 2, BF16)
    q_rope = rope(q[:, :, MLA_NOPE:].reshape(L, MLA_HEADS * MLA_ROPE), *(t * scale for t in tabs))
    k_rope = rope(kr, *tabs)[:, :MLA_ROPE]
    qf = jnp.concatenate([(q[:, :, :MLA_NOPE] * scale).astype(BF16), q_rope.reshape(L, MLA_HEADS, MLA_ROPE)], axis=-1)
    kf = jnp.concatenate([kv[:, :, :MLA_NOPE], jnp.broadcast_to(k_rope[:, None, :], (L, MLA_HEADS, MLA_ROPE))], axis=-1)
    seq_last = lambda t: t.transpose(1, 2, 0)
    unit = jnp.zeros((MLA_HEADS, MLA_XROWS, L), BF16).at[:, 0, :].set(1.0)
    poison = jnp.zeros((MLA_HEADS, MLA_XROWS, L), BF16).at[:, 0, :].set(jnp.where(jnp.arange(L) < PAD, -POISON, 0.0))
    qT = jnp.concatenate([seq_last(qf), unit], axis=1).reshape(-1, L)
    kT = jnp.concatenate([seq_last(kf), poison], axis=1).reshape(-1, L)
    oT = _mla_attention(tag + "_attn")(qT, kT, seq_last(kv[:, :, MLA_NOPE:]).reshape(-1, L))
    return _linear_km(tag + "_out")(oT, w_o)


BIG = (("fox_w_in", 2), ("fox_w_o", 1), ("swa_w_in", 2), ("swa_w_o", 1), ("mla_w_a", 1), ("mla_w_uq", 2),
       ("mla_w_ukv", 2), ("mla_w_o", 1), ("ffn_w_in", 2), ("ffn_w_out", 1))
SMALL = (("meta_tokens", 1), ("mla_g_q", 1), ("mla_g_kv", 1), ("ffn_conv_w", 2))
SHARDED = BIG + SMALL
REPL = ("ln1_g", "ln1_b", "ln2_g", "ln2_b", "fox_b_f", "swa_sinks", "ffn_conv_b")
FFN_STACKED = ("ffn_w_in", "ffn_w_out", "ffn_conv_w")
WEIGHTS = ("meta_tokens", "ln1_g", "ln1_b", "ln2_g", "ln2_b", "fox_w_in", "fox_b_f", "fox_w_o", "swa_w_in", "swa_sinks",
           "swa_w_o", "mla_w_a", "mla_g_q", "mla_g_kv", "mla_w_uq", "mla_w_ukv", "mla_w_o", "ffn_w_in", "ffn_conv_w",
           "ffn_conv_b", "ffn_w_out")


def _local_loss(stacked, repl, x, target):
    S, D = x.shape
    L = S + BLOCK
    wts = {n: _to_full(stacked[n], ax) for n, ax in SHARDED if n not in FFN_STACKED}
    wts.update(repl)
    half = N_DEV // 2
    h = jnp.concatenate([jnp.zeros((PAD, D), F32), wts["meta_tokens"], x], axis=0)
    cos_p, sin_p = _rope_tables(L, ROPE_DIM, ROPE_THETA)
    tabs_p = _rope_lanes(cos_p, sin_p, SWA_HD)
    cos_m, sin_m = _rope_tables(L, MLA_ROPE, MLA_ROPE_THETA)
    tabs_m = _rope_lanes(cos_m, sin_m, MLA_ROPE)
    for i in range(DEPTH):
        kind, j = i % 3, i // 3
        tag = f"l{i}"
        if kind == 0:
            mix = _fox_mixer(h, wts["fox_w_in"][j], wts["fox_b_f"][j], wts["fox_w_o"][j], tag + "_fox")
        elif kind == 1:
            mix = _swa_mixer(h, wts["swa_w_in"][j], wts["swa_sinks"][j], wts["swa_w_o"][j], tabs_p, tag + "_swa")
        else:
            mix = _mla_mixer(h, wts["mla_w_a"][j], wts["mla_g_q"][j], wts["mla_g_kv"][j], wts["mla_w_uq"][j],
                             wts["mla_w_ukv"][j], wts["mla_w_o"][j], tabs_m, tag + "_mla")
        h = _deepnorm(tag + "_ln1")(h, mix, wts["ln1_g"][i][None, :], wts["ln1_b"][i][None, :])
        w3 = stacked["ffn_w_in"][:, i]
        shard = w3.shape[-1]
        cw4 = stacked["ffn_conv_w"][:, i].reshape(2, half, CONV_W, shard)
        cb4 = wts["ffn_conv_b"][i].reshape(2, half, 1, shard)
        wo3 = stacked["ffn_w_out"][:, i].reshape(half, shard, D)
        act3 = _ffn_up(tag + "_ffn")(h, w3, cw4, cb4)
        ffn = _ffn_down(tag + "_ffn_down")(act3, wo3)
        h = _deepnorm(tag + "_ln2")(h, ffn, wts["ln2_g"][i][None, :], wts["ln2_b"][i][None, :])
    return _loss_head("loss_head")(h, target)


def _step(x, target, w, m, v):
    names = [n for n, _ in SHARDED]
    big = {n for n, _ in BIG}
    stacked = dict(zip(names, _gather([w[n].astype(BF16) if n in big else w[n] for n in names], "gather_weights")))
    repl = {n: w[n] for n in REPL}
    loss, (g_st, g_repl, grad_x) = jax.value_and_grad(_local_loss, argnums=(0, 1, 2))(stacked, repl, x, target)
    got = dict(zip(names, _scatter([g_st[n] for n in names], names)))
    got.update(zip(REPL, _gather([g_repl[n] for n in REPL], "gather_repl_grads")))
    out = {}
    for n in WEIGHTS:
        for kind, a in zip(("grad", "delta", "new_m", "new_v"), _adamw_call(got[n], w[n], m[n], v[n], "adamw_" + n)):
            out[(kind, n)] = a
    return loss, grad_x, out


def kernel(x, meta_tokens, ln1_g, ln1_b, ln2_g, ln2_b, fox_w_in, fox_b_f, fox_w_o, swa_w_in, swa_sinks, swa_w_o, mla_w_a, mla_g_q, mla_g_kv, mla_w_uq, mla_w_ukv, mla_w_o, ffn_w_in, ffn_conv_w, ffn_conv_b, ffn_w_out, loss_target, m_meta_tokens, m_ln1_g, m_ln1_b, m_ln2_g, m_ln2_b, m_fox_w_in, m_fox_b_f, m_fox_w_o, m_swa_w_in, m_swa_sinks, m_swa_w_o, m_mla_w_a, m_mla_g_q, m_mla_g_kv, m_mla_w_uq, m_mla_w_ukv, m_mla_w_o, m_ffn_w_in, m_ffn_conv_w, m_ffn_conv_b, m_ffn_w_out, v_meta_tokens, v_ln1_g, v_ln1_b, v_ln2_g, v_ln2_b, v_fox_w_in, v_fox_b_f, v_fox_w_o, v_swa_w_in, v_swa_sinks, v_swa_w_o, v_mla_w_a, v_mla_g_q, v_mla_g_kv, v_mla_w_uq, v_mla_w_ukv, v_mla_w_o, v_ffn_w_in, v_ffn_conv_w, v_ffn_conv_b, v_ffn_w_out):
    args = (meta_tokens, ln1_g, ln1_b, ln2_g, ln2_b, fox_w_in, fox_b_f, fox_w_o, swa_w_in, swa_sinks, swa_w_o, mla_w_a,
            mla_g_q, mla_g_kv, mla_w_uq, mla_w_ukv, mla_w_o, ffn_w_in, ffn_conv_w, ffn_conv_b, ffn_w_out)
    ms = (m_meta_tokens, m_ln1_g, m_ln1_b, m_ln2_g, m_ln2_b, m_fox_w_in, m_fox_b_f, m_fox_w_o, m_swa_w_in, m_swa_sinks,
          m_swa_w_o, m_mla_w_a, m_mla_g_q, m_mla_g_kv, m_mla_w_uq, m_mla_w_ukv, m_mla_w_o, m_ffn_w_in, m_ffn_conv_w,
          m_ffn_conv_b, m_ffn_w_out)
    vs = (v_meta_tokens, v_ln1_g, v_ln1_b, v_ln2_g, v_ln2_b, v_fox_w_in, v_fox_b_f, v_fox_w_o, v_swa_w_in, v_swa_sinks,
          v_swa_w_o, v_mla_w_a, v_mla_g_q, v_mla_g_kv, v_mla_w_uq, v_mla_w_ukv, v_mla_w_o, v_ffn_w_in, v_ffn_conv_w,
          v_ffn_conv_b, v_ffn_w_out)
    w = dict(zip(WEIGHTS, args))
    m = dict(zip(WEIGHTS, ms))
    v = dict(zip(WEIGHTS, vs))
    loss, grad_x, out = _step(x[0], loss_target[0], w, m, v)
    loss = lax.psum(loss, ("x", "y", "c"))
    res = [loss, grad_x[None]]
    for kind in ("grad", "delta", "new_m", "new_v"):
        res += [out[(kind, n)] for n in WEIGHTS]
    return tuple(res)
```

```python
import jax
import jax.numpy as jnp
import numpy as np
from jax import lax
from jax.experimental import pallas as pl
from jax.experimental.pallas import tpu as pltpu

F32 = jnp.float32
BF16 = jnp.bfloat16

DEPTH = 4
N_META = 16
BLOCK = 128
PAD = BLOCK - N_META
NEG = -1e30
ALPHA = (2.0 * DEPTH) ** 0.25
LN_EPS = 1e-5
RMS_EPS = 1e-6
FOX_HEADS, FOX_HD = 16, 64
SWA_QH, SWA_KVH, SWA_HD = 16, 2, 64
SWA_G = SWA_QH // SWA_KVH
ROPE_THETA = 500000.0
ROPE_DIM = SWA_HD // 4
MLA_HEADS, MLA_QL, MLA_KVL, MLA_NOPE, MLA_ROPE, MLA_V = 16, 384, 256, 64, 32, 64
MLA_ROPE_THETA = 10000.0
CONV_W = 3
ADAM_LR, ADAM_B1, ADAM_B2, ADAM_EPS, ADAM_WD, ADAM_STEP = 0.001, 0.9, 0.999, 1e-08, 0.01, 10

LANES = 128
SUBLANES_F32 = 8
SUBLANES_BF16 = 16
VMEM_LIMIT = 48 * 1024 * 1024
N_DEV = 8
MESH = pl.DeviceIdType.MESH
ROW_TILE = 640
ADAM_ROWS = 256
MM_TILE_M, MM_TILE_N, MM_TILE_K = 1664, 1024, 1664
ATTN_HEADS_FWD, ATTN_HEADS_BWD = 4, 4


def _tile(n, cap):
    if n <= cap:
        return n
    best = 0
    for t in range(LANES, cap + 1, LANES):
        if n % t == 0:
            best = t
    assert best, (n, cap)
    return best


def _params(*sem):
    return pltpu.CompilerParams(dimension_semantics=sem, vmem_limit_bytes=VMEM_LIMIT)


def _mm(a, b, *, la="mk", lb="kn", sa=None, sb=None, so=None, out_dtype, name):
    size, tile, parts = {}, {}, {}
    for x, lay, split in ((a, la, sa), (b, lb, sb)):
        shp = x.shape[1:] if split else x.shape
        for ax, n in zip(lay, shp):
            if ax == split:
                size[ax], tile[ax], parts[ax] = x.shape[0] * n, n, x.shape[0]
            else:
                assert size.setdefault(ax, n) == n, (name, ax, a.shape, b.shape)
    for ax, cap in (("m", MM_TILE_M), ("n", MM_TILE_N), ("k", MM_TILE_K)):
        tile.setdefault(ax, _tile(size[ax], cap))
    grid = tuple(size[ax] // tile[ax] for ax in "mnk")
    nk = grid[2]
    dn = (((la.index("k"),), (lb.index("k"),)), ((), ()))

    def kern(a_ref, b_ref, o_ref, *acc):
        p = lax.dot_general(a_ref[...].astype(BF16), b_ref[...].astype(BF16), dn, preferred_element_type=F32)
        if nk == 1:
            o_ref[...] = p.astype(o_ref.dtype)
            return
        acc_ref, = acc
        k = pl.program_id(2)

        @pl.when(k == 0)
        def _():
            acc_ref[...] = p

        @pl.when(k > 0)
        def _():
            acc_ref[...] += p

        @pl.when(k == nk - 1)
        def _():
            o_ref[...] = acc_ref[...].astype(o_ref.dtype)

    def spec(lay, split):
        blk = tuple(tile[ax] for ax in lay)
        if split is None:
            return pl.BlockSpec(blk, lambda i, j, k: tuple({"m": i, "n": j, "k": k}[ax] for ax in lay))
        return pl.BlockSpec((None,) + blk, lambda i, j, k: ({"m": i, "n": j, "k": k}[split],) + tuple(
            0 if ax == split else {"m": i, "n": j, "k": k}[ax] for ax in lay))

    if so is None:
        out_shape = (size["m"], size["n"])
    else:
        out_shape = (parts[so],) + tuple(tile[ax] if ax == so else size[ax] for ax in "mn")
    return pl.pallas_call(
        kern, name=name, grid=grid,
        in_specs=[spec(la, sa), spec(lb, sb)], out_specs=spec("mn", so),
        out_shape=jax.ShapeDtypeStruct(out_shape, out_dtype),
        scratch_shapes=[] if nk == 1 else [pltpu.VMEM((tile["m"], tile["n"]), F32)],
        compiler_params=_params("parallel", "parallel", "arbitrary"),
    )(a, b)


def _linear(name, out_dtype):
    @jax.custom_vjp
    def f(x, w):
        return _mm(x, w, out_dtype=out_dtype, name=name + "_fwd")

    def fwd(x, w):
        return f(x, w), (x, w)

    def bwd(res, g):
        x, w = res
        dx = _mm(g, w, lb="nk", out_dtype=x.dtype, name=name + "_dx")
        dw = _mm(x, g, la="km", out_dtype=w.dtype, name=name + "_dw")
        return dx, dw

    f.defvjp(fwd, bwd)
    return f


def _ln_stats(z):
    mu = jnp.mean(z, axis=-1, keepdims=True)
    zc = z - mu
    var = jnp.mean(zc * zc, axis=-1, keepdims=True)
    return zc, lax.rsqrt(var + LN_EPS)


def _ln_fwd_call(h, mix, g, b, name):
    L, D = h.shape
    tm = _tile(L, ROW_TILE)

    def kern(h_ref, m_ref, g_ref, b_ref, o_ref):
        zc, rstd = _ln_stats(ALPHA * h_ref[...] + m_ref[...])
        o_ref[...] = zc * rstd * g_ref[...] + b_ref[...]

    row = pl.BlockSpec((tm, D), lambda i: (i, 0))
    vec = pl.BlockSpec((1, D), lambda i: (0, 0))
    return pl.pallas_call(kern, name=name, grid=(L // tm,), in_specs=[row, row, vec, vec], out_specs=row,
                          out_shape=jax.ShapeDtypeStruct((L, D), F32), compiler_params=_params("parallel"))(h, mix, g, b)


def _ln_bwd_call(h, mix, g, dout, name):
    L, D = h.shape
    tm = _tile(L, ROW_TILE)

    def kern(h_ref, m_ref, g_ref, d_ref, dz_ref, dg_ref, db_ref):
        i = pl.program_id(0)
        zc, rstd = _ln_stats(ALPHA * h_ref[...] + m_ref[...])
        xhat = zc * rstd
        d = d_ref[...]
        dxh = d * g_ref[...]
        m1 = jnp.mean(dxh, axis=-1, keepdims=True)
        m2 = jnp.mean(dxh * xhat, axis=-1, keepdims=True)
        row = i * tm + lax.broadcasted_iota(jnp.int32, (tm, 1), 0)
        dz_ref[...] = jnp.where(row >= PAD, rstd * (dxh - m1 - xhat * m2), 0.0)
        pg = jnp.sum(d * xhat, axis=0, keepdims=True)
        pb = jnp.sum(d, axis=0, keepdims=True)

        @pl.when(i == 0)
        def _():
            dg_ref[...] = pg
            db_ref[...] = pb

        @pl.when(i > 0)
        def _():
            dg_ref[...] += pg
            db_ref[...] += pb

    row = pl.BlockSpec((tm, D), lambda i: (i, 0))
    vec = pl.BlockSpec((1, D), lambda i: (0, 0))
    return pl.pallas_call(
        kern, name=name, grid=(L // tm,), in_specs=[row, row, vec, row], out_specs=[row, vec, vec],
        out_shape=[jax.ShapeDtypeStruct((L, D), F32), jax.ShapeDtypeStruct((1, D), F32), jax.ShapeDtypeStruct((1, D), F32)],
        compiler_params=_params("arbitrary"))(h, mix, g, dout)


def _deepnorm(name):
    @jax.custom_vjp
    def f(h, mix, g, b):
        return _ln_fwd_call(h, mix, g, b, name + "_fwd")

    def fwd(h, mix, g, b):
        return f(h, mix, g, b), (h, mix, g)

    def bwd(res, dout):
        h, mix, g = res
        dz, dg, db = _ln_bwd_call(h, mix, g, dout, name + "_bwd")
        return ALPHA * dz, dz, dg, db

    f.defvjp(fwd, bwd)
    return f


def _rms_fwd_call(x, g, name):
    L, n = x.shape
    tm = _tile(L, ROW_TILE)

    def kern(x_ref, g_ref, o_ref):
        x = x_ref[...]
        o_ref[...] = x * lax.rsqrt(jnp.mean(x * x, axis=-1, keepdims=True) + RMS_EPS) * g_ref[...]

    row = pl.BlockSpec((tm, n), lambda i: (i, 0))
    vec = pl.BlockSpec((1, n), lambda i: (0, 0))
    return pl.pallas_call(kern, name=name, grid=(L // tm,), in_specs=[row, vec], out_specs=row,
                          out_shape=jax.ShapeDtypeStruct((L, n), F32), compiler_params=_params("parallel"))(x, g)


def _rms_bwd_call(x, g, dout, name):
    L, n = x.shape
    tm = _tile(L, ROW_TILE)

    def kern(x_ref, g_ref, d_ref, dx_ref, dg_ref):
        i = pl.program_id(0)
        x = x_ref[...]
        rstd = lax.rsqrt(jnp.mean(x * x, axis=-1, keepdims=True) + RMS_EPS)
        xhat = x * rstd
        d = d_ref[...]
        dxh = d * g_ref[...]
        dx_ref[...] = rstd * (dxh - xhat * jnp.mean(dxh * xhat, axis=-1, keepdims=True))
        pg = jnp.sum(d * xhat, axis=0, keepdims=True)

        @pl.when(i == 0)
        def _():
            dg_ref[...] = pg

        @pl.when(i > 0)
        def _():
            dg_ref[...] += pg

    row = pl.BlockSpec((tm, n), lambda i: (i, 0))
    vec = pl.BlockSpec((1, n), lambda i: (0, 0))
    return pl.pallas_call(
        kern, name=name, grid=(L // tm,), in_specs=[row, vec, row], out_specs=[row, vec],
        out_shape=[jax.ShapeDtypeStruct((L, n), F32), jax.ShapeDtypeStruct((1, n), F32)],
        compiler_params=_params("arbitrary"))(x, g, dout)


def _rmsnorm(name):
    @jax.custom_vjp
    def f(x, g):
        return _rms_fwd_call(x, g, name + "_fwd")

    def fwd(x, g):
        return f(x, g), (x, g)

    def bwd(res, dout):
        x, g = res
        dx, dg = _rms_bwd_call(x, g, dout, name + "_bwd")
        return dx, dg

    f.defvjp(fwd, bwd)
    return f


def _rope_call(x, c, s1, s2, r, out_dtype, name):
    L, W = x.shape
    reps = W // LANES
    tm = _tile(L, ROW_TILE)

    def kern(x_ref, c_ref, s1_ref, s2_ref, o_ref):
        x = x_ref[...].astype(F32)
        wide = lambda t: jnp.tile(t[...], (1, reps)) if reps > 1 else t[...]
        out = x * wide(c_ref) + pltpu.roll(x, W - r, 1) * wide(s1_ref) + pltpu.roll(x, r, 1) * wide(s2_ref)
        o_ref[...] = out.astype(o_ref.dtype)

    row = pl.BlockSpec((tm, W), lambda i: (i, 0))
    tab = pl.BlockSpec((tm, LANES), lambda i: (i, 0))
    return pl.pallas_call(kern, name=name, grid=(L // tm,), in_specs=[row, tab, tab, tab], out_specs=row,
                          out_shape=jax.ShapeDtypeStruct((L, W), out_dtype), compiler_params=_params("parallel"))(x, c, s1, s2)


def _rope(name, r, out_dtype):
    @jax.custom_vjp
    def f(x, c, s1, s2):
        return _rope_call(x, c, s1, s2, r, out_dtype, name + "_fwd")

    def fwd(x, c, s1, s2):
        return f(x, c, s1, s2), (c, s1, s2, jnp.zeros((), x.dtype))

    def bwd(res, g):
        c, s1, s2, proto = res
        dx = _rope_call(g, c, -s1, -s2, r, proto.dtype, name + "_bwd")
        return dx, jnp.zeros_like(c), jnp.zeros_like(s1), jnp.zeros_like(s2)

    f.defvjp(fwd, bwd)
    return f


def _rope_tables(L, dim, theta):
    pos = (jnp.arange(L) - PAD).astype(F32)
    inv = theta ** (-jnp.arange(0, dim, 2, dtype=F32) / dim)
    ang = pos[:, None] * inv[None, :]
    return jnp.cos(ang), jnp.sin(ang)


def _rope_lanes(cos, sin, period):
    L, half = cos.shape
    one = jnp.ones((L, period - 2 * half), F32)
    zero = jnp.zeros((L, period - 2 * half), F32)
    z_h = jnp.zeros((L, half), F32)
    c = jnp.concatenate([cos, cos, one], axis=1)
    s1 = jnp.concatenate([-sin, z_h, zero], axis=1)
    s2 = jnp.concatenate([z_h, sin, zero], axis=1)
    reps = LANES // period
    return tuple(jnp.tile(t, (1, reps)) for t in (c, s1, s2))


HALO = SUBLANES_BF16


GLU_ROWS = 128


def _strips(C):
    return [(c0, min(LANES, C - c0)) for c0 in range(0, C, LANES)]


def _shifted_down(win):
    return [pltpu.roll(win, CONV_W - 1 - t, 0)[SUBLANES_F32:, :] if t < CONV_W - 1 else win[SUBLANES_F32:, :]
            for t in range(CONV_W)]


def _conv_taps(buf_ref, g, off, rows, cols, w, b):
    u = _shifted_down(buf_ref[g, pl.ds(off - SUBLANES_F32, rows + SUBLANES_F32), cols])
    return b + u[0] * w[0:1, :] + u[1] * w[1:2, :] + u[2] * w[2:3, :]


def _glu_fwd_call(u4, cw4, cb4, name):
    _, P, L, C = u4.shape
    tm = _tile(L, ROW_TILE)
    hb = tm // HALO

    def kern(u_ref, up_ref, w_ref, b_ref, o_ref, buf):
        i = pl.program_id(1)
        row = i * tm + lax.broadcasted_iota(jnp.int32, (tm, 1), 0)
        prow = i * tm - HALO + lax.broadcasted_iota(jnp.int32, (HALO, 1), 0)
        for g in range(2):
            buf[g, pl.ds(HALO, tm), :] = jnp.where(row >= PAD, u_ref[g], 0.0)
            buf[g, pl.ds(0, HALO), :] = jnp.where(prow >= PAD, up_ref[g], 0.0)
        for c0, wd in _strips(C):
            cols = pl.ds(c0, wd)
            ws = [w_ref[g][:, c0:c0 + wd] for g in range(2)]
            bs = [b_ref[g][:, c0:c0 + wd] for g in range(2)]
            for r in range(0, tm, GLU_ROWS):
                n = min(GLU_ROWS, tm - r)
                gate, val = (_conv_taps(buf, g, HALO + r, n, cols, ws[g], bs[g]) for g in range(2))
                o_ref[pl.ds(r, n), cols] = (gate * jax.nn.sigmoid(gate) * val).astype(o_ref.dtype)

    return pl.pallas_call(
        kern, name=name, grid=(P, L // tm),
        in_specs=[pl.BlockSpec((2, None, tm, C), lambda p, i: (0, p, i, 0)),
                  pl.BlockSpec((2, None, HALO, C), lambda p, i: (0, p, jnp.maximum(i * hb - 1, 0), 0)),
                  pl.BlockSpec((2, None, CONV_W, C), lambda p, i: (0, p, 0, 0)),
                  pl.BlockSpec((2, None, 1, C), lambda p, i: (0, p, 0, 0))],
        out_specs=pl.BlockSpec((None, tm, C), lambda p, i: (p, i, 0)),
        out_shape=jax.ShapeDtypeStruct((P, L, C), BF16),
        scratch_shapes=[pltpu.VMEM((2, tm + HALO, C), F32)],
        compiler_params=_params("parallel", "parallel"))(u4, u4, cw4, cb4)


def _glu_bwd_call(u4, cw4, cb4, dact, name):
    _, P, L, C = u4.shape
    tm = _tile(L, ROW_TILE)
    hb = tm // HALO
    ext = tm + HALO
    last_halo = L // HALO - 1

    def kern(u_ref, up_ref, un_ref, w_ref, b_ref, d_ref, dn_ref, du_ref, dw_ref, db_ref, ubuf, dbuf):
        i = pl.program_id(1)
        r0 = i * tm
        mask = lambda blk, start: jnp.where(
            (start + lax.broadcasted_iota(jnp.int32, (blk.shape[0], 1), 0) >= PAD), blk, 0.0)
        for g in range(2):
            ubuf[g, pl.ds(0, HALO), :] = mask(up_ref[g], r0 - HALO)
            ubuf[g, pl.ds(HALO, tm), :] = mask(u_ref[g], r0)
            ubuf[g, pl.ds(HALO + tm, HALO), :] = un_ref[g]
        dbuf[pl.ds(0, tm), :] = d_ref[...].astype(F32)
        dbuf[pl.ds(tm, HALO), :] = jnp.where(r0 + tm < L, dn_ref[...].astype(F32), 0.0)
        for c0, wd in _strips(C):
            cols = pl.ds(c0, wd)
            ws = [w_ref[g][:, c0:c0 + wd] for g in range(2)]
            bs = [b_ref[g][:, c0:c0 + wd] for g in range(2)]
            pw = [[jnp.zeros((1, wd), F32) for _ in range(CONV_W)] for _ in range(2)]
            pb = [jnp.zeros((1, wd), F32) for _ in range(2)]
            for r in range(0, tm, GLU_ROWS):
                n = min(GLU_ROWS, tm - r)
                ne = n + SUBLANES_F32
                us = [_shifted_down(ubuf[g, pl.ds(HALO + r - SUBLANES_F32, ne + SUBLANES_F32), cols]) for g in range(2)]
                gate, val = (bs[g] + us[g][0] * ws[g][0:1, :] + us[g][1] * ws[g][1:2, :] + us[g][2] * ws[g][2:3, :]
                             for g in range(2))
                sg = jax.nn.sigmoid(gate)
                d = dbuf[pl.ds(r, ne), cols]
                dys = (d * val * (sg * (1.0 + gate * (1.0 - sg))), d * (gate * sg))
                row = r0 + r + lax.broadcasted_iota(jnp.int32, (n, 1), 0)
                for g in range(2):
                    w = ws[g]
                    dy = dys[g][:n, :]
                    du = (dy * w[2:3, :] + pltpu.roll(dys[g], ne - 1, 0)[:n, :] * w[1:2, :]
                          + pltpu.roll(dys[g], ne - 2, 0)[:n, :] * w[0:1, :])
                    du_ref[g, pl.ds(r, n), cols] = jnp.where(row >= PAD, du, 0.0).astype(du_ref.dtype)
                    for t in range(CONV_W):
                        pw[g][t] = pw[g][t] + jnp.sum(dy * us[g][t][:n, :], axis=0, keepdims=True)
                    pb[g] = pb[g] + jnp.sum(dy, axis=0, keepdims=True)
            for g in range(2):
                pwg = jnp.concatenate(pw[g], axis=0)

                @pl.when(i == 0)
                def _():
                    dw_ref[g, :, cols] = pwg
                    db_ref[g, :, cols] = pb[g]

                @pl.when(i > 0)
                def _():
                    dw_ref[g, :, cols] += pwg
                    db_ref[g, :, cols] += pb[g]

    nxt = lambda i: jnp.minimum((i + 1) * hb, last_halo)
    return pl.pallas_call(
        kern, name=name, grid=(P, L // tm),
        in_specs=[pl.BlockSpec((2, None, tm, C), lambda p, i: (0, p, i, 0)),
                  pl.BlockSpec((2, None, HALO, C), lambda p, i: (0, p, jnp.maximum(i * hb - 1, 0), 0)),
                  pl.BlockSpec((2, None, HALO, C), lambda p, i: (0, p, nxt(i), 0)),
                  pl.BlockSpec((2, None, CONV_W, C), lambda p, i: (0, p, 0, 0)),
                  pl.BlockSpec((2, None, 1, C), lambda p, i: (0, p, 0, 0)),
                  pl.BlockSpec((None, tm, C), lambda p, i: (p, i, 0)),
                  pl.BlockSpec((None, HALO, C), lambda p, i: (p, nxt(i), 0))],
        out_specs=[pl.BlockSpec((2, None, tm, C), lambda p, i: (0, p, i, 0)),
                   pl.BlockSpec((2, None, CONV_W, C), lambda p, i: (0, p, 0, 0)),
                   pl.BlockSpec((2, None, 1, C), lambda p, i: (0, p, 0, 0))],
        out_shape=[jax.ShapeDtypeStruct((2, P, L, C), BF16), jax.ShapeDtypeStruct((2, P, CONV_W, C), F32),
                   jax.ShapeDtypeStruct((2, P, 1, C), F32)],
        scratch_shapes=[pltpu.VMEM((2, tm + 2 * HALO, C), F32), pltpu.VMEM((ext, C), F32)],
        compiler_params=_params("parallel", "arbitrary"))(u4, u4, u4, cw4, cb4, dact, dact)


def _ffn_up(name):
    def run(h1, w3, cw4, cb4):
        u3 = _mm(h1, w3, sb="n", so="n", out_dtype=F32, name=name + "_up")
        u4 = u3.reshape((2, u3.shape[0] // 2) + u3.shape[1:])
        return _glu_fwd_call(u4, cw4, cb4, name + "_glu"), u4

    @jax.custom_vjp
    def f(h1, w3, cw4, cb4):
        return run(h1, w3, cw4, cb4)[0]

    def fwd(h1, w3, cw4, cb4):
        act, u4 = run(h1, w3, cw4, cb4)
        return act, (h1, w3, cw4, cb4, u4)

    def bwd(res, dact):
        h1, w3, cw4, cb4, u4 = res
        du4, dcw, dcb = _glu_bwd_call(u4, cw4, cb4, dact, name + "_glu_bwd")
        du3 = du4.reshape((du4.shape[0] * du4.shape[1],) + du4.shape[2:])
        dh1 = _mm(du3, w3, sa="k", lb="nk", sb="k", out_dtype=F32, name=name + "_up_dx")
        dw3 = _mm(h1, du3, la="km", sb="n", so="n", out_dtype=w3.dtype, name=name + "_up_dw")
        return dh1, dw3, dcw, dcb

    f.defvjp(fwd, bwd)
    return f


def _ffn_down(name):
    @jax.custom_vjp
    def f(act3, wo3):
        return _mm(act3, wo3, sa="k", sb="k", out_dtype=F32, name=name + "_fwd")

    def fwd(act3, wo3):
        return f(act3, wo3), (act3, wo3)

    def bwd(res, g):
        act3, wo3 = res
        dact = _mm(g, wo3, lb="nk", sb="n", so="n", out_dtype=act3.dtype, name=name + "_dx")
        dwo = _mm(act3, g, la="km", sa="m", so="m", out_dtype=wo3.dtype, name=name + "_dw")
        return dact, dwo

    f.defvjp(fwd, bwd)
    return f


def _pairs(n, kv_major):
    if kv_major:
        pr = [(i, j) for j in range(n) for i in range(j, n)]
    else:
        pr = [(i, j) for i in range(n) for j in range(i + 1)]
    return (jnp.asarray(np.array([p[0] for p in pr], np.int32)), jnp.asarray(np.array([p[1] for p in pr], np.int32)))


TN = (((0,), (0,)), ((), ()))
NT = (((1,), (1,)), ((), ()))


def _scores(kT, qT, i, j, tq, tk, masked):
    s = lax.dot_general(kT, qT, TN, preferred_element_type=F32)
    if not masked:
        return s, None
    col = j * tk + lax.broadcasted_iota(jnp.int32, (tk, tq), 0)
    row = i * tq + lax.broadcasted_iota(jnp.int32, (tk, tq), 1)
    mask = (col <= row) & (col >= PAD)
    return jnp.where(mask, s, NEG), mask


FOX_SCALE = 0.125
C_PARTS = 3


class _AttnLayout:
    def __init__(self, heads, q_rows, v_rows, bases=(0, 0, 0), extra=0, q_scale=None, grad_rows=None):
        self.heads, self.q_rows, self.v_rows, self.bases, self.extra, self.q_scale = heads, q_rows, v_rows, bases, extra, q_scale
        self.dk = q_rows + extra
        self.grad_rows = q_rows if grad_rows is None else grad_rows
        self.n_in = 5 if extra else 3

    def specs(self, tq, tk, hp):
        assert self.heads % hp == 0 and all(b % hp == 0 for b in self.bases)
        qb, kb, vb = (b // hp for b in self.bases)
        qmap = lambda g, t, it, jt: (qb + g, it[t])
        kmap = lambda g, t, it, jt: (kb + g, jt[t])
        vmap = lambda g, t, it, jt: (vb + g, jt[t])
        s = [pl.BlockSpec((hp * self.q_rows, tq), qmap), pl.BlockSpec((hp * self.q_rows, tk), kmap),
             pl.BlockSpec((hp * self.v_rows, tk), vmap)]
        if self.extra:
            s += [pl.BlockSpec((hp * self.extra, tq), lambda g, t, it, jt: (g, it[t])),
                  pl.BlockSpec((hp * self.extra, tk), lambda g, t, it, jt: (g, jt[t]))]
        return s

    def operands(self, refs, a):
        rows = lambda ref, n: ref[a * n:(a + 1) * n, :]
        q, k = rows(refs[0], self.q_rows), rows(refs[1], self.q_rows)
        if self.q_scale is not None:
            q = q * jnp.asarray(self.q_scale, q.dtype)
        if self.extra:
            q = jnp.concatenate([q, rows(refs[3], self.extra)], axis=0)
            k = jnp.concatenate([k, rows(refs[4], self.extra)], axis=0)
        return q, k, rows(refs[2], self.v_rows)


def _attn_fwd_call(lay, arrs, name):
    L = arrs[0].shape[1]
    H, dv, hp = lay.heads, lay.v_rows, ATTN_HEADS_FWD
    tq = tk = _tile(L, ROW_TILE)
    it, jt = _pairs(L // tq, False)

    def kern(it_ref, jt_ref, *refs):
        t = pl.program_id(1)
        i, j = it_ref[t], jt_ref[t]
        o_ref, lse_ref, m_s, l_s, acc_s = refs[lay.n_in:]

        @pl.when(j == 0)
        def _():
            m_s[...] = jnp.full_like(m_s, NEG)
            l_s[...] = jnp.zeros_like(l_s)
            acc_s[...] = jnp.zeros_like(acc_s)

        def step(masked):
            ops = [lay.operands(refs, a) for a in range(hp)]
            scores = [_scores(k, q, i, j, tq, tk, masked) for q, k, _ in ops]
            weights = []
            for a, (s, mask) in enumerate(scores):
                m_prev = m_s[a]
                m_new = jnp.maximum(m_prev, jnp.max(s, axis=0, keepdims=True))
                scale = jnp.exp(m_prev - m_new)
                p = jnp.exp(s - m_new)
                if masked:
                    p = jnp.where(mask, p, 0.0)
                l_s[a] = scale * l_s[a] + jnp.sum(p, axis=0, keepdims=True)
                m_s[a] = m_new
                weights.append((scale, p.astype(BF16)))
            for a, (scale, pb) in enumerate(weights):
                acc_s[a] = scale * acc_s[a] + jnp.dot(ops[a][2], pb, preferred_element_type=F32)

        pl.when(j == i)(lambda: step(True))
        pl.when(j != i)(lambda: step(False))

        @pl.when(j == i)
        def _():
            for a in range(hp):
                l = l_s[a]
                l = jnp.where(l == 0.0, 1.0, l)
                o_ref[a * dv:(a + 1) * dv, :] = (acc_s[a] / l).astype(o_ref.dtype)
                lse_ref[a] = m_s[a] + jnp.log(l)

    return pl.pallas_call(
        kern, name=name,
        grid_spec=pltpu.PrefetchScalarGridSpec(
            num_scalar_prefetch=2, grid=(H // hp, int(it.shape[0])), in_specs=lay.specs(tq, tk, hp),
            out_specs=[pl.BlockSpec((hp * dv, tq), lambda g, t, it, jt: (g, it[t])),
                       pl.BlockSpec((hp, 1, tq), lambda g, t, it, jt: (g, 0, it[t]))],
            scratch_shapes=[pltpu.VMEM((hp, 1, tq), F32), pltpu.VMEM((hp, 1, tq), F32), pltpu.VMEM((hp, dv, tq), F32)]),
        out_shape=[jax.ShapeDtypeStruct((H * dv, L), F32), jax.ShapeDtypeStruct((H, 1, L), F32)],
        compiler_params=_params("parallel", "arbitrary"))(it, jt, *arrs)


C_ROWS = SUBLANES_F32


def _attn_bwd_call(lay, arrs, oT, lse, doT, name):
    L = arrs[0].shape[1]
    H, dv, dk, gr, hp = lay.heads, lay.v_rows, lay.dk, lay.grad_rows, ATTN_HEADS_BWD
    tq = tk = _tile(L, ROW_TILE)
    nq = L // tq
    it, jt = _pairs(nq, True)
    npairs = int(it.shape[0])
    with_c = lay.extra > 0
    crows = pl.ds(lay.q_rows, C_ROWS)

    def kern(it_ref, jt_ref, *refs):
        t = pl.program_id(1)
        i, j = it_ref[t], jt_ref[t]
        rest = refs[lay.n_in:]
        o_ref, do_ref, lse_ref = rest[:3]
        if with_c:
            dq_ref, dk_ref, dv_ref, dqc_ref, dkc_ref, dq_s, dk_s, dv_s = rest[3:]
        else:
            dq_ref, dk_ref, dv_ref, dq_s, dk_s, dv_s = rest[3:]

        @pl.when(t == 0)
        def _():
            dq_s[...] = jnp.zeros_like(dq_s)

        @pl.when(i == j)
        def _():
            dk_s[...] = jnp.zeros_like(dk_s)
            dv_s[...] = jnp.zeros_like(dv_s)

        def step(masked):
            first = []
            for a in range(hp):
                q, k, v = lay.operands(refs, a)
                s, mask = _scores(k, q, i, j, tq, tk, masked)
                dof = do_ref[a * dv:(a + 1) * dv, :]
                dob = dof.astype(BF16)
                delta = jnp.sum(dof * o_ref[a * dv:(a + 1) * dv, :], axis=0, keepdims=True)
                dp = lax.dot_general(v, dob, TN, preferred_element_type=F32)
                first.append((q, k, s, mask, dob, delta, dp))
            second = []
            for a, (q, k, s, mask, dob, delta, dp) in enumerate(first):
                p = jnp.exp(s - lse_ref[a])
                if masked:
                    p = jnp.where(mask, p, 0.0)
                second.append((p.astype(BF16), (p * (dp - delta)).astype(BF16)))
            cols = pl.ds(pl.multiple_of(i * tq, tq), tq)
            for a, (pb, dsb) in enumerate(second):
                q, k, _, _, dob, _, _ = first[a]
                dv_s[a] += lax.dot_general(dob, pb, NT, preferred_element_type=F32)
                dk_s[a] += lax.dot_general(q, dsb, NT, preferred_element_type=F32)
                dq_s[a, :, cols] += jnp.dot(k, dsb, preferred_element_type=F32)

        pl.when(j == i)(lambda: step(True))
        pl.when(j != i)(lambda: step(False))

        @pl.when(i == nq - 1)
        def _():
            for a in range(hp):
                dk_ref[a * gr:(a + 1) * gr, :] = dk_s[a, :gr, :].astype(dk_ref.dtype)
                dv_ref[a * dv:(a + 1) * dv, :] = dv_s[a].astype(dv_ref.dtype)
                if with_c:
                    dkc_ref[a * C_ROWS:(a + 1) * C_ROWS, :] = dk_s[a, crows, :]

        @pl.when(t == npairs - 1)
        def _():
            for a in range(hp):
                dq = dq_s[a, :gr, :]
                dq_ref[a * gr:(a + 1) * gr, :] = (dq if lay.q_scale is None else dq * lay.q_scale).astype(dq_ref.dtype)
                if with_c:
                    dqc_ref[a * C_ROWS:(a + 1) * C_ROWS, :] = dq_s[a, crows, :]

    qcol = lambda g, t, it, jt: (g, it[t])
    kcol = lambda g, t, it, jt: (g, jt[t])
    whole = lambda g, t, it, jt: (g, 0)
    out_specs = [pl.BlockSpec((hp * gr, L), whole), pl.BlockSpec((hp * gr, tk), kcol), pl.BlockSpec((hp * dv, tk), kcol)]
    out_shape = [jax.ShapeDtypeStruct((H * gr, L), BF16), jax.ShapeDtypeStruct((H * gr, L), BF16),
                 jax.ShapeDtypeStruct((H * dv, L), BF16)]
    if with_c:
        out_specs += [pl.BlockSpec((hp * C_ROWS, L), whole), pl.BlockSpec((hp * C_ROWS, tk), kcol)]
        out_shape += [jax.ShapeDtypeStruct((H * C_ROWS, L), F32)] * 2
    return pl.pallas_call(
        kern, name=name,
        grid_spec=pltpu.PrefetchScalarGridSpec(
            num_scalar_prefetch=2, grid=(H // hp, npairs),
            in_specs=lay.specs(tq, tk, hp) + [pl.BlockSpec((hp * dv, tq), qcol), pl.BlockSpec((hp * dv, tq), qcol),
                                          pl.BlockSpec((hp, 1, tq), lambda g, t, it, jt: (g, 0, it[t]))],
            out_specs=out_specs,
            scratch_shapes=[pltpu.VMEM((hp, dk, L), F32), pltpu.VMEM((hp, dk, tk), F32), pltpu.VMEM((hp, dv, tk), F32)]),
        out_shape=out_shape, compiler_params=_params("parallel", "arbitrary"))(it, jt, *arrs, oT, doT, lse)


POISON = 1e30


def _fox_extras(cT):
    H, L = cT.shape
    to_bf16 = lambda t: lax.reduce_precision(t, exponent_bits=8, mantissa_bits=7)
    hi = to_bf16(cT)
    lo = to_bf16(cT - hi)
    ll = to_bf16(cT - hi - lo)
    terms = jnp.stack([hi, lo, ll], axis=1).astype(BF16)
    ones = jnp.ones_like(terms)
    fill = jnp.zeros((H, SUBLANES_BF16 - 2 * C_PARTS, L), BF16)
    pad_key = (jnp.arange(L) < PAD)[None, None, :]
    kterms = jnp.where(pad_key, jnp.asarray([POISON, 0.0, 0.0], BF16)[None, :, None], terms)
    eq = jnp.concatenate([terms, ones, fill], axis=1)
    ek = jnp.concatenate([ones, -kterms, fill], axis=1)
    return eq.reshape(H * SUBLANES_BF16, L), ek.reshape(H * SUBLANES_BF16, L)


def _fox_attention(name):
    assert FOX_SCALE == FOX_HD ** -0.5
    lay = _AttnLayout(FOX_HEADS, FOX_HD, FOX_HD, bases=(0, FOX_HEADS, 2 * FOX_HEADS), extra=SUBLANES_BF16,
                      q_scale=FOX_SCALE)

    def run(qkvT, cT):
        arrs = (qkvT, qkvT, qkvT) + _fox_extras(cT)
        oT, lse = _attn_fwd_call(lay, arrs, name + "_fwd")
        return oT, (arrs, oT, lse)

    @jax.custom_vjp
    def f(qkvT, cT):
        return run(qkvT, cT)[0]

    def fwd(qkvT, cT):
        return run(qkvT, cT)

    def bwd(res, doT):
        arrs, oT, lse = res
        dq, dk, dv, dqc, dkc = _attn_bwd_call(lay, arrs, oT, lse, doT, name + "_bwd")
        L = dq.shape[1]
        dc = dqc.reshape(FOX_HEADS, C_ROWS, L)[:, 0, :] - dkc.reshape(FOX_HEADS, C_ROWS, L)[:, C_PARTS, :]
        return jnp.concatenate([dq, dk, dv], axis=0), dc

    f.defvjp(fwd, bwd)
    return f


MLA_XROWS = SUBLANES_BF16


def _mla_attention(name):
    rows = MLA_NOPE + MLA_ROPE + MLA_XROWS
    lay = _AttnLayout(MLA_HEADS, rows, MLA_V, grad_rows=MLA_NOPE + MLA_ROPE)

    @jax.custom_vjp
    def f(qT, kT, vT):
        return _attn_fwd_call(lay, (qT, kT, vT), name + "_fwd")[0]

    def fwd(qT, kT, vT):
        oT, lse = _attn_fwd_call(lay, (qT, kT, vT), name + "_fwd")
        return oT, (qT, kT, vT, oT, lse)

    def bwd(res, doT):
        qT, kT, vT, oT, lse = res
        dq, dk, dv = _attn_bwd_call(lay, (qT, kT, vT), oT, lse, doT, name + "_bwd")
        L = dq.shape[1]
        widen = lambda t: jnp.pad(t.reshape(MLA_HEADS, -1, L), ((0, 0), (0, MLA_XROWS), (0, 0))).reshape(-1, L)
        return widen(dq), widen(dk), dv

    f.defvjp(fwd, bwd)
    return f


def _linear_t(name, out_dtype):
    @jax.custom_vjp
    def f(wT, x):
        return _mm(wT, x, lb="nk", out_dtype=out_dtype, name=name + "_fwd")

    def fwd(wT, x):
        return f(wT, x), (wT, x)

    def bwd(res, g):
        wT, x = res
        dwT = _mm(g, x, out_dtype=wT.dtype, name=name + "_dw")
        dx = _mm(g, wT, la="km", out_dtype=x.dtype, name=name + "_dx")
        return dwT, dx

    f.defvjp(fwd, bwd)
    return f


def _linear_km(name):
    @jax.custom_vjp
    def f(aT, w):
        return _mm(aT, w, la="km", out_dtype=F32, name=name + "_fwd")

    def fwd(aT, w):
        return f(aT, w), (aT, w)

    def bwd(res, g):
        aT, w = res
        daT = _mm(w, g, lb="nk", out_dtype=aT.dtype, name=name + "_dx")
        dw = _mm(aT, g, out_dtype=w.dtype, name=name + "_dw")
        return daT, dw

    f.defvjp(fwd, bwd)
    return f


def _swa_parts(q_ref, kc_ref, kp_ref, km_ref, sink_ref, i, scale):
    R = SWA_G * BLOCK
    q = q_ref[...].reshape(R, SWA_HD)
    nt = (((1,), (1,)), ((), ()))
    r = lax.broadcasted_iota(jnp.int32, (R, BLOCK), 0) & (BLOCK - 1)
    c = lax.broadcasted_iota(jnp.int32, (R, BLOCK), 1)
    s_c = jnp.where((c <= r) & (i >= 1), lax.dot_general(q, kc_ref[...], nt, preferred_element_type=F32) * scale, NEG)
    s_p = jnp.where((c > r) & (i >= 2), lax.dot_general(q, kp_ref[...], nt, preferred_element_type=F32) * scale, NEG)
    s_m = jnp.where((c >= PAD) & ((i >= 1) | (c <= r)),
                    lax.dot_general(q, km_ref[...], nt, preferred_element_type=F32) * scale, NEG)
    sink = jnp.broadcast_to(sink_ref[...], (SWA_G, BLOCK, 1)).reshape(R, 1)
    return q, s_c, s_p, s_m, sink


def _swa_specs(L):
    nb = L // BLOCK
    qs = pl.BlockSpec((None, SWA_G, BLOCK, SWA_HD), lambda g, i: (g, 0, i, 0))
    kc = pl.BlockSpec((None, BLOCK, SWA_HD), lambda g, i: (g, i, 0))
    kp = pl.BlockSpec((None, BLOCK, SWA_HD), lambda g, i: (g, jnp.maximum(i - 1, 0), 0))
    km = pl.BlockSpec((None, BLOCK, SWA_HD), lambda g, i: (g, 0, 0))
    sk = pl.BlockSpec((None, SWA_G, 1, 1), lambda g, i: (g, 0, 0, 0))
    ls = pl.BlockSpec((None, SWA_G, BLOCK, 1), lambda g, i: (g, 0, i, 0))
    return nb, qs, kc, kp, km, sk, ls


def _swa_fwd_call(q, k, v, sinks, scale, name):
    _, _, L, _ = q.shape
    nb, qs, kc, kp, km, sk, ls = _swa_specs(L)

    def kern(q_ref, kc_ref, kp_ref, km_ref, vc_ref, vp_ref, vm_ref, sink_ref, o_ref, lse_ref):
        i = pl.program_id(1)
        _, s_c, s_p, s_m, sink = _swa_parts(q_ref, kc_ref, kp_ref, km_ref, sink_ref, i, scale)
        mx = lambda s: jnp.max(s, axis=-1, keepdims=True)
        m = jnp.maximum(jnp.maximum(mx(s_c), mx(s_p)), jnp.maximum(mx(s_m), sink))
        p_c, p_p, p_m = jnp.exp(s_c - m), jnp.exp(s_p - m), jnp.exp(s_m - m)
        sm = lambda p: jnp.sum(p, axis=-1, keepdims=True)
        den = sm(p_c) + sm(p_p) + sm(p_m) + jnp.exp(sink - m)
        inv = 1.0 / den
        pv = lambda p, v_ref: jnp.dot((p * inv).astype(BF16), v_ref[...], preferred_element_type=F32)
        o = pv(p_c, vc_ref) + pv(p_p, vp_ref) + pv(p_m, vm_ref)
        o_ref[...] = o.reshape(SWA_G, BLOCK, SWA_HD)
        lse_ref[...] = (m + jnp.log(den)).reshape(SWA_G, BLOCK, 1)

    return pl.pallas_call(
        kern, name=name, grid=(SWA_KVH, nb), in_specs=[qs, kc, kp, km, kc, kp, km, sk], out_specs=[qs, ls],
        out_shape=[jax.ShapeDtypeStruct((SWA_KVH, SWA_G, L, SWA_HD), F32), jax.ShapeDtypeStruct((SWA_KVH, SWA_G, L, 1), F32)],
        compiler_params=_params("parallel", "parallel"))(q, k, k, k, v, v, v, sinks)


def _swa_bwd_call(q, k, v, sinks, o, lse, do, scale, name):
    _, _, L, _ = q.shape
    nb, qs, kc, kp, km, sk, ls = _swa_specs(L)
    R = SWA_G * BLOCK
    full = pl.BlockSpec((None, L, SWA_HD), lambda g, i: (g, 0, 0))

    def kern(q_ref, kc_ref, kp_ref, km_ref, vc_ref, vp_ref, vm_ref, sink_ref, o_ref, do_ref, lse_ref,
             dq_ref, dk_ref, dv_ref, dsink_ref):
        i = pl.program_id(1)

        @pl.when(i == 0)
        def _():
            dk_ref[...] = jnp.zeros_like(dk_ref)
            dv_ref[...] = jnp.zeros_like(dv_ref)
            dsink_ref[...] = jnp.zeros_like(dsink_ref)

        q, s_c, s_p, s_m, sink = _swa_parts(q_ref, kc_ref, kp_ref, km_ref, sink_ref, i, scale)
        lse = lse_ref[...].reshape(R, 1)
        dof = do_ref[...].reshape(R, SWA_HD)
        dob = dof.astype(BF16)
        delta = jnp.sum(dof * o_ref[...].reshape(R, SWA_HD), axis=-1, keepdims=True)
        nt = (((1,), (1,)), ((), ()))
        tn = (((0,), (0,)), ((), ()))
        cur = pl.ds(pl.multiple_of(i * BLOCK, BLOCK), BLOCK)
        prev = pl.ds(pl.multiple_of(jnp.maximum(i - 1, 0) * BLOCK, BLOCK), BLOCK)
        meta = pl.ds(0, BLOCK)
        dq = jnp.zeros((R, SWA_HD), F32)
        for s, k_ref, v_ref, rows in ((s_c, kc_ref, vc_ref, cur), (s_p, kp_ref, vp_ref, prev), (s_m, km_ref, vm_ref, meta)):
            p = jnp.exp(s - lse)
            dp = lax.dot_general(dob, v_ref[...], nt, preferred_element_type=F32)
            ds = (p * (dp - delta)).astype(BF16)
            dq = dq + jnp.dot(ds, k_ref[...], preferred_element_type=F32)
            dv_ref[rows, :] += lax.dot_general(p.astype(BF16), dob, tn, preferred_element_type=F32)
            dk_ref[rows, :] += lax.dot_general(ds, q, tn, preferred_element_type=F32) * scale
        dq_ref[...] = (dq * scale).reshape(SWA_G, BLOCK, SWA_HD).astype(dq_ref.dtype)
        dsk = -jnp.exp(sink - lse) * delta
        dsink_ref[...] += jnp.sum(dsk.reshape(SWA_G, BLOCK, 1), axis=1, keepdims=True)

    return pl.pallas_call(
        kern, name=name, grid=(SWA_KVH, nb), in_specs=[qs, kc, kp, km, kc, kp, km, sk, qs, qs, ls],
        out_specs=[qs, full, full, sk],
        out_shape=[jax.ShapeDtypeStruct((SWA_KVH, SWA_G, L, SWA_HD), BF16), jax.ShapeDtypeStruct((SWA_KVH, L, SWA_HD), F32),
                   jax.ShapeDtypeStruct((SWA_KVH, L, SWA_HD), F32), jax.ShapeDtypeStruct((SWA_KVH, SWA_G, 1, 1), F32)],
        compiler_params=_params("parallel", "arbitrary"))(q, k, k, k, v, v, v, sinks, o, do, lse)


def _swa_attn(name, scale):
    @jax.custom_vjp
    def f(q, k, v, sinks):
        return _swa_fwd_call(q, k, v, sinks, scale, name + "_fwd")[0]

    def fwd(q, k, v, sinks):
        o, lse = _swa_fwd_call(q, k, v, sinks, scale, name + "_fwd")
        return o, (q, k, v, sinks, o, lse)

    def bwd(res, do):
        q, k, v, sinks, o, lse = res
        dq, dk, dv, dsink = _swa_bwd_call(q, k, v, sinks, o, lse, do, scale, name + "_bwd")
        return dq, dk.astype(BF16), dv.astype(BF16), dsink

    f.defvjp(fwd, bwd)
    return f


def _scan_call(x, bias, mul, pre_logsig, name):
    L, W = x.shape
    tb = _tile(L, ROW_TILE)
    has_mul = mul is not None

    def kern(x_ref, b_ref, *rest):
        if has_mul:
            m_ref, o_ref, tot_ref, carry = rest
        else:
            o_ref, tot_ref, carry = rest
        i = pl.program_id(0)

        @pl.when(i == 0)
        def _():
            carry[...] = jnp.zeros_like(carry)
            tot_ref[...] = jnp.zeros_like(tot_ref)

        z = x_ref[...] + b_ref[...]
        if pre_logsig:
            z = jnp.minimum(z, 0.0) - jnp.log(1.0 + jnp.exp(-jnp.abs(z)))
        row = lax.broadcasted_iota(jnp.int32, (tb, W), 0)
        s = 1
        while s < tb:
            z = z + jnp.where(row >= s, pltpu.roll(z, s, 0), 0.0)
            s *= 2
        z = z + carry[...]
        carry[...] = z[tb - 1:tb, :]
        if has_mul:
            z = z * m_ref[...]
        o_ref[...] = z
        tot_ref[...] += jnp.sum(z, axis=0, keepdims=True)

    row = pl.BlockSpec((tb, W), lambda i: (i, 0))
    vec = pl.BlockSpec((1, W), lambda i: (0, 0))
    return pl.pallas_call(
        kern, name=name, grid=(L // tb,), in_specs=[row, vec] + ([row] if has_mul else []), out_specs=[row, vec],
        out_shape=[jax.ShapeDtypeStruct((L, W), F32), jax.ShapeDtypeStruct((1, W), F32)],
        scratch_shapes=[pltpu.VMEM((1, W), F32)],
        compiler_params=_params("arbitrary"))(*([x, bias] + ([mul] if has_mul else [])))


def _sigmoid_neg_call(x, bias, name):
    L, W = x.shape
    tb = _tile(L, ROW_TILE)

    def kern(x_ref, b_ref, o_ref):
        row = pl.program_id(0) * tb + lax.broadcasted_iota(jnp.int32, (tb, 1), 0)
        o_ref[...] = jnp.where(row >= PAD, jax.nn.sigmoid(-(x_ref[...] + b_ref[...])), 0.0)

    row = pl.BlockSpec((tb, W), lambda i: (i, 0))
    vec = pl.BlockSpec((1, W), lambda i: (0, 0))
    return pl.pallas_call(kern, name=name, grid=(L // tb,), in_specs=[row, vec], out_specs=row,
                          out_shape=jax.ShapeDtypeStruct((L, W), F32), compiler_params=_params("parallel"))(x, bias)


def _decay(name):
    @jax.custom_vjp
    def f(fg, b):
        return _scan_call(fg, b, None, True, name + "_fwd")[0]

    def fwd(fg, b):
        return f(fg, b), (fg, b)

    def bwd(res, dc):
        fg, b = res
        sg = _sigmoid_neg_call(fg, b, name + "_dsig")
        dfg_rev, db = _scan_call(dc[::-1], jnp.zeros_like(b), sg[::-1], False, name + "_bwd")
        return dfg_rev[::-1], db

    f.defvjp(fwd, bwd)
    return f


def _loss_call(hf, target, name):
    L, D = hf.shape
    nb = L // BLOCK

    def kern(h_ref, t_ref, loss_ref, dy_ref, acc):
        i = pl.program_id(0)

        @pl.when(i == 0)
        def _():
            acc[...] = jnp.zeros_like(acc)
            dy_ref[...] = jnp.zeros_like(dy_ref)

        @pl.when(i > 0)
        def _():
            e = h_ref[...] - t_ref[...]
            dy_ref[...] = e * (1.0 / D)
            acc[...] += jnp.sum(e * e, axis=0, keepdims=True)

        @pl.when(i == nb - 1)
        def _():
            loss_ref[...] = jnp.broadcast_to(jnp.sum(acc[...], axis=1, keepdims=True) * (0.5 / D), loss_ref.shape)

    return pl.pallas_call(
        kern, name=name, grid=(nb,),
        in_specs=[pl.BlockSpec((BLOCK, D), lambda i: (i, 0)), pl.BlockSpec((BLOCK, D), lambda i: (jnp.maximum(i - 1, 0), 0))],
        out_specs=[pl.BlockSpec((1, LANES), lambda i: (0, 0)), pl.BlockSpec((BLOCK, D), lambda i: (i, 0))],
        out_shape=[jax.ShapeDtypeStruct((1, LANES), F32), jax.ShapeDtypeStruct((L, D), F32)],
        scratch_shapes=[pltpu.VMEM((1, D), F32)],
        compiler_params=_params("arbitrary"))(hf, target)


def _loss_head(name):
    @jax.custom_vjp
    def f(hf, target):
        return _loss_call(hf, target, name)[0][0, 0]

    def fwd(hf, target):
        loss, dy = _loss_call(hf, target, name)
        return loss[0, 0], (dy, jnp.zeros((), F32))

    def bwd(res, g):
        dy, _ = res
        return dy * g, None

    f.defvjp(fwd, bwd)
    return f


N_CHIPS = N_DEV // 2


def _sibling_swap(arrs, name):
    n = len(arrs)

    def body(*refs):
        srcs, outs = refs[:n], refs[n:2 * n]
        send_sems, recv_sems = refs[2 * n:]
        x, y, c = lax.axis_index("x"), lax.axis_index("y"), lax.axis_index("c")
        copies = [[pltpu.make_async_remote_copy(
            src_ref=srcs[a].at[2 * q + (1 - c)], dst_ref=outs[a].at[q], send_sem=send_sems.at[a, q],
            recv_sem=recv_sems.at[a, q], device_id=(x, y, 1 - c), device_id_type=MESH) for q in range(N_CHIPS)]
            for a in range(n)]
        for row in copies:
            for cp in row:
                cp.start()
        for row in copies:
            for cp in row:
                cp.wait()

    hbm = pl.BlockSpec(memory_space=pl.ANY)
    return pl.pallas_call(
        body, name=name, out_shape=[jax.ShapeDtypeStruct((N_CHIPS,) + tuple(a.shape[1:]), a.dtype) for a in arrs],
        in_specs=[hbm] * n, out_specs=[hbm] * n,
        scratch_shapes=[pltpu.SemaphoreType.DMA((n, N_CHIPS)), pltpu.SemaphoreType.DMA((n, N_CHIPS))],
    )(*arrs)


def _pair_sum_call(a, b, name):
    shape = a.shape
    cols = shape[-1]
    rows = int(np.prod(shape[:-1]))
    tr = rows
    if rows > 2 * ADAM_ROWS:
        tr = max(t for t in range(SUBLANES_BF16, 2 * ADAM_ROWS + 1, SUBLANES_BF16) if rows % t == 0)

    def kern(a_ref, b_ref, o_ref):
        o_ref[...] = (a_ref[...].astype(F32) + b_ref[...].astype(F32)).astype(o_ref.dtype)

    slab = pl.BlockSpec((tr, cols), lambda i: (i, 0))
    return pl.pallas_call(kern, name=name, grid=(rows // tr,), in_specs=[slab, slab], out_specs=slab,
                          out_shape=jax.ShapeDtypeStruct((rows, cols), a.dtype),
                          compiler_params=_params("parallel"))(a.reshape(rows, cols), b.reshape(rows, cols)).reshape(shape)


def _chip_exchange(arrs, name):
    n = len(arrs)

    def body(*refs):
        srcs, outs = refs[:n], refs[n:2 * n]
        send_sems, recv_sems, local_sems = refs[2 * n:]
        x, y, c = lax.axis_index("x"), lax.axis_index("y"), lax.axis_index("c")
        mine = 2 * x + y
        chips = [(1 - x, y), (x, 1 - y), (1 - x, 1 - y)]
        started = []
        for a in range(n):
            own = pltpu.make_async_copy(srcs[a].at[mine], outs[a].at[mine], local_sems.at[a])
            own.start()
            started.append(own.wait)
        for k, (px, py) in enumerate(chips):
            for a in range(n):
                cp = pltpu.make_async_remote_copy(src_ref=srcs[a].at[2 * px + py], dst_ref=outs[a].at[mine],
                                                  send_sem=send_sems.at[a, k], recv_sem=recv_sems.at[a, k],
                                                  device_id=(px, py, c), device_id_type=MESH)
                cp.start()
                started.append(cp.wait_send)
        for k, (px, py) in enumerate(chips):
            for a in range(n):
                pltpu.make_async_remote_copy(src_ref=srcs[a].at[mine], dst_ref=outs[a].at[2 * px + py],
                                             send_sem=send_sems.at[a, k], recv_sem=recv_sems.at[a, k],
                                             device_id=(px, py, c), device_id_type=MESH).wait_recv()
        for wait in started:
            wait()

    hbm = pl.BlockSpec(memory_space=pl.ANY)
    return pl.pallas_call(
        body, name=name, out_shape=[jax.ShapeDtypeStruct(a.shape, a.dtype) for a in arrs],
        in_specs=[hbm] * n, out_specs=[hbm] * n,
        scratch_shapes=[pltpu.SemaphoreType.DMA((n, N_CHIPS - 1)), pltpu.SemaphoreType.DMA((n, N_CHIPS - 1)),
                        pltpu.SemaphoreType.DMA((n,))],
    )(*arrs)


def _scatter(arrs, names):
    c = lax.axis_index("c")
    theirs = _sibling_swap(arrs, "scatter_swap")
    halves = [lax.dynamic_index_in_dim(a.reshape((N_CHIPS, 2) + a.shape[1:]), c, axis=1, keepdims=False) for a in arrs]
    sums = [_pair_sum_call(h, t, "pair_sum_" + nm) for h, t, nm in zip(halves, theirs, names)]
    return _chip_exchange(sums, "scatter_chips")


def _gather(arrs, name):
    n = len(arrs)
    hops = N_DEV - 1

    def body(*refs):
        srcs, outs = refs[:n], refs[n:2 * n]
        send_sems, recv_sems, local_sems = refs[2 * n:]
        x, y, c = lax.axis_index("x"), lax.axis_index("y"), lax.axis_index("c")
        ident = lambda px, py, pc: 4 * px + 2 * py + pc
        me, sibling = ident(x, y, c), (x, y, 1 - c)
        chips = [(1 - x, y), (x, 1 - y), (1 - x, 1 - y)]

        def copy(a, k, block, to, src=None):
            return pltpu.make_async_remote_copy(
                src_ref=outs[a].at[block] if src is None else src, dst_ref=outs[a].at[block],
                send_sem=send_sems.at[a, k], recv_sem=recv_sems.at[a, k], device_id=to, device_id_type=MESH)

        started = []
        for a in range(n):
            mine = pltpu.make_async_copy(srcs[a], outs[a].at[me], local_sems.at[a])
            mine.start()
            started.append(mine.wait)
            for k, to in enumerate([sibling] + [(px, py, c) for px, py in chips]):
                cp = copy(a, k, me, to, src=srcs[a])
                cp.start()
                started.append(cp.wait_send)
        for j, (px, py) in enumerate(chips):
            for a in range(n):
                copy(a, 1 + j, ident(px, py, c), (x, y, c)).wait_recv()
                cp = copy(a, 4 + j, ident(px, py, c), sibling)
                cp.start()
                started.append(cp.wait_send)
        for a in range(n):
            copy(a, 0, ident(x, y, 1 - c), (x, y, c)).wait_recv()
            for j, (px, py) in enumerate(chips):
                copy(a, 4 + j, ident(px, py, 1 - c), (x, y, c)).wait_recv()
        for wait in started:
            wait()

    hbm = pl.BlockSpec(memory_space=pl.ANY)
    return pl.pallas_call(
        body, name=name, out_shape=[jax.ShapeDtypeStruct((N_DEV,) + tuple(a.shape), a.dtype) for a in arrs],
        in_specs=[hbm] * n, out_specs=[hbm] * n,
        scratch_shapes=[pltpu.SemaphoreType.DMA((n, hops)), pltpu.SemaphoreType.DMA((n, hops)),
                        pltpu.SemaphoreType.DMA((n,))],
    )(*arrs)


def _adamw_call(parts, w, m, v, name):
    shape = w.shape
    nparts = parts.shape[0]
    cols = shape[-1]
    rows = int(np.prod(shape[:-1]))
    tr = rows
    if rows > ADAM_ROWS:
        tr = max(t for t in range(SUBLANES_F32, ADAM_ROWS + 1, SUBLANES_F32) if rows % t == 0)
    c1 = 1.0 / (1.0 - ADAM_B1 ** ADAM_STEP)
    c2 = 1.0 / (1.0 - ADAM_B2 ** ADAM_STEP)

    def kern(p_ref, w_ref, m_ref, v_ref, g_ref, d_ref, nm_ref, nv_ref):
        g = p_ref[0].astype(F32)
        for p in range(1, nparts):
            g = g + p_ref[p].astype(F32)
        nm = ADAM_B1 * m_ref[...] + (1.0 - ADAM_B1) * g
        nv = ADAM_B2 * v_ref[...] + (1.0 - ADAM_B2) * (g * g)
        g_ref[...] = g
        nm_ref[...] = nm
        nv_ref[...] = nv
        d_ref[...] = -ADAM_LR * ((nm * c1) / (jnp.sqrt(nv * c2) + ADAM_EPS) + ADAM_WD * w_ref[...])

    slab = pl.BlockSpec((tr, cols), lambda i: (i, 0))
    flat = lambda t: t.reshape(rows, cols)
    res = pl.pallas_call(
        kern, name=name, grid=(rows // tr,),
        in_specs=[pl.BlockSpec((nparts, tr, cols), lambda i: (0, i, 0)), slab, slab, slab], out_specs=[slab] * 4,
        out_shape=[jax.ShapeDtypeStruct((rows, cols), F32)] * 4,
        compiler_params=_params("parallel"))(parts.reshape(nparts, rows, cols), flat(w), flat(m), flat(v))
    return [r.reshape(shape) for r in res]


def _to_full(stacked, axis):
    moved = jnp.moveaxis(stacked, 0, axis)
    s = list(moved.shape)
    return moved.reshape(s[:axis] + [s[axis] * s[axis + 1]] + s[axis + 2:])


def _heads(x, h, d):
    return x.reshape(x.shape[0], h, d).transpose(1, 0, 2)


def _unheads(x):
    return x.transpose(1, 0, 2).reshape(x.shape[1], -1)


def _pad_cols(x, width):
    return jnp.pad(x, ((0, 0), (0, width - x.shape[1])))


def _fox_mixer(h, w_in, b_f, w_o, tag):
    hd = FOX_HEADS * FOX_HD
    qkvT = _linear_t(tag + "_qkv", BF16)(w_in[:, :3 * hd].T, h)
    fg = _linear(tag + "_gate", F32)(h, _pad_cols(w_in[:, 3 * hd:], LANES))
    c = _decay(tag + "_decay")(fg, _pad_cols(b_f[None, :], LANES))
    oT = _fox_attention(tag + "_attn")(qkvT, c[:, :FOX_HEADS].T)
    return _linear_km(tag + "_out")(oT, w_o)


def _swa_mixer(h, w_in, sinks, w_o, tabs, tag):
    L = h.shape[0]
    qd, kd = SWA_QH * SWA_HD, SWA_KVH * SWA_HD
    proj = _linear(tag + "_qkv", F32)(h, w_in)
    rope = _rope(tag + "_rope", ROPE_DIM // 2, BF16)
    q = rope(proj[:, :qd], *tabs)
    k = rope(proj[:, qd:qd + kd], *tabs)
    v = proj[:, qd + kd:].astype(BF16)
    qg = _heads(q, SWA_QH, SWA_HD).reshape(SWA_KVH, SWA_G, L, SWA_HD)
    o = _swa_attn(tag + "_attn", SWA_HD ** -0.5)(qg, _heads(k, SWA_KVH, SWA_HD), _heads(v, SWA_KVH, SWA_HD),
                                                 sinks.reshape(SWA_KVH, SWA_G, 1, 1))
    return _linear(tag + "_out", F32)(_unheads(o.reshape(SWA_QH, L, SWA_HD)), w_o)


def _mla_mixer(h, w_a, g_q, g_kv, w_uq, w_ukv, w_o, tabs, tag):
    L = h.shape[0]
    cq = _linear(tag + "_aq", F32)(h, w_a[:, :MLA_QL])
    ckv = _linear(tag + "_akv", F32)(h, w_a[:, MLA_QL:MLA_QL + MLA_KVL])
    kr = _linear(tag + "_akr", F32)(h, _pad_cols(w_a[:, MLA_QL + MLA_KVL:], LANES))
    cq = _rmsnorm(tag + "_nq")(cq, g_q[None, :])
    ckv = _rmsnorm(tag + "_nkv")(ckv, g_kv[None, :])
    q = _linear(tag + "_uq", F32)(cq, w_uq).reshape(L, MLA_HEADS, MLA_NOPE + MLA_ROPE)
    kv = _linear(tag + "_ukv", BF16)(ckv, w_ukv).reshape(L, MLA_HEADS, MLA_NOPE + MLA_V)
    scale = (MLA_NOPE + MLA_ROPE) ** -0.5
    rope = _rope(tag + "_rope", MLA_ROPE // 2, BF16)
    q_rope = rope(q[:, :, MLA_NOPE:].reshape(L, MLA_HEADS * MLA_ROPE), *(t * scale for t in tabs))
    k_rope = rope(kr, *tabs)[:, :MLA_ROPE]
    qf = jnp.concatenate([(q[:, :, :MLA_NOPE] * scale).astype(BF16), q_rope.reshape(L, MLA_HEADS, MLA_ROPE)], axis=-1)
    kf = jnp.concatenate([kv[:, :, :MLA_NOPE], jnp.broadcast_to(k_rope[:, None, :], (L, MLA_HEADS, MLA_ROPE))], axis=-1)
    seq_last = lambda t: t.transpose(1, 2, 0)
    unit = jnp.zeros((MLA_HEADS, MLA_XROWS, L), BF16).at[:, 0, :].set(1.0)
    poison = jnp.zeros((MLA_HEADS, MLA_XROWS, L), BF16).at[:, 0, :].set(jnp.where(jnp.arange(L) < PAD, -POISON, 0.0))
    qT = jnp.concatenate([seq_last(qf), unit], axis=1).reshape(-1, L)
    kT = jnp.concatenate([seq_last(kf), poison], axis=1).reshape(-1, L)
    oT = _mla_attention(tag + "_attn")(qT, kT, seq_last(kv[:, :, MLA_NOPE:]).reshape(-1, L))
    return _linear_km(tag + "_out")(oT, w_o)


BIG = (("fox_w_in", 2), ("fox_w_o", 1), ("swa_w_in", 2), ("swa_w_o", 1), ("mla_w_a", 1), ("mla_w_uq", 2),
       ("mla_w_ukv", 2), ("mla_w_o", 1), ("ffn_w_in", 2), ("ffn_w_out", 1))
SMALL = (("meta_tokens", 1), ("mla_g_q", 1), ("mla_g_kv", 1), ("ffn_conv_w", 2))
SHARDED = BIG + SMALL
REPL = ("ln1_g", "ln1_b", "ln2_g", "ln2_b", "fox_b_f", "swa_sinks", "ffn_conv_b")
FFN_STACKED = ("ffn_w_in", "ffn_w_out", "ffn_conv_w")
WEIGHTS = ("meta_tokens", "ln1_g", "ln1_b", "ln2_g", "ln2_b", "fox_w_in", "fox_b_f", "fox_w_o", "swa_w_in", "swa_sinks",
           "swa_w_o", "mla_w_a", "mla_g_q", "mla_g_kv", "mla_w_uq", "mla_w_ukv", "mla_w_o", "ffn_w_in", "ffn_conv_w",
           "ffn_conv_b", "ffn_w_out")


def _local_loss(stacked, repl, x, target):
    S, D = x.shape
    L = S + BLOCK
    wts = {n: _to_full(stacked[n], ax) for n, ax in SHARDED if n not in FFN_STACKED}
    wts.update(repl)
    half = N_DEV // 2
    h = jnp.concatenate([jnp.zeros((PAD, D), F32), wts["meta_tokens"], x], axis=0)
    cos_p, sin_p = _rope_tables(L, ROPE_DIM, ROPE_THETA)
    tabs_p = _rope_lanes(cos_p, sin_p, SWA_HD)
    cos_m, sin_m = _rope_tables(L, MLA_ROPE, MLA_ROPE_THETA)
    tabs_m = _rope_lanes(cos_m, sin_m, MLA_ROPE)
    for i in range(DEPTH):
        kind, j = i % 3, i // 3
        tag = f"l{i}"
        if kind == 0:
            mix = _fox_mixer(h, wts["fox_w_in"][j], wts["fox_b_f"][j], wts["fox_w_o"][j], tag + "_fox")
        elif kind == 1:
            mix = _swa_mixer(h, wts["swa_w_in"][j], wts["swa_sinks"][j], wts["swa_w_o"][j], tabs_p, tag + "_swa")
        else:
            mix = _mla_mixer(h, wts["mla_w_a"][j], wts["mla_g_q"][j], wts["mla_g_kv"][j], wts["mla_w_uq"][j],
                             wts["mla_w_ukv"][j], wts["mla_w_o"][j], tabs_m, tag + "_mla")
        h = _deepnorm(tag + "_ln1")(h, mix, wts["ln1_g"][i][None, :], wts["ln1_b"][i][None, :])
        w3 = stacked["ffn_w_in"][:, i]
        shard = w3.shape[-1]
        cw4 = stacked["ffn_conv_w"][:, i].reshape(2, half, CONV_W, shard)
        cb4 = wts["ffn_conv_b"][i].reshape(2, half, 1, shard)
        wo3 = stacked["ffn_w_out"][:, i].reshape(half, shard, D)
        act3 = _ffn_up(tag + "_ffn")(h, w3, cw4, cb4)
        ffn = _ffn_down(tag + "_ffn_down")(act3, wo3)
        h = _deepnorm(tag + "_ln2")(h, ffn, wts["ln2_g"][i][None, :], wts["ln2_b"][i][None, :])
    return _loss_head("loss_head")(h, target)


def _step(x, target, w, m, v):
    names = [n for n, _ in SHARDED]
    big = {n for n, _ in BIG}
    stacked = dict(zip(names, _gather([w[n].astype(BF16) if n in big else w[n] for n in names], "gather_weights")))
    repl = {n: w[n] for n in REPL}
    loss, (g_st, g_repl, grad_x) = jax.value_and_grad(_local_loss, argnums=(0, 1, 2))(stacked, repl, x, target)
    got = dict(zip(names, _scatter([g_st[n] for n in names], names)))
    got.update(zip(REPL, _gather([g_repl[n] for n in REPL], "gather_repl_grads")))
    out = {}
    for n in WEIGHTS:
        for kind, a in zip(("grad", "delta", "new_m", "new_v"), _adamw_call(got[n], w[n], m[n], v[n], "adamw_" + n)):
            out[(kind, n)] = a
    return loss, grad_x, out


def kernel(x, meta_tokens, ln1_g, ln1_b, ln2_g, ln2_b, fox_w_in, fox_b_f, fox_w_o, swa_w_in, swa_sinks, swa_w_o, mla_w_a, mla_g_q, mla_g_kv, mla_w_uq, mla_w_ukv, mla_w_o, ffn_w_in, ffn_conv_w, ffn_conv_b, ffn_w_out, loss_target, m_meta_tokens, m_ln1_g, m_ln1_b, m_ln2_g, m_ln2_b, m_fox_w_in, m_fox_b_f, m_fox_w_o, m_swa_w_in, m_swa_sinks, m_swa_w_o, m_mla_w_a, m_mla_g_q, m_mla_g_kv, m_mla_w_uq, m_mla_w_ukv, m_mla_w_o, m_ffn_w_in, m_ffn_conv_w, m_ffn_conv_b, m_ffn_w_out, v_meta_tokens, v_ln1_g, v_ln1_b, v_ln2_g, v_ln2_b, v_fox_w_in, v_fox_b_f, v_fox_w_o, v_swa_w_in, v_swa_sinks, v_swa_w_o, v_mla_w_a, v_mla_g_q, v_mla_g_kv, v_mla_w_uq, v_mla_w_ukv, v_mla_w_o, v_ffn_w_in, v_ffn_conv_w, v_ffn_conv_b, v_ffn_w_out):
    args = (meta_tokens, ln1_g, ln1_b, ln2_g, ln2_b, fox_w_in, fox_b_f, fox_w_o, swa_w_in, swa_sinks, swa_w_o, mla_w_a,
            mla_g_q, mla_g_kv, mla_w_uq, mla_w_ukv, mla_w_o, ffn_w_in, ffn_conv_w, ffn_conv_b, ffn_w_out)
    ms = (m_meta_tokens, m_ln1_g, m_ln1_b, m_ln2_g, m_ln2_b, m_fox_w_in, m_fox_b_f, m_fox_w_o, m_swa_w_in, m_swa_sinks,
          m_swa_w_o, m_mla_w_a, m_mla_g_q, m_mla_g_kv, m_mla_w_uq, m_mla_w_ukv, m_mla_w_o, m_ffn_w_in, m_ffn_conv_w,
          m_ffn_conv_b, m_ffn_w_out)
    vs = (v_meta_tokens, v_ln1_g, v_ln1_b, v_ln2_g, v_ln2_b, v_fox_w_in, v_fox_b_f, v_fox_w_o, v_swa_w_in, v_swa_sinks,
          v_swa_w_o, v_mla_w_a, v_mla_g_q, v_mla_g_kv, v_mla_w_uq, v_mla_w_ukv, v_mla_w_o, v_ffn_w_in, v_ffn_conv_w,
          v_ffn_conv_b, v_ffn_w_out)
    w = dict(zip(WEIGHTS, args))
    m = dict(zip(WEIGHTS, ms))
    v = dict(zip(WEIGHTS, vs))
    loss, grad_x, out = _step(x[0], loss_target[0], w, m, v)
    loss = lax.psum(loss, ("x", "y", "c"))
    res = [loss, grad_x[None]]
    for kind in ("grad", "delta", "new_m", "new_v"):
        res += [out[(kind, n)] for n in WEIGHTS]
    return tuple(res)
```

```python
import jax
import jax.numpy as jnp
import numpy as np
from jax import lax
from jax.experimental import pallas as pl
from jax.experimental.pallas import tpu as pltpu

F32 = jnp.float32
BF16 = jnp.bfloat16

DEPTH = 4
N_META = 16
BLOCK = 128
PAD = BLOCK - N_META
NEG = -1e30
ALPHA = (2.0 * DEPTH) ** 0.25
LN_EPS = 1e-5
RMS_EPS = 1e-6
FOX_HEADS, FOX_HD = 16, 64
SWA_QH, SWA_KVH, SWA_HD = 16, 2, 64
SWA_G = SWA_QH // SWA_KVH
ROPE_THETA = 500000.0
ROPE_DIM = SWA_HD // 4
MLA_HEADS, MLA_QL, MLA_KVL, MLA_NOPE, MLA_ROPE, MLA_V = 16, 384, 256, 64, 32, 64
MLA_ROPE_THETA = 10000.0
CONV_W = 3
ADAM_LR, ADAM_B1, ADAM_B2, ADAM_EPS, ADAM_WD, ADAM_STEP = 0.001, 0.9, 0.999, 1e-08, 0.01, 10

LANES = 128
SUBLANES_F32 = 8
SUBLANES_BF16 = 16
VMEM_LIMIT = 48 * 1024 * 1024
N_DEV = 8
MESH = pl.DeviceIdType.MESH
ROW_TILE = 640
ADAM_ROWS = 256
MM_TILE_M, MM_TILE_N, MM_TILE_K = 1664, 1024, 1664
ATTN_HEADS_FWD, ATTN_HEADS_BWD = 8, 4


def _tile(n, cap):
    if n <= cap:
        return n
    best = 0
    for t in range(LANES, cap + 1, LANES):
        if n % t == 0:
            best = t
    assert best, (n, cap)
    return best


def _params(*sem):
    return pltpu.CompilerParams(dimension_semantics=sem, vmem_limit_bytes=VMEM_LIMIT)


def _mm(a, b, *, la="mk", lb="kn", sa=None, sb=None, so=None, out_dtype, name):
    size, tile, parts = {}, {}, {}
    for x, lay, split in ((a, la, sa), (b, lb, sb)):
        shp = x.shape[1:] if split else x.shape
        for ax, n in zip(lay, shp):
            if ax == split:
                size[ax], tile[ax], parts[ax] = x.shape[0] * n, n, x.shape[0]
            else:
                assert size.setdefault(ax, n) == n, (name, ax, a.shape, b.shape)
    for ax, cap in (("m", MM_TILE_M), ("n", MM_TILE_N), ("k", MM_TILE_K)):
        tile.setdefault(ax, _tile(size[ax], cap))
    grid = tuple(size[ax] // tile[ax] for ax in "mnk")
    nk = grid[2]
    dn = (((la.index("k"),), (lb.index("k"),)), ((), ()))

    def kern(a_ref, b_ref, o_ref, *acc):
        p = lax.dot_general(a_ref[...].astype(BF16), b_ref[...].astype(BF16), dn, preferred_element_type=F32)
        if nk == 1:
            o_ref[...] = p.astype(o_ref.dtype)
            return
        acc_ref, = acc
        k = pl.program_id(2)

        @pl.when(k == 0)
        def _():
            acc_ref[...] = p

        @pl.when(k > 0)
        def _():
            acc_ref[...] += p

        @pl.when(k == nk - 1)
        def _():
            o_ref[...] = acc_ref[...].astype(o_ref.dtype)

    def spec(lay, split):
        blk = tuple(tile[ax] for ax in lay)
        if split is None:
            return pl.BlockSpec(blk, lambda i, j, k: tuple({"m": i, "n": j, "k": k}[ax] for ax in lay))
        return pl.BlockSpec((None,) + blk, lambda i, j, k: ({"m": i, "n": j, "k": k}[split],) + tuple(
            0 if ax == split else {"m": i, "n": j, "k": k}[ax] for ax in lay))

    if so is None:
        out_shape = (size["m"], size["n"])
    else:
        out_shape = (parts[so],) + tuple(tile[ax] if ax == so else size[ax] for ax in "mn")
    return pl.pallas_call(
        kern, name=name, grid=grid,
        in_specs=[spec(la, sa), spec(lb, sb)], out_specs=spec("mn", so),
        out_shape=jax.ShapeDtypeStruct(out_shape, out_dtype),
        scratch_shapes=[] if nk == 1 else [pltpu.VMEM((tile["m"], tile["n"]), F32)],
        compiler_params=_params("parallel", "parallel", "arbitrary"),
    )(a, b)


def _linear(name, out_dtype):
    @jax.custom_vjp
    def f(x, w):
        return _mm(x, w, out_dtype=out_dtype, name=name + "_fwd")

    def fwd(x, w):
        return f(x, w), (x, w)

    def bwd(res, g):
        x, w = res
        dx = _mm(g, w, lb="nk", out_dtype=x.dtype, name=name + "_dx")
        dw = _mm(x, g, la="km", out_dtype=w.dtype, name=name + "_dw")
        return dx, dw

    f.defvjp(fwd, bwd)
    return f


def _ln_stats(z):
    mu = jnp.mean(z, axis=-1, keepdims=True)
    zc = z - mu
    var = jnp.mean(zc * zc, axis=-1, keepdims=True)
    return zc, lax.rsqrt(var + LN_EPS)


def _ln_fwd_call(h, mix, g, b, name):
    L, D = h.shape
    tm = _tile(L, ROW_TILE)

    def kern(h_ref, m_ref, g_ref, b_ref, o_ref):
        zc, rstd = _ln_stats(ALPHA * h_ref[...] + m_ref[...])
        o_ref[...] = zc * rstd * g_ref[...] + b_ref[...]

    row = pl.BlockSpec((tm, D), lambda i: (i, 0))
    vec = pl.BlockSpec((1, D), lambda i: (0, 0))
    return pl.pallas_call(kern, name=name, grid=(L // tm,), in_specs=[row, row, vec, vec], out_specs=row,
                          out_shape=jax.ShapeDtypeStruct((L, D), F32), compiler_params=_params("parallel"))(h, mix, g, b)


def _ln_bwd_call(h, mix, g, dout, name):
    L, D = h.shape
    tm = _tile(L, ROW_TILE)

    def kern(h_ref, m_ref, g_ref, d_ref, dz_ref, dg_ref, db_ref):
        i = pl.program_id(0)
        zc, rstd = _ln_stats(ALPHA * h_ref[...] + m_ref[...])
        xhat = zc * rstd
        d = d_ref[...]
        dxh = d * g_ref[...]
        m1 = jnp.mean(dxh, axis=-1, keepdims=True)
        m2 = jnp.mean(dxh * xhat, axis=-1, keepdims=True)
        row = i * tm + lax.broadcasted_iota(jnp.int32, (tm, 1), 0)
        dz_ref[...] = jnp.where(row >= PAD, rstd * (dxh - m1 - xhat * m2), 0.0)
        pg = jnp.sum(d * xhat, axis=0, keepdims=True)
        pb = jnp.sum(d, axis=0, keepdims=True)

        @pl.when(i == 0)
        def _():
            dg_ref[...] = pg
            db_ref[...] = pb

        @pl.when(i > 0)
        def _():
            dg_ref[...] += pg
            db_ref[...] += pb

    row = pl.BlockSpec((tm, D), lambda i: (i, 0))
    vec = pl.BlockSpec((1, D), lambda i: (0, 0))
    return pl.pallas_call(
        kern, name=name, grid=(L // tm,), in_specs=[row, row, vec, row], out_specs=[row, vec, vec],
        out_shape=[jax.ShapeDtypeStruct((L, D), F32), jax.ShapeDtypeStruct((1, D), F32), jax.ShapeDtypeStruct((1, D), F32)],
        compiler_params=_params("arbitrary"))(h, mix, g, dout)


def _deepnorm(name):
    @jax.custom_vjp
    def f(h, mix, g, b):
        return _ln_fwd_call(h, mix, g, b, name + "_fwd")

    def fwd(h, mix, g, b):
        return f(h, mix, g, b), (h, mix, g)

    def bwd(res, dout):
        h, mix, g = res
        dz, dg, db = _ln_bwd_call(h, mix, g, dout, name + "_bwd")
        return ALPHA * dz, dz, dg, db

    f.defvjp(fwd, bwd)
    return f


def _rms_fwd_call(x, g, name):
    L, n = x.shape
    tm = _tile(L, ROW_TILE)

    def kern(x_ref, g_ref, o_ref):
        x = x_ref[...]
        o_ref[...] = x * lax.rsqrt(jnp.mean(x * x, axis=-1, keepdims=True) + RMS_EPS) * g_ref[...]

    row = pl.BlockSpec((tm, n), lambda i: (i, 0))
    vec = pl.BlockSpec((1, n), lambda i: (0, 0))
    return pl.pallas_call(kern, name=name, grid=(L // tm,), in_specs=[row, vec], out_specs=row,
                          out_shape=jax.ShapeDtypeStruct((L, n), F32), compiler_params=_params("parallel"))(x, g)


def _rms_bwd_call(x, g, dout, name):
    L, n = x.shape
    tm = _tile(L, ROW_TILE)

    def kern(x_ref, g_ref, d_ref, dx_ref, dg_ref):
        i = pl.program_id(0)
        x = x_ref[...]
        rstd = lax.rsqrt(jnp.mean(x * x, axis=-1, keepdims=True) + RMS_EPS)
        xhat = x * rstd
        d = d_ref[...]
        dxh = d * g_ref[...]
        dx_ref[...] = rstd * (dxh - xhat * jnp.mean(dxh * xhat, axis=-1, keepdims=True))
        pg = jnp.sum(d * xhat, axis=0, keepdims=True)

        @pl.when(i == 0)
        def _():
            dg_ref[...] = pg

        @pl.when(i > 0)
        def _():
            dg_ref[...] += pg

    row = pl.BlockSpec((tm, n), lambda i: (i, 0))
    vec = pl.BlockSpec((1, n), lambda i: (0, 0))
    return pl.pallas_call(
        kern, name=name, grid=(L // tm,), in_specs=[row, vec, row], out_specs=[row, vec],
        out_shape=[jax.ShapeDtypeStruct((L, n), F32), jax.ShapeDtypeStruct((1, n), F32)],
        compiler_params=_params("arbitrary"))(x, g, dout)


def _rmsnorm(name):
    @jax.custom_vjp
    def f(x, g):
        return _rms_fwd_call(x, g, name + "_fwd")

    def fwd(x, g):
        return f(x, g), (x, g)

    def bwd(res, dout):
        x, g = res
        dx, dg = _rms_bwd_call(x, g, dout, name + "_bwd")
        return dx, dg

    f.defvjp(fwd, bwd)
    return f


def _rope_call(x, c, s1, s2, r, out_dtype, name):
    L, W = x.shape
    reps = W // LANES
    tm = _tile(L, ROW_TILE)

    def kern(x_ref, c_ref, s1_ref, s2_ref, o_ref):
        x = x_ref[...].astype(F32)
        wide = lambda t: jnp.tile(t[...], (1, reps)) if reps > 1 else t[...]
        out = x * wide(c_ref) + pltpu.roll(x, W - r, 1) * wide(s1_ref) + pltpu.roll(x, r, 1) * wide(s2_ref)
        o_ref[...] = out.astype(o_ref.dtype)

    row = pl.BlockSpec((tm, W), lambda i: (i, 0))
    tab = pl.BlockSpec((tm, LANES), lambda i: (i, 0))
    return pl.pallas_call(kern, name=name, grid=(L // tm,), in_specs=[row, tab, tab, tab], out_specs=row,
                          out_shape=jax.ShapeDtypeStruct((L, W), out_dtype), compiler_params=_params("parallel"))(x, c, s1, s2)


def _rope(name, r, out_dtype):
    @jax.custom_vjp
    def f(x, c, s1, s2):
        return _rope_call(x, c, s1, s2, r, out_dtype, name + "_fwd")

    def fwd(x, c, s1, s2):
        return f(x, c, s1, s2), (c, s1, s2, jnp.zeros((), x.dtype))

    def bwd(res, g):
        c, s1, s2, proto = res
        dx = _rope_call(g, c, -s1, -s2, r, proto.dtype, name + "_bwd")
        return dx, jnp.zeros_like(c), jnp.zeros_like(s1), jnp.zeros_like(s2)

    f.defvjp(fwd, bwd)
    return f


def _rope_tables(L, dim, theta):
    pos = (jnp.arange(L) - PAD).astype(F32)
    inv = theta ** (-jnp.arange(0, dim, 2, dtype=F32) / dim)
    ang = pos[:, None] * inv[None, :]
    return jnp.cos(ang), jnp.sin(ang)


def _rope_lanes(cos, sin, period):
    L, half = cos.shape
    one = jnp.ones((L, period - 2 * half), F32)
    zero = jnp.zeros((L, period - 2 * half), F32)
    z_h = jnp.zeros((L, half), F32)
    c = jnp.concatenate([cos, cos, one], axis=1)
    s1 = jnp.concatenate([-sin, z_h, zero], axis=1)
    s2 = jnp.concatenate([z_h, sin, zero], axis=1)
    reps = LANES // period
    return tuple(jnp.tile(t, (1, reps)) for t in (c, s1, s2))


HALO = SUBLANES_BF16


GLU_ROWS = 128


def _strips(C):
    return [(c0, min(LANES, C - c0)) for c0 in range(0, C, LANES)]


def _shifted_down(win):
    return [pltpu.roll(win, CONV_W - 1 - t, 0)[SUBLANES_F32:, :] if t < CONV_W - 1 else win[SUBLANES_F32:, :]
            for t in range(CONV_W)]


def _conv_taps(buf_ref, g, off, rows, cols, w, b):
    u = _shifted_down(buf_ref[g, pl.ds(off - SUBLANES_F32, rows + SUBLANES_F32), cols])
    return b + u[0] * w[0:1, :] + u[1] * w[1:2, :] + u[2] * w[2:3, :]


def _glu_fwd_call(u4, cw4, cb4, name):
    _, P, L, C = u4.shape
    tm = _tile(L, ROW_TILE)
    hb = tm // HALO

    def kern(u_ref, up_ref, w_ref, b_ref, o_ref, buf):
        i = pl.program_id(1)
        row = i * tm + lax.broadcasted_iota(jnp.int32, (tm, 1), 0)
        prow = i * tm - HALO + lax.broadcasted_iota(jnp.int32, (HALO, 1), 0)
        for g in range(2):
            buf[g, pl.ds(HALO, tm), :] = jnp.where(row >= PAD, u_ref[g], 0.0)
            buf[g, pl.ds(0, HALO), :] = jnp.where(prow >= PAD, up_ref[g], 0.0)
        for c0, wd in _strips(C):
            cols = pl.ds(c0, wd)
            ws = [w_ref[g][:, c0:c0 + wd] for g in range(2)]
            bs = [b_ref[g][:, c0:c0 + wd] for g in range(2)]
            for r in range(0, tm, GLU_ROWS):
                n = min(GLU_ROWS, tm - r)
                gate, val = (_conv_taps(buf, g, HALO + r, n, cols, ws[g], bs[g]) for g in range(2))
                o_ref[pl.ds(r, n), cols] = (gate * jax.nn.sigmoid(gate) * val).astype(o_ref.dtype)

    return pl.pallas_call(
        kern, name=name, grid=(P, L // tm),
        in_specs=[pl.BlockSpec((2, None, tm, C), lambda p, i: (0, p, i, 0)),
                  pl.BlockSpec((2, None, HALO, C), lambda p, i: (0, p, jnp.maximum(i * hb - 1, 0), 0)),
                  pl.BlockSpec((2, None, CONV_W, C), lambda p, i: (0, p, 0, 0)),
                  pl.BlockSpec((2, None, 1, C), lambda p, i: (0, p, 0, 0))],
        out_specs=pl.BlockSpec((None, tm, C), lambda p, i: (p, i, 0)),
        out_shape=jax.ShapeDtypeStruct((P, L, C), BF16),
        scratch_shapes=[pltpu.VMEM((2, tm + HALO, C), F32)],
        compiler_params=_params("parallel", "parallel"))(u4, u4, cw4, cb4)


def _glu_bwd_call(u4, cw4, cb4, dact, name):
    _, P, L, C = u4.shape
    tm = _tile(L, ROW_TILE)
    hb = tm // HALO
    ext = tm + HALO
    last_halo = L // HALO - 1

    def kern(u_ref, up_ref, un_ref, w_ref, b_ref, d_ref, dn_ref, du_ref, dw_ref, db_ref, ubuf, dbuf):
        i = pl.program_id(1)
        r0 = i * tm
        mask = lambda blk, start: jnp.where(
            (start + lax.broadcasted_iota(jnp.int32, (blk.shape[0], 1), 0) >= PAD), blk, 0.0)
        for g in range(2):
            ubuf[g, pl.ds(0, HALO), :] = mask(up_ref[g], r0 - HALO)
            ubuf[g, pl.ds(HALO, tm), :] = mask(u_ref[g], r0)
            ubuf[g, pl.ds(HALO + tm, HALO), :] = un_ref[g]
        dbuf[pl.ds(0, tm), :] = d_ref[...].astype(F32)
        dbuf[pl.ds(tm, HALO), :] = jnp.where(r0 + tm < L, dn_ref[...].astype(F32), 0.0)
        for c0, wd in _strips(C):
            cols = pl.ds(c0, wd)
            ws = [w_ref[g][:, c0:c0 + wd] for g in range(2)]
            bs = [b_ref[g][:, c0:c0 + wd] for g in range(2)]
            pw = [[jnp.zeros((1, wd), F32) for _ in range(CONV_W)] for _ in range(2)]
            pb = [jnp.zeros((1, wd), F32) for _ in range(2)]
            for r in range(0, tm, GLU_ROWS):
                n = min(GLU_ROWS, tm - r)
                ne = n + SUBLANES_F32
                us = [_shifted_down(ubuf[g, pl.ds(HALO + r - SUBLANES_F32, ne + SUBLANES_F32), cols]) for g in range(2)]
                gate, val = (bs[g] + us[g][0] * ws[g][0:1, :] + us[g][1] * ws[g][1:2, :] + us[g][2] * ws[g][2:3, :]
                             for g in range(2))
                sg = jax.nn.sigmoid(gate)
                d = dbuf[pl.ds(r, ne), cols]
                dys = (d * val * (sg * (1.0 + gate * (1.0 - sg))), d * (gate * sg))
                row = r0 + r + lax.broadcasted_iota(jnp.int32, (n, 1), 0)
                for g in range(2):
                    w = ws[g]
                    dy = dys[g][:n, :]
                    du = (dy * w[2:3, :] + pltpu.roll(dys[g], ne - 1, 0)[:n, :] * w[1:2, :]
                          + pltpu.roll(dys[g], ne - 2, 0)[:n, :] * w[0:1, :])
                    du_ref[g, pl.ds(r, n), cols] = jnp.where(row >= PAD, du, 0.0).astype(du_ref.dtype)
                    for t in range(CONV_W):
                        pw[g][t] = pw[g][t] + jnp.sum(dy * us[g][t][:n, :], axis=0, keepdims=True)
                    pb[g] = pb[g] + jnp.sum(dy, axis=0, keepdims=True)
            for g in range(2):
                pwg = jnp.concatenate(pw[g], axis=0)

                @pl.when(i == 0)
                def _():
                    dw_ref[g, :, cols] = pwg
                    db_ref[g, :, cols] = pb[g]

                @pl.when(i > 0)
                def _():
                    dw_ref[g, :, cols] += pwg
                    db_ref[g, :, cols] += pb[g]

    nxt = lambda i: jnp.minimum((i + 1) * hb, last_halo)
    return pl.pallas_call(
        kern, name=name, grid=(P, L // tm),
        in_specs=[pl.BlockSpec((2, None, tm, C), lambda p, i: (0, p, i, 0)),
                  pl.BlockSpec((2, None, HALO, C), lambda p, i: (0, p, jnp.maximum(i * hb - 1, 0), 0)),
                  pl.BlockSpec((2, None, HALO, C), lambda p, i: (0, p, nxt(i), 0)),
                  pl.BlockSpec((2, None, CONV_W, C), lambda p, i: (0, p, 0, 0)),
                  pl.BlockSpec((2, None, 1, C), lambda p, i: (0, p, 0, 0)),
                  pl.BlockSpec((None, tm, C), lambda p, i: (p, i, 0)),
                  pl.BlockSpec((None, HALO, C), lambda p, i: (p, nxt(i), 0))],
        out_specs=[pl.BlockSpec((2, None, tm, C), lambda p, i: (0, p, i, 0)),
                   pl.BlockSpec((2, None, CONV_W, C), lambda p, i: (0, p, 0, 0)),
                   pl.BlockSpec((2, None, 1, C), lambda p, i: (0, p, 0, 0))],
        out_shape=[jax.ShapeDtypeStruct((2, P, L, C), BF16), jax.ShapeDtypeStruct((2, P, CONV_W, C), F32),
                   jax.ShapeDtypeStruct((2, P, 1, C), F32)],
        scratch_shapes=[pltpu.VMEM((2, tm + 2 * HALO, C), F32), pltpu.VMEM((ext, C), F32)],
        compiler_params=_params("parallel", "arbitrary"))(u4, u4, u4, cw4, cb4, dact, dact)


def _ffn_up(name):
    def run(h1, w3, cw4, cb4):
        u3 = _mm(h1, w3, sb="n", so="n", out_dtype=F32, name=name + "_up")
        u4 = u3.reshape((2, u3.shape[0] // 2) + u3.shape[1:])
        return _glu_fwd_call(u4, cw4, cb4, name + "_glu"), u4

    @jax.custom_vjp
    def f(h1, w3, cw4, cb4):
        return run(h1, w3, cw4, cb4)[0]

    def fwd(h1, w3, cw4, cb4):
        act, u4 = run(h1, w3, cw4, cb4)
        return act, (h1, w3, cw4, cb4, u4)

    def bwd(res, dact):
        h1, w3, cw4, cb4, u4 = res
        du4, dcw, dcb = _glu_bwd_call(u4, cw4, cb4, dact, name + "_glu_bwd")
        du3 = du4.reshape((du4.shape[0] * du4.shape[1],) + du4.shape[2:])
        dh1 = _mm(du3, w3, sa="k", lb="nk", sb="k", out_dtype=F32, name=name + "_up_dx")
        dw3 = _mm(h1, du3, la="km", sb="n", so="n", out_dtype=w3.dtype, name=name + "_up_dw")
        return dh1, dw3, dcw, dcb

    f.defvjp(fwd, bwd)
    return f


def _ffn_down(name):
    @jax.custom_vjp
    def f(act3, wo3):
        return _mm(act3, wo3, sa="k", sb="k", out_dtype=F32, name=name + "_fwd")

    def fwd(act3, wo3):
        return f(act3, wo3), (act3, wo3)

    def bwd(res, g):
        act3, wo3 = res
        dact = _mm(g, wo3, lb="nk", sb="n", so="n", out_dtype=act3.dtype, name=name + "_dx")
        dwo = _mm(act3, g, la="km", sa="m", so="m", out_dtype=wo3.dtype, name=name + "_dw")
        return dact, dwo

    f.defvjp(fwd, bwd)
    return f


def _pairs(n, kv_major):
    if kv_major:
        pr = [(i, j) for j in range(n) for i in range(j, n)]
    else:
        pr = [(i, j) for i in range(n) for j in range(i + 1)]
    return (jnp.asarray(np.array([p[0] for p in pr], np.int32)), jnp.asarray(np.array([p[1] for p in pr], np.int32)))


TN = (((0,), (0,)), ((), ()))
NT = (((1,), (1,)), ((), ()))


def _scores(kT, qT, i, j, tq, tk, masked):
    s = lax.dot_general(kT, qT, TN, preferred_element_type=F32)
    if not masked:
        return s, None
    col = j * tk + lax.broadcasted_iota(jnp.int32, (tk, tq), 0)
    row = i * tq + lax.broadcasted_iota(jnp.int32, (tk, tq), 1)
    mask = (col <= row) & (col >= PAD)
    return jnp.where(mask, s, NEG), mask


FOX_SCALE = 0.125
C_PARTS = 3


class _AttnLayout:
    def __init__(self, heads, q_rows, v_rows, bases=(0, 0, 0), extra=0, q_scale=None, grad_rows=None):
        self.heads, self.q_rows, self.v_rows, self.bases, self.extra, self.q_scale = heads, q_rows, v_rows, bases, extra, q_scale
        self.dk = q_rows + extra
        self.grad_rows = q_rows if grad_rows is None else grad_rows
        self.n_in = 5 if extra else 3

    def specs(self, tq, tk, hp):
        assert self.heads % hp == 0 and all(b % hp == 0 for b in self.bases)
        qb, kb, vb = (b // hp for b in self.bases)
        qmap = lambda g, t, it, jt: (qb + g, it[t])
        kmap = lambda g, t, it, jt: (kb + g, jt[t])
        vmap = lambda g, t, it, jt: (vb + g, jt[t])
        s = [pl.BlockSpec((hp * self.q_rows, tq), qmap), pl.BlockSpec((hp * self.q_rows, tk), kmap),
             pl.BlockSpec((hp * self.v_rows, tk), vmap)]
        if self.extra:
            s += [pl.BlockSpec((hp * self.extra, tq), lambda g, t, it, jt: (g, it[t])),
                  pl.BlockSpec((hp * self.extra, tk), lambda g, t, it, jt: (g, jt[t]))]
        return s

    def operands(self, refs, a):
        rows = lambda ref, n: ref[a * n:(a + 1) * n, :]
        q, k = rows(refs[0], self.q_rows), rows(refs[1], self.q_rows)
        if self.q_scale is not None:
            q = q * jnp.asarray(self.q_scale, q.dtype)
        if self.extra:
            q = jnp.concatenate([q, rows(refs[3], self.extra)], axis=0)
            k = jnp.concatenate([k, rows(refs[4], self.extra)], axis=0)
        return q, k, rows(refs[2], self.v_rows)


def _attn_fwd_call(lay, arrs, name):
    L = arrs[0].shape[1]
    H, dv, hp = lay.heads, lay.v_rows, ATTN_HEADS_FWD
    tq = tk = _tile(L, ROW_TILE)
    it, jt = _pairs(L // tq, False)

    def kern(it_ref, jt_ref, *refs):
        t = pl.program_id(1)
        i, j = it_ref[t], jt_ref[t]
        o_ref, lse_ref, m_s, l_s, acc_s = refs[lay.n_in:]

        @pl.when(j == 0)
        def _():
            m_s[...] = jnp.full_like(m_s, NEG)
            l_s[...] = jnp.zeros_like(l_s)
            acc_s[...] = jnp.zeros_like(acc_s)

        def step(masked):
            ops = [lay.operands(refs, a) for a in range(hp)]
            scores = [_scores(k, q, i, j, tq, tk, masked) for q, k, _ in ops]
            weights = []
            for a, (s, mask) in enumerate(scores):
                m_prev = m_s[a]
                m_new = jnp.maximum(m_prev, jnp.max(s, axis=0, keepdims=True))
                scale = jnp.exp(m_prev - m_new)
                p = jnp.exp(s - m_new)
                if masked:
                    p = jnp.where(mask, p, 0.0)
                l_s[a] = scale * l_s[a] + jnp.sum(p, axis=0, keepdims=True)
                m_s[a] = m_new
                weights.append((scale, p.astype(BF16)))
            for a, (scale, pb) in enumerate(weights):
                acc_s[a] = scale * acc_s[a] + jnp.dot(ops[a][2], pb, preferred_element_type=F32)

        pl.when(j == i)(lambda: step(True))
        pl.when(j != i)(lambda: step(False))

        @pl.when(j == i)
        def _():
            for a in range(hp):
                l = l_s[a]
                l = jnp.where(l == 0.0, 1.0, l)
                o_ref[a * dv:(a + 1) * dv, :] = (acc_s[a] / l).astype(o_ref.dtype)
                lse_ref[a] = m_s[a] + jnp.log(l)

    return pl.pallas_call(
        kern, name=name,
        grid_spec=pltpu.PrefetchScalarGridSpec(
            num_scalar_prefetch=2, grid=(H // hp, int(it.shape[0])), in_specs=lay.specs(tq, tk, hp),
            out_specs=[pl.BlockSpec((hp * dv, tq), lambda g, t, it, jt: (g, it[t])),
                       pl.BlockSpec((hp, 1, tq), lambda g, t, it, jt: (g, 0, it[t]))],
            scratch_shapes=[pltpu.VMEM((hp, 1, tq), F32), pltpu.VMEM((hp, 1, tq), F32), pltpu.VMEM((hp, dv, tq), F32)]),
        out_shape=[jax.ShapeDtypeStruct((H * dv, L), F32), jax.ShapeDtypeStruct((H, 1, L), F32)],
        compiler_params=_params("parallel", "arbitrary"))(it, jt, *arrs)


C_ROWS = SUBLANES_F32


def _attn_bwd_call(lay, arrs, oT, lse, doT, name):
    L = arrs[0].shape[1]
    H, dv, dk, gr, hp = lay.heads, lay.v_rows, lay.dk, lay.grad_rows, ATTN_HEADS_BWD
    tq = tk = _tile(L, ROW_TILE)
    nq = L // tq
    it, jt = _pairs(nq, True)
    npairs = int(it.shape[0])
    with_c = lay.extra > 0
    crows = pl.ds(lay.q_rows, C_ROWS)

    def kern(it_ref, jt_ref, *refs):
        t = pl.program_id(1)
        i, j = it_ref[t], jt_ref[t]
        rest = refs[lay.n_in:]
        o_ref, do_ref, lse_ref = rest[:3]
        if with_c:
            dq_ref, dk_ref, dv_ref, dqc_ref, dkc_ref, dq_s, dk_s, dv_s = rest[3:]
        else:
            dq_ref, dk_ref, dv_ref, dq_s, dk_s, dv_s = rest[3:]

        @pl.when(t == 0)
        def _():
            dq_s[...] = jnp.zeros_like(dq_s)

        @pl.when(i == j)
        def _():
            dk_s[...] = jnp.zeros_like(dk_s)
            dv_s[...] = jnp.zeros_like(dv_s)

        def step(masked):
            first = []
            for a in range(hp):
                q, k, v = lay.operands(refs, a)
                s, mask = _scores(k, q, i, j, tq, tk, masked)
                dof = do_ref[a * dv:(a + 1) * dv, :]
                dob = dof.astype(BF16)
                delta = jnp.sum(dof * o_ref[a * dv:(a + 1) * dv, :], axis=0, keepdims=True)
                dp = lax.dot_general(v, dob, TN, preferred_element_type=F32)
                first.append((q, k, s, mask, dob, delta, dp))
            second = []
            for a, (q, k, s, mask, dob, delta, dp) in enumerate(first):
                p = jnp.exp(s - lse_ref[a])
                if masked:
                    p = jnp.where(mask, p, 0.0)
                second.append((p.astype(BF16), (p * (dp - delta)).astype(BF16)))
            cols = pl.ds(pl.multiple_of(i * tq, tq), tq)
            for a, (pb, dsb) in enumerate(second):
                q, k, _, _, dob, _, _ = first[a]
                dv_s[a] += lax.dot_general(dob, pb, NT, preferred_element_type=F32)
                dk_s[a] += lax.dot_general(q, dsb, NT, preferred_element_type=F32)
                dq_s[a, :, cols] += jnp.dot(k, dsb, preferred_element_type=F32)

        pl.when(j == i)(lambda: step(True))
        pl.when(j != i)(lambda: step(False))

        @pl.when(i == nq - 1)
        def _():
            for a in range(hp):
                dk_ref[a * gr:(a + 1) * gr, :] = dk_s[a, :gr, :].astype(dk_ref.dtype)
                dv_ref[a * dv:(a + 1) * dv, :] = dv_s[a].astype(dv_ref.dtype)
                if with_c:
                    dkc_ref[a * C_ROWS:(a + 1) * C_ROWS, :] = dk_s[a, crows, :]

        @pl.when(t == npairs - 1)
        def _():
            for a in range(hp):
                dq = dq_s[a, :gr, :]
                dq_ref[a * gr:(a + 1) * gr, :] = (dq if lay.q_scale is None else dq * lay.q_scale).astype(dq_ref.dtype)
                if with_c:
                    dqc_ref[a * C_ROWS:(a + 1) * C_ROWS, :] = dq_s[a, crows, :]

    qcol = lambda g, t, it, jt: (g, it[t])
    kcol = lambda g, t, it, jt: (g, jt[t])
    whole = lambda g, t, it, jt: (g, 0)
    out_specs = [pl.BlockSpec((hp * gr, L), whole), pl.BlockSpec((hp * gr, tk), kcol), pl.BlockSpec((hp * dv, tk), kcol)]
    out_shape = [jax.ShapeDtypeStruct((H * gr, L), BF16), jax.ShapeDtypeStruct((H * gr, L), BF16),
                 jax.ShapeDtypeStruct((H * dv, L), BF16)]
    if with_c:
        out_specs += [pl.BlockSpec((hp * C_ROWS, L), whole), pl.BlockSpec((hp * C_ROWS, tk), kcol)]
        out_shape += [jax.ShapeDtypeStruct((H * C_ROWS, L), F32)] * 2
    return pl.pallas_call(
        kern, name=name,
        grid_spec=pltpu.PrefetchScalarGridSpec(
            num_scalar_prefetch=2, grid=(H // hp, npairs),
            in_specs=lay.specs(tq, tk, hp) + [pl.BlockSpec((hp * dv, tq), qcol), pl.BlockSpec((hp * dv, tq), qcol),
                                          pl.BlockSpec((hp, 1, tq), lambda g, t, it, jt: (g, 0, it[t]))],
            out_specs=out_specs,
            scratch_shapes=[pltpu.VMEM((hp, dk, L), F32), pltpu.VMEM((hp, dk, tk), F32), pltpu.VMEM((hp, dv, tk), F32)]),
        out_shape=out_shape, compiler_params=_params("parallel", "arbitrary"))(it, jt, *arrs, oT, doT, lse)


POISON = 1e30


def _fox_extras(cT):
    H, L = cT.shape
    to_bf16 = lambda t: lax.reduce_precision(t, exponent_bits=8, mantissa_bits=7)
    hi = to_bf16(cT)
    lo = to_bf16(cT - hi)
    ll = to_bf16(cT - hi - lo)
    terms = jnp.stack([hi, lo, ll], axis=1).astype(BF16)
    ones = jnp.ones_like(terms)
    fill = jnp.zeros((H, SUBLANES_BF16 - 2 * C_PARTS, L), BF16)
    pad_key = (jnp.arange(L) < PAD)[None, None, :]
    kterms = jnp.where(pad_key, jnp.asarray([POISON, 0.0, 0.0], BF16)[None, :, None], terms)
    eq = jnp.concatenate([terms, ones, fill], axis=1)
    ek = jnp.concatenate([ones, -kterms, fill], axis=1)
    return eq.reshape(H * SUBLANES_BF16, L), ek.reshape(H * SUBLANES_BF16, L)


def _fox_attention(name):
    assert FOX_SCALE == FOX_HD ** -0.5
    lay = _AttnLayout(FOX_HEADS, FOX_HD, FOX_HD, bases=(0, FOX_HEADS, 2 * FOX_HEADS), extra=SUBLANES_BF16,
                      q_scale=FOX_SCALE)

    def run(qkvT, cT):
        arrs = (qkvT, qkvT, qkvT) + _fox_extras(cT)
        oT, lse = _attn_fwd_call(lay, arrs, name + "_fwd")
        return oT, (arrs, oT, lse)

    @jax.custom_vjp
    def f(qkvT, cT):
        return run(qkvT, cT)[0]

    def fwd(qkvT, cT):
        return run(qkvT, cT)

    def bwd(res, doT):
        arrs, oT, lse = res
        dq, dk, dv, dqc, dkc = _attn_bwd_call(lay, arrs, oT, lse, doT, name + "_bwd")
        L = dq.shape[1]
        dc = dqc.reshape(FOX_HEADS, C_ROWS, L)[:, 0, :] - dkc.reshape(FOX_HEADS, C_ROWS, L)[:, C_PARTS, :]
        return jnp.concatenate([dq, dk, dv], axis=0), dc

    f.defvjp(fwd, bwd)
    return f


MLA_XROWS = SUBLANES_BF16


def _mla_attention(name):
    rows = MLA_NOPE + MLA_ROPE + MLA_XROWS
    lay = _AttnLayout(MLA_HEADS, rows, MLA_V, grad_rows=MLA_NOPE + MLA_ROPE)

    @jax.custom_vjp
    def f(qT, kT, vT):
        return _attn_fwd_call(lay, (qT, kT, vT), name + "_fwd")[0]

    def fwd(qT, kT, vT):
        oT, lse = _attn_fwd_call(lay, (qT, kT, vT), name + "_fwd")
        return oT, (qT, kT, vT, oT, lse)

    def bwd(res, doT):
        qT, kT, vT, oT, lse = res
        dq, dk, dv = _attn_bwd_call(lay, (qT, kT, vT), oT, lse, doT, name + "_bwd")
        L = dq.shape[1]
        widen = lambda t: jnp.pad(t.reshape(MLA_HEADS, -1, L), ((0, 0), (0, MLA_XROWS), (0, 0))).reshape(-1, L)
        return widen(dq), widen(dk), dv

    f.defvjp(fwd, bwd)
    return f


def _linear_t(name, out_dtype):
    @jax.custom_vjp
    def f(wT, x):
        return _mm(wT, x, lb="nk", out_dtype=out_dtype, name=name + "_fwd")

    def fwd(wT, x):
        return f(wT, x), (wT, x)

    def bwd(res, g):
        wT, x = res
        dwT = _mm(g, x, out_dtype=wT.dtype, name=name + "_dw")
        dx = _mm(g, wT, la="km", out_dtype=x.dtype, name=name + "_dx")
        return dwT, dx

    f.defvjp(fwd, bwd)
    return f


def _linear_km(name):
    @jax.custom_vjp
    def f(aT, w):
        return _mm(aT, w, la="km", out_dtype=F32, name=name + "_fwd")

    def fwd(aT, w):
        return f(aT, w), (aT, w)

    def bwd(res, g):
        aT, w = res
        daT = _mm(w, g, lb="nk", out_dtype=aT.dtype, name=name + "_dx")
        dw = _mm(aT, g, out_dtype=w.dtype, name=name + "_dw")
        return daT, dw

    f.defvjp(fwd, bwd)
    return f


def _swa_parts(q_ref, kc_ref, kp_ref, km_ref, sink_ref, i, scale):
    R = SWA_G * BLOCK
    q = q_ref[...].reshape(R, SWA_HD)
    nt = (((1,), (1,)), ((), ()))
    r = lax.broadcasted_iota(jnp.int32, (R, BLOCK), 0) & (BLOCK - 1)
    c = lax.broadcasted_iota(jnp.int32, (R, BLOCK), 1)
    s_c = jnp.where((c <= r) & (i >= 1), lax.dot_general(q, kc_ref[...], nt, preferred_element_type=F32) * scale, NEG)
    s_p = jnp.where((c > r) & (i >= 2), lax.dot_general(q, kp_ref[...], nt, preferred_element_type=F32) * scale, NEG)
    s_m = jnp.where((c >= PAD) & ((i >= 1) | (c <= r)),
                    lax.dot_general(q, km_ref[...], nt, preferred_element_type=F32) * scale, NEG)
    sink = jnp.broadcast_to(sink_ref[...], (SWA_G, BLOCK, 1)).reshape(R, 1)
    return q, s_c, s_p, s_m, sink


def _swa_specs(L):
    nb = L // BLOCK
    qs = pl.BlockSpec((None, SWA_G, BLOCK, SWA_HD), lambda g, i: (g, 0, i, 0))
    kc = pl.BlockSpec((None, BLOCK, SWA_HD), lambda g, i: (g, i, 0))
    kp = pl.BlockSpec((None, BLOCK, SWA_HD), lambda g, i: (g, jnp.maximum(i - 1, 0), 0))
    km = pl.BlockSpec((None, BLOCK, SWA_HD), lambda g, i: (g, 0, 0))
    sk = pl.BlockSpec((None, SWA_G, 1, 1), lambda g, i: (g, 0, 0, 0))
    ls = pl.BlockSpec((None, SWA_G, BLOCK, 1), lambda g, i: (g, 0, i, 0))
    return nb, qs, kc, kp, km, sk, ls


def _swa_fwd_call(q, k, v, sinks, scale, name):
    _, _, L, _ = q.shape
    nb, qs, kc, kp, km, sk, ls = _swa_specs(L)

    def kern(q_ref, kc_ref, kp_ref, km_ref, vc_ref, vp_ref, vm_ref, sink_ref, o_ref, lse_ref):
        i = pl.program_id(1)
        _, s_c, s_p, s_m, sink = _swa_parts(q_ref, kc_ref, kp_ref, km_ref, sink_ref, i, scale)
        mx = lambda s: jnp.max(s, axis=-1, keepdims=True)
        m = jnp.maximum(jnp.maximum(mx(s_c), mx(s_p)), jnp.maximum(mx(s_m), sink))
        p_c, p_p, p_m = jnp.exp(s_c - m), jnp.exp(s_p - m), jnp.exp(s_m - m)
        sm = lambda p: jnp.sum(p, axis=-1, keepdims=True)
        den = sm(p_c) + sm(p_p) + sm(p_m) + jnp.exp(sink - m)
        inv = 1.0 / den
        pv = lambda p, v_ref: jnp.dot((p * inv).astype(BF16), v_ref[...], preferred_element_type=F32)
        o = pv(p_c, vc_ref) + pv(p_p, vp_ref) + pv(p_m, vm_ref)
        o_ref[...] = o.reshape(SWA_G, BLOCK, SWA_HD)
        lse_ref[...] = (m + jnp.log(den)).reshape(SWA_G, BLOCK, 1)

    return pl.pallas_call(
        kern, name=name, grid=(SWA_KVH, nb), in_specs=[qs, kc, kp, km, kc, kp, km, sk], out_specs=[qs, ls],
        out_shape=[jax.ShapeDtypeStruct((SWA_KVH, SWA_G, L, SWA_HD), F32), jax.ShapeDtypeStruct((SWA_KVH, SWA_G, L, 1), F32)],
        compiler_params=_params("parallel", "parallel"))(q, k, k, k, v, v, v, sinks)


def _swa_bwd_call(q, k, v, sinks, o, lse, do, scale, name):
    _, _, L, _ = q.shape
    nb, qs, kc, kp, km, sk, ls = _swa_specs(L)
    R = SWA_G * BLOCK
    full = pl.BlockSpec((None, L, SWA_HD), lambda g, i: (g, 0, 0))

    def kern(q_ref, kc_ref, kp_ref, km_ref, vc_ref, vp_ref, vm_ref, sink_ref, o_ref, do_ref, lse_ref,
             dq_ref, dk_ref, dv_ref, dsink_ref):
        i = pl.program_id(1)

        @pl.when(i == 0)
        def _():
            dk_ref[...] = jnp.zeros_like(dk_ref)
            dv_ref[...] = jnp.zeros_like(dv_ref)
            dsink_ref[...] = jnp.zeros_like(dsink_ref)

        q, s_c, s_p, s_m, sink = _swa_parts(q_ref, kc_ref, kp_ref, km_ref, sink_ref, i, scale)
        lse = lse_ref[...].reshape(R, 1)
        dof = do_ref[...].reshape(R, SWA_HD)
        dob = dof.astype(BF16)
        delta = jnp.sum(dof * o_ref[...].reshape(R, SWA_HD), axis=-1, keepdims=True)
        nt = (((1,), (1,)), ((), ()))
        tn = (((0,), (0,)), ((), ()))
        cur = pl.ds(pl.multiple_of(i * BLOCK, BLOCK), BLOCK)
        prev = pl.ds(pl.multiple_of(jnp.maximum(i - 1, 0) * BLOCK, BLOCK), BLOCK)
        meta = pl.ds(0, BLOCK)
        dq = jnp.zeros((R, SWA_HD), F32)
        for s, k_ref, v_ref, rows in ((s_c, kc_ref, vc_ref, cur), (s_p, kp_ref, vp_ref, prev), (s_m, km_ref, vm_ref, meta)):
            p = jnp.exp(s - lse)
            dp = lax.dot_general(dob, v_ref[...], nt, preferred_element_type=F32)
            ds = (p * (dp - delta)).astype(BF16)
            dq = dq + jnp.dot(ds, k_ref[...], preferred_element_type=F32)
            dv_ref[rows, :] += lax.dot_general(p.astype(BF16), dob, tn, preferred_element_type=F32)
            dk_ref[rows, :] += lax.dot_general(ds, q, tn, preferred_element_type=F32) * scale
        dq_ref[...] = (dq * scale).reshape(SWA_G, BLOCK, SWA_HD).astype(dq_ref.dtype)
        dsk = -jnp.exp(sink - lse) * delta
        dsink_ref[...] += jnp.sum(dsk.reshape(SWA_G, BLOCK, 1), axis=1, keepdims=True)

    return pl.pallas_call(
        kern, name=name, grid=(SWA_KVH, nb), in_specs=[qs, kc, kp, km, kc, kp, km, sk, qs, qs, ls],
        out_specs=[qs, full, full, sk],
        out_shape=[jax.ShapeDtypeStruct((SWA_KVH, SWA_G, L, SWA_HD), BF16), jax.ShapeDtypeStruct((SWA_KVH, L, SWA_HD), F32),
                   jax.ShapeDtypeStruct((SWA_KVH, L, SWA_HD), F32), jax.ShapeDtypeStruct((SWA_KVH, SWA_G, 1, 1), F32)],
        compiler_params=_params("parallel", "arbitrary"))(q, k, k, k, v, v, v, sinks, o, do, lse)


def _swa_attn(name, scale):
    @jax.custom_vjp
    def f(q, k, v, sinks):
        return _swa_fwd_call(q, k, v, sinks, scale, name + "_fwd")[0]

    def fwd(q, k, v, sinks):
        o, lse = _swa_fwd_call(q, k, v, sinks, scale, name + "_fwd")
        return o, (q, k, v, sinks, o, lse)

    def bwd(res, do):
        q, k, v, sinks, o, lse = res
        dq, dk, dv, dsink = _swa_bwd_call(q, k, v, sinks, o, lse, do, scale, name + "_bwd")
        return dq, dk.astype(BF16), dv.astype(BF16), dsink

    f.defvjp(fwd, bwd)
    return f


def _scan_call(x, bias, mul, pre_logsig, name):
    L, W = x.shape
    tb = _tile(L, ROW_TILE)
    has_mul = mul is not None

    def kern(x_ref, b_ref, *rest):
        if has_mul:
            m_ref, o_ref, tot_ref, carry = rest
        else:
            o_ref, tot_ref, carry = rest
        i = pl.program_id(0)

        @pl.when(i == 0)
        def _():
            carry[...] = jnp.zeros_like(carry)
            tot_ref[...] = jnp.zeros_like(tot_ref)

        z = x_ref[...] + b_ref[...]
        if pre_logsig:
            z = jnp.minimum(z, 0.0) - jnp.log(1.0 + jnp.exp(-jnp.abs(z)))
        row = lax.broadcasted_iota(jnp.int32, (tb, W), 0)
        s = 1
        while s < tb:
            z = z + jnp.where(row >= s, pltpu.roll(z, s, 0), 0.0)
            s *= 2
        z = z + carry[...]
        carry[...] = z[tb - 1:tb, :]
        if has_mul:
            z = z * m_ref[...]
        o_ref[...] = z
        tot_ref[...] += jnp.sum(z, axis=0, keepdims=True)

    row = pl.BlockSpec((tb, W), lambda i: (i, 0))
    vec = pl.BlockSpec((1, W), lambda i: (0, 0))
    return pl.pallas_call(
        kern, name=name, grid=(L // tb,), in_specs=[row, vec] + ([row] if has_mul else []), out_specs=[row, vec],
        out_shape=[jax.ShapeDtypeStruct((L, W), F32), jax.ShapeDtypeStruct((1, W), F32)],
        scratch_shapes=[pltpu.VMEM((1, W), F32)],
        compiler_params=_params("arbitrary"))(*([x, bias] + ([mul] if has_mul else [])))


def _sigmoid_neg_call(x, bias, name):
    L, W = x.shape
    tb = _tile(L, ROW_TILE)

    def kern(x_ref, b_ref, o_ref):
        row = pl.program_id(0) * tb + lax.broadcasted_iota(jnp.int32, (tb, 1), 0)
        o_ref[...] = jnp.where(row >= PAD, jax.nn.sigmoid(-(x_ref[...] + b_ref[...])), 0.0)

    row = pl.BlockSpec((tb, W), lambda i: (i, 0))
    vec = pl.BlockSpec((1, W), lambda i: (0, 0))
    return pl.pallas_call(kern, name=name, grid=(L // tb,), in_specs=[row, vec], out_specs=row,
                          out_shape=jax.ShapeDtypeStruct((L, W), F32), compiler_params=_params("parallel"))(x, bias)


def _decay(name):
    @jax.custom_vjp
    def f(fg, b):
        return _scan_call(fg, b, None, True, name + "_fwd")[0]

    def fwd(fg, b):
        return f(fg, b), (fg, b)

    def bwd(res, dc):
        fg, b = res
        sg = _sigmoid_neg_call(fg, b, name + "_dsig")
        dfg_rev, db = _scan_call(dc[::-1], jnp.zeros_like(b), sg[::-1], False, name + "_bwd")
        return dfg_rev[::-1], db

    f.defvjp(fwd, bwd)
    return f


def _loss_call(hf, target, name):
    L, D = hf.shape
    nb = L // BLOCK

    def kern(h_ref, t_ref, loss_ref, dy_ref, acc):
        i = pl.program_id(0)

        @pl.when(i == 0)
        def _():
            acc[...] = jnp.zeros_like(acc)
            dy_ref[...] = jnp.zeros_like(dy_ref)

        @pl.when(i > 0)
        def _():
            e = h_ref[...] - t_ref[...]
            dy_ref[...] = e * (1.0 / D)
            acc[...] += jnp.sum(e * e, axis=0, keepdims=True)

        @pl.when(i == nb - 1)
        def _():
            loss_ref[...] = jnp.broadcast_to(jnp.sum(acc[...], axis=1, keepdims=True) * (0.5 / D), loss_ref.shape)

    return pl.pallas_call(
        kern, name=name, grid=(nb,),
        in_specs=[pl.BlockSpec((BLOCK, D), lambda i: (i, 0)), pl.BlockSpec((BLOCK, D), lambda i: (jnp.maximum(i - 1, 0), 0))],
        out_specs=[pl.BlockSpec((1, LANES), lambda i: (0, 0)), pl.BlockSpec((BLOCK, D), lambda i: (i, 0))],
        out_shape=[jax.ShapeDtypeStruct((1, LANES), F32), jax.ShapeDtypeStruct((L, D), F32)],
        scratch_shapes=[pltpu.VMEM((1, D), F32)],
        compiler_params=_params("arbitrary"))(hf, target)


def _loss_head(name):
    @jax.custom_vjp
    def f(hf, target):
        return _loss_call(hf, target, name)[0][0, 0]

    def fwd(hf, target):
        loss, dy = _loss_call(hf, target, name)
        return loss[0, 0], (dy, jnp.zeros((), F32))

    def bwd(res, g):
        dy, _ = res
        return dy * g, None

    f.defvjp(fwd, bwd)
    return f


N_CHIPS = N_DEV // 2


def _sibling_swap(arrs, name):
    n = len(arrs)

    def body(*refs):
        srcs, outs = refs[:n], refs[n:2 * n]
        send_sems, recv_sems = refs[2 * n:]
        x, y, c = lax.axis_index("x"), lax.axis_index("y"), lax.axis_index("c")
        copies = [[pltpu.make_async_remote_copy(
            src_ref=srcs[a].at[2 * q + (1 - c)], dst_ref=outs[a].at[q], send_sem=send_sems.at[a, q],
            recv_sem=recv_sems.at[a, q], device_id=(x, y, 1 - c), device_id_type=MESH) for q in range(N_CHIPS)]
            for a in range(n)]
        for row in copies:
            for cp in row:
                cp.start()
        for row in copies:
            for cp in row:
                cp.wait()

    hbm = pl.BlockSpec(memory_space=pl.ANY)
    return pl.pallas_call(
        body, name=name, out_shape=[jax.ShapeDtypeStruct((N_CHIPS,) + tuple(a.shape[1:]), a.dtype) for a in arrs],
        in_specs=[hbm] * n, out_specs=[hbm] * n,
        scratch_shapes=[pltpu.SemaphoreType.DMA((n, N_CHIPS)), pltpu.SemaphoreType.DMA((n, N_CHIPS))],
    )(*arrs)


def _pair_sum_call(a, b, name):
    shape = a.shape
    cols = shape[-1]
    rows = int(np.prod(shape[:-1]))
    tr = rows
    if rows > 2 * ADAM_ROWS:
        tr = max(t for t in range(SUBLANES_BF16, 2 * ADAM_ROWS + 1, SUBLANES_BF16) if rows % t == 0)

    def kern(a_ref, b_ref, o_ref):
        o_ref[...] = (a_ref[...].astype(F32) + b_ref[...].astype(F32)).astype(o_ref.dtype)

    slab = pl.BlockSpec((tr, cols), lambda i: (i, 0))
    return pl.pallas_call(kern, name=name, grid=(rows // tr,), in_specs=[slab, slab], out_specs=slab,
                          out_shape=jax.ShapeDtypeStruct((rows, cols), a.dtype),
                          compiler_params=_params("parallel"))(a.reshape(rows, cols), b.reshape(rows, cols)).reshape(shape)


def _chip_exchange(arrs, name):
    n = len(arrs)

    def body(*refs):
        srcs, outs = refs[:n], refs[n:2 * n]
        send_sems, recv_sems, local_sems = refs[2 * n:]
        x, y, c = lax.axis_index("x"), lax.axis_index("y"), lax.axis_index("c")
        mine = 2 * x + y
        chips = [(1 - x, y), (x, 1 - y), (1 - x, 1 - y)]
        started = []
        for a in range(n):
            own = pltpu.make_async_copy(srcs[a].at[mine], outs[a].at[mine], local_sems.at[a])
            own.start()
            started.append(own.wait)
        for k, (px, py) in enumerate(chips):
            for a in range(n):
                cp = pltpu.make_async_remote_copy(src_ref=srcs[a].at[2 * px + py], dst_ref=outs[a].at[mine],
                                                  send_sem=send_sems.at[a, k], recv_sem=recv_sems.at[a, k],
                                                  device_id=(px, py, c), device_id_type=MESH)
                cp.start()
                started.append(cp.wait_send)
        for k, (px, py) in enumerate(chips):
            for a in range(n):
                pltpu.make_async_remote_copy(src_ref=srcs[a].at[mine], dst_ref=outs[a].at[2 * px + py],
                                             send_sem=send_sems.at[a, k], recv_sem=recv_sems.at[a, k],
                                             device_id=(px, py, c), device_id_type=MESH).wait_recv()
        for wait in started:
            wait()

    hbm = pl.BlockSpec(memory_space=pl.ANY)
    return pl.pallas_call(
        body, name=name, out_shape=[jax.ShapeDtypeStruct(a.shape, a.dtype) for a in arrs],
        in_specs=[hbm] * n, out_specs=[hbm] * n,
        scratch_shapes=[pltpu.SemaphoreType.DMA((n, N_CHIPS - 1)), pltpu.SemaphoreType.DMA((n, N_CHIPS - 1)),
                        pltpu.SemaphoreType.DMA((n,))],
    )(*arrs)


def _scatter(arrs, names):
    c = lax.axis_index("c")
    theirs = _sibling_swap(arrs, "scatter_swap")
    halves = [lax.dynamic_index_in_dim(a.reshape((N_CHIPS, 2) + a.shape[1:]), c, axis=1, keepdims=False) for a in arrs]
    sums = [_pair_sum_call(h, t, "pair_sum_" + nm) for h, t, nm in zip(halves, theirs, names)]
    return _chip_exchange(sums, "scatter_chips")


def _gather(arrs, name):
    n = len(arrs)
    hops = N_DEV - 1

    def body(*refs):
        srcs, outs = refs[:n], refs[n:2 * n]
        send_sems, recv_sems, local_sems = refs[2 * n:]
        x, y, c = lax.axis_index("x"), lax.axis_index("y"), lax.axis_index("c")
        ident = lambda px, py, pc: 4 * px + 2 * py + pc
        me, sibling = ident(x, y, c), (x, y, 1 - c)
        chips = [(1 - x, y), (x, 1 - y), (1 - x, 1 - y)]

        def copy(a, k, block, to, src=None):
            return pltpu.make_async_remote_copy(
                src_ref=outs[a].at[block] if src is None else src, dst_ref=outs[a].at[block],
                send_sem=send_sems.at[a, k], recv_sem=recv_sems.at[a, k], device_id=to, device_id_type=MESH)

        started = []
        for a in range(n):
            mine = pltpu.make_async_copy(srcs[a], outs[a].at[me], local_sems.at[a])
            mine.start()
            started.append(mine.wait)
            for k, to in enumerate([sibling] + [(px, py, c) for px, py in chips]):
                cp = copy(a, k, me, to, src=srcs[a])
                cp.start()
                started.append(cp.wait_send)
        for j, (px, py) in enumerate(chips):
            for a in range(n):
                copy(a, 1 + j, ident(px, py, c), (x, y, c)).wait_recv()
                cp = copy(a, 4 + j, ident(px, py, c), sibling)
                cp.start()
                started.append(cp.wait_send)
        for a in range(n):
            copy(a, 0, ident(x, y, 1 - c), (x, y, c)).wait_recv()
            for j, (px, py) in enumerate(chips):
                copy(a, 4 + j, ident(px, py, 1 - c), (x, y, c)).wait_recv()
        for wait in started:
            wait()

    hbm = pl.BlockSpec(memory_space=pl.ANY)
    return pl.pallas_call(
        body, name=name, out_shape=[jax.ShapeDtypeStruct((N_DEV,) + tuple(a.shape), a.dtype) for a in arrs],
        in_specs=[hbm] * n, out_specs=[hbm] * n,
        scratch_shapes=[pltpu.SemaphoreType.DMA((n, hops)), pltpu.SemaphoreType.DMA((n, hops)),
                        pltpu.SemaphoreType.DMA((n,))],
    )(*arrs)


def _adamw_call(parts, w, m, v, name):
    shape = w.shape
    nparts = parts.shape[0]
    cols = shape[-1]
    rows = int(np.prod(shape[:-1]))
    tr = rows
    if rows > ADAM_ROWS:
        tr = max(t for t in range(SUBLANES_F32, ADAM_ROWS + 1, SUBLANES_F32) if rows % t == 0)
    c1 = 1.0 / (1.0 - ADAM_B1 ** ADAM_STEP)
    c2 = 1.0 / (1.0 - ADAM_B2 ** ADAM_STEP)

    def kern(p_ref, w_ref, m_ref, v_ref, g_ref, d_ref, nm_ref, nv_ref):
        g = p_ref[0].astype(F32)
        for p in range(1, nparts):
            g = g + p_ref[p].astype(F32)
        nm = ADAM_B1 * m_ref[...] + (1.0 - ADAM_B1) * g
        nv = ADAM_B2 * v_ref[...] + (1.0 - ADAM_B2) * (g * g)
        g_ref[...] = g
        nm_ref[...] = nm
        nv_ref[...] = nv
        d_ref[...] = -ADAM_LR * ((nm * c1) / (jnp.sqrt(nv * c2) + ADAM_EPS) + ADAM_WD * w_ref[...])

    slab = pl.BlockSpec((tr, cols), lambda i: (i, 0))
    flat = lambda t: t.reshape(rows, cols)
    res = pl.pallas_call(
        kern, name=name, grid=(rows // tr,),
        in_specs=[pl.BlockSpec((nparts, tr, cols), lambda i: (0, i, 0)), slab, slab, slab], out_specs=[slab] * 4,
        out_shape=[jax.ShapeDtypeStruct((rows, cols), F32)] * 4,
        compiler_params=_params("parallel"))(parts.reshape(nparts, rows, cols), flat(w), flat(m), flat(v))
    return [r.reshape(shape) for r in res]


def _to_full(stacked, axis):
    moved = jnp.moveaxis(stacked, 0, axis)
    s = list(moved.shape)
    return moved.reshape(s[:axis] + [s[axis] * s[axis + 1]] + s[axis + 2:])


def _heads(x, h, d):
    return x.reshape(x.shape[0], h, d).transpose(1, 0, 2)


def _unheads(x):
    return x.transpose(1, 0, 2).reshape(x.shape[1], -1)


def _pad_cols(x, width):
    return jnp.pad(x, ((0, 0), (0, width - x.shape[1])))


def _fox_mixer(h, w_in, b_f, w_o, tag):
    hd = FOX_HEADS * FOX_HD
    qkvT = _linear_t(tag + "_qkv", BF16)(w_in[:, :3 * hd].T, h)
    fg = _linear(tag + "_gate", F32)(h, _pad_cols(w_in[:, 3 * hd:], LANES))
    c = _decay(tag + "_decay")(fg, _pad_cols(b_f[None, :], LANES))
    oT = _fox_attention(tag + "_attn")(qkvT, c[:, :FOX_HEADS].T)
    return _linear_km(tag + "_out")(oT, w_o)


def _swa_mixer(h, w_in, sinks, w_o, tabs, tag):
    L = h.shape[0]
    qd, kd = SWA_QH * SWA_HD, SWA_KVH * SWA_HD
    proj = _linear(tag + "_qkv", F32)(h, w_in)
    rope = _rope(tag + "_rope", ROPE_DIM // 2, BF16)
    q = rope(proj[:, :qd], *tabs)
    k = rope(proj[:, qd:qd + kd], *tabs)
    v = proj[:, qd + kd:].astype(BF16)
    qg = _heads(q, SWA_QH, SWA_HD).reshape(SWA_KVH, SWA_G, L, SWA_HD)
    o = _swa_attn(tag + "_attn", SWA_HD ** -0.5)(qg, _heads(k, SWA_KVH, SWA_HD), _heads(v, SWA_KVH, SWA_HD),
                                                 sinks.reshape(SWA_KVH, SWA_G, 1, 1))
    return _linear(tag + "_out", F32)(_unheads(o.reshape(SWA_QH, L, SWA_HD)), w_o)


def _mla_mixer(h, w_a, g_q, g_kv, w_uq, w_ukv, w_o, tabs, tag):
    L = h.shape[0]
    cq = _linear(tag + "_aq", F32)(h, w_a[:, :MLA_QL])
    ckv = _linear(tag + "_akv", F32)(h, w_a[:, MLA_QL:MLA_QL + MLA_KVL])
    kr = _linear(tag + "_akr", F32)(h, _pad_cols(w_a[:, MLA_QL + MLA_KVL:], LANES))
    cq = _rmsnorm(tag + "_nq")(cq, g_q[None, :])
    ckv = _rmsnorm(tag + "_nkv")(ckv, g_kv[None, :])
    q = _linear(tag + "_uq", F32)(cq, w_uq).reshape(L, MLA_HEADS, MLA_NOPE + MLA_ROPE)
    kv = _linear(tag + "_ukv", BF16)(ckv, w_ukv).reshape(L, MLA_HEADS, MLA_NOPE + MLA_V)
    scale = (MLA_NOPE + MLA_ROPE) ** -0.5
    rope = _rope(tag + "_rope", MLA_ROPE // 2, BF16)
    q_rope = rope(q[:, :, MLA_NOPE:].reshape(L, MLA_HEADS * MLA_ROPE), *(t * scale for t in tabs))
    k_rope = rope(kr, *tabs)[:, :MLA_ROPE]
    qf = jnp.concatenate([(q[:, :, :MLA_NOPE] * scale).astype(BF16), q_rope.reshape(L, MLA_HEADS, MLA_ROPE)], axis=-1)
    kf = jnp.concatenate([kv[:, :, :MLA_NOPE], jnp.broadcast_to(k_rope[:, None, :], (L, MLA_HEADS, MLA_ROPE))], axis=-1)
    seq_last = lambda t: t.transpose(1, 2, 0)
    unit = jnp.zeros((MLA_HEADS, MLA_XROWS, L), BF16).at[:, 0, :].set(1.0)
    poison = jnp.zeros((MLA_HEADS, MLA_XROWS, L), BF16).at[:, 0, :].set(jnp.where(jnp.arange(L) < PAD, -POISON, 0.0))
    qT = jnp.concatenate([seq_last(qf), unit], axis=1).reshape(-1, L)
    kT = jnp.concatenate([seq_last(kf), poison], axis=1).reshape(-1, L)
    oT = _mla_attention(tag + "_attn")(qT, kT, seq_last(kv[:, :, MLA_NOPE:]).reshape(-1, L))
    return _linear_km(tag + "_out")(oT, w_o)


BIG = (("fox_w_in", 2), ("fox_w_o", 1), ("swa_w_in", 2), ("swa_w_o", 1), ("mla_w_a", 1), ("mla_w_uq", 2),
       ("mla_w_ukv", 2), ("mla_w_o", 1), ("ffn_w_in", 2), ("ffn_w_out", 1))
SMALL = (("meta_tokens", 1), ("mla_g_q", 1), ("mla_g_kv", 1), ("ffn_conv_w", 2))
SHARDED = BIG + SMALL
REPL = ("ln1_g", "ln1_b", "ln2_g", "ln2_b", "fox_b_f", "swa_sinks", "ffn_conv_b")
FFN_STACKED = ("ffn_w_in", "ffn_w_out", "ffn_conv_w")
WEIGHTS = ("meta_tokens", "ln1_g", "ln1_b", "ln2_g", "ln2_b", "fox_w_in", "fox_b_f", "fox_w_o", "swa_w_in", "swa_sinks",
           "swa_w_o", "mla_w_a", "mla_g_q", "mla_g_kv", "mla_w_uq", "mla_w_ukv", "mla_w_o", "ffn_w_in", "ffn_conv_w",
           "ffn_conv_b", "ffn_w_out")


def _local_loss(stacked, repl, x, target):
    S, D = x.shape
    L = S + BLOCK
    wts = {n: _to_full(stacked[n], ax) for n, ax in SHARDED if n not in FFN_STACKED}
    wts.update(repl)
    half = N_DEV // 2
    h = jnp.concatenate([jnp.zeros((PAD, D), F32), wts["meta_tokens"], x], axis=0)
    cos_p, sin_p = _rope_tables(L, ROPE_DIM, ROPE_THETA)
    tabs_p = _rope_lanes(cos_p, sin_p, SWA_HD)
    cos_m, sin_m = _rope_tables(L, MLA_ROPE, MLA_ROPE_THETA)
    tabs_m = _rope_lanes(cos_m, sin_m, MLA_ROPE)
    for i in range(DEPTH):
        kind, j = i % 3, i // 3
        tag = f"l{i}"
        if kind == 0:
            mix = _fox_mixer(h, wts["fox_w_in"][j], wts["fox_b_f"][j], wts["fox_w_o"][j], tag + "_fox")
        elif kind == 1:
            mix = _swa_mixer(h, wts["swa_w_in"][j], wts["swa_sinks"][j], wts["swa_w_o"][j], tabs_p, tag + "_swa")
        else:
            mix = _mla_mixer(h, wts["mla_w_a"][j], wts["mla_g_q"][j], wts["mla_g_kv"][j], wts["mla_w_uq"][j],
                             wts["mla_w_ukv"][j], wts["mla_w_o"][j], tabs_m, tag + "_mla")
        h = _deepnorm(tag + "_ln1")(h, mix, wts["ln1_g"][i][None, :], wts["ln1_b"][i][None, :])
        w3 = stacked["ffn_w_in"][:, i]
        shard = w3.shape[-1]
        cw4 = stacked["ffn_conv_w"][:, i].reshape(2, half, CONV_W, shard)
        cb4 = wts["ffn_conv_b"][i].reshape(2, half, 1, shard)
        wo3 = stacked["ffn_w_out"][:, i].reshape(half, shard, D)
        act3 = _ffn_up(tag + "_ffn")(h, w3, cw4, cb4)
        ffn = _ffn_down(tag + "_ffn_down")(act3, wo3)
        h = _deepnorm(tag + "_ln2")(h, ffn, wts["ln2_g"][i][None, :], wts["ln2_b"][i][None, :])
    return _loss_head("loss_head")(h, target)


def _step(x, target, w, m, v):
    names = [n for n, _ in SHARDED]
    big = {n for n, _ in BIG}
    stacked = dict(zip(names, _gather([w[n].astype(BF16) if n in big else w[n] for n in names], "gather_weights")))
    repl = {n: w[n] for n in REPL}
    loss, (g_st, g_repl, grad_x) = jax.value_and_grad(_local_loss, argnums=(0, 1, 2))(stacked, repl, x, target)
    got = dict(zip(names, _scatter([g_st[n] for n in names], names)))
    got.update(zip(REPL, _gather([g_repl[n] for n in REPL], "gather_repl_grads")))
    out = {}
    for n in WEIGHTS:
        for kind, a in zip(("grad", "delta", "new_m", "new_v"), _adamw_call(got[n], w[n], m[n], v[n], "adamw_" + n)):
            out[(kind, n)] = a
    return loss, grad_x, out


def kernel(x, meta_tokens, ln1_g, ln1_b, ln2_g, ln2_b, fox_w_in, fox_b_f, fox_w_o, swa_w_in, swa_sinks, swa_w_o, mla_w_a, mla_g_q, mla_g_kv, mla_w_uq, mla_w_ukv, mla_w_o, ffn_w_in, ffn_conv_w, ffn_conv_b, ffn_w_out, loss_target, m_meta_tokens, m_ln1_g, m_ln1_b, m_ln2_g, m_ln2_b, m_fox_w_in, m_fox_b_f, m_fox_w_o, m_swa_w_in, m_swa_sinks, m_swa_w_o, m_mla_w_a, m_mla_g_q, m_mla_g_kv, m_mla_w_uq, m_mla_w_ukv, m_mla_w_o, m_ffn_w_in, m_ffn_conv_w, m_ffn_conv_b, m_ffn_w_out, v_meta_tokens, v_ln1_g, v_ln1_b, v_ln2_g, v_ln2_b, v_fox_w_in, v_fox_b_f, v_fox_w_o, v_swa_w_in, v_swa_sinks, v_swa_w_o, v_mla_w_a, v_mla_g_q, v_mla_g_kv, v_mla_w_uq, v_mla_w_ukv, v_mla_w_o, v_ffn_w_in, v_ffn_conv_w, v_ffn_conv_b, v_ffn_w_out):
    args = (meta_tokens, ln1_g, ln1_b, ln2_g, ln2_b, fox_w_in, fox_b_f, fox_w_o, swa_w_in, swa_sinks, swa_w_o, mla_w_a,
            mla_g_q, mla_g_kv, mla_w_uq, mla_w_ukv, mla_w_o, ffn_w_in, ffn_conv_w, ffn_conv_b, ffn_w_out)
    ms = (m_meta_tokens, m_ln1_g, m_ln1_b, m_ln2_g, m_ln2_b, m_fox_w_in, m_fox_b_f, m_fox_w_o, m_swa_w_in, m_swa_sinks,
          m_swa_w_o, m_mla_w_a, m_mla_g_q, m_mla_g_kv, m_mla_w_uq, m_mla_w_ukv, m_mla_w_o, m_ffn_w_in, m_ffn_conv_w,
          m_ffn_conv_b, m_ffn_w_out)
    vs = (v_meta_tokens, v_ln1_g, v_ln1_b, v_ln2_g, v_ln2_b, v_fox_w_in, v_fox_b_f, v_fox_w_o, v_swa_w_in, v_swa_sinks,
          v_swa_w_o, v_mla_w_a, v_mla_g_q, v_mla_g_kv, v_mla_w_uq, v_mla_w_ukv, v_mla_w_o, v_ffn_w_in, v_ffn_conv_w,
          v_ffn_conv_b, v_ffn_w_out)
    w = dict(zip(WEIGHTS, args))
    m = dict(zip(WEIGHTS, ms))
    v = dict(zip(WEIGHTS, vs))
    loss, grad_x, out = _step(x[0], loss_target[0], w, m, v)
    loss = lax.psum(loss, ("x", "y", "c"))
    res = [loss, grad_x[None]]
    for kind in ("grad", "delta", "new_m", "new_v"):
        res += [out[(kind, n)] for n in WEIGHTS]
    return tuple(res)
```

```python
import jax
import jax.numpy as jnp
import numpy as np
from jax import lax
from jax.experimental import pallas as pl
from jax.experimental.pallas import tpu as pltpu

F32 = jnp.float32
BF16 = jnp.bfloat16

DEPTH = 4
N_META = 16
BLOCK = 128
PAD = BLOCK - N_META
NEG = -1e30
ALPHA = (2.0 * DEPTH) ** 0.25
LN_EPS = 1e-5
RMS_EPS = 1e-6
FOX_HEADS, FOX_HD = 16, 64
SWA_QH, SWA_KVH, SWA_HD = 16, 2, 64
SWA_G = SWA_QH // SWA_KVH
ROPE_THETA = 500000.0
ROPE_DIM = SWA_HD // 4
MLA_HEADS, MLA_QL, MLA_KVL, MLA_NOPE, MLA_ROPE, MLA_V = 16, 384, 256, 64, 32, 64
MLA_ROPE_THETA = 10000.0
CONV_W = 3
ADAM_LR, ADAM_B1, ADAM_B2, ADAM_EPS, ADAM_WD, ADAM_STEP = 0.001, 0.9, 0.999, 1e-08, 0.01, 10

LANES = 128
SUBLANES_F32 = 8
SUBLANES_BF16 = 16
VMEM_LIMIT = 48 * 1024 * 1024
N_DEV = 8
MESH = pl.DeviceIdType.MESH
ROW_TILE = 640
ADAM_ROWS = 256
MM_TILE_M, MM_TILE_N, MM_TILE_K = 1664, 1024, 1664
ATTN_HEADS_FWD, ATTN_HEADS_BWD = 16, 4


def _tile(n, cap):
    if n <= cap:
        return n
    best = 0
    for t in range(LANES, cap + 1, LANES):
        if n % t == 0:
            best = t
    assert best, (n, cap)
    return best


def _params(*sem):
    return pltpu.CompilerParams(dimension_semantics=sem, vmem_limit_bytes=VMEM_LIMIT)


def _mm(a, b, *, la="mk", lb="kn", sa=None, sb=None, so=None, out_dtype, name):
    size, tile, parts = {}, {}, {}
    for x, lay, split in ((a, la, sa), (b, lb, sb)):
        shp = x.shape[1:] if split else x.shape
        for ax, n in zip(lay, shp):
            if ax == split:
                size[ax], tile[ax], parts[ax] = x.shape[0] * n, n, x.shape[0]
            else:
                assert size.setdefault(ax, n) == n, (name, ax, a.shape, b.shape)
    for ax, cap in (("m", MM_TILE_M), ("n", MM_TILE_N), ("k", MM_TILE_K)):
        tile.setdefault(ax, _tile(size[ax], cap))
    grid = tuple(size[ax] // tile[ax] for ax in "mnk")
    nk = grid[2]
    dn = (((la.index("k"),), (lb.index("k"),)), ((), ()))

    def kern(a_ref, b_ref, o_ref, *acc):
        p = lax.dot_general(a_ref[...].astype(BF16), b_ref[...].astype(BF16), dn, preferred_element_type=F32)
        if nk == 1:
            o_ref[...] = p.astype(o_ref.dtype)
            return
        acc_ref, = acc
        k = pl.program_id(2)

        @pl.when(k == 0)
        def _():
            acc_ref[...] = p

        @pl.when(k > 0)
        def _():
            acc_ref[...] += p

        @pl.when(k == nk - 1)
        def _():
            o_ref[...] = acc_ref[...].astype(o_ref.dtype)

    def spec(lay, split):
        blk = tuple(tile[ax] for ax in lay)
        if split is None:
            return pl.BlockSpec(blk, lambda i, j, k: tuple({"m": i, "n": j, "k": k}[ax] for ax in lay))
        return pl.BlockSpec((None,) + blk, lambda i, j, k: ({"m": i, "n": j, "k": k}[split],) + tuple(
            0 if ax == split else {"m": i, "n": j, "k": k}[ax] for ax in lay))

    if so is None:
        out_shape = (size["m"], size["n"])
    else:
        out_shape = (parts[so],) + tuple(tile[ax] if ax == so else size[ax] for ax in "mn")
    return pl.pallas_call(
        kern, name=name, grid=grid,
        in_specs=[spec(la, sa), spec(lb, sb)], out_specs=spec("mn", so),
        out_shape=jax.ShapeDtypeStruct(out_shape, out_dtype),
        scratch_shapes=[] if nk == 1 else [pltpu.VMEM((tile["m"], tile["n"]), F32)],
        compiler_params=_params("parallel", "parallel", "arbitrary"),
    )(a, b)


def _linear(name, out_dtype):
    @jax.custom_vjp
    def f(x, w):
        return _mm(x, w, out_dtype=out_dtype, name=name + "_fwd")

    def fwd(x, w):
        return f(x, w), (x, w)

    def bwd(res, g):
        x, w = res
        dx = _mm(g, w, lb="nk", out_dtype=x.dtype, name=name + "_dx")
        dw = _mm(x, g, la="km", out_dtype=w.dtype, name=name + "_dw")
        return dx, dw

    f.defvjp(fwd, bwd)
    return f


def _ln_stats(z):
    mu = jnp.mean(z, axis=-1, keepdims=True)
    zc = z - mu
    var = jnp.mean(zc * zc, axis=-1, keepdims=True)
    return zc, lax.rsqrt(var + LN_EPS)


def _ln_fwd_call(h, mix, g, b, name):
    L, D = h.shape
    tm = _tile(L, ROW_TILE)

    def kern(h_ref, m_ref, g_ref, b_ref, o_ref):
        zc, rstd = _ln_stats(ALPHA * h_ref[...] + m_ref[...])
        o_ref[...] = zc * rstd * g_ref[...] + b_ref[...]

    row = pl.BlockSpec((tm, D), lambda i: (i, 0))
    vec = pl.BlockSpec((1, D), lambda i: (0, 0))
    return pl.pallas_call(kern, name=name, grid=(L // tm,), in_specs=[row, row, vec, vec], out_specs=row,
                          out_shape=jax.ShapeDtypeStruct((L, D), F32), compiler_params=_params("parallel"))(h, mix, g, b)


def _ln_bwd_call(h, mix, g, dout, name):
    L, D = h.shape
    tm = _tile(L, ROW_TILE)

    def kern(h_ref, m_ref, g_ref, d_ref, dz_ref, dg_ref, db_ref):
        i = pl.program_id(0)
        zc, rstd = _ln_stats(ALPHA * h_ref[...] + m_ref[...])
        xhat = zc * rstd
        d = d_ref[...]
        dxh = d * g_ref[...]
        m1 = jnp.mean(dxh, axis=-1, keepdims=True)
        m2 = jnp.mean(dxh * xhat, axis=-1, keepdims=True)
        row = i * tm + lax.broadcasted_iota(jnp.int32, (tm, 1), 0)
        dz_ref[...] = jnp.where(row >= PAD, rstd * (dxh - m1 - xhat * m2), 0.0)
        pg = jnp.sum(d * xhat, axis=0, keepdims=True)
        pb = jnp.sum(d, axis=0, keepdims=True)

        @pl.when(i == 0)
        def _():
            dg_ref[...] = pg
            db_ref[...] = pb

        @pl.when(i > 0)
        def _():
            dg_ref[...] += pg
            db_ref[...] += pb

    row = pl.BlockSpec((tm, D), lambda i: (i, 0))
    vec = pl.BlockSpec((1, D), lambda i: (0, 0))
    return pl.pallas_call(
        kern, name=name, grid=(L // tm,), in_specs=[row, row, vec, row], out_specs=[row, vec, vec],
        out_shape=[jax.ShapeDtypeStruct((L, D), F32), jax.ShapeDtypeStruct((1, D), F32), jax.ShapeDtypeStruct((1, D), F32)],
        compiler_params=_params("arbitrary"))(h, mix, g, dout)


def _deepnorm(name):
    @jax.custom_vjp
    def f(h, mix, g, b):
        return _ln_fwd_call(h, mix, g, b, name + "_fwd")

    def fwd(h, mix, g, b):
        return f(h, mix, g, b), (h, mix, g)

    def bwd(res, dout):
        h, mix, g = res
        dz, dg, db = _ln_bwd_call(h, mix, g, dout, name + "_bwd")
        return ALPHA * dz, dz, dg, db

    f.defvjp(fwd, bwd)
    return f


def _rms_fwd_call(x, g, name):
    L, n = x.shape
    tm = _tile(L, ROW_TILE)

    def kern(x_ref, g_ref, o_ref):
        x = x_ref[...]
        o_ref[...] = x * lax.rsqrt(jnp.mean(x * x, axis=-1, keepdims=True) + RMS_EPS) * g_ref[...]

    row = pl.BlockSpec((tm, n), lambda i: (i, 0))
    vec = pl.BlockSpec((1, n), lambda i: (0, 0))
    return pl.pallas_call(kern, name=name, grid=(L // tm,), in_specs=[row, vec], out_specs=row,
                          out_shape=jax.ShapeDtypeStruct((L, n), F32), compiler_params=_params("parallel"))(x, g)


def _rms_bwd_call(x, g, dout, name):
    L, n = x.shape
    tm = _tile(L, ROW_TILE)

    def kern(x_ref, g_ref, d_ref, dx_ref, dg_ref):
        i = pl.program_id(0)
        x = x_ref[...]
        rstd = lax.rsqrt(jnp.mean(x * x, axis=-1, keepdims=True) + RMS_EPS)
        xhat = x * rstd
        d = d_ref[...]
        dxh = d * g_ref[...]
        dx_ref[...] = rstd * (dxh - xhat * jnp.mean(dxh * xhat, axis=-1, keepdims=True))
        pg = jnp.sum(d * xhat, axis=0, keepdims=True)

        @pl.when(i == 0)
        def _():
            dg_ref[...] = pg

        @pl.when(i > 0)
        def _():
            dg_ref[...] += pg

    row = pl.BlockSpec((tm, n), lambda i: (i, 0))
    vec = pl.BlockSpec((1, n), lambda i: (0, 0))
    return pl.pallas_call(
        kern, name=name, grid=(L // tm,), in_specs=[row, vec, row], out_specs=[row, vec],
        out_shape=[jax.ShapeDtypeStruct((L, n), F32), jax.ShapeDtypeStruct((1, n), F32)],
        compiler_params=_params("arbitrary"))(x, g, dout)


def _rmsnorm(name):
    @jax.custom_vjp
    def f(x, g):
        return _rms_fwd_call(x, g, name + "_fwd")

    def fwd(x, g):
        return f(x, g), (x, g)

    def bwd(res, dout):
        x, g = res
        dx, dg = _rms_bwd_call(x, g, dout, name + "_bwd")
        return dx, dg

    f.defvjp(fwd, bwd)
    return f


def _rope_call(x, c, s1, s2, r, out_dtype, name):
    L, W = x.shape
    reps = W // LANES
    tm = _tile(L, ROW_TILE)

    def kern(x_ref, c_ref, s1_ref, s2_ref, o_ref):
        x = x_ref[...].astype(F32)
        wide = lambda t: jnp.tile(t[...], (1, reps)) if reps > 1 else t[...]
        out = x * wide(c_ref) + pltpu.roll(x, W - r, 1) * wide(s1_ref) + pltpu.roll(x, r, 1) * wide(s2_ref)
        o_ref[...] = out.astype(o_ref.dtype)

    row = pl.BlockSpec((tm, W), lambda i: (i, 0))
    tab = pl.BlockSpec((tm, LANES), lambda i: (i, 0))
    return pl.pallas_call(kern, name=name, grid=(L // tm,), in_specs=[row, tab, tab, tab], out_specs=row,
                          out_shape=jax.ShapeDtypeStruct((L, W), out_dtype), compiler_params=_params("parallel"))(x, c, s1, s2)


def _rope(name, r, out_dtype):
    @jax.custom_vjp
    def f(x, c, s1, s2):
        return _rope_call(x, c, s1, s2, r, out_dtype, name + "_fwd")

    def fwd(x, c, s1, s2):
        return f(x, c, s1, s2), (c, s1, s2, jnp.zeros((), x.dtype))

    def bwd(res, g):
        c, s1, s2, proto = res
        dx = _rope_call(g, c, -s1, -s2, r, proto.dtype, name + "_bwd")
        return dx, jnp.zeros_like(c), jnp.zeros_like(s1), jnp.zeros_like(s2)

    f.defvjp(fwd, bwd)
    return f


def _rope_tables(L, dim, theta):
    pos = (jnp.arange(L) - PAD).astype(F32)
    inv = theta ** (-jnp.arange(0, dim, 2, dtype=F32) / dim)
    ang = pos[:, None] * inv[None, :]
    return jnp.cos(ang), jnp.sin(ang)


def _rope_lanes(cos, sin, period):
    L, half = cos.shape
    one = jnp.ones((L, period - 2 * half), F32)
    zero = jnp.zeros((L, period - 2 * half), F32)
    z_h = jnp.zeros((L, half), F32)
    c = jnp.concatenate([cos, cos, one], axis=1)
    s1 = jnp.concatenate([-sin, z_h, zero], axis=1)
    s2 = jnp.concatenate([z_h, sin, zero], axis=1)
    reps = LANES // period
    return tuple(jnp.tile(t, (1, reps)) for t in (c, s1, s2))


HALO = SUBLANES_BF16


GLU_ROWS = 128


def _strips(C):
    return [(c0, min(LANES, C - c0)) for c0 in range(0, C, LANES)]


def _shifted_down(win):
    return [pltpu.roll(win, CONV_W - 1 - t, 0)[SUBLANES_F32:, :] if t < CONV_W - 1 else win[SUBLANES_F32:, :]
            for t in range(CONV_W)]


def _conv_taps(buf_ref, g, off, rows, cols, w, b):
    u = _shifted_down(buf_ref[g, pl.ds(off - SUBLANES_F32, rows + SUBLANES_F32), cols])
    return b + u[0] * w[0:1, :] + u[1] * w[1:2, :] + u[2] * w[2:3, :]


def _glu_fwd_call(u4, cw4, cb4, name):
    _, P, L, C = u4.shape
    tm = _tile(L, ROW_TILE)
    hb = tm // HALO

    def kern(u_ref, up_ref, w_ref, b_ref, o_ref, buf):
        i = pl.program_id(1)
        row = i * tm + lax.broadcasted_iota(jnp.int32, (tm, 1), 0)
        prow = i * tm - HALO + lax.broadcasted_iota(jnp.int32, (HALO, 1), 0)
        for g in range(2):
            buf[g, pl.ds(HALO, tm), :] = jnp.where(row >= PAD, u_ref[g], 0.0)
            buf[g, pl.ds(0, HALO), :] = jnp.where(prow >= PAD, up_ref[g], 0.0)
        for c0, wd in _strips(C):
            cols = pl.ds(c0, wd)
            ws = [w_ref[g][:, c0:c0 + wd] for g in range(2)]
            bs = [b_ref[g][:, c0:c0 + wd] for g in range(2)]
            for r in range(0, tm, GLU_ROWS):
                n = min(GLU_ROWS, tm - r)
                gate, val = (_conv_taps(buf, g, HALO + r, n, cols, ws[g], bs[g]) for g in range(2))
                o_ref[pl.ds(r, n), cols] = (gate * jax.nn.sigmoid(gate) * val).astype(o_ref.dtype)

    return pl.pallas_call(
        kern, name=name, grid=(P, L // tm),
        in_specs=[pl.BlockSpec((2, None, tm, C), lambda p, i: (0, p, i, 0)),
                  pl.BlockSpec((2, None, HALO, C), lambda p, i: (0, p, jnp.maximum(i * hb - 1, 0), 0)),
                  pl.BlockSpec((2, None, CONV_W, C), lambda p, i: (0, p, 0, 0)),
                  pl.BlockSpec((2, None, 1, C), lambda p, i: (0, p, 0, 0))],
        out_specs=pl.BlockSpec((None, tm, C), lambda p, i: (p, i, 0)),
        out_shape=jax.ShapeDtypeStruct((P, L, C), BF16),
        scratch_shapes=[pltpu.VMEM((2, tm + HALO, C), F32)],
        compiler_params=_params("parallel", "parallel"))(u4, u4, cw4, cb4)


def _glu_bwd_call(u4, cw4, cb4, dact, name):
    _, P, L, C = u4.shape
    tm = _tile(L, ROW_TILE)
    hb = tm // HALO
    ext = tm + HALO
    last_halo = L // HALO - 1

    def kern(u_ref, up_ref, un_ref, w_ref, b_ref, d_ref, dn_ref, du_ref, dw_ref, db_ref, ubuf, dbuf):
        i = pl.program_id(1)
        r0 = i * tm
        mask = lambda blk, start: jnp.where(
            (start + lax.broadcasted_iota(jnp.int32, (blk.shape[0], 1), 0) >= PAD), blk, 0.0)
        for g in range(2):
            ubuf[g, pl.ds(0, HALO), :] = mask(up_ref[g], r0 - HALO)
            ubuf[g, pl.ds(HALO, tm), :] = mask(u_ref[g], r0)
            ubuf[g, pl.ds(HALO + tm, HALO), :] = un_ref[g]
        dbuf[pl.ds(0, tm), :] = d_ref[...].astype(F32)
        dbuf[pl.ds(tm, HALO), :] = jnp.where(r0 + tm < L, dn_ref[...].astype(F32), 0.0)
        for c0, wd in _strips(C):
            cols = pl.ds(c0, wd)
            ws = [w_ref[g][:, c0:c0 + wd] for g in range(2)]
            bs = [b_ref[g][:, c0:c0 + wd] for g in range(2)]
            pw = [[jnp.zeros((1, wd), F32) for _ in range(CONV_W)] for _ in range(2)]
            pb = [jnp.zeros((1, wd), F32) for _ in range(2)]
            for r in range(0, tm, GLU_ROWS):
                n = min(GLU_ROWS, tm - r)
                ne = n + SUBLANES_F32
                us = [_shifted_down(ubuf[g, pl.ds(HALO + r - SUBLANES_F32, ne + SUBLANES_F32), cols]) for g in range(2)]
                gate, val = (bs[g] + us[g][0] * ws[g][0:1, :] + us[g][1] * ws[g][1:2, :] + us[g][2] * ws[g][2:3, :]
                             for g in range(2))
                sg = jax.nn.sigmoid(gate)
                d = dbuf[pl.ds(r, ne), cols]
                dys = (d * val * (sg * (1.0 + gate * (1.0 - sg))), d * (gate * sg))
                row = r0 + r + lax.broadcasted_iota(jnp.int32, (n, 1), 0)
                for g in range(2):
                    w = ws[g]
                    dy = dys[g][:n, :]
                    du = (dy * w[2:3, :] + pltpu.roll(dys[g], ne - 1, 0)[:n, :] * w[1:2, :]
                          + pltpu.roll(dys[g], ne - 2, 0)[:n, :] * w[0:1, :])
                    du_ref[g, pl.ds(r, n), cols] = jnp.where(row >= PAD, du, 0.0).astype(du_ref.dtype)
                    for t in range(CONV_W):
                        pw[g][t] = pw[g][t] + jnp.sum(dy * us[g][t][:n, :], axis=0, keepdims=True)
                    pb[g] = pb[g] + jnp.sum(dy, axis=0, keepdims=True)
            for g in range(2):
                pwg = jnp.concatenate(pw[g], axis=0)

                @pl.when(i == 0)
                def _():
                    dw_ref[g, :, cols] = pwg
                    db_ref[g, :, cols] = pb[g]

                @pl.when(i > 0)
                def _():
                    dw_ref[g, :, cols] += pwg
                    db_ref[g, :, cols] += pb[g]

    nxt = lambda i: jnp.minimum((i + 1) * hb, last_halo)
    return pl.pallas_call(
        kern, name=name, grid=(P, L // tm),
        in_specs=[pl.BlockSpec((2, None, tm, C), lambda p, i: (0, p, i, 0)),
                  pl.BlockSpec((2, None, HALO, C), lambda p, i: (0, p, jnp.maximum(i * hb - 1, 0), 0)),
                  pl.BlockSpec((2, None, HALO, C), lambda p, i: (0, p, nxt(i), 0)),
                  pl.BlockSpec((2, None, CONV_W, C), lambda p, i: (0, p, 0, 0)),
                  pl.BlockSpec((2, None, 1, C), lambda p, i: (0, p, 0, 0)),
                  pl.BlockSpec((None, tm, C), lambda p, i: (p, i, 0)),
                  pl.BlockSpec((None, HALO, C), lambda p, i: (p, nxt(i), 0))],
        out_specs=[pl.BlockSpec((2, None, tm, C), lambda p, i: (0, p, i, 0)),
                   pl.BlockSpec((2, None, CONV_W, C), lambda p, i: (0, p, 0, 0)),
                   pl.BlockSpec((2, None, 1, C), lambda p, i: (0, p, 0, 0))],
        out_shape=[jax.ShapeDtypeStruct((2, P, L, C), BF16), jax.ShapeDtypeStruct((2, P, CONV_W, C), F32),
                   jax.ShapeDtypeStruct((2, P, 1, C), F32)],
        scratch_shapes=[pltpu.VMEM((2, tm + 2 * HALO, C), F32), pltpu.VMEM((ext, C), F32)],
        compiler_params=_params("parallel", "arbitrary"))(u4, u4, u4, cw4, cb4, dact, dact)


def _ffn_up(name):
    def run(h1, w3, cw4, cb4):
        u3 = _mm(h1, w3, sb="n", so="n", out_dtype=F32, name=name + "_up")
        u4 = u3.reshape((2, u3.shape[0] // 2) + u3.shape[1:])
        return _glu_fwd_call(u4, cw4, cb4, name + "_glu"), u4

    @jax.custom_vjp
    def f(h1, w3, cw4, cb4):
        return run(h1, w3, cw4, cb4)[0]

    def fwd(h1, w3, cw4, cb4):
        act, u4 = run(h1, w3, cw4, cb4)
        return act, (h1, w3, cw4, cb4, u4)

    def bwd(res, dact):
        h1, w3, cw4, cb4, u4 = res
        du4, dcw, dcb = _glu_bwd_call(u4, cw4, cb4, dact, name + "_glu_bwd")
        du3 = du4.reshape((du4.shape[0] * du4.shape[1],) + du4.shape[2:])
        dh1 = _mm(du3, w3, sa="k", lb="nk", sb="k", out_dtype=F32, name=name + "_up_dx")
        dw3 = _mm(h1, du3, la="km", sb="n", so="n", out_dtype=w3.dtype, name=name + "_up_dw")
        return dh1, dw3, dcw, dcb

    f.defvjp(fwd, bwd)
    return f


def _ffn_down(name):
    @jax.custom_vjp
    def f(act3, wo3):
        return _mm(act3, wo3, sa="k", sb="k", out_dtype=F32, name=name + "_fwd")

    def fwd(act3, wo3):
        return f(act3, wo3), (act3, wo3)

    def bwd(res, g):
        act3, wo3 = res
        dact = _mm(g, wo3, lb="nk", sb="n", so="n", out_dtype=act3.dtype, name=name + "_dx")
        dwo = _mm(act3, g, la="km", sa="m", so="m", out_dtype=wo3.dtype, name=name + "_dw")
        return dact, dwo

    f.defvjp(fwd, bwd)
    return f


def _pairs(n, kv_major):
    if kv_major:
        pr = [(i, j) for j in range(n) for i in range(j, n)]
    else:
        pr = [(i, j) for i in range(n) for j in range(i + 1)]
    return (jnp.asarray(np.array([p[0] for p in pr], np.int32)), jnp.asarray(np.array([p[1] for p in pr], np.int32)))


TN = (((0,), (0,)), ((), ()))
NT = (((1,), (1,)), ((), ()))


def _scores(kT, qT, i, j, tq, tk, masked):
    s = lax.dot_general(kT, qT, TN, preferred_element_type=F32)
    if not masked:
        return s, None
    col = j * tk + lax.broadcasted_iota(jnp.int32, (tk, tq), 0)
    row = i * tq + lax.broadcasted_iota(jnp.int32, (tk, tq), 1)
    mask = (col <= row) & (col >= PAD)
    return jnp.where(mask, s, NEG), mask


FOX_SCALE = 0.125
C_PARTS = 3


class _AttnLayout:
    def __init__(self, heads, q_rows, v_rows, bases=(0, 0, 0), extra=0, q_scale=None, grad_rows=None):
        self.heads, self.q_rows, self.v_rows, self.bases, self.extra, self.q_scale = heads, q_rows, v_rows, bases, extra, q_scale
        self.dk = q_rows + extra
        self.grad_rows = q_rows if grad_rows is None else grad_rows
        self.n_in = 5 if extra else 3

    def specs(self, tq, tk, hp):
        assert self.heads % hp == 0 and all(b % hp == 0 for b in self.bases)
        qb, kb, vb = (b // hp for b in self.bases)
        qmap = lambda g, t, it, jt: (qb + g, it[t])
        kmap = lambda g, t, it, jt: (kb + g, jt[t])
        vmap = lambda g, t, it, jt: (vb + g, jt[t])
        s = [pl.BlockSpec((hp * self.q_rows, tq), qmap), pl.BlockSpec((hp * self.q_rows, tk), kmap),
             pl.BlockSpec((hp * self.v_rows, tk), vmap)]
        if self.extra:
            s += [pl.BlockSpec((hp * self.extra, tq), lambda g, t, it, jt: (g, it[t])),
                  pl.BlockSpec((hp * self.extra, tk), lambda g, t, it, jt: (g, jt[t]))]
        return s

    def operands(self, refs, a):
        rows = lambda ref, n: ref[a * n:(a + 1) * n, :]
        q, k = rows(refs[0], self.q_rows), rows(refs[1], self.q_rows)
        if self.q_scale is not None:
            q = q * jnp.asarray(self.q_scale, q.dtype)
        if self.extra:
            q = jnp.concatenate([q, rows(refs[3], self.extra)], axis=0)
            k = jnp.concatenate([k, rows(refs[4], self.extra)], axis=0)
        return q, k, rows(refs[2], self.v_rows)


def _attn_fwd_call(lay, arrs, name):
    L = arrs[0].shape[1]
    H, dv, hp = lay.heads, lay.v_rows, ATTN_HEADS_FWD
    tq = tk = _tile(L, ROW_TILE)
    it, jt = _pairs(L // tq, False)

    def kern(it_ref, jt_ref, *refs):
        t = pl.program_id(1)
        i, j = it_ref[t], jt_ref[t]
        o_ref, lse_ref, m_s, l_s, acc_s = refs[lay.n_in:]

        @pl.when(j == 0)
        def _():
            m_s[...] = jnp.full_like(m_s, NEG)
            l_s[...] = jnp.zeros_like(l_s)
            acc_s[...] = jnp.zeros_like(acc_s)

        def step(masked):
            ops = [lay.operands(refs, a) for a in range(hp)]
            scores = [_scores(k, q, i, j, tq, tk, masked) for q, k, _ in ops]
            weights = []
            for a, (s, mask) in enumerate(scores):
                m_prev = m_s[a]
                m_new = jnp.maximum(m_prev, jnp.max(s, axis=0, keepdims=True))
                scale = jnp.exp(m_prev - m_new)
                p = jnp.exp(s - m_new)
                if masked:
                    p = jnp.where(mask, p, 0.0)
                l_s[a] = scale * l_s[a] + jnp.sum(p, axis=0, keepdims=True)
                m_s[a] = m_new
                weights.append((scale, p.astype(BF16)))
            for a, (scale, pb) in enumerate(weights):
                acc_s[a] = scale * acc_s[a] + jnp.dot(ops[a][2], pb, preferred_element_type=F32)

        pl.when(j == i)(lambda: step(True))
        pl.when(j != i)(lambda: step(False))

        @pl.when(j == i)
        def _():
            for a in range(hp):
                l = l_s[a]
                l = jnp.where(l == 0.0, 1.0, l)
                o_ref[a * dv:(a + 1) * dv, :] = (acc_s[a] / l).astype(o_ref.dtype)
                lse_ref[a] = m_s[a] + jnp.log(l)

    return pl.pallas_call(
        kern, name=name,
        grid_spec=pltpu.PrefetchScalarGridSpec(
            num_scalar_prefetch=2, grid=(H // hp, int(it.shape[0])), in_specs=lay.specs(tq, tk, hp),
            out_specs=[pl.BlockSpec((hp * dv, tq), lambda g, t, it, jt: (g, it[t])),
                       pl.BlockSpec((hp, 1, tq), lambda g, t, it, jt: (g, 0, it[t]))],
            scratch_shapes=[pltpu.VMEM((hp, 1, tq), F32), pltpu.VMEM((hp, 1, tq), F32), pltpu.VMEM((hp, dv, tq), F32)]),
        out_shape=[jax.ShapeDtypeStruct((H * dv, L), F32), jax.ShapeDtypeStruct((H, 1, L), F32)],
        compiler_params=_params("parallel", "arbitrary"))(it, jt, *arrs)


C_ROWS = SUBLANES_F32


def _attn_bwd_call(lay, arrs, oT, lse, doT, name):
    L = arrs[0].shape[1]
    H, dv, dk, gr, hp = lay.heads, lay.v_rows, lay.dk, lay.grad_rows, ATTN_HEADS_BWD
    tq = tk = _tile(L, ROW_TILE)
    nq = L // tq
    it, jt = _pairs(nq, True)
    npairs = int(it.shape[0])
    with_c = lay.extra > 0
    crows = pl.ds(lay.q_rows, C_ROWS)

    def kern(it_ref, jt_ref, *refs):
        t = pl.program_id(1)
        i, j = it_ref[t], jt_ref[t]
        rest = refs[lay.n_in:]
        o_ref, do_ref, lse_ref = rest[:3]
        if with_c:
            dq_ref, dk_ref, dv_ref, dqc_ref, dkc_ref, dq_s, dk_s, dv_s = rest[3:]
        else:
            dq_ref, dk_ref, dv_ref, dq_s, dk_s, dv_s = rest[3:]

        @pl.when(t == 0)
        def _():
            dq_s[...] = jnp.zeros_like(dq_s)

        @pl.when(i == j)
        def _():
            dk_s[...] = jnp.zeros_like(dk_s)
            dv_s[...] = jnp.zeros_like(dv_s)

        def step(masked):
            first = []
            for a in range(hp):
                q, k, v = lay.operands(refs, a)
                s, mask = _scores(k, q, i, j, tq, tk, masked)
                dof = do_ref[a * dv:(a + 1) * dv, :]
                dob = dof.astype(BF16)
                delta = jnp.sum(dof * o_ref[a * dv:(a + 1) * dv, :], axis=0, keepdims=True)
                dp = lax.dot_general(v, dob, TN, preferred_element_type=F32)
                first.append((q, k, s, mask, dob, delta, dp))
            second = []
            for a, (q, k, s, mask, dob, delta, dp) in enumerate(first):
                p = jnp.exp(s - lse_ref[a])
                if masked:
                    p = jnp.where(mask, p, 0.0)
                second.append((p.astype(BF16), (p * (dp - delta)).astype(BF16)))
            cols = pl.ds(pl.multiple_of(i * tq, tq), tq)
            for a, (pb, dsb) in enumerate(second):
                q, k, _, _, dob, _, _ = first[a]
                dv_s[a] += lax.dot_general(dob, pb, NT, preferred_element_type=F32)
                dk_s[a] += lax.dot_general(q, dsb, NT, preferred_element_type=F32)
                dq_s[a, :, cols] += jnp.dot(k, dsb, preferred_element_type=F32)

        pl.when(j == i)(lambda: step(True))
        pl.when(j != i)(lambda: step(False))

        @pl.when(i == nq - 1)
        def _():
            for a in range(hp):
                dk_ref[a * gr:(a + 1) * gr, :] = dk_s[a, :gr, :].astype(dk_ref.dtype)
                dv_ref[a * dv:(a + 1) * dv, :] = dv_s[a].astype(dv_ref.dtype)
                if with_c:
                    dkc_ref[a * C_ROWS:(a + 1) * C_ROWS, :] = dk_s[a, crows, :]

        @pl.when(t == npairs - 1)
        def _():
            for a in range(hp):
                dq = dq_s[a, :gr, :]
                dq_ref[a * gr:(a + 1) * gr, :] = (dq if lay.q_scale is None else dq * lay.q_scale).astype(dq_ref.dtype)
                if with_c:
                    dqc_ref[a * C_ROWS:(a + 1) * C_ROWS, :] = dq_s[a, crows, :]

    qcol = lambda g, t, it, jt: (g, it[t])
    kcol = lambda g, t, it, jt: (g, jt[t])
    whole = lambda g, t, it, jt: (g, 0)
    out_specs = [pl.BlockSpec((hp * gr, L), whole), pl.BlockSpec((hp * gr, tk), kcol), pl.BlockSpec((hp * dv, tk), kcol)]
    out_shape = [jax.ShapeDtypeStruct((H * gr, L), BF16), jax.ShapeDtypeStruct((H * gr, L), BF16),
                 jax.ShapeDtypeStruct((H * dv, L), BF16)]
    if with_c:
        out_specs += [pl.BlockSpec((hp * C_ROWS, L), whole), pl.BlockSpec((hp * C_ROWS, tk), kcol)]
        out_shape += [jax.ShapeDtypeStruct((H * C_ROWS, L), F32)] * 2
    return pl.pallas_call(
        kern, name=name,
        grid_spec=pltpu.PrefetchScalarGridSpec(
            num_scalar_prefetch=2, grid=(H // hp, npairs),
            in_specs=lay.specs(tq, tk, hp) + [pl.BlockSpec((hp * dv, tq), qcol), pl.BlockSpec((hp * dv, tq), qcol),
                                          pl.BlockSpec((hp, 1, tq), lambda g, t, it, jt: (g, 0, it[t]))],
            out_specs=out_specs,
            scratch_shapes=[pltpu.VMEM((hp, dk, L), F32), pltpu.VMEM((hp, dk, tk), F32), pltpu.VMEM((hp, dv, tk), F32)]),
        out_shape=out_shape, compiler_params=_params("parallel", "arbitrary"))(it, jt, *arrs, oT, doT, lse)


POISON = 1e30


def _fox_extras(cT):
    H, L = cT.shape
    to_bf16 = lambda t: lax.reduce_precision(t, exponent_bits=8, mantissa_bits=7)
    hi = to_bf16(cT)
    lo = to_bf16(cT - hi)
    ll = to_bf16(cT - hi - lo)
    terms = jnp.stack([hi, lo, ll], axis=1).astype(BF16)
    ones = jnp.ones_like(terms)
    fill = jnp.zeros((H, SUBLANES_BF16 - 2 * C_PARTS, L), BF16)
    pad_key = (jnp.arange(L) < PAD)[None, None, :]
    kterms = jnp.where(pad_key, jnp.asarray([POISON, 0.0, 0.0], BF16)[None, :, None], terms)
    eq = jnp.concatenate([terms, ones, fill], axis=1)
    ek = jnp.concatenate([ones, -kterms, fill], axis=1)
    return eq.reshape(H * SUBLANES_BF16, L), ek.reshape(H * SUBLANES_BF16, L)


def _fox_attention(name):
    assert FOX_SCALE == FOX_HD ** -0.5
    lay = _AttnLayout(FOX_HEADS, FOX_HD, FOX_HD, bases=(0, FOX_HEADS, 2 * FOX_HEADS), extra=SUBLANES_BF16,
                      q_scale=FOX_SCALE)

    def run(qkvT, cT):
        arrs = (qkvT, qkvT, qkvT) + _fox_extras(cT)
        oT, lse = _attn_fwd_call(lay, arrs, name + "_fwd")
        return oT, (arrs, oT, lse)

    @jax.custom_vjp
    def f(qkvT, cT):
        return run(qkvT, cT)[0]

    def fwd(qkvT, cT):
        return run(qkvT, cT)

    def bwd(res, doT):
        arrs, oT, lse = res
        dq, dk, dv, dqc, dkc = _attn_bwd_call(lay, arrs, oT, lse, doT, name + "_bwd")
        L = dq.shape[1]
        dc = dqc.reshape(FOX_HEADS, C_ROWS, L)[:, 0, :] - dkc.reshape(FOX_HEADS, C_ROWS, L)[:, C_PARTS, :]
        return jnp.concatenate([dq, dk, dv], axis=0), dc

    f.defvjp(fwd, bwd)
    return f


MLA_XROWS = SUBLANES_BF16


def _mla_attention(name):
    rows = MLA_NOPE + MLA_ROPE + MLA_XROWS
    lay = _AttnLayout(MLA_HEADS, rows, MLA_V, grad_rows=MLA_NOPE + MLA_ROPE)

    @jax.custom_vjp
    def f(qT, kT, vT):
        return _attn_fwd_call(lay, (qT, kT, vT), name + "_fwd")[0]

    def fwd(qT, kT, vT):
        oT, lse = _attn_fwd_call(lay, (qT, kT, vT), name + "_fwd")
        return oT, (qT, kT, vT, oT, lse)

    def bwd(res, doT):
        qT, kT, vT, oT, lse = res
        dq, dk, dv = _attn_bwd_call(lay, (qT, kT, vT), oT, lse, doT, name + "_bwd")
        L = dq.shape[1]
        widen = lambda t: jnp.pad(t.reshape(MLA_HEADS, -1, L), ((0, 0), (0, MLA_XROWS), (0, 0))).reshape(-1, L)
        return widen(dq), widen(dk), dv

    f.defvjp(fwd, bwd)
    return f


def _linear_t(name, out_dtype):
    @jax.custom_vjp
    def f(wT, x):
        return _mm(wT, x, lb="nk", out_dtype=out_dtype, name=name + "_fwd")

    def fwd(wT, x):
        return f(wT, x), (wT, x)

    def bwd(res, g):
        wT, x = res
        dwT = _mm(g, x, out_dtype=wT.dtype, name=name + "_dw")
        dx = _mm(g, wT, la="km", out_dtype=x.dtype, name=name + "_dx")
        return dwT, dx

    f.defvjp(fwd, bwd)
    return f


def _linear_km(name):
    @jax.custom_vjp
    def f(aT, w):
        return _mm(aT, w, la="km", out_dtype=F32, name=name + "_fwd")

    def fwd(aT, w):
        return f(aT, w), (aT, w)

    def bwd(res, g):
        aT, w = res
        daT = _mm(w, g, lb="nk", out_dtype=aT.dtype, name=name + "_dx")
        dw = _mm(aT, g, out_dtype=w.dtype, name=name + "_dw")
        return daT, dw

    f.defvjp(fwd, bwd)
    return f


def _swa_parts(q_ref, kc_ref, kp_ref, km_ref, sink_ref, i, scale):
    R = SWA_G * BLOCK
    q = q_ref[...].reshape(R, SWA_HD)
    nt = (((1,), (1,)), ((), ()))
    r = lax.broadcasted_iota(jnp.int32, (R, BLOCK), 0) & (BLOCK - 1)
    c = lax.broadcasted_iota(jnp.int32, (R, BLOCK), 1)
    s_c = jnp.where((c <= r) & (i >= 1), lax.dot_general(q, kc_ref[...], nt, preferred_element_type=F32) * scale, NEG)
    s_p = jnp.where((c > r) & (i >= 2), lax.dot_general(q, kp_ref[...], nt, preferred_element_type=F32) * scale, NEG)
    s_m = jnp.where((c >= PAD) & ((i >= 1) | (c <= r)),
                    lax.dot_general(q, km_ref[...], nt, preferred_element_type=F32) * scale, NEG)
    sink = jnp.broadcast_to(sink_ref[...], (SWA_G, BLOCK, 1)).reshape(R, 1)
    return q, s_c, s_p, s_m, sink


def _swa_specs(L):
    nb = L // BLOCK
    qs = pl.BlockSpec((None, SWA_G, BLOCK, SWA_HD), lambda g, i: (g, 0, i, 0))
    kc = pl.BlockSpec((None, BLOCK, SWA_HD), lambda g, i: (g, i, 0))
    kp = pl.BlockSpec((None, BLOCK, SWA_HD), lambda g, i: (g, jnp.maximum(i - 1, 0), 0))
    km = pl.BlockSpec((None, BLOCK, SWA_HD), lambda g, i: (g, 0, 0))
    sk = pl.BlockSpec((None, SWA_G, 1, 1), lambda g, i: (g, 0, 0, 0))
    ls = pl.BlockSpec((None, SWA_G, BLOCK, 1), lambda g, i: (g, 0, i, 0))
    return nb, qs, kc, kp, km, sk, ls


def _swa_fwd_call(q, k, v, sinks, scale, name):
    _, _, L, _ = q.shape
    nb, qs, kc, kp, km, sk, ls = _swa_specs(L)

    def kern(q_ref, kc_ref, kp_ref, km_ref, vc_ref, vp_ref, vm_ref, sink_ref, o_ref, lse_ref):
        i = pl.program_id(1)
        _, s_c, s_p, s_m, sink = _swa_parts(q_ref, kc_ref, kp_ref, km_ref, sink_ref, i, scale)
        mx = lambda s: jnp.max(s, axis=-1, keepdims=True)
        m = jnp.maximum(jnp.maximum(mx(s_c), mx(s_p)), jnp.maximum(mx(s_m), sink))
        p_c, p_p, p_m = jnp.exp(s_c - m), jnp.exp(s_p - m), jnp.exp(s_m - m)
        sm = lambda p: jnp.sum(p, axis=-1, keepdims=True)
        den = sm(p_c) + sm(p_p) + sm(p_m) + jnp.exp(sink - m)
        inv = 1.0 / den
        pv = lambda p, v_ref: jnp.dot((p * inv).astype(BF16), v_ref[...], preferred_element_type=F32)
        o = pv(p_c, vc_ref) + pv(p_p, vp_ref) + pv(p_m, vm_ref)
        o_ref[...] = o.reshape(SWA_G, BLOCK, SWA_HD)
        lse_ref[...] = (m + jnp.log(den)).reshape(SWA_G, BLOCK, 1)

    return pl.pallas_call(
        kern, name=name, grid=(SWA_KVH, nb), in_specs=[qs, kc, kp, km, kc, kp, km, sk], out_specs=[qs, ls],
        out_shape=[jax.ShapeDtypeStruct((SWA_KVH, SWA_G, L, SWA_HD), F32), jax.ShapeDtypeStruct((SWA_KVH, SWA_G, L, 1), F32)],
        compiler_params=_params("parallel", "parallel"))(q, k, k, k, v, v, v, sinks)


def _swa_bwd_call(q, k, v, sinks, o, lse, do, scale, name):
    _, _, L, _ = q.shape
    nb, qs, kc, kp, km, sk, ls = _swa_specs(L)
    R = SWA_G * BLOCK
    full = pl.BlockSpec((None, L, SWA_HD), lambda g, i: (g, 0, 0))

    def kern(q_ref, kc_ref, kp_ref, km_ref, vc_ref, vp_ref, vm_ref, sink_ref, o_ref, do_ref, lse_ref,
             dq_ref, dk_ref, dv_ref, dsink_ref):
        i = pl.program_id(1)

        @pl.when(i == 0)
        def _():
            dk_ref[...] = jnp.zeros_like(dk_ref)
            dv_ref[...] = jnp.zeros_like(dv_ref)
            dsink_ref[...] = jnp.zeros_like(dsink_ref)

        q, s_c, s_p, s_m, sink = _swa_parts(q_ref, kc_ref, kp_ref, km_ref, sink_ref, i, scale)
        lse = lse_ref[...].reshape(R, 1)
        dof = do_ref[...].reshape(R, SWA_HD)
        dob = dof.astype(BF16)
        delta = jnp.sum(dof * o_ref[...].reshape(R, SWA_HD), axis=-1, keepdims=True)
        nt = (((1,), (1,)), ((), ()))
        tn = (((0,), (0,)), ((), ()))
        cur = pl.ds(pl.multiple_of(i * BLOCK, BLOCK), BLOCK)
        prev = pl.ds(pl.multiple_of(jnp.maximum(i - 1, 0) * BLOCK, BLOCK), BLOCK)
        meta = pl.ds(0, BLOCK)
        dq = jnp.zeros((R, SWA_HD), F32)
        for s, k_ref, v_ref, rows in ((s_c, kc_ref, vc_ref, cur), (s_p, kp_ref, vp_ref, prev), (s_m, km_ref, vm_ref, meta)):
            p = jnp.exp(s - lse)
            dp = lax.dot_general(dob, v_ref[...], nt, preferred_element_type=F32)
            ds = (p * (dp - delta)).astype(BF16)
            dq = dq + jnp.dot(ds, k_ref[...], preferred_element_type=F32)
            dv_ref[rows, :] += lax.dot_general(p.astype(BF16), dob, tn, preferred_element_type=F32)
            dk_ref[rows, :] += lax.dot_general(ds, q, tn, preferred_element_type=F32) * scale
        dq_ref[...] = (dq * scale).reshape(SWA_G, BLOCK, SWA_HD).astype(dq_ref.dtype)
        dsk = -jnp.exp(sink - lse) * delta
        dsink_ref[...] += jnp.sum(dsk.reshape(SWA_G, BLOCK, 1), axis=1, keepdims=True)

    return pl.pallas_call(
        kern, name=name, grid=(SWA_KVH, nb), in_specs=[qs, kc, kp, km, kc, kp, km, sk, qs, qs, ls],
        out_specs=[qs, full, full, sk],
        out_shape=[jax.ShapeDtypeStruct((SWA_KVH, SWA_G, L, SWA_HD), BF16), jax.ShapeDtypeStruct((SWA_KVH, L, SWA_HD), F32),
                   jax.ShapeDtypeStruct((SWA_KVH, L, SWA_HD), F32), jax.ShapeDtypeStruct((SWA_KVH, SWA_G, 1, 1), F32)],
        compiler_params=_params("parallel", "arbitrary"))(q, k, k, k, v, v, v, sinks, o, do, lse)


def _swa_attn(name, scale):
    @jax.custom_vjp
    def f(q, k, v, sinks):
        return _swa_fwd_call(q, k, v, sinks, scale, name + "_fwd")[0]

    def fwd(q, k, v, sinks):
        o, lse = _swa_fwd_call(q, k, v, sinks, scale, name + "_fwd")
        return o, (q, k, v, sinks, o, lse)

    def bwd(res, do):
        q, k, v, sinks, o, lse = res
        dq, dk, dv, dsink = _swa_bwd_call(q, k, v, sinks, o, lse, do, scale, name + "_bwd")
        return dq, dk.astype(BF16), dv.astype(BF16), dsink

    f.defvjp(fwd, bwd)
    return f


def _scan_call(x, bias, mul, pre_logsig, name):
    L, W = x.shape
    tb = _tile(L, ROW_TILE)
    has_mul = mul is not None

    def kern(x_ref, b_ref, *rest):
        if has_mul:
            m_ref, o_ref, tot_ref, carry = rest
        else:
            o_ref, tot_ref, carry = rest
        i = pl.program_id(0)

        @pl.when(i == 0)
        def _():
            carry[...] = jnp.zeros_like(carry)
            tot_ref[...] = jnp.zeros_like(tot_ref)

        z = x_ref[...] + b_ref[...]
        if pre_logsig:
            z = jnp.minimum(z, 0.0) - jnp.log(1.0 + jnp.exp(-jnp.abs(z)))
        row = lax.broadcasted_iota(jnp.int32, (tb, W), 0)
        s = 1
        while s < tb:
            z = z + jnp.where(row >= s, pltpu.roll(z, s, 0), 0.0)
            s *= 2
        z = z + carry[...]
        carry[...] = z[tb - 1:tb, :]
        if has_mul:
            z = z * m_ref[...]
        o_ref[...] = z
        tot_ref[...] += jnp.sum(z, axis=0, keepdims=True)

    row = pl.BlockSpec((tb, W), lambda i: (i, 0))
    vec = pl.BlockSpec((1, W), lambda i: (0, 0))
    return pl.pallas_call(
        kern, name=name, grid=(L // tb,), in_specs=[row, vec] + ([row] if has_mul else []), out_specs=[row, vec],
        out_shape=[jax.ShapeDtypeStruct((L, W), F32), jax.ShapeDtypeStruct((1, W), F32)],
        scratch_shapes=[pltpu.VMEM((1, W), F32)],
        compiler_params=_params("arbitrary"))(*([x, bias] + ([mul] if has_mul else [])))


def _sigmoid_neg_call(x, bias, name):
    L, W = x.shape
    tb = _tile(L, ROW_TILE)

    def kern(x_ref, b_ref, o_ref):
        row = pl.program_id(0) * tb + lax.broadcasted_iota(jnp.int32, (tb, 1), 0)
        o_ref[...] = jnp.where(row >= PAD, jax.nn.sigmoid(-(x_ref[...] + b_ref[...])), 0.0)

    row = pl.BlockSpec((tb, W), lambda i: (i, 0))
    vec = pl.BlockSpec((1, W), lambda i: (0, 0))
    return pl.pallas_call(kern, name=name, grid=(L // tb,), in_specs=[row, vec], out_specs=row,
                          out_shape=jax.ShapeDtypeStruct((L, W), F32), compiler_params=_params("parallel"))(x, bias)


def _decay(name):
    @jax.custom_vjp
    def f(fg, b):
        return _scan_call(fg, b, None, True, name + "_fwd")[0]

    def fwd(fg, b):
        return f(fg, b), (fg, b)

    def bwd(res, dc):
        fg, b = res
        sg = _sigmoid_neg_call(fg, b, name + "_dsig")
        dfg_rev, db = _scan_call(dc[::-1], jnp.zeros_like(b), sg[::-1], False, name + "_bwd")
        return dfg_rev[::-1], db

    f.defvjp(fwd, bwd)
    return f


def _loss_call(hf, target, name):
    L, D = hf.shape
    nb = L // BLOCK

    def kern(h_ref, t_ref, loss_ref, dy_ref, acc):
        i = pl.program_id(0)

        @pl.when(i == 0)
        def _():
            acc[...] = jnp.zeros_like(acc)
            dy_ref[...] = jnp.zeros_like(dy_ref)

        @pl.when(i > 0)
        def _():
            e = h_ref[...] - t_ref[...]
            dy_ref[...] = e * (1.0 / D)
            acc[...] += jnp.sum(e * e, axis=0, keepdims=True)

        @pl.when(i == nb - 1)
        def _():
            loss_ref[...] = jnp.broadcast_to(jnp.sum(acc[...], axis=1, keepdims=True) * (0.5 / D), loss_ref.shape)

    return pl.pallas_call(
        kern, name=name, grid=(nb,),
        in_specs=[pl.BlockSpec((BLOCK, D), lambda i: (i, 0)), pl.BlockSpec((BLOCK, D), lambda i: (jnp.maximum(i - 1, 0), 0))],
        out_specs=[pl.BlockSpec((1, LANES), lambda i: (0, 0)), pl.BlockSpec((BLOCK, D), lambda i: (i, 0))],
        out_shape=[jax.ShapeDtypeStruct((1, LANES), F32), jax.ShapeDtypeStruct((L, D), F32)],
        scratch_shapes=[pltpu.VMEM((1, D), F32)],
        compiler_params=_params("arbitrary"))(hf, target)


def _loss_head(name):
    @jax.custom_vjp
    def f(hf, target):
        return _loss_call(hf, target, name)[0][0, 0]

    def fwd(hf, target):
        loss, dy = _loss_call(hf, target, name)
        return loss[0, 0], (dy, jnp.zeros((), F32))

    def bwd(res, g):
        dy, _ = res
        return dy * g, None

    f.defvjp(fwd, bwd)
    return f


N_CHIPS = N_DEV // 2


def _sibling_swap(arrs, name):
    n = len(arrs)

    def body(*refs):
        srcs, outs = refs[:n], refs[n:2 * n]
        send_sems, recv_sems = refs[2 * n:]
        x, y, c = lax.axis_index("x"), lax.axis_index("y"), lax.axis_index("c")
        copies = [[pltpu.make_async_remote_copy(
            src_ref=srcs[a].at[2 * q + (1 - c)], dst_ref=outs[a].at[q], send_sem=send_sems.at[a, q],
            recv_sem=recv_sems.at[a, q], device_id=(x, y, 1 - c), device_id_type=MESH) for q in range(N_CHIPS)]
            for a in range(n)]
        for row in copies:
            for cp in row:
                cp.start()
        for row in copies:
            for cp in row:
                cp.wait()

    hbm = pl.BlockSpec(memory_space=pl.ANY)
    return pl.pallas_call(
        body, name=name, out_shape=[jax.ShapeDtypeStruct((N_CHIPS,) + tuple(a.shape[1:]), a.dtype) for a in arrs],
        in_specs=[hbm] * n, out_specs=[hbm] * n,
        scratch_shapes=[pltpu.SemaphoreType.DMA((n, N_CHIPS)), pltpu.SemaphoreType.DMA((n, N_CHIPS))],
    )(*arrs)


def _pair_sum_call(a, b, name):
    shape = a.shape
    cols = shape[-1]
    rows = int(np.prod(shape[:-1]))
    tr = rows
    if rows > 2 * ADAM_ROWS:
        tr = max(t for t in range(SUBLANES_BF16, 2 * ADAM_ROWS + 1, SUBLANES_BF16) if rows % t == 0)

    def kern(a_ref, b_ref, o_ref):
        o_ref[...] = (a_ref[...].astype(F32) + b_ref[...].astype(F32)).astype(o_ref.dtype)

    slab = pl.BlockSpec((tr, cols), lambda i: (i, 0))
    return pl.pallas_call(kern, name=name, grid=(rows // tr,), in_specs=[slab, slab], out_specs=slab,
                          out_shape=jax.ShapeDtypeStruct((rows, cols), a.dtype),
                          compiler_params=_params("parallel"))(a.reshape(rows, cols), b.reshape(rows, cols)).reshape(shape)


def _chip_exchange(arrs, name):
    n = len(arrs)

    def body(*refs):
        srcs, outs = refs[:n], refs[n:2 * n]
        send_sems, recv_sems, local_sems = refs[2 * n:]
        x, y, c = lax.axis_index("x"), lax.axis_index("y"), lax.axis_index("c")
        mine = 2 * x + y
        chips = [(1 - x, y), (x, 1 - y), (1 - x, 1 - y)]
        started = []
        for a in range(n):
            own = pltpu.make_async_copy(srcs[a].at[mine], outs[a].at[mine], local_sems.at[a])
            own.start()
            started.append(own.wait)
        for k, (px, py) in enumerate(chips):
            for a in range(n):
                cp = pltpu.make_async_remote_copy(src_ref=srcs[a].at[2 * px + py], dst_ref=outs[a].at[mine],
                                                  send_sem=send_sems.at[a, k], recv_sem=recv_sems.at[a, k],
                                                  device_id=(px, py, c), device_id_type=MESH)
                cp.start()
                started.append(cp.wait_send)
        for k, (px, py) in enumerate(chips):
            for a in range(n):
                pltpu.make_async_remote_copy(src_ref=srcs[a].at[mine], dst_ref=outs[a].at[2 * px + py],
                                             send_sem=send_sems.at[a, k], recv_sem=recv_sems.at[a, k],
                                             device_id=(px, py, c), device_id_type=MESH).wait_recv()
        for wait in started:
            wait()

    hbm = pl.BlockSpec(memory_space=pl.ANY)
    return pl.pallas_call(
        body, name=name, out_shape=[jax.ShapeDtypeStruct(a.shape, a.dtype) for a in arrs],
        in_specs=[hbm] * n, out_specs=[hbm] * n,
        scratch_shapes=[pltpu.SemaphoreType.DMA((n, N_CHIPS - 1)), pltpu.SemaphoreType.DMA((n, N_CHIPS - 1)),
                        pltpu.SemaphoreType.DMA((n,))],
    )(*arrs)


def _scatter(arrs, names):
    c = lax.axis_index("c")
    theirs = _sibling_swap(arrs, "scatter_swap")
    halves = [lax.dynamic_index_in_dim(a.reshape((N_CHIPS, 2) + a.shape[1:]), c, axis=1, keepdims=False) for a in arrs]
    sums = [_pair_sum_call(h, t, "pair_sum_" + nm) for h, t, nm in zip(halves, theirs, names)]
    return _chip_exchange(sums, "scatter_chips")


def _gather(arrs, name):
    n = len(arrs)
    hops = N_DEV - 1

    def body(*refs):
        srcs, outs = refs[:n], refs[n:2 * n]
        send_sems, recv_sems, local_sems = refs[2 * n:]
        x, y, c = lax.axis_index("x"), lax.axis_index("y"), lax.axis_index("c")
        ident = lambda px, py, pc: 4 * px + 2 * py + pc
        me, sibling = ident(x, y, c), (x, y, 1 - c)
        chips = [(1 - x, y), (x, 1 - y), (1 - x, 1 - y)]

        def copy(a, k, block, to, src=None):
            return pltpu.make_async_remote_copy(
                src_ref=outs[a].at[block] if src is None else src, dst_ref=outs[a].at[block],
                send_sem=send_sems.at[a, k], recv_sem=recv_sems.at[a, k], device_id=to, device_id_type=MESH)

        started = []
        for a in range(n):
            mine = pltpu.make_async_copy(srcs[a], outs[a].at[me], local_sems.at[a])
            mine.start()
            started.append(mine.wait)
            for k, to in enumerate([sibling] + [(px, py, c) for px, py in chips]):
                cp = copy(a, k, me, to, src=srcs[a])
                cp.start()
                started.append(cp.wait_send)
        for j, (px, py) in enumerate(chips):
            for a in range(n):
                copy(a, 1 + j, ident(px, py, c), (x, y, c)).wait_recv()
                cp = copy(a, 4 + j, ident(px, py, c), sibling)
                cp.start()
                started.append(cp.wait_send)
        for a in range(n):
            copy(a, 0, ident(x, y, 1 - c), (x, y, c)).wait_recv()
            for j, (px, py) in enumerate(chips):
                copy(a, 4 + j, ident(px, py, 1 - c), (x, y, c)).wait_recv()
        for wait in started:
            wait()

    hbm = pl.BlockSpec(memory_space=pl.ANY)
    return pl.pallas_call(
        body, name=name, out_shape=[jax.ShapeDtypeStruct((N_DEV,) + tuple(a.shape), a.dtype) for a in arrs],
        in_specs=[hbm] * n, out_specs=[hbm] * n,
        scratch_shapes=[pltpu.SemaphoreType.DMA((n, hops)), pltpu.SemaphoreType.DMA((n, hops)),
                        pltpu.SemaphoreType.DMA((n,))],
    )(*arrs)


def _adamw_call(parts, w, m, v, name):
    shape = w.shape
    nparts = parts.shape[0]
    cols = shape[-1]
    rows = int(np.prod(shape[:-1]))
    tr = rows
    if rows > ADAM_ROWS:
        tr = max(t for t in range(SUBLANES_F32, ADAM_ROWS + 1, SUBLANES_F32) if rows % t == 0)
    c1 = 1.0 / (1.0 - ADAM_B1 ** ADAM_STEP)
    c2 = 1.0 / (1.0 - ADAM_B2 ** ADAM_STEP)

    def kern(p_ref, w_ref, m_ref, v_ref, g_ref, d_ref, nm_ref, nv_ref):
        g = p_ref[0].astype(F32)
        for p in range(1, nparts):
            g = g + p_ref[p].astype(F32)
        nm = ADAM_B1 * m_ref[...] + (1.0 - ADAM_B1) * g
        nv = ADAM_B2 * v_ref[...] + (1.0 - ADAM_B2) * (g * g)
        g_ref[...] = g
        nm_ref[...] = nm
        nv_ref[...] = nv
        d_ref[...] = -ADAM_LR * ((nm * c1) / (jnp.sqrt(nv * c2) + ADAM_EPS) + ADAM_WD * w_ref[...])

    slab = pl.BlockSpec((tr, cols), lambda i: (i, 0))
    flat = lambda t: t.reshape(rows, cols)
    res = pl.pallas_call(
        kern, name=name, grid=(rows // tr,),
        in_specs=[pl.BlockSpec((nparts, tr, cols), lambda i: (0, i, 0)), slab, slab, slab], out_specs=[slab] * 4,
        out_shape=[jax.ShapeDtypeStruct((rows, cols), F32)] * 4,
        compiler_params=_params("parallel"))(parts.reshape(nparts, rows, cols), flat(w), flat(m), flat(v))
    return [r.reshape(shape) for r in res]


def _to_full(stacked, axis):
    moved = jnp.moveaxis(stacked, 0, axis)
    s = list(moved.shape)
    return moved.reshape(s[:axis] + [s[axis] * s[axis + 1]] + s[axis + 2:])


def _heads(x, h, d):
    return x.reshape(x.shape[0], h, d).transpose(1, 0, 2)


def _unheads(x):
    return x.transpose(1, 0, 2).reshape(x.shape[1], -1)


def _pad_cols(x, width):
    return jnp.pad(x, ((0, 0), (0, width - x.shape[1])))


def _fox_mixer(h, w_in, b_f, w_o, tag):
    hd = FOX_HEADS * FOX_HD
    qkvT = _linear_t(tag + "_qkv", BF16)(w_in[:, :3 * hd].T, h)
    fg = _linear(tag + "_gate", F32)(h, _pad_cols(w_in[:, 3 * hd:], LANES))
    c = _decay(tag + "_decay")(fg, _pad_cols(b_f[None, :], LANES))
    oT = _fox_attention(tag + "_attn")(qkvT, c[:, :FOX_HEADS].T)
    return _linear_km(tag + "_out")(oT, w_o)


def _swa_mixer(h, w_in, sinks, w_o, tabs, tag):
    L = h.shape[0]
    qd, kd = SWA_QH * SWA_HD, SWA_KVH * SWA_HD
    proj = _linear(tag + "_qkv", F32)(h, w_in)
    rope = _rope(tag + "_rope", ROPE_DIM // 2, BF16)
    q = rope(proj[:, :qd], *tabs)
    k = rope(proj[:, qd:qd + kd], *tabs)
    v = proj[:, qd + kd:].astype(BF16)
    qg = _heads(q, SWA_QH, SWA_HD).reshape(SWA_KVH, SWA_G, L, SWA_HD)
    o = _swa_attn(tag + "_attn", SWA_HD ** -0.5)(qg, _heads(k, SWA_KVH, SWA_HD), _heads(v, SWA_KVH, SWA_HD),
                                                 sinks.reshape(SWA_KVH, SWA_G, 1, 1))
    return _linear(tag + "_out", F32)(_unheads(o.reshape(SWA_QH, L, SWA_HD)), w_o)


def _mla_mixer(h, w_a, g_q, g_kv, w_uq, w_ukv, w_o, tabs, tag):
    L = h.shape[0]
    cq = _linear(tag + "_aq", F32)(h, w_a[:, :MLA_QL])
    ckv = _linear(tag + "_akv", F32)(h, w_a[:, MLA_QL:MLA_QL + MLA_KVL])
    kr = _linear(tag + "_akr", F32)(h, _pad_cols(w_a[:, MLA_QL + MLA_KVL:], LANES))
    cq = _rmsnorm(tag + "_nq")(cq, g_q[None, :])
    ckv = _rmsnorm(tag + "_nkv")(ckv, g_kv[None, :])
    q = _linear(tag + "_uq", F32)(cq, w_uq).reshape(L, MLA_HEADS, MLA_NOPE + MLA_ROPE)
    kv = _linear(tag + "_ukv", BF16)(ckv, w_ukv).reshape(L, MLA_HEADS, MLA_NOPE + MLA_V)
    scale = (MLA_NOPE + MLA_ROPE) ** -0.5
    rope = _rope(tag + "_rope", MLA_ROPE // 2, BF16)
    q_rope = rope(q[:, :, MLA_NOPE:].reshape(L, MLA_HEADS * MLA_ROPE), *(t * scale for t in tabs))
    k_rope = rope(kr, *tabs)[:, :MLA_ROPE]
    qf = jnp.concatenate([(q[:, :, :MLA_NOPE] * scale).astype(BF16), q_rope.reshape(L, MLA_HEADS, MLA_ROPE)], axis=-1)
    kf = jnp.concatenate([kv[:, :, :MLA_NOPE], jnp.broadcast_to(k_rope[:, None, :], (L, MLA_HEADS, MLA_ROPE))], axis=-1)
    seq_last = lambda t: t.transpose(1, 2, 0)
    unit = jnp.zeros((MLA_HEADS, MLA_XROWS, L), BF16).at[:, 0, :].set(1.0)
    poison = jnp.zeros((MLA_HEADS, MLA_XROWS, L), BF16).at[:, 0, :].set(jnp.where(jnp.arange(L) < PAD, -POISON, 0.0))
    qT = jnp.concatenate([seq_last(qf), unit], axis=1).reshape(-1, L)
    kT = jnp.concatenate([seq_last(kf), poison], axis=1).reshape(-1, L)
    oT = _mla_attention(tag + "_attn")(qT, kT, seq_last(kv[:, :, MLA_NOPE:]).reshape(-1, L))
    return _linear_km(tag + "_out")(oT, w_o)


BIG = (("fox_w_in", 2), ("fox_w_o", 1), ("swa_w_in", 2), ("swa_w_o", 1), ("mla_w_a", 1), ("mla_w_uq", 2),
       ("mla_w_ukv", 2), ("mla_w_o", 1), ("ffn_w_in", 2), ("ffn_w_out", 1))
SMALL = (("meta_tokens", 1), ("mla_g_q", 1), ("mla_g_kv", 1), ("ffn_conv_w", 2))
SHARDED = BIG + SMALL
REPL = ("ln1_g", "ln1_b", "ln2_g", "ln2_b", "fox_b_f", "swa_sinks", "ffn_conv_b")
FFN_STACKED = ("ffn_w_in", "ffn_w_out", "ffn_conv_w")
WEIGHTS = ("meta_tokens", "ln1_g", "ln1_b", "ln2_g", "ln2_b", "fox_w_in", "fox_b_f", "fox_w_o", "swa_w_in", "swa_sinks",
           "swa_w_o", "mla_w_a", "mla_g_q", "mla_g_kv", "mla_w_uq", "mla_w_ukv", "mla_w_o", "ffn_w_in", "ffn_conv_w",
           "ffn_conv_b", "ffn_w_out")


def _local_loss(stacked, repl, x, target):
    S, D = x.shape
    L = S + BLOCK
    wts = {n: _to_full(stacked[n], ax) for n, ax in SHARDED if n not in FFN_STACKED}
    wts.update(repl)
    half = N_DEV // 2
    h = jnp.concatenate([jnp.zeros((PAD, D), F32), wts["meta_tokens"], x], axis=0)
    cos_p, sin_p = _rope_tables(L, ROPE_DIM, ROPE_THETA)
    tabs_p = _rope_lanes(cos_p, sin_p, SWA_HD)
    cos_m, sin_m = _rope_tables(L, MLA_ROPE, MLA_ROPE_THETA)
    tabs_m = _rope_lanes(cos_m, sin_m, MLA_ROPE)
    for i in range(DEPTH):
        kind, j = i % 3, i // 3
        tag = f"l{i}"
        if kind == 0:
            mix = _fox_mixer(h, wts["fox_w_in"][j], wts["fox_b_f"][j], wts["fox_w_o"][j], tag + "_fox")
        elif kind == 1:
            mix = _swa_mixer(h, wts["swa_w_in"][j], wts["swa_sinks"][j], wts["swa_w_o"][j], tabs_p, tag + "_swa")
        else:
            mix = _mla_mixer(h, wts["mla_w_a"][j], wts["mla_g_q"][j], wts["mla_g_kv"][j], wts["mla_w_uq"][j],
                             wts["mla_w_ukv"][j], wts["mla_w_o"][j], tabs_m, tag + "_mla")
        h = _deepnorm(tag + "_ln1")(h, mix, wts["ln1_g"][i][None, :], wts["ln1_b"][i][None, :])
        w3 = stacked["ffn_w_in"][:, i]
        shard = w3.shape[-1]
        cw4 = stacked["ffn_conv_w"][:, i].reshape(2, half, CONV_W, shard)
        cb4 = wts["ffn_conv_b"][i].reshape(2, half, 1, shard)
        wo3 = stacked["ffn_w_out"][:, i].reshape(half, shard, D)
        act3 = _ffn_up(tag + "_ffn")(h, w3, cw4, cb4)
        ffn = _ffn_down(tag + "_ffn_down")(act3, wo3)
        h = _deepnorm(tag + "_ln2")(h, ffn, wts["ln2_g"][i][None, :], wts["ln2_b"][i][None, :])
    return _loss_head("loss_head")(h, target)


def _step(x, target, w, m, v):
    names = [n for n, _ in SHARDED]
    big = {n for n, _ in BIG}
    stacked = dict(zip(names, _gather([w[n].astype(BF16) if n in big else w[n] for n in names], "gather_weights")))
    repl = {n: w[n] for n in REPL}
    loss, (g_st, g_repl, grad_x) = jax.value_and_grad(_local_loss, argnums=(0, 1, 2))(stacked, repl, x, target)
    got = dict(zip(names, _scatter([g_st[n] for n in names], names)))
    got.update(zip(REPL, _gather([g_repl[n] for n in REPL], "gather_repl_grads")))
    out = {}
    for n in WEIGHTS:
        for kind, a in zip(("grad", "delta", "new_m", "new_v"), _adamw_call(got[n], w[n], m[n], v[n], "adamw_" + n)):
            out[(kind, n)] = a
    return loss, grad_x, out


def kernel(x, meta_tokens, ln1_g, ln1_b, ln2_g, ln2_b, fox_w_in, fox_b_f, fox_w_o, swa_w_in, swa_sinks, swa_w_o, mla_w_a, mla_g_q, mla_g_kv, mla_w_uq, mla_w_ukv, mla_w_o, ffn_w_in, ffn_conv_w, ffn_conv_b, ffn_w_out, loss_target, m_meta_tokens, m_ln1_g, m_ln1_b, m_ln2_g, m_ln2_b, m_fox_w_in, m_fox_b_f, m_fox_w_o, m_swa_w_in, m_swa_sinks, m_swa_w_o, m_mla_w_a, m_mla_g_q, m_mla_g_kv, m_mla_w_uq, m_mla_w_ukv, m_mla_w_o, m_ffn_w_in, m_ffn_conv_w, m_ffn_conv_b, m_ffn_w_out, v_meta_tokens, v_ln1_g, v_ln1_b, v_ln2_g, v_ln2_b, v_fox_w_in, v_fox_b_f, v_fox_w_o, v_swa_w_in, v_swa_sinks, v_swa_w_o, v_mla_w_a, v_mla_g_q, v_mla_g_kv, v_mla_w_uq, v_mla_w_ukv, v_mla_w_o, v_ffn_w_in, v_ffn_conv_w, v_ffn_conv_b, v_ffn_w_out):
    args = (meta_tokens, ln1_g, ln1_b, ln2_g, ln2_b, fox_w_in, fox_b_f, fox_w_o, swa_w_in, swa_sinks, swa_w_o, mla_w_a,
            mla_g_q, mla_g_kv, mla_w_uq, mla_w_ukv, mla_w_o, ffn_w_in, ffn_conv_w, ffn_conv_b, ffn_w_out)
    ms = (m_meta_tokens, m_ln1_g, m_ln1_b, m_ln2_g, m_ln2_b, m_fox_w_in, m_fox_b_f, m_fox_w_o, m_swa_w_in, m_swa_sinks,
          m_swa_w_o, m_mla_w_a, m_mla_g_q, m_mla_g_kv, m_mla_w_uq, m_mla_w_ukv, m_mla_w_o, m_ffn_w_in, m_ffn_conv_w,
          m_ffn_conv_b, m_ffn_w_out)
    vs = (v_meta_tokens, v_ln1_g, v_ln1_b, v_ln2_g, v_ln2_b, v_fox_w_in, v_fox_b_f, v_fox_w_o, v_swa_w_in, v_swa_sinks,
          v_swa_w_o, v_mla_w_a, v_mla_g_q, v_mla_g_kv, v_mla_w_uq, v_mla_w_ukv, v_mla_w_o, v_ffn_w_in, v_ffn_conv_w,
          v_ffn_conv_b, v_ffn_w_out)
    w = dict(zip(WEIGHTS, args))
    m = dict(zip(WEIGHTS, ms))
    v = dict(zip(WEIGHTS, vs))
    loss, grad_x, out = _step(x[0], loss_target[0], w, m, v)
    loss = lax.psum(loss, ("x", "y", "c"))
    res = [loss, grad_x[None]]
    for kind in ("grad", "delta", "new_m", "new_v"):
        res += [out[(kind, n)] for n in WEIGHTS]
    return tuple(res)
```

```python
import jax
import jax.numpy as jnp
import numpy as np
from jax import lax
from jax.experimental import pallas as pl
from jax.experimental.pallas import tpu as pltpu

F32 = jnp.float32
BF16 = jnp.bfloat16

DEPTH = 4
N_META = 16
BLOCK = 128
PAD = BLOCK - N_META
NEG = -1e30
ALPHA = (2.0 * DEPTH) ** 0.25
LN_EPS = 1e-5
RMS_EPS = 1e-6
FOX_HEADS, FOX_HD = 16, 64
SWA_QH, SWA_KVH, SWA_HD = 16, 2, 64
SWA_G = SWA_QH // SWA_KVH
ROPE_THETA = 500000.0
ROPE_DIM = SWA_HD // 4
MLA_HEADS, MLA_QL, MLA_KVL, MLA_NOPE, MLA_ROPE, MLA_V = 16, 384, 256, 64, 32, 64
MLA_ROPE_THETA = 10000.0
CONV_W = 3
ADAM_LR, ADAM_B1, ADAM_B2, ADAM_EPS, ADAM_WD, ADAM_STEP = 0.001, 0.9, 0.999, 1e-08, 0.01, 10

LANES = 128
SUBLANES_F32 = 8
SUBLANES_BF16 = 16
VMEM_LIMIT = 48 * 1024 * 1024
N_DEV = 8
MESH = pl.DeviceIdType.MESH
ROW_TILE = 640
ADAM_ROWS = 256
MM_TILE_M, MM_TILE_N, MM_TILE_K = 1664, 1024, 1664
ATTN_HEADS_FWD, ATTN_HEADS_BWD = 16, 4


def _tile(n, cap):
    if n <= cap:
        return n
    best = 0
    for t in range(LANES, cap + 1, LANES):
        if n % t == 0:
            best = t
    assert best, (n, cap)
    return best


def _params(*sem):
    return pltpu.CompilerParams(dimension_semantics=sem, vmem_limit_bytes=VMEM_LIMIT)


def _mm(a, b, *, la="mk", lb="kn", sa=None, sb=None, so=None, out_dtype, name):
    size, tile, parts = {}, {}, {}
    for x, lay, split in ((a, la, sa), (b, lb, sb)):
        shp = x.shape[1:] if split else x.shape
        for ax, n in zip(lay, shp):
            if ax == split:
                size[ax], tile[ax], parts[ax] = x.shape[0] * n, n, x.shape[0]
            else:
                assert size.setdefault(ax, n) == n, (name, ax, a.shape, b.shape)
    for ax, cap in (("m", MM_TILE_M), ("n", MM_TILE_N), ("k", MM_TILE_K)):
        tile.setdefault(ax, _tile(size[ax], cap))
    grid = tuple(size[ax] // tile[ax] for ax in "mnk")
    nk = grid[2]
    dn = (((la.index("k"),), (lb.index("k"),)), ((), ()))

    def kern(a_ref, b_ref, o_ref, *acc):
        p = lax.dot_general(a_ref[...].astype(BF16), b_ref[...].astype(BF16), dn, preferred_element_type=F32)
        if nk == 1:
            o_ref[...] = p.astype(o_ref.dtype)
            return
        acc_ref, = acc
        k = pl.program_id(2)

        @pl.when(k == 0)
        def _():
            acc_ref[...] = p

        @pl.when(k > 0)
        def _():
            acc_ref[...] += p

        @pl.when(k == nk - 1)
        def _():
            o_ref[...] = acc_ref[...].astype(o_ref.dtype)

    def spec(lay, split):
        blk = tuple(tile[ax] for ax in lay)
        if split is None:
            return pl.BlockSpec(blk, lambda i, j, k: tuple({"m": i, "n": j, "k": k}[ax] for ax in lay))
        return pl.BlockSpec((None,) + blk, lambda i, j, k: ({"m": i, "n": j, "k": k}[split],) + tuple(
            0 if ax == split else {"m": i, "n": j, "k": k}[ax] for ax in lay))

    if so is None:
        out_shape = (size["m"], size["n"])
    else:
        out_shape = (parts[so],) + tuple(tile[ax] if ax == so else size[ax] for ax in "mn")
    return pl.pallas_call(
        kern, name=name, grid=grid,
        in_specs=[spec(la, sa), spec(lb, sb)], out_specs=spec("mn", so),
        out_shape=jax.ShapeDtypeStruct(out_shape, out_dtype),
        scratch_shapes=[] if nk == 1 else [pltpu.VMEM((tile["m"], tile["n"]), F32)],
        compiler_params=_params("parallel", "parallel", "arbitrary"),
    )(a, b)


def _linear(name, out_dtype):
    @jax.custom_vjp
    def f(x, w):
        return _mm(x, w, out_dtype=out_dtype, name=name + "_fwd")

    def fwd(x, w):
        return f(x, w), (x, w)

    def bwd(res, g):
        x, w = res
        dx = _mm(g, w, lb="nk", out_dtype=x.dtype, name=name + "_dx")
        dw = _mm(x, g, la="km", out_dtype=w.dtype, name=name + "_dw")
        return dx, dw

    f.defvjp(fwd, bwd)
    return f


def _ln_stats(z):
    mu = jnp.mean(z, axis=-1, keepdims=True)
    zc = z - mu
    var = jnp.mean(zc * zc, axis=-1, keepdims=True)
    return zc, lax.rsqrt(var + LN_EPS)


def _ln_fwd_call(h, mix, g, b, name):
    L, D = h.shape
    tm = _tile(L, ROW_TILE)

    def kern(h_ref, m_ref, g_ref, b_ref, o_ref):
        zc, rstd = _ln_stats(ALPHA * h_ref[...] + m_ref[...])
        o_ref[...] = zc * rstd * g_ref[...] + b_ref[...]

    row = pl.BlockSpec((tm, D), lambda i: (i, 0))
    vec = pl.BlockSpec((1, D), lambda i: (0, 0))
    return pl.pallas_call(kern, name=name, grid=(L // tm,), in_specs=[row, row, vec, vec], out_specs=row,
                          out_shape=jax.ShapeDtypeStruct((L, D), F32), compiler_params=_params("parallel"))(h, mix, g, b)


def _ln_bwd_call(h, mix, g, dout, name):
    L, D = h.shape
    tm = _tile(L, ROW_TILE)

    def kern(h_ref, m_ref, g_ref, d_ref, dz_ref, dg_ref, db_ref):
        i = pl.program_id(0)
        zc, rstd = _ln_stats(ALPHA * h_ref[...] + m_ref[...])
        xhat = zc * rstd
        d = d_ref[...]
        dxh = d * g_ref[...]
        m1 = jnp.mean(dxh, axis=-1, keepdims=True)
        m2 = jnp.mean(dxh * xhat, axis=-1, keepdims=True)
        row = i * tm + lax.broadcasted_iota(jnp.int32, (tm, 1), 0)
        dz_ref[...] = jnp.where(row >= PAD, rstd * (dxh - m1 - xhat * m2), 0.0)
        pg = jnp.sum(d * xhat, axis=0, keepdims=True)
        pb = jnp.sum(d, axis=0, keepdims=True)

        @pl.when(i == 0)
        def _():
            dg_ref[...] = pg
            db_ref[...] = pb

        @pl.when(i > 0)
        def _():
            dg_ref[...] += pg
            db_ref[...] += pb

    row = pl.BlockSpec((tm, D), lambda i: (i, 0))
    vec = pl.BlockSpec((1, D), lambda i: (0, 0))
    return pl.pallas_call(
        kern, name=name, grid=(L // tm,), in_specs=[row, row, vec, row], out_specs=[row, vec, vec],
        out_shape=[jax.ShapeDtypeStruct((L, D), F32), jax.ShapeDtypeStruct((1, D), F32), jax.ShapeDtypeStruct((1, D), F32)],
        compiler_params=_params("arbitrary"))(h, mix, g, dout)


def _deepnorm(name):
    @jax.custom_vjp
    def f(h, mix, g, b):
        return _ln_fwd_call(h, mix, g, b, name + "_fwd")

    def fwd(h, mix, g, b):
        return f(h, mix, g, b), (h, mix, g)

    def bwd(res, dout):
        h, mix, g = res
        dz, dg, db = _ln_bwd_call(h, mix, g, dout, name + "_bwd")
        return ALPHA * dz, dz, dg, db

    f.defvjp(fwd, bwd)
    return f


def _rms_fwd_call(x, g, name):
    L, n = x.shape
    tm = _tile(L, ROW_TILE)

    def kern(x_ref, g_ref, o_ref):
        x = x_ref[...]
        o_ref[...] = x * lax.rsqrt(jnp.mean(x * x, axis=-1, keepdims=True) + RMS_EPS) * g_ref[...]

    row = pl.BlockSpec((tm, n), lambda i: (i, 0))
    vec = pl.BlockSpec((1, n), lambda i: (0, 0))
    return pl.pallas_call(kern, name=name, grid=(L // tm,), in_specs=[row, vec], out_specs=row,
                          out_shape=jax.ShapeDtypeStruct((L, n), F32), compiler_params=_params("parallel"))(x, g)


def _rms_bwd_call(x, g, dout, name):
    L, n = x.shape
    tm = _tile(L, ROW_TILE)

    def kern(x_ref, g_ref, d_ref, dx_ref, dg_ref):
        i = pl.program_id(0)
        x = x_ref[...]
        rstd = lax.rsqrt(jnp.mean(x * x, axis=-1, keepdims=True) + RMS_EPS)
        xhat = x * rstd
        d = d_ref[...]
        dxh = d * g_ref[...]
        dx_ref[...] = rstd * (dxh - xhat * jnp.mean(dxh * xhat, axis=-1, keepdims=True))
        pg = jnp.sum(d * xhat, axis=0, keepdims=True)

        @pl.when(i == 0)
        def _():
            dg_ref[...] = pg

        @pl.when(i > 0)
        def _():
            dg_ref[...] += pg

    row = pl.BlockSpec((tm, n), lambda i: (i, 0))
    vec = pl.BlockSpec((1, n), lambda i: (0, 0))
    return pl.pallas_call(
        kern, name=name, grid=(L // tm,), in_specs=[row, vec, row], out_specs=[row, vec],
        out_shape=[jax.ShapeDtypeStruct((L, n), F32), jax.ShapeDtypeStruct((1, n), F32)],
        compiler_params=_params("arbitrary"))(x, g, dout)


def _rmsnorm(name):
    @jax.custom_vjp
    def f(x, g):
        return _rms_fwd_call(x, g, name + "_fwd")

    def fwd(x, g):
        return f(x, g), (x, g)

    def bwd(res, dout):
        x, g = res
        dx, dg = _rms_bwd_call(x, g, dout, name + "_bwd")
        return dx, dg

    f.defvjp(fwd, bwd)
    return f


def _rope_call(x, c, s1, s2, r, out_dtype, name):
    L, W = x.shape
    reps = W // LANES
    tm = _tile(L, ROW_TILE)

    def kern(x_ref, c_ref, s1_ref, s2_ref, o_ref):
        x = x_ref[...].astype(F32)
        wide = lambda t: jnp.tile(t[...], (1, reps)) if reps > 1 else t[...]
        out = x * wide(c_ref) + pltpu.roll(x, W - r, 1) * wide(s1_ref) + pltpu.roll(x, r, 1) * wide(s2_ref)
        o_ref[...] = out.astype(o_ref.dtype)

    row = pl.BlockSpec((tm, W), lambda i: (i, 0))
    tab = pl.BlockSpec((tm, LANES), lambda i: (i, 0))
    return pl.pallas_call(kern, name=name, grid=(L // tm,), in_specs=[row, tab, tab, tab], out_specs=row,
                          out_shape=jax.ShapeDtypeStruct((L, W), out_dtype), compiler_params=_params("parallel"))(x, c, s1, s2)


def _rope(name, r, out_dtype):
    @jax.custom_vjp
    def f(x, c, s1, s2):
        return _rope_call(x, c, s1, s2, r, out_dtype, name + "_fwd")

    def fwd(x, c, s1, s2):
        return f(x, c, s1, s2), (c, s1, s2, jnp.zeros((), x.dtype))

    def bwd(res, g):
        c, s1, s2, proto = res
        dx = _rope_call(g, c, -s1, -s2, r, proto.dtype, name + "_bwd")
        return dx, jnp.zeros_like(c), jnp.zeros_like(s1), jnp.zeros_like(s2)

    f.defvjp(fwd, bwd)
    return f


def _rope_tables(L, dim, theta):
    pos = (jnp.arange(L) - PAD).astype(F32)
    inv = theta ** (-jnp.arange(0, dim, 2, dtype=F32) / dim)
    ang = pos[:, None] * inv[None, :]
    return jnp.cos(ang), jnp.sin(ang)


def _rope_lanes(cos, sin, period):
    L, half = cos.shape
    one = jnp.ones((L, period - 2 * half), F32)
    zero = jnp.zeros((L, period - 2 * half), F32)
    z_h = jnp.zeros((L, half), F32)
    c = jnp.concatenate([cos, cos, one], axis=1)
    s1 = jnp.concatenate([-sin, z_h, zero], axis=1)
    s2 = jnp.concatenate([z_h, sin, zero], axis=1)
    reps = LANES // period
    return tuple(jnp.tile(t, (1, reps)) for t in (c, s1, s2))


HALO = SUBLANES_BF16


GLU_ROWS = 128


def _strips(C):
    return [(c0, min(LANES, C - c0)) for c0 in range(0, C, LANES)]


def _shifted_down(win):
    return [pltpu.roll(win, CONV_W - 1 - t, 0)[SUBLANES_F32:, :] if t < CONV_W - 1 else win[SUBLANES_F32:, :]
            for t in range(CONV_W)]


def _conv_taps(buf_ref, g, off, rows, cols, w, b):
    u = _shifted_down(buf_ref[g, pl.ds(off - SUBLANES_F32, rows + SUBLANES_F32), cols])
    return b + u[0] * w[0:1, :] + u[1] * w[1:2, :] + u[2] * w[2:3, :]


def _glu_fwd_call(u4, cw4, cb4, name):
    _, P, L, C = u4.shape
    tm = _tile(L, ROW_TILE)
    hb = tm // HALO

    def kern(u_ref, up_ref, w_ref, b_ref, o_ref, buf):
        i = pl.program_id(1)
        row = i * tm + lax.broadcasted_iota(jnp.int32, (tm, 1), 0)
        prow = i * tm - HALO + lax.broadcasted_iota(jnp.int32, (HALO, 1), 0)
        for g in range(2):
            buf[g, pl.ds(HALO, tm), :] = jnp.where(row >= PAD, u_ref[g], 0.0)
            buf[g, pl.ds(0, HALO), :] = jnp.where(prow >= PAD, up_ref[g], 0.0)
        for c0, wd in _strips(C):
            cols = pl.ds(c0, wd)
            ws = [w_ref[g][:, c0:c0 + wd] for g in range(2)]
            bs = [b_ref[g][:, c0:c0 + wd] for g in range(2)]
            for r in range(0, tm, GLU_ROWS):
                n = min(GLU_ROWS, tm - r)
                gate, val = (_conv_taps(buf, g, HALO + r, n, cols, ws[g], bs[g]) for g in range(2))
                o_ref[pl.ds(r, n), cols] = (gate * jax.nn.sigmoid(gate) * val).astype(o_ref.dtype)

    return pl.pallas_call(
        kern, name=name, grid=(P, L // tm),
        in_specs=[pl.BlockSpec((2, None, tm, C), lambda p, i: (0, p, i, 0)),
                  pl.BlockSpec((2, None, HALO, C), lambda p, i: (0, p, jnp.maximum(i * hb - 1, 0), 0)),
                  pl.BlockSpec((2, None, CONV_W, C), lambda p, i: (0, p, 0, 0)),
                  pl.BlockSpec((2, None, 1, C), lambda p, i: (0, p, 0, 0))],
        out_specs=pl.BlockSpec((None, tm, C), lambda p, i: (p, i, 0)),
        out_shape=jax.ShapeDtypeStruct((P, L, C), BF16),
        scratch_shapes=[pltpu.VMEM((2, tm + HALO, C), F32)],
        compiler_params=_params("parallel", "parallel"))(u4, u4, cw4, cb4)


def _glu_bwd_call(u4, cw4, cb4, dact, name):
    _, P, L, C = u4.shape
    tm = _tile(L, ROW_TILE)
    hb = tm // HALO
    ext = tm + HALO
    last_halo = L // HALO - 1

    def kern(u_ref, up_ref, un_ref, w_ref, b_ref, d_ref, dn_ref, du_ref, dw_ref, db_ref, ubuf, dbuf):
        i = pl.program_id(1)
        r0 = i * tm
        mask = lambda blk, start: jnp.where(
            (start + lax.broadcasted_iota(jnp.int32, (blk.shape[0], 1), 0) >= PAD), blk, 0.0)
        for g in range(2):
            ubuf[g, pl.ds(0, HALO), :] = mask(up_ref[g], r0 - HALO)
            ubuf[g, pl.ds(HALO, tm), :] = mask(u_ref[g], r0)
            ubuf[g, pl.ds(HALO + tm, HALO), :] = un_ref[g]
        dbuf[pl.ds(0, tm), :] = d_ref[...].astype(F32)
        dbuf[pl.ds(tm, HALO), :] = jnp.where(r0 + tm < L, dn_ref[...].astype(F32), 0.0)
        for c0, wd in _strips(C):
            cols = pl.ds(c0, wd)
            ws = [w_ref[g][:, c0:c0 + wd] for g in range(2)]
            bs = [b_ref[g][:, c0:c0 + wd] for g in range(2)]
            pw = [[jnp.zeros((1, wd), F32) for _ in range(CONV_W)] for _ in range(2)]
            pb = [jnp.zeros((1, wd), F32) for _ in range(2)]
            for r in range(0, tm, GLU_ROWS):
                n = min(GLU_ROWS, tm - r)
                ne = n + SUBLANES_F32
                us = [_shifted_down(ubuf[g, pl.ds(HALO + r - SUBLANES_F32, ne + SUBLANES_F32), cols]) for g in range(2)]
                gate, val = (bs[g] + us[g][0] * ws[g][0:1, :] + us[g][1] * ws[g][1:2, :] + us[g][2] * ws[g][2:3, :]
                             for g in range(2))
                sg = jax.nn.sigmoid(gate)
                d = dbuf[pl.ds(r, ne), cols]
                dys = (d * val * (sg * (1.0 + gate * (1.0 - sg))), d * (gate * sg))
                row = r0 + r + lax.broadcasted_iota(jnp.int32, (n, 1), 0)
                for g in range(2):
                    w = ws[g]
                    dy = dys[g][:n, :]
                    du = (dy * w[2:3, :] + pltpu.roll(dys[g], ne - 1, 0)[:n, :] * w[1:2, :]
                          + pltpu.roll(dys[g], ne - 2, 0)[:n, :] * w[0:1, :])
                    du_ref[g, pl.ds(r, n), cols] = jnp.where(row >= PAD, du, 0.0).astype(du_ref.dtype)
                    for t in range(CONV_W):
                        pw[g][t] = pw[g][t] + jnp.sum(dy * us[g][t][:n, :], axis=0, keepdims=True)
                    pb[g] = pb[g] + jnp.sum(dy, axis=0, keepdims=True)
            for g in range(2):
                pwg = jnp.concatenate(pw[g], axis=0)

                @pl.when(i == 0)
                def _():
                    dw_ref[g, :, cols] = pwg
                    db_ref[g, :, cols] = pb[g]

                @pl.when(i > 0)
                def _():
                    dw_ref[g, :, cols] += pwg
                    db_ref[g, :, cols] += pb[g]

    nxt = lambda i: jnp.minimum((i + 1) * hb, last_halo)
    return pl.pallas_call(
        kern, name=name, grid=(P, L // tm),
        in_specs=[pl.BlockSpec((2, None, tm, C), lambda p, i: (0, p, i, 0)),
                  pl.BlockSpec((2, None, HALO, C), lambda p, i: (0, p, jnp.maximum(i * hb - 1, 0), 0)),
                  pl.BlockSpec((2, None, HALO, C), lambda p, i: (0, p, nxt(i), 0)),
                  pl.BlockSpec((2, None, CONV_W, C), lambda p, i: (0, p, 0, 0)),
                  pl.BlockSpec((2, None, 1, C), lambda p, i: (0, p, 0, 0)),
                  pl.BlockSpec((None, tm, C), lambda p, i: (p, i, 0)),
                  pl.BlockSpec((None, HALO, C), lambda p, i: (p, nxt(i), 0))],
        out_specs=[pl.BlockSpec((2, None, tm, C), lambda p, i: (0, p, i, 0)),
                   pl.BlockSpec((2, None, CONV_W, C), lambda p, i: (0, p, 0, 0)),
                   pl.BlockSpec((2, None, 1, C), lambda p, i: (0, p, 0, 0))],
        out_shape=[jax.ShapeDtypeStruct((2, P, L, C), BF16), jax.ShapeDtypeStruct((2, P, CONV_W, C), F32),
                   jax.ShapeDtypeStruct((2, P, 1, C), F32)],
        scratch_shapes=[pltpu.VMEM((2, tm + 2 * HALO, C), F32), pltpu.VMEM((ext, C), F32)],
        compiler_params=_params("parallel", "arbitrary"))(u4, u4, u4, cw4, cb4, dact, dact)


def _ffn_up(name):
    def run(h1, w3, cw4, cb4):
        u3 = _mm(h1, w3, sb="n", so="n", out_dtype=F32, name=name + "_up")
        u4 = u3.reshape((2, u3.shape[0] // 2) + u3.shape[1:])
        return _glu_fwd_call(u4, cw4, cb4, name + "_glu"), u4

    @jax.custom_vjp
    def f(h1, w3, cw4, cb4):
        return run(h1, w3, cw4, cb4)[0]

    def fwd(h1, w3, cw4, cb4):
        act, u4 = run(h1, w3, cw4, cb4)
        return act, (h1, w3, cw4, cb4, u4)

    def bwd(res, dact):
        h1, w3, cw4, cb4, u4 = res
        du4, dcw, dcb = _glu_bwd_call(u4, cw4, cb4, dact, name + "_glu_bwd")
        du3 = du4.reshape((du4.shape[0] * du4.shape[1],) + du4.shape[2:])
        dh1 = _mm(du3, w3, sa="k", lb="nk", sb="k", out_dtype=F32, name=name + "_up_dx")
        dw3 = _mm(h1, du3, la="km", sb="n", so="n", out_dtype=w3.dtype, name=name + "_up_dw")
        return dh1, dw3, dcw, dcb

    f.defvjp(fwd, bwd)
    return f


def _ffn_down(name):
    @jax.custom_vjp
    def f(act3, wo3):
        return _mm(act3, wo3, sa="k", sb="k", out_dtype=F32, name=name + "_fwd")

    def fwd(act3, wo3):
        return f(act3, wo3), (act3, wo3)

    def bwd(res, g):
        act3, wo3 = res
        dact = _mm(g, wo3, lb="nk", sb="n", so="n", out_dtype=act3.dtype, name=name + "_dx")
        dwo = _mm(act3, g, la="km", sa="m", so="m", out_dtype=wo3.dtype, name=name + "_dw")
        return dact, dwo

    f.defvjp(fwd, bwd)
    return f


def _pairs(n, kv_major):
    if kv_major:
        pr = [(i, j) for j in range(n) for i in range(j, n)]
    else:
        pr = [(i, j) for i in range(n) for j in range(i + 1)]
    return (jnp.asarray(np.array([p[0] for p in pr], np.int32)), jnp.asarray(np.array([p[1] for p in pr], np.int32)))


TN = (((0,), (0,)), ((), ()))
NT = (((1,), (1,)), ((), ()))


def _scores(kT, qT, i, j, tq, tk, masked):
    s = lax.dot_general(kT, qT, TN, preferred_element_type=F32)
    if not masked:
        return s, None
    col = j * tk + lax.broadcasted_iota(jnp.int32, (tk, tq), 0)
    row = i * tq + lax.broadcasted_iota(jnp.int32, (tk, tq), 1)
    mask = (col <= row) & (col >= PAD)
    return jnp.where(mask, s, NEG), mask


FOX_SCALE = 0.125
C_PARTS = 3


class _AttnLayout:
    def __init__(self, heads, q_rows, v_rows, bases=(0, 0, 0), extra=0, q_scale=None, grad_rows=None):
        self.heads, self.q_rows, self.v_rows, self.bases, self.extra, self.q_scale = heads, q_rows, v_rows, bases, extra, q_scale
        self.dk = q_rows + extra
        self.grad_rows = q_rows if grad_rows is None else grad_rows
        self.n_in = 5 if extra else 3

    def specs(self, tq, tk, hp):
        assert self.heads % hp == 0 and all(b % hp == 0 for b in self.bases)
        qb, kb, vb = (b // hp for b in self.bases)
        qmap = lambda g, t, it, jt: (qb + g, it[t])
        kmap = lambda g, t, it, jt: (kb + g, jt[t])
        vmap = lambda g, t, it, jt: (vb + g, jt[t])
        s = [pl.BlockSpec((hp * self.q_rows, tq), qmap), pl.BlockSpec((hp * self.q_rows, tk), kmap),
             pl.BlockSpec((hp * self.v_rows, tk), vmap)]
        if self.extra:
            s += [pl.BlockSpec((hp * self.extra, tq), lambda g, t, it, jt: (g, it[t])),
                  pl.BlockSpec((hp * self.extra, tk), lambda g, t, it, jt: (g, jt[t]))]
        return s

    def operands(self, refs, a):
        rows = lambda ref, n: ref[a * n:(a + 1) * n, :]
        q, k = rows(refs[0], self.q_rows), rows(refs[1], self.q_rows)
        if self.q_scale is not None:
            q = q * jnp.asarray(self.q_scale, q.dtype)
        if self.extra:
            q = jnp.concatenate([q, rows(refs[3], self.extra)], axis=0)
            k = jnp.concatenate([k, rows(refs[4], self.extra)], axis=0)
        return q, k, rows(refs[2], self.v_rows)


def _attn_fwd_call(lay, arrs, name):
    L = arrs[0].shape[1]
    H, dv, hp = lay.heads, lay.v_rows, ATTN_HEADS_FWD
    tq = tk = _tile(L, ROW_TILE)
    it, jt = _pairs(L // tq, False)

    def kern(it_ref, jt_ref, *refs):
        t = pl.program_id(1)
        i, j = it_ref[t], jt_ref[t]
        o_ref, lse_ref, m_s, l_s, acc_s = refs[lay.n_in:]

        @pl.when(j == 0)
        def _():
            m_s[...] = jnp.full_like(m_s, NEG)
            l_s[...] = jnp.zeros_like(l_s)
            acc_s[...] = jnp.zeros_like(acc_s)

        def step(masked):
            ops = [lay.operands(refs, a) for a in range(hp)]
            scores = [_scores(k, q, i, j, tq, tk, masked) for q, k, _ in ops]
            weights = []
            for a, (s, mask) in enumerate(scores):
                m_prev = m_s[a]
                m_new = jnp.maximum(m_prev, jnp.max(s, axis=0, keepdims=True))
                scale = jnp.exp(m_prev - m_new)
                p = jnp.exp(s - m_new)
                if masked:
                    p = jnp.where(mask, p, 0.0)
                l_s[a] = scale * l_s[a] + jnp.sum(p, axis=0, keepdims=True)
                m_s[a] = m_new
                weights.append((scale, p.astype(BF16)))
            for a, (scale, pb) in enumerate(weights):
                acc_s[a] = scale * acc_s[a] + jnp.dot(ops[a][2], pb, preferred_element_type=F32)

        pl.when(j == i)(lambda: step(True))
        pl.when(j != i)(lambda: step(False))

        @pl.when(j == i)
        def _():
            for a in range(hp):
                l = l_s[a]
                l = jnp.where(l == 0.0, 1.0, l)
                o_ref[a * dv:(a + 1) * dv, :] = (acc_s[a] / l).astype(o_ref.dtype)
                lse_ref[a] = m_s[a] + jnp.log(l)

    return pl.pallas_call(
        kern, name=name,
        grid_spec=pltpu.PrefetchScalarGridSpec(
            num_scalar_prefetch=2, grid=(H // hp, int(it.shape[0])), in_specs=lay.specs(tq, tk, hp),
            out_specs=[pl.BlockSpec((hp * dv, tq), lambda g, t, it, jt: (g, it[t])),
                       pl.BlockSpec((hp, 1, tq), lambda g, t, it, jt: (g, 0, it[t]))],
            scratch_shapes=[pltpu.VMEM((hp, 1, tq), F32), pltpu.VMEM((hp, 1, tq), F32), pltpu.VMEM((hp, dv, tq), F32)]),
        out_shape=[jax.ShapeDtypeStruct((H * dv, L), F32), jax.ShapeDtypeStruct((H, 1, L), F32)],
        compiler_params=_params("parallel", "arbitrary"))(it, jt, *arrs)


C_ROWS = SUBLANES_F32


def _attn_bwd_call(lay, arrs, oT, lse, doT, name):
    L = arrs[0].shape[1]
    H, dv, dk, gr, hp = lay.heads, lay.v_rows, lay.dk, lay.grad_rows, ATTN_HEADS_BWD
    tq = tk = _tile(L, ROW_TILE)
    nq = L // tq
    it, jt = _pairs(nq, True)
    npairs = int(it.shape[0])
    with_c = lay.extra > 0
    crows = pl.ds(lay.q_rows, C_ROWS)

    def kern(it_ref, jt_ref, *refs):
        t = pl.program_id(1)
        i, j = it_ref[t], jt_ref[t]
        rest = refs[lay.n_in:]
        o_ref, do_ref, lse_ref = rest[:3]
        if with_c:
            dq_ref, dk_ref, dv_ref, dqc_ref, dkc_ref, dq_s, dk_s, dv_s = rest[3:]
        else:
            dq_ref, dk_ref, dv_ref, dq_s, dk_s, dv_s = rest[3:]

        @pl.when(t == 0)
        def _():
            dq_s[...] = jnp.zeros_like(dq_s)

        @pl.when(i == j)
        def _():
            dk_s[...] = jnp.zeros_like(dk_s)
            dv_s[...] = jnp.zeros_like(dv_s)

        def step(masked):
            first = []
            for a in range(hp):
                q, k, v = lay.operands(refs, a)
                s, mask = _scores(k, q, i, j, tq, tk, masked)
                dof = do_ref[a * dv:(a + 1) * dv, :]
                dob = dof.astype(BF16)
                delta = jnp.sum(dof * o_ref[a * dv:(a + 1) * dv, :], axis=0, keepdims=True)
                dp = lax.dot_general(v, dob, TN, preferred_element_type=F32)
                first.append((q, k, s, mask, dob, delta, dp))
            second = []
            for a, (q, k, s, mask, dob, delta, dp) in enumerate(first):
                p = jnp.exp(s - lse_ref[a])
                if masked:
                    p = jnp.where(mask, p, 0.0)
                second.append((p.astype(BF16), (p * (dp - delta)).astype(BF16)))
            cols = pl.ds(pl.multiple_of(i * tq, tq), tq)
            for a, (pb, dsb) in enumerate(second):
                q, k, _, _, dob, _, _ = first[a]
                dv_s[a] += lax.dot_general(dob, pb, NT, preferred_element_type=F32)
                dk_s[a] += lax.dot_general(q, dsb, NT, preferred_element_type=F32)
                dq_s[a, :, cols] += jnp.dot(k, dsb, preferred_element_type=F32)

        pl.when(j == i)(lambda: step(True))
        pl.when(j != i)(lambda: step(False))

        @pl.when(i == nq - 1)
        def _():
            for a in range(hp):
                dk_ref[a * gr:(a + 1) * gr, :] = dk_s[a, :gr, :].astype(dk_ref.dtype)
                dv_ref[a * dv:(a + 1) * dv, :] = dv_s[a].astype(dv_ref.dtype)
                if with_c:
                    dkc_ref[a * C_ROWS:(a + 1) * C_ROWS, :] = dk_s[a, crows, :]

        @pl.when(t == npairs - 1)
        def _():
            for a in range(hp):
                dq = dq_s[a, :gr, :]
                dq_ref[a * gr:(a + 1) * gr, :] = (dq if lay.q_scale is None else dq * lay.q_scale).astype(dq_ref.dtype)
                if with_c:
                    dqc_ref[a * C_ROWS:(a + 1) * C_ROWS, :] = dq_s[a, crows, :]

    qcol = lambda g, t, it, jt: (g, it[t])
    kcol = lambda g, t, it, jt: (g, jt[t])
    whole = lambda g, t, it, jt: (g, 0)
    out_specs = [pl.BlockSpec((hp * gr, L), whole), pl.BlockSpec((hp * gr, tk), kcol), pl.BlockSpec((hp * dv, tk), kcol)]
    out_shape = [jax.ShapeDtypeStruct((H * gr, L), BF16), jax.ShapeDtypeStruct((H * gr, L), BF16),
                 jax.ShapeDtypeStruct((H * dv, L), BF16)]
    if with_c:
        out_specs += [pl.BlockSpec((hp * C_ROWS, L), whole), pl.BlockSpec((hp * C_ROWS, tk), kcol)]
        out_shape += [jax.ShapeDtypeStruct((H * C_ROWS, L), F32)] * 2
    return pl.pallas_call(
        kern, name=name,
        grid_spec=pltpu.PrefetchScalarGridSpec(
            num_scalar_prefetch=2, grid=(H // hp, npairs),
            in_specs=lay.specs(tq, tk, hp) + [pl.BlockSpec((hp * dv, tq), qcol), pl.BlockSpec((hp * dv, tq), qcol),
                                          pl.BlockSpec((hp, 1, tq), lambda g, t, it, jt: (g, 0, it[t]))],
            out_specs=out_specs,
            scratch_shapes=[pltpu.VMEM((hp, dk, L), F32), pltpu.VMEM((hp, dk, tk), F32), pltpu.VMEM((hp, dv, tk), F32)]),
        out_shape=out_shape, compiler_params=_params("parallel", "arbitrary"))(it, jt, *arrs, oT, doT, lse)


POISON = 1e30


def _fox_extras(cT):
    H, L = cT.shape
    to_bf16 = lambda t: lax.reduce_precision(t, exponent_bits=8, mantissa_bits=7)
    hi = to_bf16(cT)
    lo = to_bf16(cT - hi)
    ll = to_bf16(cT - hi - lo)
    terms = jnp.stack([hi, lo, ll], axis=1).astype(BF16)
    ones = jnp.ones_like(terms)
    fill = jnp.zeros((H, SUBLANES_BF16 - 2 * C_PARTS, L), BF16)
    pad_key = (jnp.arange(L) < PAD)[None, None, :]
    kterms = jnp.where(pad_key, jnp.asarray([POISON, 0.0, 0.0], BF16)[None, :, None], terms)
    eq = jnp.concatenate([terms, ones, fill], axis=1)
    ek = jnp.concatenate([ones, -kterms, fill], axis=1)
    return eq.reshape(H * SUBLANES_BF16, L), ek.reshape(H * SUBLANES_BF16, L)


def _fox_attention(name):
    assert FOX_SCALE == FOX_HD ** -0.5
    lay = _AttnLayout(FOX_HEADS, FOX_HD, FOX_HD, bases=(0, FOX_HEADS, 2 * FOX_HEADS), extra=SUBLANES_BF16,
                      q_scale=FOX_SCALE)

    def run(qkvT, cT):
        arrs = (qkvT, qkvT, qkvT) + _fox_extras(cT)
        oT, lse = _attn_fwd_call(lay, arrs, name + "_fwd")
        return oT, (arrs, oT, lse)

    @jax.custom_vjp
    def f(qkvT, cT):
        return run(qkvT, cT)[0]

    def fwd(qkvT, cT):
        return run(qkvT, cT)

    def bwd(res, doT):
        arrs, oT, lse = res
        dq, dk, dv, dqc, dkc = _attn_bwd_call(lay, arrs, oT, lse, doT, name + "_bwd")
        L = dq.shape[1]
        dc = dqc.reshape(FOX_HEADS, C_ROWS, L)[:, 0, :] - dkc.reshape(FOX_HEADS, C_ROWS, L)[:, C_PARTS, :]
        return jnp.concatenate([dq, dk, dv], axis=0), dc

    f.defvjp(fwd, bwd)
    return f


MLA_XROWS = SUBLANES_BF16


def _mla_attention(name):
    rows = MLA_NOPE + MLA_ROPE + MLA_XROWS
    lay = _AttnLayout(MLA_HEADS, rows, MLA_V, grad_rows=MLA_NOPE + MLA_ROPE)

    @jax.custom_vjp
    def f(qT, kT, vT):
        return _attn_fwd_call(lay, (qT, kT, vT), name + "_fwd")[0]

    def fwd(qT, kT, vT):
        oT, lse = _attn_fwd_call(lay, (qT, kT, vT), name + "_fwd")
        return oT, (qT, kT, vT, oT, lse)

    def bwd(res, doT):
        qT, kT, vT, oT, lse = res
        dq, dk, dv = _attn_bwd_call(lay, (qT, kT, vT), oT, lse, doT, name + "_bwd")
        L = dq.shape[1]
        widen = lambda t: jnp.pad(t.reshape(MLA_HEADS, -1, L), ((0, 0), (0, MLA_XROWS), (0, 0))).reshape(-1, L)
        return widen(dq), widen(dk), dv

    f.defvjp(fwd, bwd)
    return f


def _linear_t(name, out_dtype):
    @jax.custom_vjp
    def f(wT, x):
        return _mm(wT, x, lb="nk", out_dtype=out_dtype, name=name + "_fwd")

    def fwd(wT, x):
        return f(wT, x), (wT, x)

    def bwd(res, g):
        wT, x = res
        dwT = _mm(g, x, out_dtype=wT.dtype, name=name + "_dw")
        dx = _mm(g, wT, la="km", out_dtype=x.dtype, name=name + "_dx")
        return dwT, dx

    f.defvjp(fwd, bwd)
    return f


def _linear_km(name):
    @jax.custom_vjp
    def f(aT, w):
        return _mm(aT, w, la="km", out_dtype=F32, name=name + "_fwd")

    def fwd(aT, w):
        return f(aT, w), (aT, w)

    def bwd(res, g):
        aT, w = res
        daT = _mm(w, g, lb="nk", out_dtype=aT.dtype, name=name + "_dx")
        dw = _mm(aT, g, out_dtype=w.dtype, name=name + "_dw")
        return daT, dw

    f.defvjp(fwd, bwd)
    return f


def _swa_parts(q_ref, kc_ref, kp_ref, km_ref, sink_ref, i, scale):
    R = SWA_G * BLOCK
    q = q_ref[...].reshape(R, SWA_HD)
    nt = (((1,), (1,)), ((), ()))
    r = lax.broadcasted_iota(jnp.int32, (R, BLOCK), 0) & (BLOCK - 1)
    c = lax.broadcasted_iota(jnp.int32, (R, BLOCK), 1)
    s_c = jnp.where((c <= r) & (i >= 1), lax.dot_general(q, kc_ref[...], nt, preferred_element_type=F32) * scale, NEG)
    s_p = jnp.where((c > r) & (i >= 2), lax.dot_general(q, kp_ref[...], nt, preferred_element_type=F32) * scale, NEG)
    s_m = jnp.where((c >= PAD) & ((i >= 1) | (c <= r)),
                    lax.dot_general(q, km_ref[...], nt, preferred_element_type=F32) * scale, NEG)
    sink = jnp.broadcast_to(sink_ref[...], (SWA_G, BLOCK, 1)).reshape(R, 1)
    return q, s_c, s_p, s_m, sink


def _swa_specs(L):
    nb = L // BLOCK
    qs = pl.BlockSpec((SWA_KVH, SWA_G, BLOCK, SWA_HD), lambda i: (0, 0, i, 0))
    kc = pl.BlockSpec((SWA_KVH, BLOCK, SWA_HD), lambda i: (0, i, 0))
    kp = pl.BlockSpec((SWA_KVH, BLOCK, SWA_HD), lambda i: (0, jnp.maximum(i - 1, 0), 0))
    km = pl.BlockSpec((SWA_KVH, BLOCK, SWA_HD), lambda i: (0, 0, 0))
    sk = pl.BlockSpec((SWA_KVH, SWA_G, 1, 1), lambda i: (0, 0, 0, 0))
    ls = pl.BlockSpec((SWA_KVH, SWA_G, BLOCK, 1), lambda i: (0, 0, i, 0))
    return nb, qs, kc, kp, km, sk, ls


def _swa_fwd_call(q, k, v, sinks, scale, name):
    _, _, L, _ = q.shape
    nb, qs, kc, kp, km, sk, ls = _swa_specs(L)

    def kern(q_ref, kc_ref, kp_ref, km_ref, vc_ref, vp_ref, vm_ref, sink_ref, o_ref, lse_ref):
        i = pl.program_id(0)
        parts = [_swa_parts(q_ref.at[g], kc_ref.at[g], kp_ref.at[g], km_ref.at[g], sink_ref.at[g], i, scale)
                 for g in range(SWA_KVH)]
        for g, (_, s_c, s_p, s_m, sink) in enumerate(parts):
            mx = lambda s: jnp.max(s, axis=-1, keepdims=True)
            m = jnp.maximum(jnp.maximum(mx(s_c), mx(s_p)), jnp.maximum(mx(s_m), sink))
            p_c, p_p, p_m = jnp.exp(s_c - m), jnp.exp(s_p - m), jnp.exp(s_m - m)
            sm = lambda p: jnp.sum(p, axis=-1, keepdims=True)
            den = sm(p_c) + sm(p_p) + sm(p_m) + jnp.exp(sink - m)
            inv = 1.0 / den
            pv = lambda p, v_ref: jnp.dot((p * inv).astype(BF16), v_ref[g], preferred_element_type=F32)
            o = pv(p_c, vc_ref) + pv(p_p, vp_ref) + pv(p_m, vm_ref)
            o_ref[g] = o.reshape(SWA_G, BLOCK, SWA_HD)
            lse_ref[g] = (m + jnp.log(den)).reshape(SWA_G, BLOCK, 1)

    return pl.pallas_call(
        kern, name=name, grid=(nb,), in_specs=[qs, kc, kp, km, kc, kp, km, sk], out_specs=[qs, ls],
        out_shape=[jax.ShapeDtypeStruct((SWA_KVH, SWA_G, L, SWA_HD), F32), jax.ShapeDtypeStruct((SWA_KVH, SWA_G, L, 1), F32)],
        compiler_params=_params("parallel"))(q, k, k, k, v, v, v, sinks)


def _swa_bwd_call(q, k, v, sinks, o, lse, do, scale, name):
    _, _, L, _ = q.shape
    nb, qs, kc, kp, km, sk, ls = _swa_specs(L)
    R = SWA_G * BLOCK
    full = pl.BlockSpec((SWA_KVH, L, SWA_HD), lambda i: (0, 0, 0))

    def kern(q_ref, kc_ref, kp_ref, km_ref, vc_ref, vp_ref, vm_ref, sink_ref, o_ref, do_ref, lse_ref,
             dq_ref, dk_ref, dv_ref, dsink_ref):
        i = pl.program_id(0)

        @pl.when(i == 0)
        def _():
            dk_ref[...] = jnp.zeros_like(dk_ref)
            dv_ref[...] = jnp.zeros_like(dv_ref)
            dsink_ref[...] = jnp.zeros_like(dsink_ref)

        nt = (((1,), (1,)), ((), ()))
        tn = (((0,), (0,)), ((), ()))
        cur = pl.ds(pl.multiple_of(i * BLOCK, BLOCK), BLOCK)
        prev = pl.ds(pl.multiple_of(jnp.maximum(i - 1, 0) * BLOCK, BLOCK), BLOCK)
        meta = pl.ds(0, BLOCK)
        for g in range(SWA_KVH):
            q, s_c, s_p, s_m, sink = _swa_parts(q_ref.at[g], kc_ref.at[g], kp_ref.at[g], km_ref.at[g], sink_ref.at[g], i, scale)
            lse = lse_ref[g].reshape(R, 1)
            dof = do_ref[g].reshape(R, SWA_HD)
            dob = dof.astype(BF16)
            delta = jnp.sum(dof * o_ref[g].reshape(R, SWA_HD), axis=-1, keepdims=True)
            dq = jnp.zeros((R, SWA_HD), F32)
            for s, k_ref, v_ref, rows in ((s_c, kc_ref, vc_ref, cur), (s_p, kp_ref, vp_ref, prev), (s_m, km_ref, vm_ref, meta)):
                p = jnp.exp(s - lse)
                dp = lax.dot_general(dob, v_ref[g], nt, preferred_element_type=F32)
                ds = (p * (dp - delta)).astype(BF16)
                dq = dq + jnp.dot(ds, k_ref[g], preferred_element_type=F32)
                dv_ref[g, rows, :] += lax.dot_general(p.astype(BF16), dob, tn, preferred_element_type=F32)
                dk_ref[g, rows, :] += lax.dot_general(ds, q, tn, preferred_element_type=F32) * scale
            dq_ref[g] = (dq * scale).reshape(SWA_G, BLOCK, SWA_HD).astype(dq_ref.dtype)
            dsk = -jnp.exp(sink - lse) * delta
            dsink_ref[g] += jnp.sum(dsk.reshape(SWA_G, BLOCK, 1), axis=1, keepdims=True)

    return pl.pallas_call(
        kern, name=name, grid=(nb,), in_specs=[qs, kc, kp, km, kc, kp, km, sk, qs, qs, ls],
        out_specs=[qs, full, full, sk],
        out_shape=[jax.ShapeDtypeStruct((SWA_KVH, SWA_G, L, SWA_HD), BF16), jax.ShapeDtypeStruct((SWA_KVH, L, SWA_HD), F32),
                   jax.ShapeDtypeStruct((SWA_KVH, L, SWA_HD), F32), jax.ShapeDtypeStruct((SWA_KVH, SWA_G, 1, 1), F32)],
        compiler_params=_params("arbitrary"))(q, k, k, k, v, v, v, sinks, o, do, lse)


def _swa_attn(name, scale):
    @jax.custom_vjp
    def f(q, k, v, sinks):
        return _swa_fwd_call(q, k, v, sinks, scale, name + "_fwd")[0]

    def fwd(q, k, v, sinks):
        o, lse = _swa_fwd_call(q, k, v, sinks, scale, name + "_fwd")
        return o, (q, k, v, sinks, o, lse)

    def bwd(res, do):
        q, k, v, sinks, o, lse = res
        dq, dk, dv, dsink = _swa_bwd_call(q, k, v, sinks, o, lse, do, scale, name + "_bwd")
        return dq, dk.astype(BF16), dv.astype(BF16), dsink

    f.defvjp(fwd, bwd)
    return f


def _scan_call(x, bias, mul, pre_logsig, name):
    L, W = x.shape
    tb = _tile(L, ROW_TILE)
    has_mul = mul is not None

    def kern(x_ref, b_ref, *rest):
        if has_mul:
            m_ref, o_ref, tot_ref, carry = rest
        else:
            o_ref, tot_ref, carry = rest
        i = pl.program_id(0)

        @pl.when(i == 0)
        def _():
            carry[...] = jnp.zeros_like(carry)
            tot_ref[...] = jnp.zeros_like(tot_ref)

        z = x_ref[...] + b_ref[...]
        if pre_logsig:
            z = jnp.minimum(z, 0.0) - jnp.log(1.0 + jnp.exp(-jnp.abs(z)))
        row = lax.broadcasted_iota(jnp.int32, (tb, W), 0)
        s = 1
        while s < tb:
            z = z + jnp.where(row >= s, pltpu.roll(z, s, 0), 0.0)
            s *= 2
        z = z + carry[...]
        carry[...] = z[tb - 1:tb, :]
        if has_mul:
            z = z * m_ref[...]
        o_ref[...] = z
        tot_ref[...] += jnp.sum(z, axis=0, keepdims=True)

    row = pl.BlockSpec((tb, W), lambda i: (i, 0))
    vec = pl.BlockSpec((1, W), lambda i: (0, 0))
    return pl.pallas_call(
        kern, name=name, grid=(L // tb,), in_specs=[row, vec] + ([row] if has_mul else []), out_specs=[row, vec],
        out_shape=[jax.ShapeDtypeStruct((L, W), F32), jax.ShapeDtypeStruct((1, W), F32)],
        scratch_shapes=[pltpu.VMEM((1, W), F32)],
        compiler_params=_params("arbitrary"))(*([x, bias] + ([mul] if has_mul else [])))


def _sigmoid_neg_call(x, bias, name):
    L, W = x.shape
    tb = _tile(L, ROW_TILE)

    def kern(x_ref, b_ref, o_ref):
        row = pl.program_id(0) * tb + lax.broadcasted_iota(jnp.int32, (tb, 1), 0)
        o_ref[...] = jnp.where(row >= PAD, jax.nn.sigmoid(-(x_ref[...] + b_ref[...])), 0.0)

    row = pl.BlockSpec((tb, W), lambda i: (i, 0))
    vec = pl.BlockSpec((1, W), lambda i: (0, 0))
    return pl.pallas_call(kern, name=name, grid=(L // tb,), in_specs=[row, vec], out_specs=row,
                          out_shape=jax.ShapeDtypeStruct((L, W), F32), compiler_params=_params("parallel"))(x, bias)


def _decay(name):
    @jax.custom_vjp
    def f(fg, b):
        return _scan_call(fg, b, None, True, name + "_fwd")[0]

    def fwd(fg, b):
        return f(fg, b), (fg, b)

    def bwd(res, dc):
        fg, b = res
        sg = _sigmoid_neg_call(fg, b, name + "_dsig")
        dfg_rev, db = _scan_call(dc[::-1], jnp.zeros_like(b), sg[::-1], False, name + "_bwd")
        return dfg_rev[::-1], db

    f.defvjp(fwd, bwd)
    return f


def _loss_call(hf, target, name):
    L, D = hf.shape
    nb = L // BLOCK

    def kern(h_ref, t_ref, loss_ref, dy_ref, acc):
        i = pl.program_id(0)

        @pl.when(i == 0)
        def _():
            acc[...] = jnp.zeros_like(acc)
            dy_ref[...] = jnp.zeros_like(dy_ref)

        @pl.when(i > 0)
        def _():
            e = h_ref[...] - t_ref[...]
            dy_ref[...] = e * (1.0 / D)
            acc[...] += jnp.sum(e * e, axis=0, keepdims=True)

        @pl.when(i == nb - 1)
        def _():
            loss_ref[...] = jnp.broadcast_to(jnp.sum(acc[...], axis=1, keepdims=True) * (0.5 / D), loss_ref.shape)

    return pl.pallas_call(
        kern, name=name, grid=(nb,),
        in_specs=[pl.BlockSpec((BLOCK, D), lambda i: (i, 0)), pl.BlockSpec((BLOCK, D), lambda i: (jnp.maximum(i - 1, 0), 0))],
        out_specs=[pl.BlockSpec((1, LANES), lambda i: (0, 0)), pl.BlockSpec((BLOCK, D), lambda i: (i, 0))],
        out_shape=[jax.ShapeDtypeStruct((1, LANES), F32), jax.ShapeDtypeStruct((L, D), F32)],
        scratch_shapes=[pltpu.VMEM((1, D), F32)],
        compiler_params=_params("arbitrary"))(hf, target)


def _loss_head(name):
    @jax.custom_vjp
    def f(hf, target):
        return _loss_call(hf, target, name)[0][0, 0]

    def fwd(hf, target):
        loss, dy = _loss_call(hf, target, name)
        return loss[0, 0], (dy, jnp.zeros((), F32))

    def bwd(res, g):
        dy, _ = res
        return dy * g, None

    f.defvjp(fwd, bwd)
    return f


N_CHIPS = N_DEV // 2


def _sibling_swap(arrs, name):
    n = len(arrs)

    def body(*refs):
        srcs, outs = refs[:n], refs[n:2 * n]
        send_sems, recv_sems = refs[2 * n:]
        x, y, c = lax.axis_index("x"), lax.axis_index("y"), lax.axis_index("c")
        copies = [[pltpu.make_async_remote_copy(
            src_ref=srcs[a].at[2 * q + (1 - c)], dst_ref=outs[a].at[q], send_sem=send_sems.at[a, q],
            recv_sem=recv_sems.at[a, q], device_id=(x, y, 1 - c), device_id_type=MESH) for q in range(N_CHIPS)]
            for a in range(n)]
        for row in copies:
            for cp in row:
                cp.start()
        for row in copies:
            for cp in row:
                cp.wait()

    hbm = pl.BlockSpec(memory_space=pl.ANY)
    return pl.pallas_call(
        body, name=name, out_shape=[jax.ShapeDtypeStruct((N_CHIPS,) + tuple(a.shape[1:]), a.dtype) for a in arrs],
        in_specs=[hbm] * n, out_specs=[hbm] * n,
        scratch_shapes=[pltpu.SemaphoreType.DMA((n, N_CHIPS)), pltpu.SemaphoreType.DMA((n, N_CHIPS))],
    )(*arrs)


def _pair_sum_call(a, b, name):
    shape = a.shape
    cols = shape[-1]
    rows = int(np.prod(shape[:-1]))
    tr = rows
    if rows > 2 * ADAM_ROWS:
        tr = max(t for t in range(SUBLANES_BF16, 2 * ADAM_ROWS + 1, SUBLANES_BF16) if rows % t == 0)

    def kern(a_ref, b_ref, o_ref):
        o_ref[...] = (a_ref[...].astype(F32) + b_ref[...].astype(F32)).astype(o_ref.dtype)

    slab = pl.BlockSpec((tr, cols), lambda i: (i, 0))
    return pl.pallas_call(kern, name=name, grid=(rows // tr,), in_specs=[slab, slab], out_specs=slab,
                          out_shape=jax.ShapeDtypeStruct((rows, cols), a.dtype),
                          compiler_params=_params("parallel"))(a.reshape(rows, cols), b.reshape(rows, cols)).reshape(shape)


def _chip_exchange(arrs, name):
    n = len(arrs)

    def body(*refs):
        srcs, outs = refs[:n], refs[n:2 * n]
        send_sems, recv_sems, local_sems = refs[2 * n:]
        x, y, c = lax.axis_index("x"), lax.axis_index("y"), lax.axis_index("c")
        mine = 2 * x + y
        chips = [(1 - x, y), (x, 1 - y), (1 - x, 1 - y)]
        started = []
        for a in range(n):
            own = pltpu.make_async_copy(srcs[a].at[mine], outs[a].at[mine], local_sems.at[a])
            own.start()
            started.append(own.wait)
        for k, (px, py) in enumerate(chips):
            for a in range(n):
                cp = pltpu.make_async_remote_copy(src_ref=srcs[a].at[2 * px + py], dst_ref=outs[a].at[mine],
                                                  send_sem=send_sems.at[a, k], recv_sem=recv_sems.at[a, k],
                                                  device_id=(px, py, c), device_id_type=MESH)
                cp.start()
                started.append(cp.wait_send)
        for k, (px, py) in enumerate(chips):
            for a in range(n):
                pltpu.make_async_remote_copy(src_ref=srcs[a].at[mine], dst_ref=outs[a].at[2 * px + py],
                                             send_sem=send_sems.at[a, k], recv_sem=recv_sems.at[a, k],
                                             device_id=(px, py, c), device_id_type=MESH).wait_recv()
        for wait in started:
            wait()

    hbm = pl.BlockSpec(memory_space=pl.ANY)
    return pl.pallas_call(
        body, name=name, out_shape=[jax.ShapeDtypeStruct(a.shape, a.dtype) for a in arrs],
        in_specs=[hbm] * n, out_specs=[hbm] * n,
        scratch_shapes=[pltpu.SemaphoreType.DMA((n, N_CHIPS - 1)), pltpu.SemaphoreType.DMA((n, N_CHIPS - 1)),
                        pltpu.SemaphoreType.DMA((n,))],
    )(*arrs)


def _scatter(arrs, names):
    c = lax.axis_index("c")
    theirs = _sibling_swap(arrs, "scatter_swap")
    halves = [lax.dynamic_index_in_dim(a.reshape((N_CHIPS, 2) + a.shape[1:]), c, axis=1, keepdims=False) for a in arrs]
    sums = [_pair_sum_call(h, t, "pair_sum_" + nm) for h, t, nm in zip(halves, theirs, names)]
    return _chip_exchange(sums, "scatter_chips")


def _gather(arrs, name):
    n = len(arrs)
    hops = N_DEV - 1

    def body(*refs):
        srcs, outs = refs[:n], refs[n:2 * n]
        send_sems, recv_sems, local_sems = refs[2 * n:]
        x, y, c = lax.axis_index("x"), lax.axis_index("y"), lax.axis_index("c")
        ident = lambda px, py, pc: 4 * px + 2 * py + pc
        me, sibling = ident(x, y, c), (x, y, 1 - c)
        chips = [(1 - x, y), (x, 1 - y), (1 - x, 1 - y)]

        def copy(a, k, block, to, src=None):
            return pltpu.make_async_remote_copy(
                src_ref=outs[a].at[block] if src is None else src, dst_ref=outs[a].at[block],
                send_sem=send_sems.at[a, k], recv_sem=recv_sems.at[a, k], device_id=to, device_id_type=MESH)

        started = []
        for a in range(n):
            mine = pltpu.make_async_copy(srcs[a], outs[a].at[me], local_sems.at[a])
            mine.start()
            started.append(mine.wait)
            for k, to in enumerate([sibling] + [(px, py, c) for px, py in chips]):
                cp = copy(a, k, me, to, src=srcs[a])
                cp.start()
                started.append(cp.wait_send)
        for j, (px, py) in enumerate(chips):
            for a in range(n):
                copy(a, 1 + j, ident(px, py, c), (x, y, c)).wait_recv()
                cp = copy(a, 4 + j, ident(px, py, c), sibling)
                cp.start()
                started.append(cp.wait_send)
        for a in range(n):
            copy(a, 0, ident(x, y, 1 - c), (x, y, c)).wait_recv()
            for j, (px, py) in enumerate(chips):
                copy(a, 4 + j, ident(px, py, 1 - c), (x, y, c)).wait_recv()
        for wait in started:
            wait()

    hbm = pl.BlockSpec(memory_space=pl.ANY)
    return pl.pallas_call(
        body, name=name, out_shape=[jax.ShapeDtypeStruct((N_DEV,) + tuple(a.shape), a.dtype) for a in arrs],
        in_specs=[hbm] * n, out_specs=[hbm] * n,
        scratch_shapes=[pltpu.SemaphoreType.DMA((n, hops)), pltpu.SemaphoreType.DMA((n, hops)),
                        pltpu.SemaphoreType.DMA((n,))],
    )(*arrs)


def _adamw_call(parts, w, m, v, name):
    shape = w.shape
    nparts = parts.shape[0]
    cols = shape[-1]
    rows = int(np.prod(shape[:-1]))
    tr = rows
    if rows > ADAM_ROWS:
        tr = max(t for t in range(SUBLANES_F32, ADAM_ROWS + 1, SUBLANES_F32) if rows % t == 0)
    c1 = 1.0 / (1.0 - ADAM_B1 ** ADAM_STEP)
    c2 = 1.0 / (1.0 - ADAM_B2 ** ADAM_STEP)

    def kern(p_ref, w_ref, m_ref, v_ref, g_ref, d_ref, nm_ref, nv_ref):
        g = p_ref[0].astype(F32)
        for p in range(1, nparts):
            g = g + p_ref[p].astype(F32)
        nm = ADAM_B1 * m_ref[...] + (1.0 - ADAM_B1) * g
        nv = ADAM_B2 * v_ref[...] + (1.0 - ADAM_B2) * (g * g)
        g_ref[...] = g
        nm_ref[...] = nm
        nv_ref[...] = nv
        d_ref[...] = -ADAM_LR * ((nm * c1) / (jnp.sqrt(nv * c2) + ADAM_EPS) + ADAM_WD * w_ref[...])

    slab = pl.BlockSpec((tr, cols), lambda i: (i, 0))
    flat = lambda t: t.reshape(rows, cols)
    res = pl.pallas_call(
        kern, name=name, grid=(rows // tr,),
        in_specs=[pl.BlockSpec((nparts, tr, cols), lambda i: (0, i, 0)), slab, slab, slab], out_specs=[slab] * 4,
        out_shape=[jax.ShapeDtypeStruct((rows, cols), F32)] * 4,
        compiler_params=_params("parallel"))(parts.reshape(nparts, rows, cols), flat(w), flat(m), flat(v))
    return [r.reshape(shape) for r in res]


def _to_full(stacked, axis):
    moved = jnp.moveaxis(stacked, 0, axis)
    s = list(moved.shape)
    return moved.reshape(s[:axis] + [s[axis] * s[axis + 1]] + s[axis + 2:])


def _heads(x, h, d):
    return x.reshape(x.shape[0], h, d).transpose(1, 0, 2)


def _unheads(x):
    return x.transpose(1, 0, 2).reshape(x.shape[1], -1)


def _pad_cols(x, width):
    return jnp.pad(x, ((0, 0), (0, width - x.shape[1])))


def _fox_mixer(h, w_in, b_f, w_o, tag):
    hd = FOX_HEADS * FOX_HD
    qkvT = _linear_t(tag + "_qkv", BF16)(w_in[:, :3 * hd].T, h)
    fg = _linear(tag + "_gate", F32)(h, _pad_cols(w_in[:, 3 * hd:], LANES))
    c = _decay(tag + "_decay")(fg, _pad_cols(b_f[None, :], LANES))
    oT = _fox_attention(tag + "_attn")(qkvT, c[:, :FOX_HEADS].T)
    return _linear_km(tag + "_out")(oT, w_o)


def _swa_mixer(h, w_in, sinks, w_o, tabs, tag):
    L = h.shape[0]
    qd, kd = SWA_QH * SWA_HD, SWA_KVH * SWA_HD
    proj = _linear(tag + "_qkv", F32)(h, w_in)
    rope = _rope(tag + "_rope", ROPE_DIM // 2, BF16)
    q = rope(proj[:, :qd], *tabs)
    k = rope(proj[:, qd:qd + kd], *tabs)
    v = proj[:, qd + kd:].astype(BF16)
    qg = _heads(q, SWA_QH, SWA_HD).reshape(SWA_KVH, SWA_G, L, SWA_HD)
    o = _swa_attn(tag + "_attn", SWA_HD ** -0.5)(qg, _heads(k, SWA_KVH, SWA_HD), _heads(v, SWA_KVH, SWA_HD),
                                                 sinks.reshape(SWA_KVH, SWA_G, 1, 1))
    return _linear(tag + "_out", F32)(_unheads(o.reshape(SWA_QH, L, SWA_HD)), w_o)


def _mla_mixer(h, w_a, g_q, g_kv, w_uq, w_ukv, w_o, tabs, tag):
    L = h.shape[0]
    cq = _linear(tag + "_aq", F32)(h, w_a[:, :MLA_QL])
    ckv = _linear(tag + "_akv", F32)(h, w_a[:, MLA_QL:MLA_QL + MLA_KVL])
    kr = _linear(tag + "_akr", F32)(h, _pad_cols(w_a[:, MLA_QL + MLA_KVL:], LANES))
    cq = _rmsnorm(tag + "_nq")(cq, g_q[None, :])
    ckv = _rmsnorm(tag + "_nkv")(ckv, g_kv[None, :])
    q = _linear(tag + "_uq", F32)(cq, w_uq).reshape(L, MLA_HEADS, MLA_NOPE + MLA_ROPE)
    kv = _linear(tag + "_ukv", BF16)(ckv, w_ukv).reshape(L, MLA_HEADS, MLA_NOPE + MLA_V)
    scale = (MLA_NOPE + MLA_ROPE) ** -0.5
    rope = _rope(tag + "_rope", MLA_ROPE // 2, BF16)
    q_rope = rope(q[:, :, MLA_NOPE:].reshape(L, MLA_HEADS * MLA_ROPE), *(t * scale for t in tabs))
    k_rope = rope(kr, *tabs)[:, :MLA_ROPE]
    qf = jnp.concatenate([(q[:, :, :MLA_NOPE] * scale).astype(BF16), q_rope.reshape(L, MLA_HEADS, MLA_ROPE)], axis=-1)
    kf = jnp.concatenate([kv[:, :, :MLA_NOPE], jnp.broadcast_to(k_rope[:, None, :], (L, MLA_HEADS, MLA_ROPE))], axis=-1)
    seq_last = lambda t: t.transpose(1, 2, 0)
    unit = jnp.zeros((MLA_HEADS, MLA_XROWS, L), BF16).at[:, 0, :].set(1.0)
    poison = jnp.zeros((MLA_HEADS, MLA_XROWS, L), BF16).at[:, 0, :].set(jnp.where(jnp.arange(L) < PAD, -POISON, 0.0))
    qT = jnp.concatenate([seq_last(qf), unit], axis=1).reshape(-1, L)
    kT = jnp.concatenate([seq_last(kf), poison], axis=1).reshape(-1, L)
    oT = _mla_attention(tag + "_attn")(qT, kT, seq_last(kv[:, :, MLA_NOPE:]).reshape(-1, L))
    return _linear_km(tag + "_out")(oT, w_o)


BIG = (("fox_w_in", 2), ("fox_w_o", 1), ("swa_w_in", 2), ("swa_w_o", 1), ("mla_w_a", 1), ("mla_w_uq", 2),
       ("mla_w_ukv", 2), ("mla_w_o", 1), ("ffn_w_in", 2), ("ffn_w_out", 1))
SMALL = (("meta_tokens", 1), ("mla_g_q", 1), ("mla_g_kv", 1), ("ffn_conv_w", 2))
SHARDED = BIG + SMALL
REPL = ("ln1_g", "ln1_b", "ln2_g", "ln2_b", "fox_b_f", "swa_sinks", "ffn_conv_b")
FFN_STACKED = ("ffn_w_in", "ffn_w_out", "ffn_conv_w")
WEIGHTS = ("meta_tokens", "ln1_g", "ln1_b", "ln2_g", "ln2_b", "fox_w_in", "fox_b_f", "fox_w_o", "swa_w_in", "swa_sinks",
           "swa_w_o", "mla_w_a", "mla_g_q", "mla_g_kv", "mla_w_uq", "mla_w_ukv", "mla_w_o", "ffn_w_in", "ffn_conv_w",
           "ffn_conv_b", "ffn_w_out")


def _local_loss(stacked, repl, x, target):
    S, D = x.shape
    L = S + BLOCK
    wts = {n: _to_full(stacked[n], ax) for n, ax in SHARDED if n not in FFN_STACKED}
    wts.update(repl)
    half = N_DEV // 2
    h = jnp.concatenate([jnp.zeros((PAD, D), F32), wts["meta_tokens"], x], axis=0)
    cos_p, sin_p = _rope_tables(L, ROPE_DIM, ROPE_THETA)
    tabs_p = _rope_lanes(cos_p, sin_p, SWA_HD)
    cos_m, sin_m = _rope_tables(L, MLA_ROPE, MLA_ROPE_THETA)
    tabs_m = _rope_lanes(cos_m, sin_m, MLA_ROPE)
    for i in range(DEPTH):
        kind, j = i % 3, i // 3
        tag = f"l{i}"
        if kind == 0:
            mix = _fox_mixer(h, wts["fox_w_in"][j], wts["fox_b_f"][j], wts["fox_w_o"][j], tag + "_fox")
        elif kind == 1:
            mix = _swa_mixer(h, wts["swa_w_in"][j], wts["swa_sinks"][j], wts["swa_w_o"][j], tabs_p, tag + "_swa")
        else:
            mix = _mla_mixer(h, wts["mla_w_a"][j], wts["mla_g_q"][j], wts["mla_g_kv"][j], wts["mla_w_uq"][j],
                             wts["mla_w_ukv"][j], wts["mla_w_o"][j], tabs_m, tag + "_mla")
        h = _deepnorm(tag + "_ln1")(h, mix, wts["ln1_g"][i][None, :], wts["ln1_b"][i][None, :])
        w3 = stacked["ffn_w_in"][:, i]
        shard = w3.shape[-1]
        cw4 = stacked["ffn_conv_w"][:, i].reshape(2, half, CONV_W, shard)
        cb4 = wts["ffn_conv_b"][i].reshape(2, half, 1, shard)
        wo3 = stacked["ffn_w_out"][:, i].reshape(half, shard, D)
        act3 = _ffn_up(tag + "_ffn")(h, w3, cw4, cb4)
        ffn = _ffn_down(tag + "_ffn_down")(act3, wo3)
        h = _deepnorm(tag + "_ln2")(h, ffn, wts["ln2_g"][i][None, :], wts["ln2_b"][i][None, :])
    return _loss_head("loss_head")(h, target)


def _step(x, target, w, m, v):
    names = [n for n, _ in SHARDED]
    big = {n for n, _ in BIG}
    stacked = dict(zip(names, _gather([w[n].astype(BF16) if n in big else w[n] for n in names], "gather_weights")))
    repl = {n: w[n] for n in REPL}
    loss, (g_st, g_repl, grad_x) = jax.value_and_grad(_local_loss, argnums=(0, 1, 2))(stacked, repl, x, target)
    got = dict(zip(names, _scatter([g_st[n] for n in names], names)))
    got.update(zip(REPL, _gather([g_repl[n] for n in REPL], "gather_repl_grads")))
    out = {}
    for n in WEIGHTS:
        for kind, a in zip(("grad", "delta", "new_m", "new_v"), _adamw_call(got[n], w[n], m[n], v[n], "adamw_" + n)):
            out[(kind, n)] = a
    return loss, grad_x, out


def kernel(x, meta_tokens, ln1_g, ln1_b, ln2_g, ln2_b, fox_w_in, fox_b_f, fox_w_o, swa_w_in, swa_sinks, swa_w_o, mla_w_a, mla_g_q, mla_g_kv, mla_w_uq, mla_w_ukv, mla_w_o, ffn_w_in, ffn_conv_w, ffn_conv_b, ffn_w_out, loss_target, m_meta_tokens, m_ln1_g, m_ln1_b, m_ln2_g, m_ln2_b, m_fox_w_in, m_fox_b_f, m_fox_w_o, m_swa_w_in, m_swa_sinks, m_swa_w_o, m_mla_w_a, m_mla_g_q, m_mla_g_kv, m_mla_w_uq, m_mla_w_ukv, m_mla_w_o, m_ffn_w_in, m_ffn_conv_w, m_ffn_conv_b, m_ffn_w_out, v_meta_tokens, v_ln1_g, v_ln1_b, v_ln2_g, v_ln2_b, v_fox_w_in, v_fox_b_f, v_fox_w_o, v_swa_w_in, v_swa_sinks, v_swa_w_o, v_mla_w_a, v_mla_g_q, v_mla_g_kv, v_mla_w_uq, v_mla_w_ukv, v_mla_w_o, v_ffn_w_in, v_ffn_conv_w, v_ffn_conv_b, v_ffn_w_out):
    args = (meta_tokens, ln1_g, ln1_b, ln2_g, ln2_b, fox_w_in, fox_b_f, fox_w_o, swa_w_in, swa_sinks, swa_w_o, mla_w_a,
            mla_g_q, mla_g_kv, mla_w_uq, mla_w_ukv, mla_w_o, ffn_w_in, ffn_conv_w, ffn_conv_b, ffn_w_out)
    ms = (m_meta_tokens, m_ln1_g, m_ln1_b, m_ln2_g, m_ln2_b, m_fox_w_in, m_fox_b_f, m_fox_w_o, m_swa_w_in, m_swa_sinks,
          m_swa_w_o, m_mla_w_a, m_mla_g_q, m_mla_g_kv, m_mla_w_uq, m_mla_w_ukv, m_mla_w_o, m_ffn_w_in, m_ffn_conv_w,
          m_ffn_conv_b, m_ffn_w_out)
    vs = (v_meta_tokens, v_ln1_g, v_ln1_b, v_ln2_g, v_ln2_b, v_fox_w_in, v_fox_b_f, v_fox_w_o, v_swa_w_in, v_swa_sinks,
          v_swa_w_o, v_mla_w_a, v_mla_g_q, v_mla_g_kv, v_mla_w_uq, v_mla_w_ukv, v_mla_w_o, v_ffn_w_in, v_ffn_conv_w,
          v_ffn_conv_b, v_ffn_w_out)
    w = dict(zip(WEIGHTS, args))
    m = dict(zip(WEIGHTS, ms))
    v = dict(zip(WEIGHTS, vs))
    loss, grad_x, out = _step(x[0], loss_target[0], w, m, v)
    loss = lax.psum(loss, ("x", "y", "c"))
    res = [loss, grad_x[None]]
    for kind in ("grad", "delta", "new_m", "new_v"):
        res += [out[(kind, n)] for n in WEIGHTS]
    return tuple(res)
```
